```python
import math
import jax
import jax.numpy as jnp
from jax import lax
import numpy as np

D_MODEL = 1024
BATCH = 8
SEQ = 2048
DEPTH = 4

HEAD_DIM = 64
CONV_CH = 512
CONV_K = 3
FOX_HEADS = 8
FOX_WIDTH = FOX_HEADS * HEAD_DIM
EVEN_IN = 3 * CONV_CH + 3 * FOX_WIDTH + FOX_HEADS
Q_BLOCK = 128
NSA_GROUPS = 4
NSA_HPG = 4
NSA_HEADS = NSA_GROUPS * NSA_HPG
NSA_WIDTH = NSA_HEADS * HEAD_DIM
NSA_KV = NSA_GROUPS * HEAD_DIM
CMP_LEN = 32
CMP_STRIDE = 16
SLC_LEN = 64
SLC_TOPN = 16
WINDOW = 512
SLC_QCHUNK = 16
ODD_IN = NSA_WIDTH + 6 * NSA_KV + 3 * NSA_HEADS
REL_BUCKETS = 32
REL_MAX_DIST = 128
PEER_HEADS = 8
PEER_NKEYS = 128
PEER_EXPERTS = PEER_NKEYS * PEER_NKEYS
PEER_QDIM = 128
PEER_TOPK = 16
PEER_CHUNK = 128
N_EVEN = (DEPTH + 1) // 2
N_ODD = DEPTH // 2
EPS = 1e-6
NEG_INF = -1e30
FORCE_SCORE = 1e6

kernel_name = 'hybrid_conv_fox_nsa_peer_block'


def rms_norm(x, g):
    xf = x.astype(jnp.float32)
    y = xf * lax.rsqrt(jnp.mean(xf * xf, axis=-1, keepdims=True) + EPS)
    return (y * g.astype(jnp.float32)).astype(x.dtype)


def modulate(h, shift, scale):
    return h * (1 + scale[:, None, :]) + shift[:, None, :]


def masked_softmax(s, mask):
    s = jnp.where(mask, s.astype(jnp.float32), NEG_INF)
    return jax.nn.softmax(s, axis=-1) * mask


def rel_bucket(dist):
    n = jnp.maximum(dist, 0)
    max_exact = REL_BUCKETS // 2
    nf = jnp.maximum(n, 1).astype(jnp.float32)
    large = max_exact + (jnp.log(nf / max_exact) / math.log(REL_MAX_DIST / max_exact)
                         * (REL_BUCKETS - max_exact)).astype(jnp.int32)
    large = jnp.minimum(large, REL_BUCKETS - 1)
    return jnp.where(n < max_exact, n, large)


def short_conv(u, w):
    return lax.conv_general_dilated(
        u, w.astype(u.dtype)[:, None, :], window_strides=(1,), padding=((CONV_K - 1, 0),),
        dimension_numbers=('NWC', 'WIO', 'NWC'), feature_group_count=u.shape[-1])


def fox_attention(q, k, v, logf):
    B, S, H, Dh = q.shape
    nb = S // Q_BLOCK
    cum_t = jnp.cumsum(logf, axis=1).transpose(0, 2, 1)
    kpos = jnp.arange(S)
    scale = Dh ** -0.5
    qb = q.reshape(B, nb, Q_BLOCK, H, Dh).swapaxes(0, 1)
    cb = cum_t.reshape(B, H, nb, Q_BLOCK).transpose(2, 0, 1, 3)
    tb = kpos.reshape(nb, Q_BLOCK)

    def block(args):
        qi, ci, ti = args
        s = jnp.einsum('bqhd,bkhd->bhqk', qi, k).astype(jnp.float32) * scale
        s = s + (ci[..., None] - cum_t[:, :, None, :])
        p = masked_softmax(s, ti[:, None] >= kpos[None, :])
        return jnp.einsum('bhqk,bkhd->bqhd', p.astype(v.dtype), v)

    out = lax.map(block, (qb, cb, tb))
    return out.swapaxes(0, 1).reshape(B, S, H, Dh)


def conv_fox_mixer(h, w_in, b_f, conv_w, q_g, k_g, w_out):
    B, S, _ = h.shape
    splits = list(np.cumsum([CONV_CH] * 3 + [FOX_WIDTH] * 3))
    cb, cc, ch, q, k, v, f = jnp.split(h @ w_in, splits, axis=-1)
    conv_out = cb * short_conv(cc * ch, conv_w)
    q = rms_norm(q.reshape(B, S, FOX_HEADS, HEAD_DIM), q_g)
    k = rms_norm(k.reshape(B, S, FOX_HEADS, HEAD_DIM), k_g)
    v = v.reshape(B, S, FOX_HEADS, HEAD_DIM)
    logf = jax.nn.log_sigmoid((f + b_f).astype(jnp.float32))
    attn = fox_attention(q, k, v, logf).reshape(B, S, FOX_WIDTH)
    return jnp.concatenate([conv_out, attn], axis=-1) @ w_out


def nsa_mixer(h, w_in, b_gate, q_g, k_g, cmp_pos, cmp_w1, cmp_w2, rel_table, w_out):
    B, S, _ = h.shape
    G, J, Dh = NSA_GROUPS, NSA_HPG, HEAD_DIM
    f32 = jnp.float32
    splits = list(np.cumsum([NSA_WIDTH] + [NSA_KV] * 6))
    q, kc, vc, ks, vs, kw, vw, gl = jnp.split(h @ w_in, splits, axis=-1)
    q = rms_norm(q.reshape(B, S, G, J, Dh), q_g)
    kc, vc, ks, vs, kw, vw = [a.reshape(B, S, G, Dh) for a in (kc, vc, ks, vs, kw, vw)]
    ks = rms_norm(ks, k_g[1])
    kw = rms_norm(kw, k_g[2])
    gates = jax.nn.sigmoid((gl + b_gate).astype(f32)).reshape(B, S, G, J, 3).astype(h.dtype)
    tpos = jnp.arange(S)
    table = rel_table.reshape(REL_BUCKETS, G, J)
    scale = Dh ** -0.5

    n_cmp = (S - CMP_LEN) // CMP_STRIDE + 1
    starts = jnp.arange(n_cmp) * CMP_STRIDE
    ends = starts + CMP_LEN - 1
    tok = starts[:, None] + jnp.arange(CMP_LEN)[None, :]

    def compress(a, pos, w1, w2):
        blk = a[:, tok] + pos[:, None, :]
        blk = blk.transpose(0, 1, 3, 2, 4).reshape(B, n_cmp, G, CMP_LEN * Dh)
        return jax.nn.gelu(blk @ w1) @ w2

    Kc = rms_norm(compress(kc, cmp_pos[0], cmp_w1[0], cmp_w2[0]), k_g[0])
    Vc = compress(vc, cmp_pos[1], cmp_w1[1], cmp_w2[1])
    s_c = jnp.einsum('bsgjd,bngd->bgjsn', q, Kc).astype(f32) * scale
    s_c = s_c + table[rel_bucket(tpos[:, None] - ends[None, :])].transpose(2, 3, 0, 1)
    p_c = masked_softmax(s_c, ends[None, :] <= tpos[:, None])
    o_c = jnp.einsum('bgjsn,bngd->bsgjd', p_c.astype(Vc.dtype), Vc)

    n_slc = S // SLC_LEN
    top_n = min(SLC_TOPN, n_slc)
    blk_ids = jnp.arange(n_slc)
    bstart = blk_ids * SLC_LEN
    overlap = ((starts[:, None] < bstart[None, :] + SLC_LEN)
               & (starts[:, None] + CMP_LEN > bstart[None, :])).astype(f32)
    imp = jnp.einsum('bgjsn,nm->bgsm', p_c, overlap)
    cur = tpos // SLC_LEN
    forced = (blk_ids[None, :] == 0) | (blk_ids[None, :] == cur[:, None]) | (blk_ids[None, :] == cur[:, None] - 1)
    imp = jnp.where(forced, FORCE_SCORE, jnp.where(bstart[None, :] <= tpos[:, None], imp, -1.0))
    sel_score, sel_idx = lax.top_k(imp, top_n)
    sel_valid = sel_score >= 0

    Ks = ks.reshape(B, n_slc, SLC_LEN, G, Dh).transpose(0, 3, 1, 2, 4)
    Vs = vs.reshape(B, n_slc, SLC_LEN, G, Dh).transpose(0, 3, 1, 2, 4)
    nq = S // SLC_QCHUNK
    bi = jnp.arange(B)[:, None, None, None]
    gi = jnp.arange(G)[None, :, None, None]

    def slc_chunk(args):
        qi, idx, val, ti = args
        Kg = Ks[bi, gi, idx]
        Vg = Vs[bi, gi, idx]
        kpos = idx[..., None] * SLC_LEN + jnp.arange(SLC_LEN)
        dist = ti[None, None, :, None, None] - kpos
        mask = val[..., None] & (dist >= 0)
        bias = jnp.moveaxis(table[rel_bucket(dist), gi[..., None]], -1, 2)
        s = jnp.einsum('bqgjd,bgqnld->bgjqnl', qi, Kg).astype(f32) * scale + bias
        sh = s.shape
        m = jnp.broadcast_to(mask[:, :, None], sh).reshape(sh[:4] + (-1,))
        p = masked_softmax(s.reshape(sh[:4] + (-1,)), m).reshape(sh)
        return jnp.einsum('bgjqnl,bgqnld->bqgjd', p.astype(Vg.dtype), Vg)

    qs = q.reshape(B, nq, SLC_QCHUNK, G, J, Dh).swapaxes(0, 1)
    idx_s = sel_idx.reshape(B, G, nq, SLC_QCHUNK, top_n).transpose(2, 0, 1, 3, 4)
    val_s = sel_valid.reshape(B, G, nq, SLC_QCHUNK, top_n).transpose(2, 0, 1, 3, 4)
    t_s = tpos.reshape(nq, SLC_QCHUNK)
    o_s = lax.map(slc_chunk, (qs, idx_s, val_s, t_s)).swapaxes(0, 1).reshape(B, S, G, J, Dh)

    kwp = jnp.pad(kw, ((0, 0), (WINDOW, 0), (0, 0), (0, 0)))
    vwp = jnp.pad(vw, ((0, 0), (WINDOW, 0), (0, 0), (0, 0)))
    nb = S // Q_BLOCK
    span = Q_BLOCK + WINDOW
    koff = jnp.arange(span)
    dist_w = jnp.arange(Q_BLOCK)[:, None] + WINDOW - koff[None, :]
    band = (dist_w >= 0) & (dist_w < WINDOW)
    bias_w = table[rel_bucket(dist_w)].transpose(2, 3, 0, 1)

    def win_block(args):
        qi, i = args
        start = i * Q_BLOCK
        kb = lax.dynamic_slice_in_dim(kwp, start, span, axis=1)
        vb = lax.dynamic_slice_in_dim(vwp, start, span, axis=1)
        s = jnp.einsum('bqgjd,bkgd->bgjqk', qi, kb).astype(f32) * scale + bias_w
        mask = band & ((start - WINDOW + koff) >= 0)[None, :]
        p = masked_softmax(s, mask)
        return jnp.einsum('bgjqk,bkgd->bqgjd', p.astype(vb.dtype), vb)

    qw = q.reshape(B, nb, Q_BLOCK, G, J, Dh).swapaxes(0, 1)
    o_w = lax.map(win_block, (qw, jnp.arange(nb))).swapaxes(0, 1).reshape(B, S, G, J, Dh)

    o = gates[..., 0:1] * o_c + gates[..., 1:2] * o_s + gates[..., 2:3] * o_w
    return o.reshape(B, S, NSA_WIDTH) @ w_out


def peer_ffn(h, w_q, sub_keys, u, v):
    B, S, D = h.shape
    T = B * S
    half = PEER_QDIM // 2
    hc = h.reshape(T // PEER_CHUNK, PEER_CHUNK, D)

    def chunk(xc):
        q = (xc @ w_q).reshape(-1, PEER_HEADS, 2, half)
        s = jnp.einsum('thpd,pkd->thpk', q, sub_keys).astype(jnp.float32)
        sv, si = lax.top_k(s, PEER_TOPK)
        cand = (sv[:, :, 0, :, None] + sv[:, :, 1, None, :]).reshape(-1, PEER_HEADS, PEER_TOPK * PEER_TOPK)
        cand_id = (si[:, :, 0, :, None] * PEER_NKEYS + si[:, :, 1, None, :]).reshape(-1, PEER_HEADS, PEER_TOPK * PEER_TOPK)
        cv, ci = lax.top_k(cand, PEER_TOPK)
        eid = jnp.take_along_axis(cand_id, ci, axis=-1)
        g = jax.nn.softmax(cv, axis=-1)
        ue = u[eid]
        ve = v[eid]
        a = jax.nn.gelu(jnp.einsum('td,thkd->thk', xc, ue)).astype(jnp.float32)
        return jnp.einsum('thk,thkd->td', (g * a).astype(ve.dtype), ve)

    return lax.map(chunk, hc).reshape(B, S, D)


def setup_inputs(seed: int = 0) -> dict:
    key = jax.random.key(seed)
    ks = jax.random.split(key, 24)
    D = D_MODEL

    def nrm(k, shape, s):
        return jax.random.normal(k, shape, jnp.float32) * s

    return {
        'x': nrm(ks[0], (BATCH, SEQ, D), 1.0),
        'c': nrm(ks[1], (BATCH, D), 1.0),
        'ada_w': nrm(ks[2], (DEPTH, D, 6 * D), 0.5 * D ** -0.5),
        'ada_b': nrm(ks[3], (DEPTH, 6 * D), 0.01),
        'norm_g': 1.0 + nrm(ks[4], (DEPTH, 2, D), 0.05),
        'even_w_in': nrm(ks[5], (N_EVEN, D, EVEN_IN), D ** -0.5),
        'even_b_f': jax.random.uniform(ks[6], (N_EVEN, FOX_HEADS), jnp.float32, 1.0, 6.0),
        'even_conv_w': nrm(ks[7], (N_EVEN, CONV_K, CONV_CH), CONV_K ** -0.5),
        'even_q_g': 1.0 + nrm(ks[8], (N_EVEN, HEAD_DIM), 0.05),
        'even_k_g': 1.0 + nrm(ks[9], (N_EVEN, HEAD_DIM), 0.05),
        'even_w_out': nrm(ks[10], (N_EVEN, CONV_CH + FOX_WIDTH, D), (CONV_CH + FOX_WIDTH) ** -0.5),
        'odd_w_in': nrm(ks[11], (N_ODD, D, ODD_IN), D ** -0.5),
        'odd_b_gate': nrm(ks[12], (N_ODD, 3 * NSA_HEADS), 0.01),
        'odd_q_g': 1.0 + nrm(ks[13], (N_ODD, HEAD_DIM), 0.05),
        'odd_k_g': 1.0 + nrm(ks[14], (N_ODD, 3, HEAD_DIM), 0.05),
        'odd_cmp_pos': nrm(ks[15], (N_ODD, 2, CMP_LEN, HEAD_DIM), 0.1),
        'odd_cmp_w1': nrm(ks[16], (N_ODD, 2, CMP_LEN * HEAD_DIM, HEAD_DIM), (CMP_LEN * HEAD_DIM) ** -0.5),
        'odd_cmp_w2': nrm(ks[17], (N_ODD, 2, HEAD_DIM, HEAD_DIM), HEAD_DIM ** -0.5),
        'odd_w_out': nrm(ks[18], (N_ODD, NSA_WIDTH, D), NSA_WIDTH ** -0.5),
        'rel_table': nrm(ks[19], (REL_BUCKETS, NSA_HEADS), 0.5),
        'peer_w_q': nrm(ks[20], (DEPTH, D, PEER_HEADS * PEER_QDIM), D ** -0.5),
        'peer_keys': nrm(ks[21], (DEPTH, 2, PEER_NKEYS, PEER_QDIM // 2), (PEER_QDIM // 2) ** -0.5),
        'peer_u': nrm(ks[22], (DEPTH, PEER_EXPERTS, D), D ** -0.5),
        'peer_v': nrm(ks[23], (DEPTH, PEER_EXPERTS, D), PEER_HEADS ** -0.5),
    }


def reference(x, c, ada_w, ada_b, norm_g, even_w_in, even_b_f, even_conv_w, even_q_g, even_k_g,
              even_w_out, odd_w_in, odd_b_gate, odd_q_g, odd_k_g, odd_cmp_pos, odd_cmp_w1, odd_cmp_w2,
              odd_w_out, rel_table, peer_w_q, peer_keys, peer_u, peer_v):
    mods = jnp.einsum('bd,lde->lbe', jax.nn.silu(c), ada_w) + ada_b[:, None, :]
    for layer in range(DEPTH):
        sh1, sc1, g1, sh2, sc2, g2 = jnp.split(mods[layer], 6, axis=-1)
        h = modulate(rms_norm(x, norm_g[layer, 0]), sh1, sc1)
        i = layer // 2
        if layer % 2 == 0:
            y = conv_fox_mixer(h, even_w_in[i], even_b_f[i], even_conv_w[i], even_q_g[i], even_k_g[i],
                               even_w_out[i])
        else:
            y = nsa_mixer(h, odd_w_in[i], odd_b_gate[i], odd_q_g[i], odd_k_g[i], odd_cmp_pos[i],
                          odd_cmp_w1[i], odd_cmp_w2[i], rel_table, odd_w_out[i])
        x = x + g1[:, None, :] * y
        h = modulate(rms_norm(x, norm_g[layer, 1]), sh2, sc2)
        x = x + g2[:, None, :] * peer_ffn(h, peer_w_q[layer], peer_keys[layer], peer_u[layer], peer_v[layer])
    return x
```

```python
import functools
import math

import jax
import jax.numpy as jnp
from jax import lax
from jax.experimental import pallas as pl
from jax.experimental.pallas import tpu as pltpu

D_MODEL = 1024
DEPTH = 4
HEAD_DIM = 64
CONV_CH = 512
CONV_K = 3
FOX_HEADS = 8
FOX_WIDTH = FOX_HEADS * HEAD_DIM
EVEN_IN = 3 * CONV_CH + 3 * FOX_WIDTH + FOX_HEADS
NSA_GROUPS = 4
NSA_HPG = 4
NSA_HEADS = NSA_GROUPS * NSA_HPG
NSA_WIDTH = NSA_HEADS * HEAD_DIM
NSA_KV = NSA_GROUPS * HEAD_DIM
CMP_LEN = 32
CMP_STRIDE = 16
SLC_LEN = 64
SLC_TOPN = 16
WINDOW = 512
ODD_IN = NSA_WIDTH + 6 * NSA_KV + 3 * NSA_HEADS
REL_BUCKETS = 32
REL_MAX_DIST = 128
PEER_HEADS = 8
PEER_NKEYS = 128
PEER_EXPERTS = PEER_NKEYS * PEER_NKEYS
PEER_QDIM = 128
PEER_TOPK = 16
EPS = 1e-6
NEG_INF = -1e30
FORCE_SCORE = 1e6

LANES = 128
F32 = jnp.float32
BF16 = jnp.bfloat16
HIGHEST = lax.Precision.HIGHEST
VMEM_LIMIT = 56 * 1024 * 1024

EVEN_IN_PAD = 3200
ODD_IN_PAD = 2688
ROW_TILE = 512
ATT_TILE = 128
PEER_TN = 512
PEER_EC = 1024
SCALE = HEAD_DIM ** -0.5


def _cparams(sem):
    return pltpu.CompilerParams(dimension_semantics=sem, vmem_limit_bytes=VMEM_LIMIT)


def _dot(a, b, precision=None):
    return jnp.dot(a, b, precision=precision, preferred_element_type=F32)


def _dot_nt(a, b, precision=None):
    return lax.dot_general(a, b, (((1,), (1,)), ((), ())), precision=precision,
                           preferred_element_type=F32)


def _rms(x, g):
    return x * lax.rsqrt(jnp.mean(x * x, axis=-1, keepdims=True) + EPS) * g


def _ada_kernel(c_ref, w_ref, b_ref, o_ref):
    c = c_ref[...]
    sc = c * jax.nn.sigmoid(c)
    o_ref[0] = _dot(sc, w_ref[0], HIGHEST) + b_ref[0]


def _ada(c, ada_w, ada_b):
    B, D = c.shape
    E = ada_w.shape[2]
    tn = 2048
    return pl.pallas_call(
        _ada_kernel,
        out_shape=jax.ShapeDtypeStruct((DEPTH, B, E), F32),
        grid=(DEPTH, E // tn),
        in_specs=[pl.BlockSpec((B, D), lambda l, j: (0, 0)),
                  pl.BlockSpec((1, D, tn), lambda l, j: (l, 0, j)),
                  pl.BlockSpec((1, 1, tn), lambda l, j: (l, 0, j))],
        out_specs=pl.BlockSpec((1, B, tn), lambda l, j: (l, 0, j)),
        compiler_params=_cparams(("arbitrary", "arbitrary")),
        name="ada_mod",
    )(c, ada_w, ada_b.reshape(DEPTH, 1, E))


def _norm_proj_kernel(x_ref, sh_ref, sc_ref, ng_ref, w_ref, y_ref):
    h = _rms(x_ref[0], ng_ref[...]) * (1.0 + sc_ref[0]) + sh_ref[0]
    y_ref[0] = _dot(h.astype(BF16), w_ref[...])


def _norm_proj(x, sh, sc, ng, w):
    B, S, D = x.shape
    N = w.shape[1]
    return pl.pallas_call(
        _norm_proj_kernel,
        out_shape=jax.ShapeDtypeStruct((B, S, N), F32),
        grid=(B, S // ROW_TILE),
        in_specs=[pl.BlockSpec((1, ROW_TILE, D), lambda b, i: (b, i, 0)),
                  pl.BlockSpec((1, 1, D), lambda b, i: (b, 0, 0)),
                  pl.BlockSpec((1, 1, D), lambda b, i: (b, 0, 0)),
                  pl.BlockSpec((1, D), lambda b, i: (0, 0)),
                  pl.BlockSpec((D, N), lambda b, i: (0, 0))],
        out_specs=pl.BlockSpec((1, ROW_TILE, N), lambda b, i: (b, i, 0)),
        compiler_params=_cparams(("arbitrary", "arbitrary")),
        name="norm_proj",
    )(x, sh, sc, ng, w)


def _out_proj_kernel(x_ref, a_ref, b_ref, wa_ref, wb_ref, g_ref, o_ref):
    y = _dot(a_ref[0].astype(BF16), wa_ref[...]) + _dot(b_ref[0].astype(BF16), wb_ref[...])
    o_ref[0] = x_ref[0] + g_ref[0] * y


def _out_proj(x, a, a_blk, b, b_blk, w, gate):
    B, S, D = x.shape
    half = w.shape[0] // 2
    wa, wb = w[:half], w[half:]
    return pl.pallas_call(
        _out_proj_kernel,
        out_shape=jax.ShapeDtypeStruct((B, S, D), F32),
        grid=(B, S // ROW_TILE),
        in_specs=[pl.BlockSpec((1, ROW_TILE, D), lambda b, i: (b, i, 0)),
                  pl.BlockSpec((1, ROW_TILE, half), lambda b, i: (b, i, a_blk)),
                  pl.BlockSpec((1, ROW_TILE, half), lambda b, i: (b, i, b_blk)),
                  pl.BlockSpec((half, D), lambda b, i: (0, 0)),
                  pl.BlockSpec((half, D), lambda b, i: (0, 0)),
                  pl.BlockSpec((1, 1, D), lambda b, i: (b, 0, 0))],
        out_specs=pl.BlockSpec((1, ROW_TILE, D), lambda b, i: (b, i, 0)),
        compiler_params=_cparams(("arbitrary", "arbitrary")),
        name="out_proj",
    )(x, a, b, wa, wb, gate)


def _conv_kernel(cb_ref, cc_ref, ch_ref, w_ref, o_ref):
    u = cc_ref[0] * ch_ref[0]
    row = lax.broadcasted_iota(jnp.int32, u.shape, 0)
    u1 = jnp.where(row >= 1, pltpu.roll(u, 1, axis=0), 0.0)
    u2 = jnp.where(row >= 2, pltpu.roll(u, 2, axis=0), 0.0)
    w = w_ref[...]
    o_ref[0] = cb_ref[0] * (w[0:1] * u2 + w[1:2] * u1 + w[2:3] * u)


def _short_conv(y, conv_w):
    B, S, _ = y.shape
    nblk = CONV_CH // LANES
    w = jnp.zeros((8, CONV_CH), F32).at[:CONV_K].set(conv_w)
    return pl.pallas_call(
        _conv_kernel,
        out_shape=jax.ShapeDtypeStruct((B, S, CONV_CH), F32),
        grid=(B, nblk),
        in_specs=[pl.BlockSpec((1, S, LANES), lambda b, j: (b, 0, j)),
                  pl.BlockSpec((1, S, LANES), lambda b, j: (b, 0, nblk + j)),
                  pl.BlockSpec((1, S, LANES), lambda b, j: (b, 0, 2 * nblk + j)),
                  pl.BlockSpec((8, LANES), lambda b, j: (0, j))],
        out_specs=pl.BlockSpec((1, S, LANES), lambda b, j: (b, 0, j)),
        compiler_params=_cparams(("arbitrary", "arbitrary")),
        name="short_conv",
    )(y, y, y, w)


def _cumsum_rows(x):
    n = x.shape[0]
    row = lax.broadcasted_iota(jnp.int32, x.shape, 0)
    sh = 1
    while sh < n:
        x = x + jnp.where(row >= sh, pltpu.roll(x, sh, axis=0), 0.0)
        sh *= 2
    return x


def _fox_kernel(q_ref, k_ref, v_ref, f_ref, bf_ref, qg_ref, kg_ref, o_ref,
                kn_scr, vb_scr, cum_scr, cumt_scr):
    qi = pl.program_id(1)
    S = k_ref.shape[1]
    T = ATT_TILE
    nk = S // T

    @pl.when(qi == 0)
    def _prep():
        for h in range(FOX_HEADS):
            cols = slice(h * HEAD_DIM, (h + 1) * HEAD_DIM)
            kn_scr[:, cols] = _rms(k_ref[0, :, cols], kg_ref[...]).astype(BF16)
        vb_scr[...] = v_ref[0].astype(BF16)
        logf = jax.nn.log_sigmoid(f_ref[0] + bf_ref[...])
        cum = _cumsum_rows(logf)
        cum_scr[...] = cum
        for kt in range(nk):
            cumt_scr[kt] = cum[kt * T:(kt + 1) * T, :].T[0:8, :]

    row = qi * T + lax.broadcasted_iota(jnp.int32, (T, T), 0)
    col0 = lax.broadcasted_iota(jnp.int32, (T, T), 1)
    cq = cum_scr[pl.ds(pl.multiple_of(qi * T, T), T), :]
    for h in range(FOX_HEADS):
        cols = slice(h * HEAD_DIM, (h + 1) * HEAD_DIM)
        qh = (_rms(q_ref[0, :, cols], qg_ref[...]) * SCALE).astype(BF16)
        ct = cq[:, h:h + 1]

        def body(kt, carry, cols=cols, qh=qh, ct=ct, h=h):
            m, l, acc = carry
            ks = pl.ds(pl.multiple_of(kt * T, T), T)
            s = _dot_nt(qh, kn_scr[ks, cols])
            s = s + (ct - cumt_scr[kt][h:h + 1, :])
            mask = row >= kt * T + col0
            s = jnp.where(mask, s, NEG_INF)
            m_new = jnp.maximum(m, jnp.max(s, axis=-1, keepdims=True))
            alpha = jnp.exp(m - m_new)
            p = jnp.where(mask, jnp.exp(s - m_new), 0.0)
            l = alpha * l + jnp.sum(p, axis=-1, keepdims=True)
            acc = alpha * acc + _dot(p.astype(BF16), vb_scr[ks, cols])
            return m_new, l, acc

        init = (jnp.full((T, 1), NEG_INF, F32), jnp.zeros((T, 1), F32), jnp.zeros((T, HEAD_DIM), F32))
        m, l, acc = lax.fori_loop(0, qi + 1, body, init)
        o_ref[0, :, cols] = acc / l


def _fox_attention(y, b_f, q_g, k_g):
    B, S, _ = y.shape
    T = ATT_TILE
    qblk = 3 * CONV_CH // FOX_WIDTH
    fblk = (3 * CONV_CH + 3 * FOX_WIDTH) // LANES
    bf = jnp.zeros((1, LANES), F32).at[0, :FOX_HEADS].set(b_f)
    return pl.pallas_call(
        _fox_kernel,
        out_shape=jax.ShapeDtypeStruct((B, S, FOX_WIDTH), F32),
        grid=(B, S // T),
        in_specs=[pl.BlockSpec((1, T, FOX_WIDTH), lambda b, i: (b, i, qblk)),
                  pl.BlockSpec((1, S, FOX_WIDTH), lambda b, i: (b, 0, qblk + 1)),
                  pl.BlockSpec((1, S, FOX_WIDTH), lambda b, i: (b, 0, qblk + 2)),
                  pl.BlockSpec((1, S, LANES), lambda b, i: (b, 0, fblk)),
                  pl.BlockSpec((1, LANES), lambda b, i: (0, 0)),
                  pl.BlockSpec((1, HEAD_DIM), lambda b, i: (0, 0)),
                  pl.BlockSpec((1, HEAD_DIM), lambda b, i: (0, 0))],
        out_specs=pl.BlockSpec((1, T, FOX_WIDTH), lambda b, i: (b, i, 0)),
        scratch_shapes=[pltpu.VMEM((S, FOX_WIDTH), BF16),
                        pltpu.VMEM((S, FOX_WIDTH), BF16),
                        pltpu.VMEM((S, LANES), F32),
                        pltpu.VMEM((S // T, 8, T), F32)],
        compiler_params=_cparams(("arbitrary", "arbitrary")),
        name="fox_attention",
    )(y, y, y, y, bf, q_g.reshape(1, HEAD_DIM), k_g.reshape(1, HEAD_DIM))


def _extract_top(work, iota, k):
    nrows = work.shape[0]
    vals = []
    for _ in range(k):
        m = jnp.max(work, axis=0, keepdims=True)
        idx = jnp.min(jnp.where(work == m, iota, nrows), axis=0, keepdims=True)
        work = jnp.where(iota == idx, -jnp.inf, work)
        vals.append(m)
    return vals, work


def _peer_kernel(x_ref, sh_ref, sc_ref, gt_ref, ng_ref, wq_ref, keys_ref, u_ref, vt_ref, o_ref,
                 ht_scr, qt_scr, sp_scr, s0_scr, e0_scr, s1_scr, e1_scr, tau_scr, a_scr, p_scr, acc_scr):
    c = pl.program_id(2)
    nc = pl.num_programs(2)
    tn = x_ref.shape[1]
    nlt = tn // LANES
    half = PEER_QDIM // 2

    @pl.when(c == 0)
    def _prep():
        h = _rms(x_ref[0], ng_ref[...]) * (1.0 + sc_ref[0]) + sh_ref[0]
        q = _dot(h, wq_ref[...], HIGHEST)
        qt_scr[...] = q.T
        ht_scr[...] = h.T.astype(BF16)
        acc_scr[...] = jnp.zeros_like(acc_scr)
        iota_k = lax.broadcasted_iota(jnp.int32, (PEER_NKEYS, LANES), 0)
        iota_c = lax.broadcasted_iota(jnp.int32, (PEER_TOPK * PEER_TOPK, LANES), 0)

        def head_body(hd, carry):
            for p in range(2):
                qs = qt_scr[pl.ds(pl.multiple_of((hd * 2 + p) * half, half), half), :]
                sp_scr[p] = _dot(keys_ref[p], qs, HIGHEST)
            for lt in range(nlt):
                lanes = slice(lt * LANES, (lt + 1) * LANES)
                s0 = sp_scr[0, :, lanes]
                s1 = sp_scr[1, :, lanes]
                sv0, w0 = _extract_top(s0, iota_k, PEER_TOPK)
                sv1, w1 = _extract_top(s1, iota_k, PEER_TOPK)
                s0m = jnp.where(w0 == -jnp.inf, s0, -jnp.inf)
                s1m = jnp.where(w1 == -jnp.inf, s1, -jnp.inf)
                sv1_all = jnp.concatenate(sv1, axis=0)
                cand = jnp.concatenate([v0 + sv1_all for v0 in sv0], axis=0)
                cv, _ = _extract_top(cand, iota_c, PEER_TOPK)
                z = cv[0] - cv[0]
                for r in range(PEER_TOPK):
                    z = z + jnp.exp(cv[r] - cv[0])
                s0_scr[hd, :, lanes] = s0m
                s1_scr[hd, :, lanes] = s1m
                e0_scr[hd, :, lanes] = jnp.exp(s0m - sv0[0])
                e1_scr[hd, :, lanes] = jnp.exp(s1m - sv1[0]) / z
                tau_scr[hd, :, lanes] = jnp.broadcast_to(cv[PEER_TOPK - 1], (8, LANES))
            return carry

        lax.fori_loop(0, PEER_HEADS, head_body, 0)

    a_scr[...] = _dot(u_ref[...], ht_scr[...])
    nsub = PEER_EC // PEER_NKEYS
    i0 = pl.ds(pl.multiple_of(c * nsub, nsub), nsub)
    for ii in range(nsub):
        rows = slice(ii * PEER_NKEYS, (ii + 1) * PEER_NKEYS)
        for lt in range(nlt):
            lanes = slice(lt * LANES, (lt + 1) * LANES)
            w = jnp.zeros((PEER_NKEYS, LANES), F32)
            for hd in range(PEER_HEADS):
                s = s1_scr[hd, :, lanes] + s0_scr[hd, i0, lanes][ii:ii + 1]
                e = e1_scr[hd, :, lanes] * e0_scr[hd, i0, lanes][ii:ii + 1]
                w = w + jnp.where(s >= tau_scr[hd, 0:1, lanes], e, 0.0)
            act = jax.nn.gelu(a_scr[rows, lanes], approximate=True)
            p_scr[rows, lanes] = (w * act).astype(BF16)
    acc_scr[...] += _dot(vt_ref[...], p_scr[...])

    @pl.when(c == nc - 1)
    def _fin():
        o_ref[0] = x_ref[0] + gt_ref[0] * acc_scr[...].T


def _peer(x, sh, sc, gate, ng, w_q, keys, u_b, vt_b):
    B, S, D = x.shape
    tn = PEER_TN
    nc = PEER_EXPERTS // PEER_EC
    tab = pltpu.VMEM((PEER_HEADS, PEER_NKEYS, tn), F32)
    return pl.pallas_call(
        _peer_kernel,
        out_shape=jax.ShapeDtypeStruct((B, S, D), F32),
        grid=(B, S // tn, nc),
        in_specs=[pl.BlockSpec((1, tn, D), lambda b, i, c: (b, i, 0)),
                  pl.BlockSpec((1, 1, D), lambda b, i, c: (b, 0, 0)),
                  pl.BlockSpec((1, 1, D), lambda b, i, c: (b, 0, 0)),
                  pl.BlockSpec((1, 1, D), lambda b, i, c: (b, 0, 0)),
                  pl.BlockSpec((1, D), lambda b, i, c: (0, 0)),
                  pl.BlockSpec((D, PEER_HEADS * PEER_QDIM), lambda b, i, c: (0, 0)),
                  pl.BlockSpec((2, PEER_NKEYS, PEER_QDIM // 2), lambda b, i, c: (0, 0, 0)),
                  pl.BlockSpec((PEER_EC, D), lambda b, i, c: (c, 0)),
                  pl.BlockSpec((D, PEER_EC), lambda b, i, c: (0, c))],
        out_specs=pl.BlockSpec((1, tn, D), lambda b, i, c: (b, i, 0)),
        scratch_shapes=[pltpu.VMEM((D, tn), BF16),
                        pltpu.VMEM((PEER_HEADS * PEER_QDIM, tn), F32),
                        pltpu.VMEM((2, PEER_NKEYS, tn), F32),
                        tab, tab, tab, tab,
                        pltpu.VMEM((PEER_HEADS, 8, tn), F32),
                        pltpu.VMEM((PEER_EC, tn), F32),
                        pltpu.VMEM((PEER_EC, tn), BF16),
                        pltpu.VMEM((D, tn), F32)],
        compiler_params=_cparams(("arbitrary", "arbitrary", "arbitrary")),
        name="peer",
    )(x, sh, sc, gate, ng, w_q, keys, u_b, vt_b)


def _rel_bucket(dist):
    n = jnp.maximum(dist, 0)
    max_exact = REL_BUCKETS // 2
    nf = jnp.maximum(n, 1).astype(F32)
    large = max_exact + (jnp.log(nf / max_exact) / math.log(REL_MAX_DIST / max_exact)
                         * (REL_BUCKETS - max_exact)).astype(jnp.int32)
    large = jnp.minimum(large, REL_BUCKETS - 1)
    return jnp.where(n < max_exact, n, large)


def _bias_lookup(bucket, tab_ref, h):
    bias = jnp.zeros(bucket.shape, F32)
    for b in range(REL_BUCKETS):
        bias = jnp.where(bucket == b, tab_ref[b, h], bias)
    return bias


def _relbias_kernel(tab_ref, o_ref):
    h = pl.program_id(0)
    T = o_ref.shape[2]
    a = lax.broadcasted_iota(jnp.int32, (T, T), 0)
    b = lax.broadcasted_iota(jnp.int32, (T, T), 1)
    for k in range(2):
        o_ref[0, k] = _bias_lookup(_rel_bucket(a - b + k * T), tab_ref, h)
    o_ref[0, 2] = jnp.full((T, T), tab_ref[REL_BUCKETS - 1, h], F32)


def _relbias(rel_table):
    T = ATT_TILE
    assert 2 * T - (T - 1) >= REL_MAX_DIST
    return pl.pallas_call(
        _relbias_kernel,
        out_shape=jax.ShapeDtypeStruct((NSA_HEADS, 3, T, T), F32),
        grid=(NSA_HEADS,),
        in_specs=[pl.BlockSpec(memory_space=pltpu.SMEM)],
        out_specs=pl.BlockSpec((1, 3, T, T), lambda h: (h, 0, 0, 0)),
        compiler_params=_cparams(("arbitrary",)),
        name="rel_bias_tiles",
    )(rel_table)


def _cmp_kernel(x_ref, pos_ref, w1_ref, w2_ref, kg_ref, o_ref):
    j = pl.program_id(1)
    half = CMP_STRIDE * HEAD_DIM
    x = x_ref[0, 0]
    pos = pos_ref[0]
    pa = _dot((x + pos[:, :half]).astype(BF16), w1_ref[0, :half, :])
    pb = _dot((x + pos[:, half:]).astype(BF16), w1_ref[0, half:, :])
    pre = pa + pltpu.roll(pb, pb.shape[0] - 1, axis=0)
    out = _dot(jax.nn.gelu(pre, approximate=True).astype(BF16), w2_ref[0])
    o_ref[0, 0] = jnp.where(j < NSA_GROUPS, _rms(out, kg_ref[...]), out)


def _nsa_compress(xr, pos, w1, w2, kg0):
    B, _, M, W = xr.shape
    G = NSA_GROUPS
    return pl.pallas_call(
        _cmp_kernel,
        out_shape=jax.ShapeDtypeStruct((B, 2 * G, M, HEAD_DIM), F32),
        grid=(B, 2 * G),
        in_specs=[pl.BlockSpec((1, 1, M, W), lambda b, j: (b, j, 0, 0)),
                  pl.BlockSpec((1, 1, 2 * W), lambda b, j: (j // G, 0, 0)),
                  pl.BlockSpec((1, 2 * W, HEAD_DIM), lambda b, j: (j // G, 0, 0)),
                  pl.BlockSpec((1, HEAD_DIM, HEAD_DIM), lambda b, j: (j // G, 0, 0)),
                  pl.BlockSpec((1, HEAD_DIM), lambda b, j: (0, 0))],
        out_specs=pl.BlockSpec((1, 1, M, HEAD_DIM), lambda b, j: (b, j, 0, 0)),
        compiler_params=_cparams(("arbitrary", "arbitrary")),
        name="nsa_compress",
    )(xr, pos.reshape(2, 1, 2 * W), w1.astype(BF16), w2.astype(BF16), kg0.reshape(1, HEAD_DIM))


def _cmp_attn_kernel(q_ref, kv_ref, ov_ref, tab_ref, qg_ref, oc_ref, sel_ref, qn_ref):
    qi = pl.program_id(1)
    T = q_ref.shape[1]
    NC = kv_ref.shape[2]
    NS = ov_ref.shape[1]
    G, J = NSA_GROUPS, NSA_HPG
    t = qi * T + lax.broadcasted_iota(jnp.int32, (T, NC), 0)
    n = lax.broadcasted_iota(jnp.int32, (T, NC), 1)
    d = t - (n * CMP_STRIDE + CMP_LEN - 1)
    vis = d >= 0
    visf = vis.astype(F32)
    bucket = _rel_bucket(d)
    tt = qi * T + lax.broadcasted_iota(jnp.int32, (T, NS), 0)
    mm = lax.broadcasted_iota(jnp.int32, (T, NS), 1)
    cur = tt // SLC_LEN
    forced = (mm == 0) | (mm == cur) | (mm == cur - 1)
    past = mm * SLC_LEN <= tt
    for g in range(G):
        kc = kv_ref[0, g].astype(BF16)
        vc = kv_ref[0, G + g].astype(BF16)
        psum = jnp.zeros((T, NC), F32)
        for j in range(J):
            h = g * J + j
            cols = slice(h * HEAD_DIM, (h + 1) * HEAD_DIM)
            qh = (_rms(q_ref[0, :, cols], qg_ref[...]) * SCALE).astype(BF16)
            qn_ref[0, :, cols] = qh
            s = _dot_nt(qh, kc) + _bias_lookup(bucket, tab_ref, h)
            s = jnp.where(vis, s, NEG_INF)
            e = jnp.exp(s - jnp.max(s, axis=-1, keepdims=True))
            p = e / jnp.sum(e, axis=-1, keepdims=True) * visf
            oc_ref[0, :, cols] = _dot(p.astype(BF16), vc)
            psum = psum + p
        imp = _dot(psum, ov_ref[...], HIGHEST)
        imp = jnp.where(forced, FORCE_SCORE, jnp.where(past, imp, -1.0))
        rank = jnp.zeros((T, NS), jnp.int32)
        for mp in range(NS):
            col = imp[:, mp:mp + 1]
            ahead = (col > imp) | ((col == imp) & (mm > mp))
            rank = rank + ahead.astype(jnp.int32)
        sel = (rank < SLC_TOPN) & (imp >= 0.0)
        sel_ref[0, :, g * NS:(g + 1) * NS] = sel.astype(F32)


def _nsa_cmp_attn(y, kcvc, overlap, rel_table, q_g):
    B, S, _ = y.shape
    T = ATT_TILE
    NC = kcvc.shape[2]
    NS = overlap.shape[1]
    return pl.pallas_call(
        _cmp_attn_kernel,
        out_shape=(jax.ShapeDtypeStruct((B, S, NSA_WIDTH), F32),
                   jax.ShapeDtypeStruct((B, S, NSA_GROUPS * NS), F32),
                   jax.ShapeDtypeStruct((B, S, NSA_WIDTH), BF16)),
        grid=(B, S // T),
        in_specs=[pl.BlockSpec((1, T, NSA_WIDTH), lambda b, i: (b, i, 0)),
                  pl.BlockSpec((1, 2 * NSA_GROUPS, NC, HEAD_DIM), lambda b, i: (b, 0, 0, 0)),
                  pl.BlockSpec((NC, NS), lambda b, i: (0, 0)),
                  pl.BlockSpec(memory_space=pltpu.SMEM),
                  pl.BlockSpec((1, HEAD_DIM), lambda b, i: (0, 0))],
        out_specs=(pl.BlockSpec((1, T, NSA_WIDTH), lambda b, i: (b, i, 0)),
                   pl.BlockSpec((1, T, NSA_GROUPS * NS), lambda b, i: (b, i, 0)),
                   pl.BlockSpec((1, T, NSA_WIDTH), lambda b, i: (b, i, 0))),
        compiler_params=_cparams(("arbitrary", "arbitrary")),
        name="nsa_cmp_attn",
    )(y, kcvc, overlap, rel_table, q_g.reshape(1, HEAD_DIM))


def _flash_step(carry, s, mask, v):
    m, l, acc = carry
    s = jnp.where(mask, s, NEG_INF)
    m_new = jnp.maximum(m, jnp.max(s, axis=-1, keepdims=True))
    alpha = jnp.exp(m - m_new)
    p = jnp.where(mask, jnp.exp(s - m_new), 0.0)
    l = alpha * l + jnp.sum(p, axis=-1, keepdims=True)
    acc = alpha * acc + _dot(p.astype(BF16), v)
    return m_new, l, acc


def _sel_win_kernel(qn_ref, ks_ref, kw_ref, gl_ref, sel_ref, oc_ref, bias_ref, e_ref, kg1_ref, kg2_ref,
                    bg_ref, o_ref, ksn_scr, vsb_scr, kwn_scr, vwb_scr, mk_scr):
    qi = pl.program_id(1)
    T = qn_ref.shape[1]
    S = ks_ref.shape[1]
    nk = S // T
    G, J = NSA_GROUPS, NSA_HPG
    NS = e_ref.shape[0]
    KV = NSA_KV

    @pl.when(qi == 0)
    def _prep():
        for g in range(G):
            cols = slice(g * HEAD_DIM, (g + 1) * HEAD_DIM)
            vcols = slice(KV + g * HEAD_DIM, KV + (g + 1) * HEAD_DIM)
            ksn_scr[:, cols] = _rms(ks_ref[0, :, cols], kg1_ref[...]).astype(BF16)
            kwn_scr[:, cols] = _rms(kw_ref[0, :, cols], kg2_ref[...]).astype(BF16)
            vsb_scr[:, cols] = ks_ref[0, :, vcols].astype(BF16)
            vwb_scr[:, cols] = kw_ref[0, :, vcols].astype(BF16)

    gates = jax.nn.sigmoid(gl_ref[0] + bg_ref[...])
    row = qi * T + lax.broadcasted_iota(jnp.int32, (T, T), 0)
    col0 = lax.broadcasted_iota(jnp.int32, (T, T), 1)
    init = (jnp.full((T, 1), NEG_INF, F32), jnp.zeros((T, 1), F32), jnp.zeros((T, HEAD_DIM), F32))
    for g in range(G):
        gcols = slice(g * HEAD_DIM, (g + 1) * HEAD_DIM)
        selg = sel_ref[0, :, g * NS:(g + 1) * NS].astype(BF16)
        for kt in range(nk):
            mk_scr[kt] = _dot(selg, e_ref[:, kt * T:(kt + 1) * T])
        for j in range(J):
            h = g * J + j
            cols = slice(h * HEAD_DIM, (h + 1) * HEAD_DIM)
            qh = qn_ref[0, :, cols]

            def sel_body(kt, carry, qh=qh, gcols=gcols, h=h):
                ks = pl.ds(pl.multiple_of(kt * T, T), T)
                s = _dot_nt(qh, ksn_scr[ks, gcols]) + bias_ref[h, jnp.minimum(qi - kt, 2)]
                mask = (mk_scr[kt] > 0.5) & (row >= kt * T + col0)
                return _flash_step(carry, s, mask, vsb_scr[ks, gcols])

            def win_body(kt, carry, qh=qh, gcols=gcols, h=h):
                ks = pl.ds(pl.multiple_of(kt * T, T), T)
                s = _dot_nt(qh, kwn_scr[ks, gcols]) + bias_ref[h, jnp.minimum(qi - kt, 2)]
                dist = row - (kt * T + col0)
                mask = (dist >= 0) & (dist < WINDOW)
                return _flash_step(carry, s, mask, vwb_scr[ks, gcols])

            _, ls, accs = lax.fori_loop(0, qi + 1, sel_body, init)
            _, lw, accw = lax.fori_loop(jnp.maximum(qi - WINDOW // T, 0), qi + 1, win_body, init)
            o_ref[0, :, cols] = (gates[:, 3 * h:3 * h + 1] * oc_ref[0, :, cols]
                                 + gates[:, 3 * h + 1:3 * h + 2] * (accs / ls)
                                 + gates[:, 3 * h + 2:3 * h + 3] * (accw / lw))


def _nsa_sel_win(qn, y, sel, oc, bias_tiles, expand, k_g, b_gate):
    B, S, _ = y.shape
    T = ATT_TILE
    ksblk = (NSA_WIDTH + 2 * NSA_KV) // (2 * NSA_KV)
    glblk = (NSA_WIDTH + 6 * NSA_KV) // LANES
    bg = jnp.zeros((1, LANES), F32).at[0, :3 * NSA_HEADS].set(b_gate)
    kvs = pltpu.VMEM((S, NSA_KV), BF16)
    return pl.pallas_call(
        _sel_win_kernel,
        out_shape=jax.ShapeDtypeStruct((B, S, NSA_WIDTH), F32),
        grid=(B, S // T),
        in_specs=[pl.BlockSpec((1, T, NSA_WIDTH), lambda b, i: (b, i, 0)),
                  pl.BlockSpec((1, S, 2 * NSA_KV), lambda b, i: (b, 0, ksblk)),
                  pl.BlockSpec((1, S, 2 * NSA_KV), lambda b, i: (b, 0, ksblk + 1)),
                  pl.BlockSpec((1, T, LANES), lambda b, i: (b, i, glblk)),
                  pl.BlockSpec((1, T, sel.shape[2]), lambda b, i: (b, i, 0)),
                  pl.BlockSpec((1, T, NSA_WIDTH), lambda b, i: (b, i, 0)),
                  pl.BlockSpec(bias_tiles.shape, lambda b, i: (0, 0, 0, 0)),
                  pl.BlockSpec(expand.shape, lambda b, i: (0, 0)),
                  pl.BlockSpec((1, HEAD_DIM), lambda b, i: (0, 0)),
                  pl.BlockSpec((1, HEAD_DIM), lambda b, i: (0, 0)),
                  pl.BlockSpec((1, LANES), lambda b, i: (0, 0))],
        out_specs=pl.BlockSpec((1, T, NSA_WIDTH), lambda b, i: (b, i, 0)),
        scratch_shapes=[kvs, kvs, kvs, kvs, pltpu.VMEM((S // T, T, T), F32)],
        compiler_params=_cparams(("arbitrary", "arbitrary")),
        name="nsa_sel_win",
    )(qn, y, y, y, sel, oc, bias_tiles, expand, k_g[1].reshape(1, HEAD_DIM), k_g[2].reshape(1, HEAD_DIM), bg)


def _odd_mixer(x, sh, sc, gate, ng, w_in, b_gate, q_g, k_g, cmp_pos, cmp_w1, cmp_w2, rel_table, bias_tiles,
               w_out):
    B, S, _ = x.shape
    G = NSA_GROUPS
    w = jnp.zeros((D_MODEL, ODD_IN_PAD), BF16).at[:, :ODD_IN].set(w_in.astype(BF16))
    y = _norm_proj(x, sh, sc, ng, w)
    xr = y[:, :, NSA_WIDTH:NSA_WIDTH + 2 * NSA_KV].reshape(B, S, 2, G, HEAD_DIM)
    xr = xr.transpose(0, 2, 3, 1, 4).reshape(B, 2 * G, S // CMP_STRIDE, CMP_STRIDE * HEAD_DIM)
    kcvc = _nsa_compress(xr, cmp_pos.reshape(2, CMP_LEN * HEAD_DIM), cmp_w1, cmp_w2, k_g[0])
    n_slc = S // SLC_LEN
    starts = jnp.arange(S // CMP_STRIDE) * CMP_STRIDE
    bstart = jnp.arange(n_slc) * SLC_LEN
    real = (jnp.arange(S // CMP_STRIDE) < (S - CMP_LEN) // CMP_STRIDE + 1)[:, None]
    overlap = (real & (starts[:, None] < bstart[None, :] + SLC_LEN)
               & (starts[:, None] + CMP_LEN > bstart[None, :])).astype(F32)
    expand = (jnp.arange(S)[None, :] // SLC_LEN == jnp.arange(n_slc)[:, None]).astype(BF16)
    oc, sel, qn = _nsa_cmp_attn(y, kcvc, overlap, rel_table, q_g)
    o = _nsa_sel_win(qn, y, sel, oc, bias_tiles, expand, k_g, b_gate)
    return _out_proj(x, o, 0, o, 1, w_out.astype(BF16), gate)


def _even_mixer(x, sh, sc, gate, ng, w_in, b_f, conv_w, q_g, k_g, w_out):
    w = jnp.zeros((D_MODEL, EVEN_IN_PAD), BF16).at[:, :EVEN_IN].set(w_in.astype(BF16))
    y = _norm_proj(x, sh, sc, ng, w)
    conv = _short_conv(y, conv_w)
    attn = _fox_attention(y, b_f, q_g, k_g)
    return _out_proj(x, conv, 0, attn, 0, w_out.astype(BF16), gate)


def kernel(x, c, ada_w, ada_b, norm_g, even_w_in, even_b_f, even_conv_w, even_q_g, even_k_g, even_w_out, odd_w_in, odd_b_gate, odd_q_g, odd_k_g, odd_cmp_pos, odd_cmp_w1, odd_cmp_w2, odd_w_out, rel_table, peer_w_q, peer_keys, peer_u, peer_v):
    B = x.shape[0]
    mods = _ada(c, ada_w, ada_b)
    bias_tiles = _relbias(rel_table)
    for layer in range(DEPTH):
        sh1, sc1, g1, sh2, sc2, g2 = [m.reshape(B, 1, D_MODEL) for m in jnp.split(mods[layer], 6, axis=-1)]
        i = layer // 2
        ng1 = norm_g[layer, 0].reshape(1, D_MODEL)
        ng2 = norm_g[layer, 1].reshape(1, D_MODEL)
        if layer % 2 == 0:
            x = _even_mixer(x, sh1, sc1, g1, ng1, even_w_in[i], even_b_f[i], even_conv_w[i], even_q_g[i],
                            even_k_g[i], even_w_out[i])
        else:
            x = _odd_mixer(x, sh1, sc1, g1, ng1, odd_w_in[i], odd_b_gate[i], odd_q_g[i], odd_k_g[i],
                           odd_cmp_pos[i], odd_cmp_w1[i], odd_cmp_w2[i], rel_table, bias_tiles, odd_w_out[i])
        x = _peer(x, sh2, sc2, g2, ng2, peer_w_q[layer], peer_keys[layer],
                  peer_u[layer].astype(BF16), peer_v[layer].T.astype(BF16))
    return x
```

```python
import functools
import math

import jax
import jax.numpy as jnp
from jax import lax
from jax.experimental import pallas as pl
from jax.experimental.pallas import tpu as pltpu

D_MODEL = 1024
DEPTH = 4
HEAD_DIM = 64
CONV_CH = 512
CONV_K = 3
FOX_HEADS = 8
FOX_WIDTH = FOX_HEADS * HEAD_DIM
EVEN_IN = 3 * CONV_CH + 3 * FOX_WIDTH + FOX_HEADS
NSA_GROUPS = 4
NSA_HPG = 4
NSA_HEADS = NSA_GROUPS * NSA_HPG
NSA_WIDTH = NSA_HEADS * HEAD_DIM
NSA_KV = NSA_GROUPS * HEAD_DIM
CMP_LEN = 32
CMP_STRIDE = 16
SLC_LEN = 64
SLC_TOPN = 16
WINDOW = 512
ODD_IN = NSA_WIDTH + 6 * NSA_KV + 3 * NSA_HEADS
REL_BUCKETS = 32
REL_MAX_DIST = 128
PEER_HEADS = 8
PEER_NKEYS = 128
PEER_EXPERTS = PEER_NKEYS * PEER_NKEYS
PEER_QDIM = 128
PEER_TOPK = 16
EPS = 1e-6
NEG_INF = -1e30
FORCE_SCORE = 1e6

LANES = 128
F32 = jnp.float32
BF16 = jnp.bfloat16
HIGHEST = lax.Precision.HIGHEST
VMEM_LIMIT = 56 * 1024 * 1024

EVEN_IN_PAD = 3200
ODD_IN_PAD = 2688
ROW_TILE = 512
ATT_TILE = 128
FOX_TILE = 256
PEER_TN = 512
PEER_EC = 1024
SCALE = HEAD_DIM ** -0.5


def _cparams(sem):
    return pltpu.CompilerParams(dimension_semantics=sem, vmem_limit_bytes=VMEM_LIMIT)


def _dot(a, b, precision=None):
    return jnp.dot(a, b, precision=precision, preferred_element_type=F32)


def _dot_nt(a, b, precision=None):
    return lax.dot_general(a, b, (((1,), (1,)), ((), ())), precision=precision,
                           preferred_element_type=F32)


def _rms(x, g):
    return x * lax.rsqrt(jnp.mean(x * x, axis=-1, keepdims=True) + EPS) * g


def _ada_kernel(c_ref, w_ref, b_ref, o_ref):
    c = c_ref[...]
    sc = c * jax.nn.sigmoid(c)
    o_ref[0] = _dot(sc, w_ref[0], HIGHEST) + b_ref[0]


def _ada(c, ada_w, ada_b):
    B, D = c.shape
    E = ada_w.shape[2]
    tn = 2048
    return pl.pallas_call(
        _ada_kernel,
        out_shape=jax.ShapeDtypeStruct((DEPTH, B, E), F32),
        grid=(DEPTH, E // tn),
        in_specs=[pl.BlockSpec((B, D), lambda l, j: (0, 0)),
                  pl.BlockSpec((1, D, tn), lambda l, j: (l, 0, j)),
                  pl.BlockSpec((1, 1, tn), lambda l, j: (l, 0, j))],
        out_specs=pl.BlockSpec((1, B, tn), lambda l, j: (l, 0, j)),
        compiler_params=_cparams(("arbitrary", "arbitrary")),
        name="ada_mod",
    )(c, ada_w, ada_b.reshape(DEPTH, 1, E))


def _norm_proj_kernel(x_ref, sh_ref, sc_ref, ng_ref, w_ref, y_ref):
    h = _rms(x_ref[0], ng_ref[...]) * (1.0 + sc_ref[0]) + sh_ref[0]
    y_ref[0] = _dot(h.astype(BF16), w_ref[...])


def _norm_proj(x, sh, sc, ng, w):
    B, S, D = x.shape
    N = w.shape[1]
    return pl.pallas_call(
        _norm_proj_kernel,
        out_shape=jax.ShapeDtypeStruct((B, S, N), F32),
        grid=(B, S // ROW_TILE),
        in_specs=[pl.BlockSpec((1, ROW_TILE, D), lambda b, i: (b, i, 0)),
                  pl.BlockSpec((1, 1, D), lambda b, i: (b, 0, 0)),
                  pl.BlockSpec((1, 1, D), lambda b, i: (b, 0, 0)),
                  pl.BlockSpec((1, D), lambda b, i: (0, 0)),
                  pl.BlockSpec((D, N), lambda b, i: (0, 0))],
        out_specs=pl.BlockSpec((1, ROW_TILE, N), lambda b, i: (b, i, 0)),
        compiler_params=_cparams(("arbitrary", "arbitrary")),
        name="norm_proj",
    )(x, sh, sc, ng, w)


def _out_proj_kernel(x_ref, a_ref, b_ref, wa_ref, wb_ref, g_ref, o_ref):
    y = _dot(a_ref[0].astype(BF16), wa_ref[...]) + _dot(b_ref[0].astype(BF16), wb_ref[...])
    o_ref[0] = x_ref[0] + g_ref[0] * y


def _out_proj(x, a, a_blk, b, b_blk, w, gate):
    B, S, D = x.shape
    half = w.shape[0] // 2
    wa, wb = w[:half], w[half:]
    return pl.pallas_call(
        _out_proj_kernel,
        out_shape=jax.ShapeDtypeStruct((B, S, D), F32),
        grid=(B, S // ROW_TILE),
        in_specs=[pl.BlockSpec((1, ROW_TILE, D), lambda b, i: (b, i, 0)),
                  pl.BlockSpec((1, ROW_TILE, half), lambda b, i: (b, i, a_blk)),
                  pl.BlockSpec((1, ROW_TILE, half), lambda b, i: (b, i, b_blk)),
                  pl.BlockSpec((half, D), lambda b, i: (0, 0)),
                  pl.BlockSpec((half, D), lambda b, i: (0, 0)),
                  pl.BlockSpec((1, 1, D), lambda b, i: (b, 0, 0))],
        out_specs=pl.BlockSpec((1, ROW_TILE, D), lambda b, i: (b, i, 0)),
        compiler_params=_cparams(("arbitrary", "arbitrary")),
        name="out_proj",
    )(x, a, b, wa, wb, gate)


def _conv_kernel(cb_ref, cc_ref, ch_ref, w_ref, o_ref):
    u = cc_ref[0] * ch_ref[0]
    row = lax.broadcasted_iota(jnp.int32, u.shape, 0)
    u1 = jnp.where(row >= 1, pltpu.roll(u, 1, axis=0), 0.0)
    u2 = jnp.where(row >= 2, pltpu.roll(u, 2, axis=0), 0.0)
    w = w_ref[...]
    o_ref[0] = cb_ref[0] * (w[0:1] * u2 + w[1:2] * u1 + w[2:3] * u)


def _short_conv(y, conv_w):
    B, S, _ = y.shape
    nblk = CONV_CH // LANES
    w = jnp.zeros((8, CONV_CH), F32).at[:CONV_K].set(conv_w)
    return pl.pallas_call(
        _conv_kernel,
        out_shape=jax.ShapeDtypeStruct((B, S, CONV_CH), F32),
        grid=(B, nblk),
        in_specs=[pl.BlockSpec((1, S, LANES), lambda b, j: (b, 0, j)),
                  pl.BlockSpec((1, S, LANES), lambda b, j: (b, 0, nblk + j)),
                  pl.BlockSpec((1, S, LANES), lambda b, j: (b, 0, 2 * nblk + j)),
                  pl.BlockSpec((8, LANES), lambda b, j: (0, j))],
        out_specs=pl.BlockSpec((1, S, LANES), lambda b, j: (b, 0, j)),
        compiler_params=_cparams(("arbitrary", "arbitrary")),
        name="short_conv",
    )(y, y, y, w)


def _cumsum_rows(x):
    n = x.shape[0]
    row = lax.broadcasted_iota(jnp.int32, x.shape, 0)
    sh = 1
    while sh < n:
        x = x + jnp.where(row >= sh, pltpu.roll(x, sh, axis=0), 0.0)
        sh *= 2
    return x


def _softmax_step(carry, s, v):
    m, l, acc = carry
    m_new = jnp.maximum(m, jnp.max(s, axis=-1, keepdims=True))
    alpha = jnp.exp(m - m_new)
    p = jnp.exp(s - m_new)
    l = alpha * l + jnp.sum(p, axis=-1, keepdims=True)
    acc = alpha * acc + _dot(p.astype(BF16), v)
    return m_new, l, acc


def _fox_kernel(q_ref, k_ref, v_ref, f_ref, bf_ref, qg_ref, kg_ref, o_ref,
                kn_scr, vb_scr, cum_scr, cumt_scr):
    qi = pl.program_id(1)
    S = k_ref.shape[1]
    T = q_ref.shape[1]
    nk = S // T

    @pl.when(qi == 0)
    def _prep():
        for h in range(FOX_HEADS):
            cols = slice(h * HEAD_DIM, (h + 1) * HEAD_DIM)
            kn_scr[:, cols] = _rms(k_ref[0, :, cols], kg_ref[...]).astype(BF16)
        vb_scr[...] = v_ref[0].astype(BF16)
        logf = jax.nn.log_sigmoid(f_ref[0] + bf_ref[...])
        cum = _cumsum_rows(logf)
        cum_scr[...] = cum
        for kt in range(nk):
            cumt_scr[kt] = cum[kt * T:(kt + 1) * T, :].T[0:8, :]

    causal = jnp.where(lax.broadcasted_iota(jnp.int32, (T, T), 0) >= lax.broadcasted_iota(jnp.int32, (T, T), 1),
                       0.0, NEG_INF)
    cq = cum_scr[pl.ds(pl.multiple_of(qi * T, T), T), :]
    heads = []
    for h in range(FOX_HEADS):
        cols = slice(h * HEAD_DIM, (h + 1) * HEAD_DIM)
        heads.append((cols, (_rms(q_ref[0, :, cols], qg_ref[...]) * SCALE).astype(BF16), cq[:, h:h + 1]))

    def step(kt, carry, diag):
        ks = pl.ds(pl.multiple_of(kt * T, T), T)
        ctile = cumt_scr[kt]
        out = []
        for h, (cols, qh, ct) in enumerate(heads):
            s = _dot_nt(qh, kn_scr[ks, cols]) + (ct - ctile[h:h + 1, :])
            if diag:
                s = s + causal
            out.append(_softmax_step(carry[h], s, vb_scr[ks, cols]))
        return tuple(out)

    init = tuple((jnp.full((T, 1), NEG_INF, F32), jnp.zeros((T, 1), F32), jnp.zeros((T, HEAD_DIM), F32))
                 for _ in range(FOX_HEADS))
    carry = lax.fori_loop(0, qi, lambda kt, c: step(kt, c, False), init)
    carry = step(qi, carry, True)
    for h, (cols, _, _) in enumerate(heads):
        o_ref[0, :, cols] = carry[h][2] / carry[h][1]


def _fox_attention(y, b_f, q_g, k_g):
    B, S, _ = y.shape
    T = FOX_TILE
    qblk = 3 * CONV_CH // FOX_WIDTH
    fblk = (3 * CONV_CH + 3 * FOX_WIDTH) // LANES
    bf = jnp.zeros((1, LANES), F32).at[0, :FOX_HEADS].set(b_f)
    return pl.pallas_call(
        _fox_kernel,
        out_shape=jax.ShapeDtypeStruct((B, S, FOX_WIDTH), F32),
        grid=(B, S // T),
        in_specs=[pl.BlockSpec((1, T, FOX_WIDTH), lambda b, i: (b, i, qblk)),
                  pl.BlockSpec((1, S, FOX_WIDTH), lambda b, i: (b, 0, qblk + 1)),
                  pl.BlockSpec((1, S, FOX_WIDTH), lambda b, i: (b, 0, qblk + 2)),
                  pl.BlockSpec((1, S, LANES), lambda b, i: (b, 0, fblk)),
                  pl.BlockSpec((1, LANES), lambda b, i: (0, 0)),
                  pl.BlockSpec((1, HEAD_DIM), lambda b, i: (0, 0)),
                  pl.BlockSpec((1, HEAD_DIM), lambda b, i: (0, 0))],
        out_specs=pl.BlockSpec((1, T, FOX_WIDTH), lambda b, i: (b, i, 0)),
        scratch_shapes=[pltpu.VMEM((S, FOX_WIDTH), BF16),
                        pltpu.VMEM((S, FOX_WIDTH), BF16),
                        pltpu.VMEM((S, LANES), F32),
                        pltpu.VMEM((S // T, 8, T), F32)],
        compiler_params=_cparams(("arbitrary", "arbitrary")),
        name="fox_attention",
    )(y, y, y, y, bf, q_g.reshape(1, HEAD_DIM), k_g.reshape(1, HEAD_DIM))


def _extract_top(work, iota, k):
    nrows = work.shape[0]
    vals = []
    for _ in range(k):
        m = jnp.max(work, axis=0, keepdims=True)
        idx = jnp.min(jnp.where(work == m, iota, nrows), axis=0, keepdims=True)
        work = jnp.where(iota == idx, -jnp.inf, work)
        vals.append(m)
    return vals, work


def _peer_kernel(x_ref, sh_ref, sc_ref, gt_ref, ng_ref, wq_ref, keys_ref, u_ref, vt_ref, o_ref,
                 ht_scr, qt_scr, sp_scr, s0_scr, e0_scr, s1_scr, e1_scr, tau_scr, a_scr, p_scr, acc_scr):
    c = pl.program_id(2)
    nc = pl.num_programs(2)
    tn = x_ref.shape[1]
    nlt = tn // LANES
    half = PEER_QDIM // 2

    @pl.when(c == 0)
    def _prep():
        h = _rms(x_ref[0], ng_ref[...]) * (1.0 + sc_ref[0]) + sh_ref[0]
        q = _dot(h, wq_ref[...], HIGHEST)
        qt_scr[...] = q.T
        ht_scr[...] = h.T.astype(BF16)
        acc_scr[...] = jnp.zeros_like(acc_scr)
        iota_k = lax.broadcasted_iota(jnp.int32, (PEER_NKEYS, LANES), 0)
        iota_c = lax.broadcasted_iota(jnp.int32, (PEER_TOPK * PEER_TOPK, LANES), 0)

        def head_body(hd, carry):
            for p in range(2):
                qs = qt_scr[pl.ds(pl.multiple_of((hd * 2 + p) * half, half), half), :]
                sp_scr[p] = _dot(keys_ref[p], qs, HIGHEST)
            for lt in range(nlt):
                lanes = slice(lt * LANES, (lt + 1) * LANES)
                s0 = sp_scr[0, :, lanes]
                s1 = sp_scr[1, :, lanes]
                sv0, w0 = _extract_top(s0, iota_k, PEER_TOPK)
                sv1, w1 = _extract_top(s1, iota_k, PEER_TOPK)
                s0m = jnp.where(w0 == -jnp.inf, s0, -jnp.inf)
                s1m = jnp.where(w1 == -jnp.inf, s1, -jnp.inf)
                sv1_all = jnp.concatenate(sv1, axis=0)
                cand = jnp.concatenate([v0 + sv1_all for v0 in sv0], axis=0)
                cv, _ = _extract_top(cand, iota_c, PEER_TOPK)
                z = cv[0] - cv[0]
                for r in range(PEER_TOPK):
                    z = z + jnp.exp(cv[r] - cv[0])
                s0_scr[hd, :, lanes] = s0m
                s1_scr[hd, :, lanes] = s1m
                e0_scr[hd, :, lanes] = jnp.exp(s0m - sv0[0])
                e1_scr[hd, :, lanes] = jnp.exp(s1m - sv1[0]) / z
                tau_scr[hd, :, lanes] = jnp.broadcast_to(cv[PEER_TOPK - 1], (8, LANES))
            return carry

        lax.fori_loop(0, PEER_HEADS, head_body, 0)

    a_scr[...] = _dot(u_ref[...], ht_scr[...])
    nsub = PEER_EC // PEER_NKEYS
    i0 = pl.ds(pl.multiple_of(c * nsub, nsub), nsub)
    for ii in range(nsub):
        rows = slice(ii * PEER_NKEYS, (ii + 1) * PEER_NKEYS)
        for lt in range(nlt):
            lanes = slice(lt * LANES, (lt + 1) * LANES)
            w = jnp.zeros((PEER_NKEYS, LANES), F32)
            for hd in range(PEER_HEADS):
                s = s1_scr[hd, :, lanes] + s0_scr[hd, i0, lanes][ii:ii + 1]
                e = e1_scr[hd, :, lanes] * e0_scr[hd, i0, lanes][ii:ii + 1]
                w = w + jnp.where(s >= tau_scr[hd, 0:1, lanes], e, 0.0)
            act = jax.nn.gelu(a_scr[rows, lanes], approximate=True)
            p_scr[rows, lanes] = (w * act).astype(BF16)
    acc_scr[...] += _dot(vt_ref[...], p_scr[...])

    @pl.when(c == nc - 1)
    def _fin():
        o_ref[0] = x_ref[0] + gt_ref[0] * acc_scr[...].T


def _peer(x, sh, sc, gate, ng, w_q, keys, u_b, vt_b):
    B, S, D = x.shape
    tn = PEER_TN
    nc = PEER_EXPERTS // PEER_EC
    tab = pltpu.VMEM((PEER_HEADS, PEER_NKEYS, tn), F32)
    return pl.pallas_call(
        _peer_kernel,
        out_shape=jax.ShapeDtypeStruct((B, S, D), F32),
        grid=(B, S // tn, nc),
        in_specs=[pl.BlockSpec((1, tn, D), lambda b, i, c: (b, i, 0)),
                  pl.BlockSpec((1, 1, D), lambda b, i, c: (b, 0, 0)),
                  pl.BlockSpec((1, 1, D), lambda b, i, c: (b, 0, 0)),
                  pl.BlockSpec((1, 1, D), lambda b, i, c: (b, 0, 0)),
                  pl.BlockSpec((1, D), lambda b, i, c: (0, 0)),
                  pl.BlockSpec((D, PEER_HEADS * PEER_QDIM), lambda b, i, c: (0, 0)),
                  pl.BlockSpec((2, PEER_NKEYS, PEER_QDIM // 2), lambda b, i, c: (0, 0, 0)),
                  pl.BlockSpec((PEER_EC, D), lambda b, i, c: (c, 0)),
                  pl.BlockSpec((D, PEER_EC), lambda b, i, c: (0, c))],
        out_specs=pl.BlockSpec((1, tn, D), lambda b, i, c: (b, i, 0)),
        scratch_shapes=[pltpu.VMEM((D, tn), BF16),
                        pltpu.VMEM((PEER_HEADS * PEER_QDIM, tn), F32),
                        pltpu.VMEM((2, PEER_NKEYS, tn), F32),
                        tab, tab, tab, tab,
                        pltpu.VMEM((PEER_HEADS, 8, tn), F32),
                        pltpu.VMEM((PEER_EC, tn), F32),
                        pltpu.VMEM((PEER_EC, tn), BF16),
                        pltpu.VMEM((D, tn), F32)],
        compiler_params=_cparams(("arbitrary", "arbitrary", "arbitrary")),
        name="peer",
    )(x, sh, sc, gate, ng, w_q, keys, u_b, vt_b)


def _rel_bucket(dist):
    n = jnp.maximum(dist, 0)
    max_exact = REL_BUCKETS // 2
    nf = jnp.maximum(n, 1).astype(F32)
    large = max_exact + (jnp.log(nf / max_exact) / math.log(REL_MAX_DIST / max_exact)
                         * (REL_BUCKETS - max_exact)).astype(jnp.int32)
    large = jnp.minimum(large, REL_BUCKETS - 1)
    return jnp.where(n < max_exact, n, large)


def _bias_lookup(bucket, tab_ref, h):
    bias = jnp.zeros(bucket.shape, F32)
    for b in range(REL_BUCKETS):
        bias = jnp.where(bucket == b, tab_ref[b, h], bias)
    return bias


N_BIAS_TILES = 5


def _relbias_kernel(tab_ref, o_ref):
    h = pl.program_id(0)
    T = o_ref.shape[2]
    a = lax.broadcasted_iota(jnp.int32, (T, T), 0)
    b = lax.broadcasted_iota(jnp.int32, (T, T), 1)
    last = tab_ref[REL_BUCKETS - 1, h]
    o_ref[0, 0] = jnp.where(a >= b, _bias_lookup(_rel_bucket(a - b), tab_ref, h), NEG_INF)
    o_ref[0, 1] = _bias_lookup(_rel_bucket(a - b + T), tab_ref, h)
    o_ref[0, 2] = jnp.full((T, T), last, F32)
    o_ref[0, 3] = jnp.where(a < b, last, NEG_INF)
    o_ref[0, 4] = jnp.full((T, T), NEG_INF, F32)


def _relbias(rel_table):
    T = ATT_TILE
    assert 2 * T - (T - 1) >= REL_MAX_DIST
    assert WINDOW % T == 0
    return pl.pallas_call(
        _relbias_kernel,
        out_shape=jax.ShapeDtypeStruct((NSA_HEADS, N_BIAS_TILES, T, T), F32),
        grid=(NSA_HEADS,),
        in_specs=[pl.BlockSpec(memory_space=pltpu.SMEM)],
        out_specs=pl.BlockSpec((1, N_BIAS_TILES, T, T), lambda h: (h, 0, 0, 0)),
        compiler_params=_cparams(("arbitrary",)),
        name="rel_bias_tiles",
    )(rel_table)


def _cmp_kernel(x_ref, pos_ref, w1_ref, w2_ref, kg_ref, o_ref):
    j = pl.program_id(1)
    half = CMP_STRIDE * HEAD_DIM
    x = x_ref[0, 0]
    pos = pos_ref[0]
    pa = _dot((x + pos[:, :half]).astype(BF16), w1_ref[0, :half, :])
    pb = _dot((x + pos[:, half:]).astype(BF16), w1_ref[0, half:, :])
    pre = pa + pltpu.roll(pb, pb.shape[0] - 1, axis=0)
    out = _dot(jax.nn.gelu(pre, approximate=True).astype(BF16), w2_ref[0])
    o_ref[0, 0] = jnp.where(j < NSA_GROUPS, _rms(out, kg_ref[...]), out)


def _nsa_compress(xr, pos, w1, w2, kg0):
    B, _, M, W = xr.shape
    G = NSA_GROUPS
    return pl.pallas_call(
        _cmp_kernel,
        out_shape=jax.ShapeDtypeStruct((B, 2 * G, M, HEAD_DIM), F32),
        grid=(B, 2 * G),
        in_specs=[pl.BlockSpec((1, 1, M, W), lambda b, j: (b, j, 0, 0)),
                  pl.BlockSpec((1, 1, 2 * W), lambda b, j: (j // G, 0, 0)),
                  pl.BlockSpec((1, 2 * W, HEAD_DIM), lambda b, j: (j // G, 0, 0)),
                  pl.BlockSpec((1, HEAD_DIM, HEAD_DIM), lambda b, j: (j // G, 0, 0)),
                  pl.BlockSpec((1, HEAD_DIM), lambda b, j: (0, 0))],
        out_specs=pl.BlockSpec((1, 1, M, HEAD_DIM), lambda b, j: (b, j, 0, 0)),
        compiler_params=_cparams(("arbitrary", "arbitrary")),
        name="nsa_compress",
    )(xr, pos.reshape(2, 1, 2 * W), w1.astype(BF16), w2.astype(BF16), kg0.reshape(1, HEAD_DIM))


def _cmp_attn_kernel(q_ref, kv_ref, ov_ref, tab_ref, qg_ref, oc_ref, sel_ref, qn_ref):
    qi = pl.program_id(1)
    T = q_ref.shape[1]
    NC = kv_ref.shape[2]
    NS = ov_ref.shape[1]
    G, J = NSA_GROUPS, NSA_HPG
    t = qi * T + lax.broadcasted_iota(jnp.int32, (T, NC), 0)
    n = lax.broadcasted_iota(jnp.int32, (T, NC), 1)
    d = t - (n * CMP_STRIDE + CMP_LEN - 1)
    vis = d >= 0
    visf = vis.astype(F32)
    bucket = _rel_bucket(d)
    tt = qi * T + lax.broadcasted_iota(jnp.int32, (T, NS), 0)
    mm = lax.broadcasted_iota(jnp.int32, (T, NS), 1)
    cur = tt // SLC_LEN
    forced = (mm == 0) | (mm == cur) | (mm == cur - 1)
    past = mm * SLC_LEN <= tt
    for g in range(G):
        kc = kv_ref[0, g].astype(BF16)
        vc = kv_ref[0, G + g].astype(BF16)
        psum = jnp.zeros((T, NC), F32)
        for j in range(J):
            h = g * J + j
            cols = slice(h * HEAD_DIM, (h + 1) * HEAD_DIM)
            qh = (_rms(q_ref[0, :, cols], qg_ref[...]) * SCALE).astype(BF16)
            qn_ref[0, :, cols] = qh
            s = _dot_nt(qh, kc) + _bias_lookup(bucket, tab_ref, h)
            s = jnp.where(vis, s, NEG_INF)
            e = jnp.exp(s - jnp.max(s, axis=-1, keepdims=True))
            p = e / jnp.sum(e, axis=-1, keepdims=True) * visf
            oc_ref[0, :, cols] = _dot(p.astype(BF16), vc)
            psum = psum + p
        imp = _dot(psum, ov_ref[...], HIGHEST)
        imp = jnp.where(forced, FORCE_SCORE, jnp.where(past, imp, -1.0))
        rank = jnp.zeros((T, NS), jnp.int32)
        for mp in range(NS):
            col = imp[:, mp:mp + 1]
            ahead = (col > imp) | ((col == imp) & (mm > mp))
            rank = rank + ahead.astype(jnp.int32)
        sel = (rank < SLC_TOPN) & (imp >= 0.0)
        sel_ref[0, :, g * NS:(g + 1) * NS] = sel.astype(F32)


def _nsa_cmp_attn(y, kcvc, overlap, rel_table, q_g):
    B, S, _ = y.shape
    T = ATT_TILE
    NC = kcvc.shape[2]
    NS = overlap.shape[1]
    return pl.pallas_call(
        _cmp_attn_kernel,
        out_shape=(jax.ShapeDtypeStruct((B, S, NSA_WIDTH), F32),
                   jax.ShapeDtypeStruct((B, S, NSA_GROUPS * NS), F32),
                   jax.ShapeDtypeStruct((B, S, NSA_WIDTH), BF16)),
        grid=(B, S // T),
        in_specs=[pl.BlockSpec((1, T, NSA_WIDTH), lambda b, i: (b, i, 0)),
                  pl.BlockSpec((1, 2 * NSA_GROUPS, NC, HEAD_DIM), lambda b, i: (b, 0, 0, 0)),
                  pl.BlockSpec((NC, NS), lambda b, i: (0, 0)),
                  pl.BlockSpec(memory_space=pltpu.SMEM),
                  pl.BlockSpec((1, HEAD_DIM), lambda b, i: (0, 0))],
        out_specs=(pl.BlockSpec((1, T, NSA_WIDTH), lambda b, i: (b, i, 0)),
                   pl.BlockSpec((1, T, NSA_GROUPS * NS), lambda b, i: (b, i, 0)),
                   pl.BlockSpec((1, T, NSA_WIDTH), lambda b, i: (b, i, 0))),
        compiler_params=_cparams(("arbitrary", "arbitrary")),
        name="nsa_cmp_attn",
    )(y, kcvc, overlap, rel_table, q_g.reshape(1, HEAD_DIM))


def _kv_prep_kernel(ks_ref, kw_ref, kg1_ref, kg2_ref, ksn_ref, vs_ref, kwn_ref, vw_ref):
    KV = NSA_KV
    for g in range(NSA_GROUPS):
        cols = slice(g * HEAD_DIM, (g + 1) * HEAD_DIM)
        ksn_ref[0, :, cols] = _rms(ks_ref[0, :, cols], kg1_ref[...]).astype(BF16)
        kwn_ref[0, :, cols] = _rms(kw_ref[0, :, cols], kg2_ref[...]).astype(BF16)
    vs_ref[0] = ks_ref[0, :, KV:].astype(BF16)
    vw_ref[0] = kw_ref[0, :, KV:].astype(BF16)


def _nsa_kv_prep(y, k_g):
    B, S, _ = y.shape
    ksblk = (NSA_WIDTH + 2 * NSA_KV) // (2 * NSA_KV)
    out = jax.ShapeDtypeStruct((B, S, NSA_KV), BF16)
    ospec = pl.BlockSpec((1, ROW_TILE, NSA_KV), lambda b, i: (b, i, 0))
    return pl.pallas_call(
        _kv_prep_kernel,
        out_shape=(out, out, out, out),
        grid=(B, S // ROW_TILE),
        in_specs=[pl.BlockSpec((1, ROW_TILE, 2 * NSA_KV), lambda b, i: (b, i, ksblk)),
                  pl.BlockSpec((1, ROW_TILE, 2 * NSA_KV), lambda b, i: (b, i, ksblk + 1)),
                  pl.BlockSpec((1, HEAD_DIM), lambda b, i: (0, 0)),
                  pl.BlockSpec((1, HEAD_DIM), lambda b, i: (0, 0))],
        out_specs=(ospec, ospec, ospec, ospec),
        compiler_params=_cparams(("arbitrary", "arbitrary")),
        name="nsa_kv_prep",
    )(y, y, k_g[1].reshape(1, HEAD_DIM), k_g[2].reshape(1, HEAD_DIM))


def _sel_win_kernel(qn_ref, ks_ref, vs_ref, kw_ref, vw_ref, gl_ref, sel_ref, oc_ref, bias_ref, e_ref, bg_ref,
                    o_ref, mk_scr):
    qi = pl.program_id(1)
    T = qn_ref.shape[1]
    S = ks_ref.shape[1]
    npair = S // (2 * T)
    G, J = NSA_GROUPS, NSA_HPG
    NS = e_ref.shape[0]
    wt = WINDOW // T

    def sel_tile(r):
        return jnp.where(r < 0, 4, jnp.minimum(r, 2))

    def win_tile(r):
        return jnp.where(r < 0, 4, jnp.where(r <= 1, r, jnp.where(r < wt, 2, jnp.where(r == wt, 3, 4))))

    def bias4(g, i0, i1):
        return jnp.concatenate(
            [jnp.concatenate([bias_ref[g * J + j, i0], bias_ref[g * J + j, i1]], axis=1) for j in range(J)],
            axis=0)

    gates = jax.nn.sigmoid(gl_ref[0] + bg_ref[...])
    init = (jnp.full((J * T, 1), NEG_INF, F32), jnp.zeros((J * T, 1), F32), jnp.zeros((J * T, HEAD_DIM), F32))
    kw0 = jnp.maximum(qi - wt, 0) // 2
    for g in range(G):
        gcols = slice(g * HEAD_DIM, (g + 1) * HEAD_DIM)
        selg = sel_ref[0, :, g * NS:(g + 1) * NS].astype(BF16)
        for kp in range(npair):
            mk_scr[kp] = (_dot(selg, e_ref[:, kp * 2 * T:(kp + 1) * 2 * T]) - 1.0) * (-NEG_INF)
        q4 = jnp.concatenate([qn_ref[0, :, (g * J + j) * HEAD_DIM:(g * J + j + 1) * HEAD_DIM] for j in range(J)],
                             axis=0)

        def sel_step(kp, carry, q4=q4, gcols=gcols, g=g):
            ks = pl.ds(pl.multiple_of(kp * 2 * T, 2 * T), 2 * T)
            r0 = qi - 2 * kp
            s = _dot_nt(q4, ks_ref[0, ks, gcols]) + bias4(g, sel_tile(r0), sel_tile(r0 - 1))
            s = s + jnp.concatenate([mk_scr[kp]] * J, axis=0)
            return _softmax_step(carry, s, vs_ref[0, ks, gcols])

        def win_step(kp, carry, q4=q4, gcols=gcols, g=g):
            ks = pl.ds(pl.multiple_of(kp * 2 * T, 2 * T), 2 * T)
            r0 = qi - 2 * kp
            s = _dot_nt(q4, kw_ref[0, ks, gcols]) + bias4(g, win_tile(r0), win_tile(r0 - 1))
            return _softmax_step(carry, s, vw_ref[0, ks, gcols])

        cs = lax.fori_loop(0, kw0, sel_step, init)
        cs, cw = lax.fori_loop(kw0, qi // 2 + 1,
                               lambda kp, c: (sel_step(kp, c[0]), win_step(kp, c[1])), (cs, init))
        o_s = cs[2] / cs[1]
        o_w = cw[2] / cw[1]
        for j in range(J):
            h = g * J + j
            cols = slice(h * HEAD_DIM, (h + 1) * HEAD_DIM)
            rows = slice(j * T, (j + 1) * T)
            o_ref[0, :, cols] = (gates[:, 3 * h:3 * h + 1] * oc_ref[0, :, cols]
                                 + gates[:, 3 * h + 1:3 * h + 2] * o_s[rows]
                                 + gates[:, 3 * h + 2:3 * h + 3] * o_w[rows])


def _nsa_sel_win(qn, y, kv, sel, oc, bias_tiles, expand, b_gate):
    B, S, _ = y.shape
    T = ATT_TILE
    glblk = (NSA_WIDTH + 6 * NSA_KV) // LANES
    bg = jnp.zeros((1, LANES), F32).at[0, :3 * NSA_HEADS].set(b_gate)
    kvspec = pl.BlockSpec((1, S, NSA_KV), lambda b, i: (b, 0, 0))
    return pl.pallas_call(
        _sel_win_kernel,
        out_shape=jax.ShapeDtypeStruct((B, S, NSA_WIDTH), F32),
        grid=(B, S // T),
        in_specs=[pl.BlockSpec((1, T, NSA_WIDTH), lambda b, i: (b, i, 0)),
                  kvspec, kvspec, kvspec, kvspec,
                  pl.BlockSpec((1, T, LANES), lambda b, i: (b, i, glblk)),
                  pl.BlockSpec((1, T, sel.shape[2]), lambda b, i: (b, i, 0)),
                  pl.BlockSpec((1, T, NSA_WIDTH), lambda b, i: (b, i, 0)),
                  pl.BlockSpec(bias_tiles.shape, lambda b, i: (0, 0, 0, 0)),
                  pl.BlockSpec(expand.shape, lambda b, i: (0, 0)),
                  pl.BlockSpec((1, LANES), lambda b, i: (0, 0))],
        out_specs=pl.BlockSpec((1, T, NSA_WIDTH), lambda b, i: (b, i, 0)),
        scratch_shapes=[pltpu.VMEM((S // (2 * T), T, 2 * T), F32)],
        compiler_params=_cparams(("arbitrary", "arbitrary")),
        name="nsa_sel_win",
    )(qn, *kv, y, sel, oc, bias_tiles, expand, bg)


def _odd_mixer(x, sh, sc, gate, ng, w_in, b_gate, q_g, k_g, cmp_pos, cmp_w1, cmp_w2, rel_table, bias_tiles,
               w_out):
    B, S, _ = x.shape
    G = NSA_GROUPS
    w = jnp.zeros((D_MODEL, ODD_IN_PAD), BF16).at[:, :ODD_IN].set(w_in.astype(BF16))
    y = _norm_proj(x, sh, sc, ng, w)
    xr = y[:, :, NSA_WIDTH:NSA_WIDTH + 2 * NSA_KV].reshape(B, S, 2, G, HEAD_DIM)
    xr = xr.transpose(0, 2, 3, 1, 4).reshape(B, 2 * G, S // CMP_STRIDE, CMP_STRIDE * HEAD_DIM)
    kcvc = _nsa_compress(xr, cmp_pos.reshape(2, CMP_LEN * HEAD_DIM), cmp_w1, cmp_w2, k_g[0])
    n_slc = S // SLC_LEN
    starts = jnp.arange(S // CMP_STRIDE) * CMP_STRIDE
    bstart = jnp.arange(n_slc) * SLC_LEN
    real = (jnp.arange(S // CMP_STRIDE) < (S - CMP_LEN) // CMP_STRIDE + 1)[:, None]
    overlap = (real & (starts[:, None] < bstart[None, :] + SLC_LEN)
               & (starts[:, None] + CMP_LEN > bstart[None, :])).astype(F32)
    expand = (jnp.arange(S)[None, :] // SLC_LEN == jnp.arange(n_slc)[:, None]).astype(BF16)
    oc, sel, qn = _nsa_cmp_attn(y, kcvc, overlap, rel_table, q_g)
    o = _nsa_sel_win(qn, y, _nsa_kv_prep(y, k_g), sel, oc, bias_tiles, expand, b_gate)
    return _out_proj(x, o, 0, o, 1, w_out.astype(BF16), gate)


def _even_mixer(x, sh, sc, gate, ng, w_in, b_f, conv_w, q_g, k_g, w_out):
    w = jnp.zeros((D_MODEL, EVEN_IN_PAD), BF16).at[:, :EVEN_IN].set(w_in.astype(BF16))
    y = _norm_proj(x, sh, sc, ng, w)
    conv = _short_conv(y, conv_w)
    attn = _fox_attention(y, b_f, q_g, k_g)
    return _out_proj(x, conv, 0, attn, 0, w_out.astype(BF16), gate)


def kernel(x, c, ada_w, ada_b, norm_g, even_w_in, even_b_f, even_conv_w, even_q_g, even_k_g, even_w_out, odd_w_in, odd_b_gate, odd_q_g, odd_k_g, odd_cmp_pos, odd_cmp_w1, odd_cmp_w2, odd_w_out, rel_table, peer_w_q, peer_keys, peer_u, peer_v):
    B = x.shape[0]
    mods = _ada(c, ada_w, ada_b)
    bias_tiles = _relbias(rel_table)
    for layer in range(DEPTH):
        sh1, sc1, g1, sh2, sc2, g2 = [m.reshape(B, 1, D_MODEL) for m in jnp.split(mods[layer], 6, axis=-1)]
        i = layer // 2
        ng1 = norm_g[layer, 0].reshape(1, D_MODEL)
        ng2 = norm_g[layer, 1].reshape(1, D_MODEL)
        if layer % 2 == 0:
            x = _even_mixer(x, sh1, sc1, g1, ng1, even_w_in[i], even_b_f[i], even_conv_w[i], even_q_g[i],
                            even_k_g[i], even_w_out[i])
        else:
            x = _odd_mixer(x, sh1, sc1, g1, ng1, odd_w_in[i], odd_b_gate[i], odd_q_g[i], odd_k_g[i],
                           odd_cmp_pos[i], odd_cmp_w1[i], odd_cmp_w2[i], rel_table, bias_tiles, odd_w_out[i])
        x = _peer(x, sh2, sc2, g2, ng2, peer_w_q[layer], peer_keys[layer],
                  peer_u[layer].astype(BF16), peer_v[layer].T.astype(BF16))
    return x
```

```python
import functools
import math

import jax
import jax.numpy as jnp
from jax import lax
from jax.experimental import pallas as pl
from jax.experimental.pallas import tpu as pltpu

D_MODEL = 1024
DEPTH = 4
HEAD_DIM = 64
CONV_CH = 512
CONV_K = 3
FOX_HEADS = 8
FOX_WIDTH = FOX_HEADS * HEAD_DIM
EVEN_IN = 3 * CONV_CH + 3 * FOX_WIDTH + FOX_HEADS
NSA_GROUPS = 4
NSA_HPG = 4
NSA_HEADS = NSA_GROUPS * NSA_HPG
NSA_WIDTH = NSA_HEADS * HEAD_DIM
NSA_KV = NSA_GROUPS * HEAD_DIM
CMP_LEN = 32
CMP_STRIDE = 16
SLC_LEN = 64
SLC_TOPN = 16
WINDOW = 512
ODD_IN = NSA_WIDTH + 6 * NSA_KV + 3 * NSA_HEADS
REL_BUCKETS = 32
REL_MAX_DIST = 128
PEER_HEADS = 8
PEER_NKEYS = 128
PEER_EXPERTS = PEER_NKEYS * PEER_NKEYS
PEER_QDIM = 128
PEER_TOPK = 16
EPS = 1e-6
NEG_INF = -1e30
FORCE_SCORE = 1e6

LANES = 128
F32 = jnp.float32
BF16 = jnp.bfloat16
HIGHEST = lax.Precision.HIGHEST
VMEM_LIMIT = 56 * 1024 * 1024

EVEN_IN_PAD = 3200
ODD_IN_PAD = 2688
ROW_TILE = 512
ATT_TILE = 128
FOX_TILE = 256
PEER_TN = 512
PEER_EC = 1024
SCALE = HEAD_DIM ** -0.5


def _cparams(sem, flags=None):
    return pltpu.CompilerParams(dimension_semantics=sem, vmem_limit_bytes=VMEM_LIMIT, flags=flags)


def _dot(a, b, precision=None):
    return jnp.dot(a, b, precision=precision, preferred_element_type=F32)


def _dot_nt(a, b, precision=None):
    return lax.dot_general(a, b, (((1,), (1,)), ((), ())), precision=precision,
                           preferred_element_type=F32)


def _rms(x, g):
    return x * lax.rsqrt(jnp.mean(x * x, axis=-1, keepdims=True) + EPS) * g


def _ada_kernel(c_ref, w_ref, b_ref, o_ref):
    c = c_ref[...]
    sc = c * jax.nn.sigmoid(c)
    o_ref[0] = _dot(sc, w_ref[0], HIGHEST) + b_ref[0]


def _ada(c, ada_w, ada_b):
    B, D = c.shape
    E = ada_w.shape[2]
    tn = 2048
    return pl.pallas_call(
        _ada_kernel,
        out_shape=jax.ShapeDtypeStruct((DEPTH, B, E), F32),
        grid=(DEPTH, E // tn),
        in_specs=[pl.BlockSpec((B, D), lambda l, j: (0, 0)),
                  pl.BlockSpec((1, D, tn), lambda l, j: (l, 0, j)),
                  pl.BlockSpec((1, 1, tn), lambda l, j: (l, 0, j))],
        out_specs=pl.BlockSpec((1, B, tn), lambda l, j: (l, 0, j)),
        compiler_params=_cparams(("arbitrary", "arbitrary")),
        name="ada_mod",
    )(c, ada_w, ada_b.reshape(DEPTH, 1, E))


def _norm_proj_kernel(x_ref, sh_ref, sc_ref, ng_ref, w_ref, y_ref):
    h = _rms(x_ref[0], ng_ref[...]) * (1.0 + sc_ref[0]) + sh_ref[0]
    y_ref[0] = _dot(h.astype(BF16), w_ref[...])


def _norm_proj(x, sh, sc, ng, w):
    B, S, D = x.shape
    N = w.shape[1]
    return pl.pallas_call(
        _norm_proj_kernel,
        out_shape=jax.ShapeDtypeStruct((B, S, N), F32),
        grid=(B, S // ROW_TILE),
        in_specs=[pl.BlockSpec((1, ROW_TILE, D), lambda b, i: (b, i, 0)),
                  pl.BlockSpec((1, 1, D), lambda b, i: (b, 0, 0)),
                  pl.BlockSpec((1, 1, D), lambda b, i: (b, 0, 0)),
                  pl.BlockSpec((1, D), lambda b, i: (0, 0)),
                  pl.BlockSpec((D, N), lambda b, i: (0, 0))],
        out_specs=pl.BlockSpec((1, ROW_TILE, N), lambda b, i: (b, i, 0)),
        compiler_params=_cparams(("arbitrary", "arbitrary")),
        name="norm_proj",
    )(x, sh, sc, ng, w)


def _out_proj_kernel(x_ref, a_ref, b_ref, wa_ref, wb_ref, g_ref, o_ref):
    y = _dot(a_ref[0].astype(BF16), wa_ref[...]) + _dot(b_ref[0].astype(BF16), wb_ref[...])
    o_ref[0] = x_ref[0] + g_ref[0] * y


def _out_proj(x, a, a_blk, b, b_blk, w, gate):
    B, S, D = x.shape
    half = w.shape[0] // 2
    wa, wb = w[:half], w[half:]
    return pl.pallas_call(
        _out_proj_kernel,
        out_shape=jax.ShapeDtypeStruct((B, S, D), F32),
        grid=(B, S // ROW_TILE),
        in_specs=[pl.BlockSpec((1, ROW_TILE, D), lambda b, i: (b, i, 0)),
                  pl.BlockSpec((1, ROW_TILE, half), lambda b, i: (b, i, a_blk)),
                  pl.BlockSpec((1, ROW_TILE, half), lambda b, i: (b, i, b_blk)),
                  pl.BlockSpec((half, D), lambda b, i: (0, 0)),
                  pl.BlockSpec((half, D), lambda b, i: (0, 0)),
                  pl.BlockSpec((1, 1, D), lambda b, i: (b, 0, 0))],
        out_specs=pl.BlockSpec((1, ROW_TILE, D), lambda b, i: (b, i, 0)),
        compiler_params=_cparams(("arbitrary", "arbitrary")),
        name="out_proj",
    )(x, a, b, wa, wb, gate)


def _conv_kernel(cb_ref, cc_ref, ch_ref, w_ref, o_ref):
    u = cc_ref[0] * ch_ref[0]
    row = lax.broadcasted_iota(jnp.int32, u.shape, 0)
    u1 = jnp.where(row >= 1, pltpu.roll(u, 1, axis=0), 0.0)
    u2 = jnp.where(row >= 2, pltpu.roll(u, 2, axis=0), 0.0)
    w = w_ref[...]
    o_ref[0] = cb_ref[0] * (w[0:1] * u2 + w[1:2] * u1 + w[2:3] * u)


def _short_conv(y, conv_w):
    B, S, _ = y.shape
    nblk = CONV_CH // LANES
    w = jnp.zeros((8, CONV_CH), F32).at[:CONV_K].set(conv_w)
    return pl.pallas_call(
        _conv_kernel,
        out_shape=jax.ShapeDtypeStruct((B, S, CONV_CH), F32),
        grid=(B, nblk),
        in_specs=[pl.BlockSpec((1, S, LANES), lambda b, j: (b, 0, j)),
                  pl.BlockSpec((1, S, LANES), lambda b, j: (b, 0, nblk + j)),
                  pl.BlockSpec((1, S, LANES), lambda b, j: (b, 0, 2 * nblk + j)),
                  pl.BlockSpec((8, LANES), lambda b, j: (0, j))],
        out_specs=pl.BlockSpec((1, S, LANES), lambda b, j: (b, 0, j)),
        compiler_params=_cparams(("arbitrary", "arbitrary")),
        name="short_conv",
    )(y, y, y, w)


def _cumsum_rows(x):
    n = x.shape[0]
    row = lax.broadcasted_iota(jnp.int32, x.shape, 0)
    sh = 1
    while sh < n:
        x = x + jnp.where(row >= sh, pltpu.roll(x, sh, axis=0), 0.0)
        sh *= 2
    return x


def _softmax_step(carry, s, v):
    m, l, acc = carry
    m_new = jnp.maximum(m, jnp.max(s, axis=-1, keepdims=True))
    alpha = jnp.exp(m - m_new)
    p = jnp.exp(s - m_new)
    l = alpha * l + jnp.sum(p, axis=-1, keepdims=True)
    acc = alpha * acc + _dot(p.astype(BF16), v)
    return m_new, l, acc


def _fox_kernel(q_ref, k_ref, v_ref, f_ref, bf_ref, qg_ref, kg_ref, o_ref,
                kn_scr, vb_scr, cum_scr, cumt_scr):
    qi = pl.program_id(1)
    S = k_ref.shape[1]
    T = q_ref.shape[1]
    nk = S // T

    @pl.when(qi == 0)
    def _prep():
        for h in range(FOX_HEADS):
            cols = slice(h * HEAD_DIM, (h + 1) * HEAD_DIM)
            kn_scr[:, cols] = _rms(k_ref[0, :, cols], kg_ref[...]).astype(BF16)
        vb_scr[...] = v_ref[0].astype(BF16)
        logf = jax.nn.log_sigmoid(f_ref[0] + bf_ref[...])
        cum = _cumsum_rows(logf)
        cum_scr[...] = cum
        for kt in range(nk):
            cumt_scr[kt] = cum[kt * T:(kt + 1) * T, :].T[0:8, :]

    causal = jnp.where(lax.broadcasted_iota(jnp.int32, (T, T), 0) >= lax.broadcasted_iota(jnp.int32, (T, T), 1),
                       0.0, NEG_INF)
    cq = cum_scr[pl.ds(pl.multiple_of(qi * T, T), T), :]
    heads = []
    for h in range(FOX_HEADS):
        cols = slice(h * HEAD_DIM, (h + 1) * HEAD_DIM)
        heads.append((cols, (_rms(q_ref[0, :, cols], qg_ref[...]) * SCALE).astype(BF16), cq[:, h:h + 1]))

    def step(kt, carry, diag):
        ks = pl.ds(pl.multiple_of(kt * T, T), T)
        ctile = cumt_scr[kt]
        out = []
        for h, (cols, qh, ct) in enumerate(heads):
            s = _dot_nt(qh, kn_scr[ks, cols]) + (ct - ctile[h:h + 1, :])
            if diag:
                s = s + causal
            out.append(_softmax_step(carry[h], s, vb_scr[ks, cols]))
        return tuple(out)

    init = tuple((jnp.full((T, 1), NEG_INF, F32), jnp.zeros((T, 1), F32), jnp.zeros((T, HEAD_DIM), F32))
                 for _ in range(FOX_HEADS))
    carry = lax.fori_loop(0, qi, lambda kt, c: step(kt, c, False), init)
    carry = step(qi, carry, True)
    for h, (cols, _, _) in enumerate(heads):
        o_ref[0, :, cols] = carry[h][2] / carry[h][1]


def _fox_attention(y, b_f, q_g, k_g):
    B, S, _ = y.shape
    T = FOX_TILE
    qblk = 3 * CONV_CH // FOX_WIDTH
    fblk = (3 * CONV_CH + 3 * FOX_WIDTH) // LANES
    bf = jnp.zeros((1, LANES), F32).at[0, :FOX_HEADS].set(b_f)
    return pl.pallas_call(
        _fox_kernel,
        out_shape=jax.ShapeDtypeStruct((B, S, FOX_WIDTH), F32),
        grid=(B, S // T),
        in_specs=[pl.BlockSpec((1, T, FOX_WIDTH), lambda b, i: (b, i, qblk)),
                  pl.BlockSpec((1, S, FOX_WIDTH), lambda b, i: (b, 0, qblk + 1)),
                  pl.BlockSpec((1, S, FOX_WIDTH), lambda b, i: (b, 0, qblk + 2)),
                  pl.BlockSpec((1, S, LANES), lambda b, i: (b, 0, fblk)),
                  pl.BlockSpec((1, LANES), lambda b, i: (0, 0)),
                  pl.BlockSpec((1, HEAD_DIM), lambda b, i: (0, 0)),
                  pl.BlockSpec((1, HEAD_DIM), lambda b, i: (0, 0))],
        out_specs=pl.BlockSpec((1, T, FOX_WIDTH), lambda b, i: (b, i, 0)),
        scratch_shapes=[pltpu.VMEM((S, FOX_WIDTH), BF16),
                        pltpu.VMEM((S, FOX_WIDTH), BF16),
                        pltpu.VMEM((S, LANES), F32),
                        pltpu.VMEM((S // T, 8, T), F32)],
        compiler_params=_cparams(("arbitrary", "arbitrary")),
        name="fox_attention",
    )(y, y, y, y, bf, q_g.reshape(1, HEAD_DIM), k_g.reshape(1, HEAD_DIM))


def _extract_top(work, iota, k):
    nrows = work.shape[0]
    vals = []
    for _ in range(k):
        m = jnp.max(work, axis=0, keepdims=True)
        idx = jnp.min(jnp.where(work == m, iota, nrows), axis=0, keepdims=True)
        work = jnp.where(iota == idx, -jnp.inf, work)
        vals.append(m)
    return vals, work


def _peer_kernel(x_ref, sh_ref, sc_ref, gt_ref, ng_ref, wq_ref, keys_ref, u_ref, vt_ref, o_ref,
                 ht_scr, qt_scr, sp_scr, s0_scr, e0_scr, s1_scr, e1_scr, tau_scr, a_scr, p_scr, acc_scr):
    c = pl.program_id(2)
    nc = pl.num_programs(2)
    tn = x_ref.shape[1]
    nlt = tn // LANES
    half = PEER_QDIM // 2

    @pl.when(c == 0)
    def _prep():
        h = _rms(x_ref[0], ng_ref[...]) * (1.0 + sc_ref[0]) + sh_ref[0]
        q = _dot(h, wq_ref[...], HIGHEST)
        qt_scr[...] = q.T
        ht_scr[...] = h.T.astype(BF16)
        acc_scr[...] = jnp.zeros_like(acc_scr)
        iota_k = lax.broadcasted_iota(jnp.int32, (PEER_NKEYS, LANES), 0)
        iota_c = lax.broadcasted_iota(jnp.int32, (PEER_TOPK + 8 * 8, LANES), 0)

        def head_body(hd, carry):
            for p in range(2):
                qs = qt_scr[pl.ds(pl.multiple_of((hd * 2 + p) * half, half), half), :]
                sp_scr[p] = _dot(keys_ref[p], qs, HIGHEST)
            for lt in range(nlt):
                lanes = slice(lt * LANES, (lt + 1) * LANES)
                s0 = sp_scr[0, :, lanes]
                s1 = sp_scr[1, :, lanes]
                sv0, w0 = _extract_top(s0, iota_k, PEER_TOPK)
                sv1, w1 = _extract_top(s1, iota_k, PEER_TOPK)
                s0m = jnp.where(w0 == -jnp.inf, s0, -jnp.inf)
                s1m = jnp.where(w1 == -jnp.inf, s1, -jnp.inf)
                sv0_all = jnp.concatenate(sv0, axis=0)
                sv1_all = jnp.concatenate(sv1, axis=0)
                parts = [sv0[0] + sv1_all, sv0[1] + sv1_all[:8]]
                for r0 in range(2, 8):
                    parts.append(jnp.where(iota_c[:8] < PEER_TOPK // (r0 + 1), sv0[r0] + sv1_all[:8], -jnp.inf))
                parts.append(sv0_all[8:] + sv1[0])
                cv, _ = _extract_top(jnp.concatenate(parts, axis=0), iota_c, PEER_TOPK)
                z = cv[0] - cv[0]
                for r in range(PEER_TOPK):
                    z = z + jnp.exp(cv[r] - cv[0])
                s0_scr[hd, :, lanes] = s0m
                s1_scr[hd, :, lanes] = s1m
                e0_scr[hd, :, lanes] = jnp.exp(s0m - sv0[0])
                e1_scr[hd, :, lanes] = jnp.exp(s1m - sv1[0]) / z
                tau_scr[hd, :, lanes] = jnp.broadcast_to(cv[PEER_TOPK - 1], (8, LANES))
            return carry

        lax.fori_loop(0, PEER_HEADS, head_body, 0)

    a_scr[...] = _dot(u_ref[...], ht_scr[...])
    nsub = PEER_EC // PEER_NKEYS
    i0 = pl.ds(pl.multiple_of(c * nsub, nsub), nsub)
    nrq = 32

    def weights_tile(rq, carry, lt):
        lanes = slice(lt * LANES, (lt + 1) * LANES)
        jr = pl.ds(pl.multiple_of(rq * nrq, nrq), nrq)
        w = [jnp.zeros((nrq, LANES), F32) for _ in range(nsub)]
        for hd in range(PEER_HEADS):
            s1 = s1_scr[hd, jr, lanes]
            e1 = e1_scr[hd, jr, lanes]
            s0 = s0_scr[hd, i0, lanes]
            e0 = e0_scr[hd, i0, lanes]
            tau = tau_scr[hd, 0:1, lanes]
            for ii in range(nsub):
                keep = s1 + s0[ii:ii + 1] >= tau
                w[ii] = w[ii] + jnp.where(keep, e1 * e0[ii:ii + 1], 0.0)
        for ii in range(nsub):
            rows = pl.ds(pl.multiple_of(ii * PEER_NKEYS + rq * nrq, nrq), nrq)
            act = jax.nn.gelu(a_scr[rows, lanes], approximate=True)
            p_scr[rows, lanes] = (w[ii] * act).astype(BF16)
        return carry

    for lt in range(nlt):
        lax.fori_loop(0, PEER_NKEYS // nrq, functools.partial(weights_tile, lt=lt), 0)
    acc_scr[...] += _dot(vt_ref[...], p_scr[...])

    @pl.when(c == nc - 1)
    def _fin():
        o_ref[0] = x_ref[0] + gt_ref[0] * acc_scr[...].T


def _peer(x, sh, sc, gate, ng, w_q, keys, u_b, vt_b):
    B, S, D = x.shape
    tn = PEER_TN
    nc = PEER_EXPERTS // PEER_EC
    tab = pltpu.VMEM((PEER_HEADS, PEER_NKEYS, tn), F32)
    return pl.pallas_call(
        _peer_kernel,
        out_shape=jax.ShapeDtypeStruct((B, S, D), F32),
        grid=(B, S // tn, nc),
        in_specs=[pl.BlockSpec((1, tn, D), lambda b, i, c: (b, i, 0)),
                  pl.BlockSpec((1, 1, D), lambda b, i, c: (b, 0, 0)),
                  pl.BlockSpec((1, 1, D), lambda b, i, c: (b, 0, 0)),
                  pl.BlockSpec((1, 1, D), lambda b, i, c: (b, 0, 0)),
                  pl.BlockSpec((1, D), lambda b, i, c: (0, 0)),
                  pl.BlockSpec((D, PEER_HEADS * PEER_QDIM), lambda b, i, c: (0, 0)),
                  pl.BlockSpec((2, PEER_NKEYS, PEER_QDIM // 2), lambda b, i, c: (0, 0, 0)),
                  pl.BlockSpec((PEER_EC, D), lambda b, i, c: (c, 0)),
                  pl.BlockSpec((D, PEER_EC), lambda b, i, c: (0, c))],
        out_specs=pl.BlockSpec((1, tn, D), lambda b, i, c: (b, i, 0)),
        scratch_shapes=[pltpu.VMEM((D, tn), BF16),
                        pltpu.VMEM((PEER_HEADS * PEER_QDIM, tn), F32),
                        pltpu.VMEM((2, PEER_NKEYS, tn), F32),
                        tab, tab, tab, tab,
                        pltpu.VMEM((PEER_HEADS, 8, tn), F32),
                        pltpu.VMEM((PEER_EC, tn), F32),
                        pltpu.VMEM((PEER_EC, tn), BF16),
                        pltpu.VMEM((D, tn), F32)],
        compiler_params=_cparams(("arbitrary", "arbitrary", "arbitrary")),
        name="peer",
    )(x, sh, sc, gate, ng, w_q, keys, u_b, vt_b)


def _rel_bucket(dist):
    n = jnp.maximum(dist, 0)
    max_exact = REL_BUCKETS // 2
    nf = jnp.maximum(n, 1).astype(F32)
    large = max_exact + (jnp.log(nf / max_exact) / math.log(REL_MAX_DIST / max_exact)
                         * (REL_BUCKETS - max_exact)).astype(jnp.int32)
    large = jnp.minimum(large, REL_BUCKETS - 1)
    return jnp.where(n < max_exact, n, large)


def _bias_lookup(bucket, tab_ref, h):
    bias = jnp.zeros(bucket.shape, F32)
    for b in range(REL_BUCKETS):
        bias = jnp.where(bucket == b, tab_ref[b, h], bias)
    return bias


N_BIAS_TILES = 5


def _relbias_kernel(tab_ref, o_ref):
    h = pl.program_id(0)
    T = o_ref.shape[2]
    a = lax.broadcasted_iota(jnp.int32, (T, T), 0)
    b = lax.broadcasted_iota(jnp.int32, (T, T), 1)
    last = tab_ref[REL_BUCKETS - 1, h]
    o_ref[0, 0] = jnp.where(a >= b, _bias_lookup(_rel_bucket(a - b), tab_ref, h), NEG_INF)
    o_ref[0, 1] = _bias_lookup(_rel_bucket(a - b + T), tab_ref, h)
    o_ref[0, 2] = jnp.full((T, T), last, F32)
    o_ref[0, 3] = jnp.where(a < b, last, NEG_INF)
    o_ref[0, 4] = jnp.full((T, T), NEG_INF, F32)


def _relbias(rel_table):
    T = ATT_TILE
    assert 2 * T - (T - 1) >= REL_MAX_DIST
    assert WINDOW % T == 0
    return pl.pallas_call(
        _relbias_kernel,
        out_shape=jax.ShapeDtypeStruct((NSA_HEADS, N_BIAS_TILES, T, T), F32),
        grid=(NSA_HEADS,),
        in_specs=[pl.BlockSpec(memory_space=pltpu.SMEM)],
        out_specs=pl.BlockSpec((1, N_BIAS_TILES, T, T), lambda h: (h, 0, 0, 0)),
        compiler_params=_cparams(("arbitrary",)),
        name="rel_bias_tiles",
    )(rel_table)


def _cmp_kernel(x_ref, pos_ref, w1_ref, w2_ref, kg_ref, o_ref):
    j = pl.program_id(1)
    half = CMP_STRIDE * HEAD_DIM
    x = x_ref[0, 0]
    pos = pos_ref[0]
    pa = _dot((x + pos[:, :half]).astype(BF16), w1_ref[0, :half, :])
    pb = _dot((x + pos[:, half:]).astype(BF16), w1_ref[0, half:, :])
    pre = pa + pltpu.roll(pb, pb.shape[0] - 1, axis=0)
    out = _dot(jax.nn.gelu(pre, approximate=True).astype(BF16), w2_ref[0])
    o_ref[0, 0] = jnp.where(j < NSA_GROUPS, _rms(out, kg_ref[...]), out)


def _nsa_compress(xr, pos, w1, w2, kg0):
    B, _, M, W = xr.shape
    G = NSA_GROUPS
    return pl.pallas_call(
        _cmp_kernel,
        out_shape=jax.ShapeDtypeStruct((B, 2 * G, M, HEAD_DIM), F32),
        grid=(B, 2 * G),
        in_specs=[pl.BlockSpec((1, 1, M, W), lambda b, j: (b, j, 0, 0)),
                  pl.BlockSpec((1, 1, 2 * W), lambda b, j: (j // G, 0, 0)),
                  pl.BlockSpec((1, 2 * W, HEAD_DIM), lambda b, j: (j // G, 0, 0)),
                  pl.BlockSpec((1, HEAD_DIM, HEAD_DIM), lambda b, j: (j // G, 0, 0)),
                  pl.BlockSpec((1, HEAD_DIM), lambda b, j: (0, 0))],
        out_specs=pl.BlockSpec((1, 1, M, HEAD_DIM), lambda b, j: (b, j, 0, 0)),
        compiler_params=_cparams(("arbitrary", "arbitrary")),
        name="nsa_compress",
    )(xr, pos.reshape(2, 1, 2 * W), w1.astype(BF16), w2.astype(BF16), kg0.reshape(1, HEAD_DIM))


def _cmp_attn_kernel(q_ref, kv_ref, ov_ref, tab_ref, qg_ref, oc_ref, sel_ref, qn_ref):
    qi = pl.program_id(1)
    T = q_ref.shape[1]
    NC = kv_ref.shape[2]
    NS = ov_ref.shape[1]
    G, J = NSA_GROUPS, NSA_HPG
    t = qi * T + lax.broadcasted_iota(jnp.int32, (T, NC), 0)
    n = lax.broadcasted_iota(jnp.int32, (T, NC), 1)
    d = t - (n * CMP_STRIDE + CMP_LEN - 1)
    vis = d >= 0
    visf = vis.astype(F32)
    bucket = _rel_bucket(d)
    tt = qi * T + lax.broadcasted_iota(jnp.int32, (T, NS), 0)
    mm = lax.broadcasted_iota(jnp.int32, (T, NS), 1)
    cur = tt // SLC_LEN
    forced = (mm == 0) | (mm == cur) | (mm == cur - 1)
    past = mm * SLC_LEN <= tt
    for g in range(G):
        kc = kv_ref[0, g].astype(BF16)
        vc = kv_ref[0, G + g].astype(BF16)
        psum = jnp.zeros((T, NC), F32)
        for j in range(J):
            h = g * J + j
            cols = slice(h * HEAD_DIM, (h + 1) * HEAD_DIM)
            qh = (_rms(q_ref[0, :, cols], qg_ref[...]) * SCALE).astype(BF16)
            qn_ref[0, :, cols] = qh
            s = _dot_nt(qh, kc) + _bias_lookup(bucket, tab_ref, h)
            s = jnp.where(vis, s, NEG_INF)
            e = jnp.exp(s - jnp.max(s, axis=-1, keepdims=True))
            p = e / jnp.sum(e, axis=-1, keepdims=True) * visf
            oc_ref[0, :, cols] = _dot(p.astype(BF16), vc)
            psum = psum + p
        imp = _dot(psum, ov_ref[...], HIGHEST)
        imp = jnp.where(forced, FORCE_SCORE, jnp.where(past, imp, -1.0))
        rank = jnp.zeros((T, NS), jnp.int32)
        for mp in range(NS):
            col = imp[:, mp:mp + 1]
            ahead = (col > imp) | ((col == imp) & (mm > mp))
            rank = rank + ahead.astype(jnp.int32)
        sel = (rank < SLC_TOPN) & (imp >= 0.0)
        sel_ref[0, :, g * NS:(g + 1) * NS] = sel.astype(F32)


def _nsa_cmp_attn(y, kcvc, overlap, rel_table, q_g):
    B, S, _ = y.shape
    T = ATT_TILE
    NC = kcvc.shape[2]
    NS = overlap.shape[1]
    return pl.pallas_call(
        _cmp_attn_kernel,
        out_shape=(jax.ShapeDtypeStruct((B, S, NSA_WIDTH), F32),
                   jax.ShapeDtypeStruct((B, S, NSA_GROUPS * NS), F32),
                   jax.ShapeDtypeStruct((B, S, NSA_WIDTH), BF16)),
        grid=(B, S // T),
        in_specs=[pl.BlockSpec((1, T, NSA_WIDTH), lambda b, i: (b, i, 0)),
                  pl.BlockSpec((1, 2 * NSA_GROUPS, NC, HEAD_DIM), lambda b, i: (b, 0, 0, 0)),
                  pl.BlockSpec((NC, NS), lambda b, i: (0, 0)),
                  pl.BlockSpec(memory_space=pltpu.SMEM),
                  pl.BlockSpec((1, HEAD_DIM), lambda b, i: (0, 0))],
        out_specs=(pl.BlockSpec((1, T, NSA_WIDTH), lambda b, i: (b, i, 0)),
                   pl.BlockSpec((1, T, NSA_GROUPS * NS), lambda b, i: (b, i, 0)),
                   pl.BlockSpec((1, T, NSA_WIDTH), lambda b, i: (b, i, 0))),
        compiler_params=_cparams(("arbitrary", "arbitrary")),
        name="nsa_cmp_attn",
    )(y, kcvc, overlap, rel_table, q_g.reshape(1, HEAD_DIM))


def _kv_prep_kernel(ks_ref, kw_ref, kg1_ref, kg2_ref, ksn_ref, vs_ref, kwn_ref, vw_ref):
    KV = NSA_KV
    for g in range(NSA_GROUPS):
        cols = slice(g * HEAD_DIM, (g + 1) * HEAD_DIM)
        ksn_ref[0, :, cols] = _rms(ks_ref[0, :, cols], kg1_ref[...]).astype(BF16)
        kwn_ref[0, :, cols] = _rms(kw_ref[0, :, cols], kg2_ref[...]).astype(BF16)
    vs_ref[0] = ks_ref[0, :, KV:].astype(BF16)
    vw_ref[0] = kw_ref[0, :, KV:].astype(BF16)


def _nsa_kv_prep(y, k_g):
    B, S, _ = y.shape
    ksblk = (NSA_WIDTH + 2 * NSA_KV) // (2 * NSA_KV)
    out = jax.ShapeDtypeStruct((B, S, NSA_KV), BF16)
    ospec = pl.BlockSpec((1, ROW_TILE, NSA_KV), lambda b, i: (b, i, 0))
    return pl.pallas_call(
        _kv_prep_kernel,
        out_shape=(out, out, out, out),
        grid=(B, S // ROW_TILE),
        in_specs=[pl.BlockSpec((1, ROW_TILE, 2 * NSA_KV), lambda b, i: (b, i, ksblk)),
                  pl.BlockSpec((1, ROW_TILE, 2 * NSA_KV), lambda b, i: (b, i, ksblk + 1)),
                  pl.BlockSpec((1, HEAD_DIM), lambda b, i: (0, 0)),
                  pl.BlockSpec((1, HEAD_DIM), lambda b, i: (0, 0))],
        out_specs=(ospec, ospec, ospec, ospec),
        compiler_params=_cparams(("arbitrary", "arbitrary")),
        name="nsa_kv_prep",
    )(y, y, k_g[1].reshape(1, HEAD_DIM), k_g[2].reshape(1, HEAD_DIM))


def _sel_win_kernel(qn_ref, ks_ref, vs_ref, kw_ref, vw_ref, gl_ref, sel_ref, oc_ref, bias_ref, e_ref, bg_ref,
                    o_ref, mk_scr):
    qi = pl.program_id(1)
    T = qn_ref.shape[1]
    S = ks_ref.shape[1]
    npair = S // (2 * T)
    G, J = NSA_GROUPS, NSA_HPG
    NS = e_ref.shape[0]
    wt = WINDOW // T

    def sel_tile(r):
        return jnp.where(r < 0, 4, jnp.minimum(r, 2))

    def win_tile(r):
        return jnp.where(r < 0, 4, jnp.where(r <= 1, r, jnp.where(r < wt, 2, jnp.where(r == wt, 3, 4))))

    def bias4(g, i0, i1):
        return jnp.concatenate(
            [jnp.concatenate([bias_ref[g * J + j, i0], bias_ref[g * J + j, i1]], axis=1) for j in range(J)],
            axis=0)

    gates = jax.nn.sigmoid(gl_ref[0] + bg_ref[...])
    init = (jnp.full((J * T, 1), NEG_INF, F32), jnp.zeros((J * T, 1), F32), jnp.zeros((J * T, HEAD_DIM), F32))
    kw0 = jnp.maximum(qi - wt, 0) // 2
    for g in range(G):
        gcols = slice(g * HEAD_DIM, (g + 1) * HEAD_DIM)
        selg = sel_ref[0, :, g * NS:(g + 1) * NS].astype(BF16)
        for kp in range(npair):
            mk_scr[kp] = (_dot(selg, e_ref[:, kp * 2 * T:(kp + 1) * 2 * T]) - 1.0) * (-NEG_INF)
        q4 = jnp.concatenate([qn_ref[0, :, (g * J + j) * HEAD_DIM:(g * J + j + 1) * HEAD_DIM] for j in range(J)],
                             axis=0)

        def sel_step(kp, carry, q4=q4, gcols=gcols, g=g):
            ks = pl.ds(pl.multiple_of(kp * 2 * T, 2 * T), 2 * T)
            r0 = qi - 2 * kp
            s = _dot_nt(q4, ks_ref[0, ks, gcols]) + bias4(g, sel_tile(r0), sel_tile(r0 - 1))
            s = s + jnp.concatenate([mk_scr[kp]] * J, axis=0)
            return _softmax_step(carry, s, vs_ref[0, ks, gcols])

        def win_step(kp, carry, q4=q4, gcols=gcols, g=g):
            ks = pl.ds(pl.multiple_of(kp * 2 * T, 2 * T), 2 * T)
            r0 = qi - 2 * kp
            s = _dot_nt(q4, kw_ref[0, ks, gcols]) + bias4(g, win_tile(r0), win_tile(r0 - 1))
            return _softmax_step(carry, s, vw_ref[0, ks, gcols])

        cs = lax.fori_loop(0, kw0, sel_step, init)
        cs, cw = lax.fori_loop(kw0, qi // 2 + 1,
                               lambda kp, c: (sel_step(kp, c[0]), win_step(kp, c[1])), (cs, init))
        o_s = cs[2] / cs[1]
        o_w = cw[2] / cw[1]
        for j in range(J):
            h = g * J + j
            cols = slice(h * HEAD_DIM, (h + 1) * HEAD_DIM)
            rows = slice(j * T, (j + 1) * T)
            o_ref[0, :, cols] = (gates[:, 3 * h:3 * h + 1] * oc_ref[0, :, cols]
                                 + gates[:, 3 * h + 1:3 * h + 2] * o_s[rows]
                                 + gates[:, 3 * h + 2:3 * h + 3] * o_w[rows])


def _nsa_sel_win(qn, y, kv, sel, oc, bias_tiles, expand, b_gate):
    B, S, _ = y.shape
    T = ATT_TILE
    glblk = (NSA_WIDTH + 6 * NSA_KV) // LANES
    bg = jnp.zeros((1, LANES), F32).at[0, :3 * NSA_HEADS].set(b_gate)
    kvspec = pl.BlockSpec((1, S, NSA_KV), lambda b, i: (b, 0, 0))
    return pl.pallas_call(
        _sel_win_kernel,
        out_shape=jax.ShapeDtypeStruct((B, S, NSA_WIDTH), F32),
        grid=(B, S // T),
        in_specs=[pl.BlockSpec((1, T, NSA_WIDTH), lambda b, i: (b, i, 0)),
                  kvspec, kvspec, kvspec, kvspec,
                  pl.BlockSpec((1, T, LANES), lambda b, i: (b, i, glblk)),
                  pl.BlockSpec((1, T, sel.shape[2]), lambda b, i: (b, i, 0)),
                  pl.BlockSpec((1, T, NSA_WIDTH), lambda b, i: (b, i, 0)),
                  pl.BlockSpec(bias_tiles.shape, lambda b, i: (0, 0, 0, 0)),
                  pl.BlockSpec(expand.shape, lambda b, i: (0, 0)),
                  pl.BlockSpec((1, LANES), lambda b, i: (0, 0))],
        out_specs=pl.BlockSpec((1, T, NSA_WIDTH), lambda b, i: (b, i, 0)),
        scratch_shapes=[pltpu.VMEM((S // (2 * T), T, 2 * T), F32)],
        compiler_params=_cparams(("arbitrary", "arbitrary")),
        name="nsa_sel_win",
    )(qn, *kv, y, sel, oc, bias_tiles, expand, bg)


def _odd_mixer(x, sh, sc, gate, ng, w_in, b_gate, q_g, k_g, cmp_pos, cmp_w1, cmp_w2, rel_table, bias_tiles,
               w_out):
    B, S, _ = x.shape
    G = NSA_GROUPS
    w = jnp.zeros((D_MODEL, ODD_IN_PAD), BF16).at[:, :ODD_IN].set(w_in.astype(BF16))
    y = _norm_proj(x, sh, sc, ng, w)
    xr = y[:, :, NSA_WIDTH:NSA_WIDTH + 2 * NSA_KV].reshape(B, S, 2, G, HEAD_DIM)
    xr = xr.transpose(0, 2, 3, 1, 4).reshape(B, 2 * G, S // CMP_STRIDE, CMP_STRIDE * HEAD_DIM)
    kcvc = _nsa_compress(xr, cmp_pos.reshape(2, CMP_LEN * HEAD_DIM), cmp_w1, cmp_w2, k_g[0])
    n_slc = S // SLC_LEN
    starts = jnp.arange(S // CMP_STRIDE) * CMP_STRIDE
    bstart = jnp.arange(n_slc) * SLC_LEN
    real = (jnp.arange(S // CMP_STRIDE) < (S - CMP_LEN) // CMP_STRIDE + 1)[:, None]
    overlap = (real & (starts[:, None] < bstart[None, :] + SLC_LEN)
               & (starts[:, None] + CMP_LEN > bstart[None, :])).astype(F32)
    expand = (jnp.arange(S)[None, :] // SLC_LEN == jnp.arange(n_slc)[:, None]).astype(BF16)
    oc, sel, qn = _nsa_cmp_attn(y, kcvc, overlap, rel_table, q_g)
    o = _nsa_sel_win(qn, y, _nsa_kv_prep(y, k_g), sel, oc, bias_tiles, expand, b_gate)
    return _out_proj(x, o, 0, o, 1, w_out.astype(BF16), gate)


def _even_mixer(x, sh, sc, gate, ng, w_in, b_f, conv_w, q_g, k_g, w_out):
    w = jnp.zeros((D_MODEL, EVEN_IN_PAD), BF16).at[:, :EVEN_IN].set(w_in.astype(BF16))
    y = _norm_proj(x, sh, sc, ng, w)
    conv = _short_conv(y, conv_w)
    attn = _fox_attention(y, b_f, q_g, k_g)
    return _out_proj(x, conv, 0, attn, 0, w_out.astype(BF16), gate)


def kernel(x, c, ada_w, ada_b, norm_g, even_w_in, even_b_f, even_conv_w, even_q_g, even_k_g, even_w_out, odd_w_in, odd_b_gate, odd_q_g, odd_k_g, odd_cmp_pos, odd_cmp_w1, odd_cmp_w2, odd_w_out, rel_table, peer_w_q, peer_keys, peer_u, peer_v):
    B = x.shape[0]
    mods = _ada(c, ada_w, ada_b)
    bias_tiles = _relbias(rel_table)
    for layer in range(DEPTH):
        sh1, sc1, g1, sh2, sc2, g2 = [m.reshape(B, 1, D_MODEL) for m in jnp.split(mods[layer], 6, axis=-1)]
        i = layer // 2
        ng1 = norm_g[layer, 0].reshape(1, D_MODEL)
        ng2 = norm_g[layer, 1].reshape(1, D_MODEL)
        if layer % 2 == 0:
            x = _even_mixer(x, sh1, sc1, g1, ng1, even_w_in[i], even_b_f[i], even_conv_w[i], even_q_g[i],
                            even_k_g[i], even_w_out[i])
        else:
            x = _odd_mixer(x, sh1, sc1, g1, ng1, odd_w_in[i], odd_b_gate[i], odd_q_g[i], odd_k_g[i],
                           odd_cmp_pos[i], odd_cmp_w1[i], odd_cmp_w2[i], rel_table, bias_tiles, odd_w_out[i])
        x = _peer(x, sh2, sc2, g2, ng2, peer_w_q[layer], peer_keys[layer],
                  peer_u[layer].astype(BF16), peer_v[layer].T.astype(BF16))
    return x
```

```python
import functools
import math

import jax
import jax.numpy as jnp
from jax import lax
from jax.experimental import pallas as pl
from jax.experimental.pallas import tpu as pltpu

D_MODEL = 1024
DEPTH = 4
HEAD_DIM = 64
CONV_CH = 512
CONV_K = 3
FOX_HEADS = 8
FOX_WIDTH = FOX_HEADS * HEAD_DIM
EVEN_IN = 3 * CONV_CH + 3 * FOX_WIDTH + FOX_HEADS
NSA_GROUPS = 4
NSA_HPG = 4
NSA_HEADS = NSA_GROUPS * NSA_HPG
NSA_WIDTH = NSA_HEADS * HEAD_DIM
NSA_KV = NSA_GROUPS * HEAD_DIM
CMP_LEN = 32
CMP_STRIDE = 16
SLC_LEN = 64
SLC_TOPN = 16
WINDOW = 512
ODD_IN = NSA_WIDTH + 6 * NSA_KV + 3 * NSA_HEADS
REL_BUCKETS = 32
REL_MAX_DIST = 128
PEER_HEADS = 8
PEER_NKEYS = 128
PEER_EXPERTS = PEER_NKEYS * PEER_NKEYS
PEER_QDIM = 128
PEER_TOPK = 16
EPS = 1e-6
NEG_INF = -1e30
FORCE_SCORE = 1e6

LANES = 128
F32 = jnp.float32
BF16 = jnp.bfloat16
HIGHEST = lax.Precision.HIGHEST
VMEM_LIMIT = 56 * 1024 * 1024

EVEN_IN_PAD = 3200
ODD_IN_PAD = 2688
ROW_TILE = 512
ATT_TILE = 128
FOX_TILE = 256
PEER_TN = 512
PEER_EC = 1024
SCALE = HEAD_DIM ** -0.5


def _cparams(sem, flags=None):
    return pltpu.CompilerParams(dimension_semantics=sem, vmem_limit_bytes=VMEM_LIMIT, flags=flags)


def _dot(a, b, precision=None):
    return jnp.dot(a, b, precision=precision, preferred_element_type=F32)


def _dot_nt(a, b, precision=None):
    return lax.dot_general(a, b, (((1,), (1,)), ((), ())), precision=precision,
                           preferred_element_type=F32)


def _rms(x, g):
    return x * lax.rsqrt(jnp.mean(x * x, axis=-1, keepdims=True) + EPS) * g


def _ada_kernel(c_ref, w_ref, b_ref, o_ref):
    c = c_ref[...]
    sc = c * jax.nn.sigmoid(c)
    o_ref[0] = _dot(sc, w_ref[0], HIGHEST) + b_ref[0]


def _ada(c, ada_w, ada_b):
    B, D = c.shape
    E = ada_w.shape[2]
    tn = 2048
    return pl.pallas_call(
        _ada_kernel,
        out_shape=jax.ShapeDtypeStruct((DEPTH, B, E), F32),
        grid=(DEPTH, E // tn),
        in_specs=[pl.BlockSpec((B, D), lambda l, j: (0, 0)),
                  pl.BlockSpec((1, D, tn), lambda l, j: (l, 0, j)),
                  pl.BlockSpec((1, 1, tn), lambda l, j: (l, 0, j))],
        out_specs=pl.BlockSpec((1, B, tn), lambda l, j: (l, 0, j)),
        compiler_params=_cparams(("arbitrary", "arbitrary")),
        name="ada_mod",
    )(c, ada_w, ada_b.reshape(DEPTH, 1, E))


def _norm_proj_kernel(x_ref, sh_ref, sc_ref, ng_ref, w_ref, y_ref):
    h = _rms(x_ref[0], ng_ref[...]) * (1.0 + sc_ref[0]) + sh_ref[0]
    y_ref[0] = _dot(h.astype(BF16), w_ref[...])


def _norm_proj(x, sh, sc, ng, w):
    B, S, D = x.shape
    N = w.shape[1]
    return pl.pallas_call(
        _norm_proj_kernel,
        out_shape=jax.ShapeDtypeStruct((B, S, N), F32),
        grid=(B, S // ROW_TILE),
        in_specs=[pl.BlockSpec((1, ROW_TILE, D), lambda b, i: (b, i, 0)),
                  pl.BlockSpec((1, 1, D), lambda b, i: (b, 0, 0)),
                  pl.BlockSpec((1, 1, D), lambda b, i: (b, 0, 0)),
                  pl.BlockSpec((1, D), lambda b, i: (0, 0)),
                  pl.BlockSpec((D, N), lambda b, i: (0, 0))],
        out_specs=pl.BlockSpec((1, ROW_TILE, N), lambda b, i: (b, i, 0)),
        compiler_params=_cparams(("arbitrary", "arbitrary")),
        name="norm_proj",
    )(x, sh, sc, ng, w)


def _out_proj_kernel(x_ref, a_ref, b_ref, wa_ref, wb_ref, g_ref, o_ref):
    y = _dot(a_ref[0].astype(BF16), wa_ref[...]) + _dot(b_ref[0].astype(BF16), wb_ref[...])
    o_ref[0] = x_ref[0] + g_ref[0] * y


def _out_proj(x, a, a_blk, b, b_blk, w, gate):
    B, S, D = x.shape
    half = w.shape[0] // 2
    wa, wb = w[:half], w[half:]
    return pl.pallas_call(
        _out_proj_kernel,
        out_shape=jax.ShapeDtypeStruct((B, S, D), F32),
        grid=(B, S // ROW_TILE),
        in_specs=[pl.BlockSpec((1, ROW_TILE, D), lambda b, i: (b, i, 0)),
                  pl.BlockSpec((1, ROW_TILE, half), lambda b, i: (b, i, a_blk)),
                  pl.BlockSpec((1, ROW_TILE, half), lambda b, i: (b, i, b_blk)),
                  pl.BlockSpec((half, D), lambda b, i: (0, 0)),
                  pl.BlockSpec((half, D), lambda b, i: (0, 0)),
                  pl.BlockSpec((1, 1, D), lambda b, i: (b, 0, 0))],
        out_specs=pl.BlockSpec((1, ROW_TILE, D), lambda b, i: (b, i, 0)),
        compiler_params=_cparams(("arbitrary", "arbitrary")),
        name="out_proj",
    )(x, a, b, wa, wb, gate)


def _conv_kernel(cb_ref, cc_ref, ch_ref, w_ref, o_ref):
    u = cc_ref[0] * ch_ref[0]
    row = lax.broadcasted_iota(jnp.int32, u.shape, 0)
    u1 = jnp.where(row >= 1, pltpu.roll(u, 1, axis=0), 0.0)
    u2 = jnp.where(row >= 2, pltpu.roll(u, 2, axis=0), 0.0)
    w = w_ref[...]
    o_ref[0] = cb_ref[0] * (w[0:1] * u2 + w[1:2] * u1 + w[2:3] * u)


def _short_conv(y, conv_w):
    B, S, _ = y.shape
    nblk = CONV_CH // LANES
    w = jnp.zeros((8, CONV_CH), F32).at[:CONV_K].set(conv_w)
    return pl.pallas_call(
        _conv_kernel,
        out_shape=jax.ShapeDtypeStruct((B, S, CONV_CH), F32),
        grid=(B, nblk),
        in_specs=[pl.BlockSpec((1, S, LANES), lambda b, j: (b, 0, j)),
                  pl.BlockSpec((1, S, LANES), lambda b, j: (b, 0, nblk + j)),
                  pl.BlockSpec((1, S, LANES), lambda b, j: (b, 0, 2 * nblk + j)),
                  pl.BlockSpec((8, LANES), lambda b, j: (0, j))],
        out_specs=pl.BlockSpec((1, S, LANES), lambda b, j: (b, 0, j)),
        compiler_params=_cparams(("arbitrary", "arbitrary")),
        name="short_conv",
    )(y, y, y, w)


def _cumsum_rows(x):
    n = x.shape[0]
    row = lax.broadcasted_iota(jnp.int32, x.shape, 0)
    sh = 1
    while sh < n:
        x = x + jnp.where(row >= sh, pltpu.roll(x, sh, axis=0), 0.0)
        sh *= 2
    return x


def _softmax_step_ones(carry, s, v):
    m, acc = carry
    m_new = jnp.maximum(m, jnp.max(s, axis=-1, keepdims=True))
    p = jnp.exp(s - m_new)
    return m_new, jnp.exp(m - m_new) * acc + _dot(p.astype(BF16), v)


def _fox_kernel(q_ref, k_ref, v_ref, f_ref, bf_ref, qg_ref, kg_ref, o_ref,
                kn_scr, vb_scr, cum_scr, cumt_scr):
    qi = pl.program_id(1)
    S = k_ref.shape[1]
    T = q_ref.shape[1]
    nk = S // T

    @pl.when(qi == 0)
    def _prep():
        ones_col = (lax.broadcasted_iota(jnp.int32, (S, LANES - HEAD_DIM), 1) == 0).astype(BF16)
        for h in range(FOX_HEADS):
            cols = slice(h * HEAD_DIM, (h + 1) * HEAD_DIM)
            kn_scr[:, cols] = _rms(k_ref[0, :, cols], kg_ref[...]).astype(BF16)
            vb_scr[:, h * LANES:h * LANES + HEAD_DIM] = v_ref[0, :, cols].astype(BF16)
            vb_scr[:, h * LANES + HEAD_DIM:(h + 1) * LANES] = ones_col
        logf = jax.nn.log_sigmoid(f_ref[0] + bf_ref[...])
        cum = _cumsum_rows(logf)
        cum_scr[...] = cum
        for kt in range(nk):
            cumt_scr[kt] = cum[kt * T:(kt + 1) * T, :].T[0:8, :]

    causal = jnp.where(lax.broadcasted_iota(jnp.int32, (T, T), 0) >= lax.broadcasted_iota(jnp.int32, (T, T), 1),
                       0.0, NEG_INF)
    cq = cum_scr[pl.ds(pl.multiple_of(qi * T, T), T), :]
    heads = []
    for h in range(FOX_HEADS):
        cols = slice(h * HEAD_DIM, (h + 1) * HEAD_DIM)
        heads.append((cols, (_rms(q_ref[0, :, cols], qg_ref[...]) * SCALE).astype(BF16), cq[:, h:h + 1]))

    def step(kt, carry, diag):
        ks = pl.ds(pl.multiple_of(kt * T, T), T)
        ctile = cumt_scr[kt]
        out = []
        for h, (cols, qh, ct) in enumerate(heads):
            s = _dot_nt(qh, kn_scr[ks, cols]) + (ct - ctile[h:h + 1, :])
            if diag:
                s = s + causal
            out.append(_softmax_step_ones(carry[h], s, vb_scr[ks, h * LANES:(h + 1) * LANES]))
        return tuple(out)

    init = tuple((jnp.full((T, 1), NEG_INF, F32), jnp.zeros((T, LANES), F32)) for _ in range(FOX_HEADS))
    carry = lax.fori_loop(0, qi, lambda kt, c: step(kt, c, False), init)
    carry = step(qi, carry, True)
    for h, (cols, _, _) in enumerate(heads):
        o_ref[0, :, cols] = carry[h][1][:, :HEAD_DIM] / carry[h][1][:, HEAD_DIM:HEAD_DIM + 1]


def _fox_attention(y, b_f, q_g, k_g):
    B, S, _ = y.shape
    T = FOX_TILE
    qblk = 3 * CONV_CH // FOX_WIDTH
    fblk = (3 * CONV_CH + 3 * FOX_WIDTH) // LANES
    bf = jnp.zeros((1, LANES), F32).at[0, :FOX_HEADS].set(b_f)
    return pl.pallas_call(
        _fox_kernel,
        out_shape=jax.ShapeDtypeStruct((B, S, FOX_WIDTH), F32),
        grid=(B, S // T),
        in_specs=[pl.BlockSpec((1, T, FOX_WIDTH), lambda b, i: (b, i, qblk)),
                  pl.BlockSpec((1, S, FOX_WIDTH), lambda b, i: (b, 0, qblk + 1)),
                  pl.BlockSpec((1, S, FOX_WIDTH), lambda b, i: (b, 0, qblk + 2)),
                  pl.BlockSpec((1, S, LANES), lambda b, i: (b, 0, fblk)),
                  pl.BlockSpec((1, LANES), lambda b, i: (0, 0)),
                  pl.BlockSpec((1, HEAD_DIM), lambda b, i: (0, 0)),
                  pl.BlockSpec((1, HEAD_DIM), lambda b, i: (0, 0))],
        out_specs=pl.BlockSpec((1, T, FOX_WIDTH), lambda b, i: (b, i, 0)),
        scratch_shapes=[pltpu.VMEM((S, FOX_WIDTH), BF16),
                        pltpu.VMEM((S, FOX_HEADS * LANES), BF16),
                        pltpu.VMEM((S, LANES), F32),
                        pltpu.VMEM((S // T, 8, T), F32)],
        compiler_params=_cparams(("arbitrary", "arbitrary")),
        name="fox_attention",
    )(y, y, y, y, bf, q_g.reshape(1, HEAD_DIM), k_g.reshape(1, HEAD_DIM))


def _extract_top(work, iota, k):
    nrows = work.shape[0]
    vals = []
    for _ in range(k):
        m = jnp.max(work, axis=0, keepdims=True)
        idx = jnp.min(jnp.where(work == m, iota, nrows), axis=0, keepdims=True)
        work = jnp.where(iota == idx, -jnp.inf, work)
        vals.append(m)
    return vals, work


def _peer_kernel(x_ref, sh_ref, sc_ref, gt_ref, ng_ref, wq_ref, keys_ref, u_ref, vt_ref, o_ref,
                 ht_scr, qt_scr, sp_scr, s0_scr, e0_scr, s1_scr, e1_scr, tau_scr, a_scr, p_scr, acc_scr):
    c = pl.program_id(2)
    nc = pl.num_programs(2)
    tn = x_ref.shape[1]
    nlt = tn // LANES
    half = PEER_QDIM // 2

    @pl.when(c == 0)
    def _prep():
        h = _rms(x_ref[0], ng_ref[...]) * (1.0 + sc_ref[0]) + sh_ref[0]
        q = _dot(h, wq_ref[...], HIGHEST)
        qt_scr[...] = q.T
        ht_scr[...] = h.T.astype(BF16)
        acc_scr[...] = jnp.zeros_like(acc_scr)
        iota_k = lax.broadcasted_iota(jnp.int32, (PEER_NKEYS, LANES), 0)
        iota_c = lax.broadcasted_iota(jnp.int32, (PEER_TOPK + 8 * 8, LANES), 0)

        def head_body(hd, carry):
            for p in range(2):
                qs = qt_scr[pl.ds(pl.multiple_of((hd * 2 + p) * half, half), half), :]
                sp_scr[p] = _dot(keys_ref[p], qs, HIGHEST)
            for lt in range(nlt):
                lanes = slice(lt * LANES, (lt + 1) * LANES)
                s0 = sp_scr[0, :, lanes]
                s1 = sp_scr[1, :, lanes]
                sv0, w0 = _extract_top(s0, iota_k, PEER_TOPK)
                sv1, w1 = _extract_top(s1, iota_k, PEER_TOPK)
                s0m = jnp.where(w0 == -jnp.inf, s0, -jnp.inf)
                s1m = jnp.where(w1 == -jnp.inf, s1, -jnp.inf)
                sv0_all = jnp.concatenate(sv0, axis=0)
                sv1_all = jnp.concatenate(sv1, axis=0)
                parts = [sv0[0] + sv1_all, sv0[1] + sv1_all[:8]]
                for r0 in range(2, 8):
                    parts.append(jnp.where(iota_c[:8] < PEER_TOPK // (r0 + 1), sv0[r0] + sv1_all[:8], -jnp.inf))
                parts.append(sv0_all[8:] + sv1[0])
                cv, _ = _extract_top(jnp.concatenate(parts, axis=0), iota_c, PEER_TOPK)
                z = cv[0] - cv[0]
                for r in range(PEER_TOPK):
                    z = z + jnp.exp(cv[r] - cv[0])
                s0_scr[hd, :, lanes] = s0m
                s1_scr[hd, :, lanes] = s1m
                e0_scr[hd, :, lanes] = jnp.exp(s0m - sv0[0])
                e1_scr[hd, :, lanes] = jnp.exp(s1m - sv1[0]) / z
                tau_scr[hd, :, lanes] = jnp.broadcast_to(cv[PEER_TOPK - 1], (8, LANES))
            return carry

        lax.fori_loop(0, PEER_HEADS, head_body, 0)

    a_scr[...] = _dot(u_ref[...], ht_scr[...])
    nsub = PEER_EC // PEER_NKEYS
    i0 = pl.ds(pl.multiple_of(c * nsub, nsub), nsub)
    nrq = 32

    def weights_tile(rq, carry, lt):
        lanes = slice(lt * LANES, (lt + 1) * LANES)
        jr = pl.ds(pl.multiple_of(rq * nrq, nrq), nrq)
        w = [jnp.zeros((nrq, LANES), F32) for _ in range(nsub)]
        for hd in range(PEER_HEADS):
            s1 = s1_scr[hd, jr, lanes]
            e1 = e1_scr[hd, jr, lanes]
            s0 = s0_scr[hd, i0, lanes]
            e0 = e0_scr[hd, i0, lanes]
            tau = tau_scr[hd, 0:1, lanes]
            for ii in range(nsub):
                keep = s1 + s0[ii:ii + 1] >= tau
                w[ii] = w[ii] + jnp.where(keep, e1 * e0[ii:ii + 1], 0.0)
        for ii in range(nsub):
            rows = pl.ds(pl.multiple_of(ii * PEER_NKEYS + rq * nrq, nrq), nrq)
            act = jax.nn.gelu(a_scr[rows, lanes], approximate=True)
            p_scr[rows, lanes] = (w[ii] * act).astype(BF16)
        return carry

    for lt in range(nlt):
        lax.fori_loop(0, PEER_NKEYS // nrq, functools.partial(weights_tile, lt=lt), 0)
    acc_scr[...] += _dot(vt_ref[...], p_scr[...])

    @pl.when(c == nc - 1)
    def _fin():
        o_ref[0] = x_ref[0] + gt_ref[0] * acc_scr[...].T


def _peer(x, sh, sc, gate, ng, w_q, keys, u_b, vt_b):
    B, S, D = x.shape
    tn = PEER_TN
    nc = PEER_EXPERTS // PEER_EC
    tab = pltpu.VMEM((PEER_HEADS, PEER_NKEYS, tn), F32)
    return pl.pallas_call(
        _peer_kernel,
        out_shape=jax.ShapeDtypeStruct((B, S, D), F32),
        grid=(B, S // tn, nc),
        in_specs=[pl.BlockSpec((1, tn, D), lambda b, i, c: (b, i, 0)),
                  pl.BlockSpec((1, 1, D), lambda b, i, c: (b, 0, 0)),
                  pl.BlockSpec((1, 1, D), lambda b, i, c: (b, 0, 0)),
                  pl.BlockSpec((1, 1, D), lambda b, i, c: (b, 0, 0)),
                  pl.BlockSpec((1, D), lambda b, i, c: (0, 0)),
                  pl.BlockSpec((D, PEER_HEADS * PEER_QDIM), lambda b, i, c: (0, 0)),
                  pl.BlockSpec((2, PEER_NKEYS, PEER_QDIM // 2), lambda b, i, c: (0, 0, 0)),
                  pl.BlockSpec((PEER_EC, D), lambda b, i, c: (c, 0)),
                  pl.BlockSpec((D, PEER_EC), lambda b, i, c: (0, c))],
        out_specs=pl.BlockSpec((1, tn, D), lambda b, i, c: (b, i, 0)),
        scratch_shapes=[pltpu.VMEM((D, tn), BF16),
                        pltpu.VMEM((PEER_HEADS * PEER_QDIM, tn), F32),
                        pltpu.VMEM((2, PEER_NKEYS, tn), F32),
                        tab, tab, tab, tab,
                        pltpu.VMEM((PEER_HEADS, 8, tn), F32),
                        pltpu.VMEM((PEER_EC, tn), F32),
                        pltpu.VMEM((PEER_EC, tn), BF16),
                        pltpu.VMEM((D, tn), F32)],
        compiler_params=_cparams(("arbitrary", "arbitrary", "arbitrary")),
        name="peer",
    )(x, sh, sc, gate, ng, w_q, keys, u_b, vt_b)


def _rel_bucket(dist):
    n = jnp.maximum(dist, 0)
    max_exact = REL_BUCKETS // 2
    nf = jnp.maximum(n, 1).astype(F32)
    large = max_exact + (jnp.log(nf / max_exact) / math.log(REL_MAX_DIST / max_exact)
                         * (REL_BUCKETS - max_exact)).astype(jnp.int32)
    large = jnp.minimum(large, REL_BUCKETS - 1)
    return jnp.where(n < max_exact, n, large)


def _bias_lookup(bucket, tab_ref, h):
    bias = jnp.zeros(bucket.shape, F32)
    for b in range(REL_BUCKETS):
        bias = jnp.where(bucket == b, tab_ref[b, h], bias)
    return bias


N_BIAS_TILES = 5


def _relbias_kernel(tab_ref, o_ref, c_ref):
    h = pl.program_id(0)
    T = o_ref.shape[2]
    a = lax.broadcasted_iota(jnp.int32, (T, T), 0)
    b = lax.broadcasted_iota(jnp.int32, (T, T), 1)
    last = tab_ref[REL_BUCKETS - 1, h]
    o_ref[0, 0] = jnp.where(a >= b, _bias_lookup(_rel_bucket(a - b), tab_ref, h), NEG_INF)
    o_ref[0, 1] = _bias_lookup(_rel_bucket(a - b + T), tab_ref, h)
    o_ref[0, 2] = jnp.full((T, T), last, F32)
    o_ref[0, 3] = jnp.where(a < b, last, NEG_INF)
    o_ref[0, 4] = jnp.full((T, T), NEG_INF, F32)
    dist = a - ((b - T // 2) * CMP_STRIDE + CMP_LEN - 1)
    c_ref[0] = jnp.where(dist >= 0, _bias_lookup(_rel_bucket(dist), tab_ref, h), last)


def _relbias(rel_table):
    T = ATT_TILE
    assert 2 * T - (T - 1) >= REL_MAX_DIST
    assert WINDOW % T == 0
    return pl.pallas_call(
        _relbias_kernel,
        out_shape=(jax.ShapeDtypeStruct((NSA_HEADS, N_BIAS_TILES, T, T), F32),
                   jax.ShapeDtypeStruct((NSA_HEADS, T, T), F32)),
        grid=(NSA_HEADS,),
        in_specs=[pl.BlockSpec(memory_space=pltpu.SMEM)],
        out_specs=(pl.BlockSpec((1, N_BIAS_TILES, T, T), lambda h: (h, 0, 0, 0)),
                   pl.BlockSpec((1, T, T), lambda h: (h, 0, 0))),
        compiler_params=_cparams(("arbitrary",)),
        name="rel_bias_tiles",
    )(rel_table)


def _cmp_kernel(x_ref, pos_ref, w1_ref, w2_ref, kg_ref, o_ref):
    j = pl.program_id(1)
    half = CMP_STRIDE * HEAD_DIM
    x = x_ref[0, 0]
    pos = pos_ref[0]
    pa = _dot((x + pos[:, :half]).astype(BF16), w1_ref[0, :half, :])
    pb = _dot((x + pos[:, half:]).astype(BF16), w1_ref[0, half:, :])
    pre = pa + pltpu.roll(pb, pb.shape[0] - 1, axis=0)
    out = _dot(jax.nn.gelu(pre, approximate=True).astype(BF16), w2_ref[0])
    o_ref[0, 0] = jnp.where(j < NSA_GROUPS, _rms(out, kg_ref[...]), out)


def _nsa_compress(xr, pos, w1, w2, kg0):
    B, _, M, W = xr.shape
    G = NSA_GROUPS
    return pl.pallas_call(
        _cmp_kernel,
        out_shape=jax.ShapeDtypeStruct((B, 2 * G, M, HEAD_DIM), F32),
        grid=(B, 2 * G),
        in_specs=[pl.BlockSpec((1, 1, M, W), lambda b, j: (b, j, 0, 0)),
                  pl.BlockSpec((1, 1, 2 * W), lambda b, j: (j // G, 0, 0)),
                  pl.BlockSpec((1, 2 * W, HEAD_DIM), lambda b, j: (j // G, 0, 0)),
                  pl.BlockSpec((1, HEAD_DIM, HEAD_DIM), lambda b, j: (j // G, 0, 0)),
                  pl.BlockSpec((1, HEAD_DIM), lambda b, j: (0, 0))],
        out_specs=pl.BlockSpec((1, 1, M, HEAD_DIM), lambda b, j: (b, j, 0, 0)),
        compiler_params=_cparams(("arbitrary", "arbitrary")),
        name="nsa_compress",
    )(xr, pos.reshape(2, 1, 2 * W), w1.astype(BF16), w2.astype(BF16), kg0.reshape(1, HEAD_DIM))


def _cmp_attn_kernel(q_ref, kv_ref, ov_ref, cb_ref, qg_ref, oc_ref, sel_ref, qn_ref):
    qi = pl.program_id(1)
    T = q_ref.shape[1]
    NC = kv_ref.shape[2]
    G, J = NSA_GROUPS, NSA_HPG
    NS = sel_ref.shape[2] // G
    assert T == NC == LANES and T % CMP_STRIDE == 0 and G * NS == LANES
    t = qi * T + lax.broadcasted_iota(jnp.int32, (T, NC), 0)
    lane = lax.broadcasted_iota(jnp.int32, (T, NC), 1)
    vis = t - (lane * CMP_STRIDE + CMP_LEN - 1) >= 0
    visf = vis.astype(F32)
    shift = (qi * (T // CMP_STRIDE) + T // 2) % LANES
    imp = jnp.zeros((T, LANES), F32)
    for g in range(G):
        kc = kv_ref[0, g].astype(BF16)
        vc = kv_ref[0, G + g].astype(BF16)
        psum = jnp.zeros((T, NC), F32)
        for j in range(J):
            h = g * J + j
            cols = slice(h * HEAD_DIM, (h + 1) * HEAD_DIM)
            qh = (_rms(q_ref[0, :, cols], qg_ref[...]) * SCALE).astype(BF16)
            qn_ref[0, :, cols] = qh
            s = _dot_nt(qh, kc) + pltpu.roll(cb_ref[h], shift, axis=1)
            s = jnp.where(vis, s, NEG_INF)
            e = jnp.exp(s - jnp.max(s, axis=-1, keepdims=True))
            p = e / jnp.sum(e, axis=-1, keepdims=True) * visf
            oc_ref[0, :, cols] = _dot(p.astype(BF16), vc)
            psum = psum + p
        imp = imp + _dot(psum, ov_ref[g], HIGHEST)
    mm = lane % NS
    cur = t // SLC_LEN
    forced = (mm == 0) | (mm == cur) | (mm == cur - 1)
    imp = jnp.where(forced, FORCE_SCORE, jnp.where(mm * SLC_LEN <= t, imp, -1.0))
    rank = jnp.zeros((T, LANES), F32)
    for r in range(1, NS):
        wrapped = mm >= NS - r
        other = jnp.where(wrapped, pltpu.roll(imp, NS - r, axis=1), pltpu.roll(imp, LANES - r, axis=1))
        rank = rank + jnp.where(other > imp, 1.0, jnp.where(other == imp, wrapped.astype(F32), 0.0))
    sel_ref[0] = jnp.where(rank < SLC_TOPN, jnp.where(imp >= 0.0, 1.0, 0.0), 0.0)


def _nsa_cmp_attn(y, kcvc, overlap, cmp_bias, q_g):
    B, S, _ = y.shape
    T = ATT_TILE
    NC = kcvc.shape[2]
    return pl.pallas_call(
        _cmp_attn_kernel,
        out_shape=(jax.ShapeDtypeStruct((B, S, NSA_WIDTH), F32),
                   jax.ShapeDtypeStruct((B, S, LANES), F32),
                   jax.ShapeDtypeStruct((B, S, NSA_WIDTH), BF16)),
        grid=(B, S // T),
        in_specs=[pl.BlockSpec((1, T, NSA_WIDTH), lambda b, i: (b, i, 0)),
                  pl.BlockSpec((1, 2 * NSA_GROUPS, NC, HEAD_DIM), lambda b, i: (b, 0, 0, 0)),
                  pl.BlockSpec(overlap.shape, lambda b, i: (0, 0, 0)),
                  pl.BlockSpec(cmp_bias.shape, lambda b, i: (0, 0, 0)),
                  pl.BlockSpec((1, HEAD_DIM), lambda b, i: (0, 0))],
        out_specs=(pl.BlockSpec((1, T, NSA_WIDTH), lambda b, i: (b, i, 0)),
                   pl.BlockSpec((1, T, LANES), lambda b, i: (b, i, 0)),
                   pl.BlockSpec((1, T, NSA_WIDTH), lambda b, i: (b, i, 0))),
        compiler_params=_cparams(("arbitrary", "arbitrary")),
        name="nsa_cmp_attn",
    )(y, kcvc, overlap, cmp_bias, q_g.reshape(1, HEAD_DIM))


def _kv_prep_kernel(ks_ref, kw_ref, kg1_ref, kg2_ref, ksn_ref, vs_ref, kwn_ref, vw_ref):
    KV = NSA_KV
    rows = ks_ref.shape[1]
    ones_col = (lax.broadcasted_iota(jnp.int32, (rows, LANES - HEAD_DIM), 1) == 0).astype(BF16)
    for g in range(NSA_GROUPS):
        cols = slice(g * HEAD_DIM, (g + 1) * HEAD_DIM)
        vcols = slice(KV + g * HEAD_DIM, KV + (g + 1) * HEAD_DIM)
        ksn_ref[0, :, cols] = _rms(ks_ref[0, :, cols], kg1_ref[...]).astype(BF16)
        kwn_ref[0, :, cols] = _rms(kw_ref[0, :, cols], kg2_ref[...]).astype(BF16)
        vs_ref[0, :, g * LANES:g * LANES + HEAD_DIM] = ks_ref[0, :, vcols].astype(BF16)
        vw_ref[0, :, g * LANES:g * LANES + HEAD_DIM] = kw_ref[0, :, vcols].astype(BF16)
        vs_ref[0, :, g * LANES + HEAD_DIM:(g + 1) * LANES] = ones_col
        vw_ref[0, :, g * LANES + HEAD_DIM:(g + 1) * LANES] = ones_col


def _nsa_kv_prep(y, k_g):
    B, S, _ = y.shape
    ksblk = (NSA_WIDTH + 2 * NSA_KV) // (2 * NSA_KV)
    kout = jax.ShapeDtypeStruct((B, S, NSA_KV), BF16)
    vout = jax.ShapeDtypeStruct((B, S, NSA_GROUPS * LANES), BF16)
    kspec = pl.BlockSpec((1, ROW_TILE, NSA_KV), lambda b, i: (b, i, 0))
    vspec = pl.BlockSpec((1, ROW_TILE, NSA_GROUPS * LANES), lambda b, i: (b, i, 0))
    return pl.pallas_call(
        _kv_prep_kernel,
        out_shape=(kout, vout, kout, vout),
        grid=(B, S // ROW_TILE),
        in_specs=[pl.BlockSpec((1, ROW_TILE, 2 * NSA_KV), lambda b, i: (b, i, ksblk)),
                  pl.BlockSpec((1, ROW_TILE, 2 * NSA_KV), lambda b, i: (b, i, ksblk + 1)),
                  pl.BlockSpec((1, HEAD_DIM), lambda b, i: (0, 0)),
                  pl.BlockSpec((1, HEAD_DIM), lambda b, i: (0, 0))],
        out_specs=(kspec, vspec, kspec, vspec),
        compiler_params=_cparams(("arbitrary", "arbitrary")),
        name="nsa_kv_prep",
    )(y, y, k_g[1].reshape(1, HEAD_DIM), k_g[2].reshape(1, HEAD_DIM))


def _sel_win_kernel(qn_ref, ks_ref, vs_ref, kw_ref, vw_ref, gl_ref, sel_ref, oc_ref, bias_ref, e_ref, bg_ref,
                    o_ref, mk_scr):
    qi = pl.program_id(1)
    T = qn_ref.shape[1]
    S = ks_ref.shape[1]
    npair = S // (2 * T)
    G, J = NSA_GROUPS, NSA_HPG
    NS = e_ref.shape[0]
    wt = WINDOW // T

    def sel_tile(r):
        return jnp.where(r < 0, 4, jnp.minimum(r, 2))

    def win_tile(r):
        return jnp.where(r < 0, 4, jnp.where(r <= 1, r, jnp.where(r < wt, 2, jnp.where(r == wt, 3, 4))))

    def bias4(g, i0, i1):
        return jnp.concatenate(
            [jnp.concatenate([bias_ref[g * J + j, i0], bias_ref[g * J + j, i1]], axis=1) for j in range(J)],
            axis=0)

    q4 = []
    for g in range(G):
        selg = sel_ref[0, :, g * NS:(g + 1) * NS].astype(BF16)
        for kp in range(npair):
            mk_scr[g, kp] = (_dot(selg, e_ref[:, kp * 2 * T:(kp + 1) * 2 * T]) - 1.0) * (-NEG_INF)
        q4.append(jnp.concatenate(
            [qn_ref[0, :, (g * J + j) * HEAD_DIM:(g * J + j + 1) * HEAD_DIM] for j in range(J)], axis=0))

    def sel_step(kp, carry, g):
        ks = pl.ds(pl.multiple_of(kp * 2 * T, 2 * T), 2 * T)
        r0 = qi - 2 * kp
        s = _dot_nt(q4[g], ks_ref[0, ks, g * HEAD_DIM:(g + 1) * HEAD_DIM])
        s = s + bias4(g, sel_tile(r0), sel_tile(r0 - 1)) + jnp.concatenate([mk_scr[g, kp]] * J, axis=0)
        return _softmax_step_ones(carry, s, vs_ref[0, ks, g * LANES:(g + 1) * LANES])

    def win_step(kp, carry, g):
        ks = pl.ds(pl.multiple_of(kp * 2 * T, 2 * T), 2 * T)
        r0 = qi - 2 * kp
        s = _dot_nt(q4[g], kw_ref[0, ks, g * HEAD_DIM:(g + 1) * HEAD_DIM])
        s = s + bias4(g, win_tile(r0), win_tile(r0 - 1))
        return _softmax_step_ones(carry, s, vw_ref[0, ks, g * LANES:(g + 1) * LANES])

    init = tuple((jnp.full((J * T, 1), NEG_INF, F32), jnp.zeros((J * T, LANES), F32)) for _ in range(G))
    kw0 = jnp.maximum(qi - wt, 0) // 2
    cs = lax.fori_loop(0, kw0, lambda kp, c: tuple(sel_step(kp, c[g], g) for g in range(G)), init)
    cs, cw = lax.fori_loop(
        kw0, qi // 2 + 1,
        lambda kp, c: (tuple(sel_step(kp, c[0][g], g) for g in range(G)),
                       tuple(win_step(kp, c[1][g], g) for g in range(G))),
        (cs, init))

    gates = jax.nn.sigmoid(gl_ref[0] + bg_ref[...])
    for g in range(G):
        o_s = cs[g][1][:, :HEAD_DIM] / cs[g][1][:, HEAD_DIM:HEAD_DIM + 1]
        o_w = cw[g][1][:, :HEAD_DIM] / cw[g][1][:, HEAD_DIM:HEAD_DIM + 1]
        for j in range(J):
            h = g * J + j
            cols = slice(h * HEAD_DIM, (h + 1) * HEAD_DIM)
            rows = slice(j * T, (j + 1) * T)
            o_ref[0, :, cols] = (gates[:, 3 * h:3 * h + 1] * oc_ref[0, :, cols]
                                 + gates[:, 3 * h + 1:3 * h + 2] * o_s[rows]
                                 + gates[:, 3 * h + 2:3 * h + 3] * o_w[rows])


def _nsa_sel_win(qn, y, kv, sel, oc, bias_tiles, expand, b_gate):
    B, S, _ = y.shape
    T = ATT_TILE
    glblk = (NSA_WIDTH + 6 * NSA_KV) // LANES
    bg = jnp.zeros((1, LANES), F32).at[0, :3 * NSA_HEADS].set(b_gate)
    kspec = pl.BlockSpec((1, S, NSA_KV), lambda b, i: (b, 0, 0))
    vspec = pl.BlockSpec((1, S, NSA_GROUPS * LANES), lambda b, i: (b, 0, 0))
    return pl.pallas_call(
        _sel_win_kernel,
        out_shape=jax.ShapeDtypeStruct((B, S, NSA_WIDTH), F32),
        grid=(B, S // T),
        in_specs=[pl.BlockSpec((1, T, NSA_WIDTH), lambda b, i: (b, i, 0)),
                  kspec, vspec, kspec, vspec,
                  pl.BlockSpec((1, T, LANES), lambda b, i: (b, i, glblk)),
                  pl.BlockSpec((1, T, sel.shape[2]), lambda b, i: (b, i, 0)),
                  pl.BlockSpec((1, T, NSA_WIDTH), lambda b, i: (b, i, 0)),
                  pl.BlockSpec(bias_tiles.shape, lambda b, i: (0, 0, 0, 0)),
                  pl.BlockSpec(expand.shape, lambda b, i: (0, 0)),
                  pl.BlockSpec((1, LANES), lambda b, i: (0, 0))],
        out_specs=pl.BlockSpec((1, T, NSA_WIDTH), lambda b, i: (b, i, 0)),
        scratch_shapes=[pltpu.VMEM((NSA_GROUPS, S // (2 * T), T, 2 * T), F32)],
        compiler_params=_cparams(("arbitrary", "arbitrary")),
        name="nsa_sel_win",
    )(qn, *kv, y, sel, oc, bias_tiles, expand, bg)


def _odd_mixer(x, sh, sc, gate, ng, w_in, b_gate, q_g, k_g, cmp_pos, cmp_w1, cmp_w2, bias_tiles, cmp_bias,
               w_out):
    B, S, _ = x.shape
    G = NSA_GROUPS
    w = jnp.zeros((D_MODEL, ODD_IN_PAD), BF16).at[:, :ODD_IN].set(w_in.astype(BF16))
    y = _norm_proj(x, sh, sc, ng, w)
    xr = y[:, :, NSA_WIDTH:NSA_WIDTH + 2 * NSA_KV].reshape(B, S, 2, G, HEAD_DIM)
    xr = xr.transpose(0, 2, 3, 1, 4).reshape(B, 2 * G, S // CMP_STRIDE, CMP_STRIDE * HEAD_DIM)
    kcvc = _nsa_compress(xr, cmp_pos.reshape(2, CMP_LEN * HEAD_DIM), cmp_w1, cmp_w2, k_g[0])
    n_slc = S // SLC_LEN
    starts = jnp.arange(S // CMP_STRIDE) * CMP_STRIDE
    bstart = jnp.arange(n_slc) * SLC_LEN
    real = (jnp.arange(S // CMP_STRIDE) < (S - CMP_LEN) // CMP_STRIDE + 1)[:, None]
    overlap = (real & (starts[:, None] < bstart[None, :] + SLC_LEN)
               & (starts[:, None] + CMP_LEN > bstart[None, :])).astype(F32)
    overlap = jnp.stack([jnp.pad(overlap, ((0, 0), (g * n_slc, (G - 1 - g) * n_slc))) for g in range(G)])
    expand = (jnp.arange(S)[None, :] // SLC_LEN == jnp.arange(n_slc)[:, None]).astype(BF16)
    oc, sel, qn = _nsa_cmp_attn(y, kcvc, overlap, cmp_bias, q_g)
    o = _nsa_sel_win(qn, y, _nsa_kv_prep(y, k_g), sel, oc, bias_tiles, expand, b_gate)
    return _out_proj(x, o, 0, o, 1, w_out.astype(BF16), gate)


def _even_mixer(x, sh, sc, gate, ng, w_in, b_f, conv_w, q_g, k_g, w_out):
    w = jnp.zeros((D_MODEL, EVEN_IN_PAD), BF16).at[:, :EVEN_IN].set(w_in.astype(BF16))
    y = _norm_proj(x, sh, sc, ng, w)
    conv = _short_conv(y, conv_w)
    attn = _fox_attention(y, b_f, q_g, k_g)
    return _out_proj(x, conv, 0, attn, 0, w_out.astype(BF16), gate)


def kernel(x, c, ada_w, ada_b, norm_g, even_w_in, even_b_f, even_conv_w, even_q_g, even_k_g, even_w_out, odd_w_in, odd_b_gate, odd_q_g, odd_k_g, odd_cmp_pos, odd_cmp_w1, odd_cmp_w2, odd_w_out, rel_table, peer_w_q, peer_keys, peer_u, peer_v):
    B = x.shape[0]
    mods = _ada(c, ada_w, ada_b)
    bias_tiles, cmp_bias = _relbias(rel_table)
    for layer in range(DEPTH):
        sh1, sc1, g1, sh2, sc2, g2 = [m.reshape(B, 1, D_MODEL) for m in jnp.split(mods[layer], 6, axis=-1)]
        i = layer // 2
        ng1 = norm_g[layer, 0].reshape(1, D_MODEL)
        ng2 = norm_g[layer, 1].reshape(1, D_MODEL)
        if layer % 2 == 0:
            x = _even_mixer(x, sh1, sc1, g1, ng1, even_w_in[i], even_b_f[i], even_conv_w[i], even_q_g[i],
                            even_k_g[i], even_w_out[i])
        else:
            x = _odd_mixer(x, sh1, sc1, g1, ng1, odd_w_in[i], odd_b_gate[i], odd_q_g[i], odd_k_g[i],
                           odd_cmp_pos[i], odd_cmp_w1[i], odd_cmp_w2[i], bias_tiles, cmp_bias, odd_w_out[i])
        x = _peer(x, sh2, sc2, g2, ng2, peer_w_q[layer], peer_keys[layer],
                  peer_u[layer].astype(BF16), peer_v[layer].T.astype(BF16))
    return x
```

```python
import functools
import math

import jax
import jax.numpy as jnp
from jax import lax
from jax.experimental import pallas as pl
from jax.experimental.pallas import tpu as pltpu

D_MODEL = 1024
DEPTH = 4
HEAD_DIM = 64
CONV_CH = 512
CONV_K = 3
FOX_HEADS = 8
FOX_WIDTH = FOX_HEADS * HEAD_DIM
EVEN_IN = 3 * CONV_CH + 3 * FOX_WIDTH + FOX_HEADS
NSA_GROUPS = 4
NSA_HPG = 4
NSA_HEADS = NSA_GROUPS * NSA_HPG
NSA_WIDTH = NSA_HEADS * HEAD_DIM
NSA_KV = NSA_GROUPS * HEAD_DIM
CMP_LEN = 32
CMP_STRIDE = 16
SLC_LEN = 64
SLC_TOPN = 16
WINDOW = 512
ODD_IN = NSA_WIDTH + 6 * NSA_KV + 3 * NSA_HEADS
REL_BUCKETS = 32
REL_MAX_DIST = 128
PEER_HEADS = 8
PEER_NKEYS = 128
PEER_EXPERTS = PEER_NKEYS * PEER_NKEYS
PEER_QDIM = 128
PEER_TOPK = 16
EPS = 1e-6
NEG_INF = -1e30
FORCE_SCORE = 1e6

LANES = 128
F32 = jnp.float32
BF16 = jnp.bfloat16
HIGHEST = lax.Precision.HIGHEST
VMEM_LIMIT = 56 * 1024 * 1024

EVEN_IN_PAD = 3200
ODD_IN_PAD = 2688
ROW_TILE = 512
ATT_TILE = 128
FOX_TILE = 256
PEER_TN = 512
PEER_EC = 2048
SCALE = HEAD_DIM ** -0.5


def _cparams(sem, flags=None):
    return pltpu.CompilerParams(dimension_semantics=sem, vmem_limit_bytes=VMEM_LIMIT, flags=flags)


def _dot(a, b, precision=None):
    return jnp.dot(a, b, precision=precision, preferred_element_type=F32)


def _dot_nt(a, b, precision=None):
    return lax.dot_general(a, b, (((1,), (1,)), ((), ())), precision=precision,
                           preferred_element_type=F32)


def _rms(x, g):
    return x * lax.rsqrt(jnp.mean(x * x, axis=-1, keepdims=True) + EPS) * g


def _ada_kernel(c_ref, w_ref, b_ref, o_ref):
    c = c_ref[...]
    sc = c * jax.nn.sigmoid(c)
    o_ref[0] = _dot(sc, w_ref[0], HIGHEST) + b_ref[0]


def _ada(c, ada_w, ada_b):
    B, D = c.shape
    E = ada_w.shape[2]
    tn = 2048
    return pl.pallas_call(
        _ada_kernel,
        out_shape=jax.ShapeDtypeStruct((DEPTH, B, E), F32),
        grid=(DEPTH, E // tn),
        in_specs=[pl.BlockSpec((B, D), lambda l, j: (0, 0)),
                  pl.BlockSpec((1, D, tn), lambda l, j: (l, 0, j)),
                  pl.BlockSpec((1, 1, tn), lambda l, j: (l, 0, j))],
        out_specs=pl.BlockSpec((1, B, tn), lambda l, j: (l, 0, j)),
        compiler_params=_cparams(("arbitrary", "arbitrary")),
        name="ada_mod",
    )(c, ada_w, ada_b.reshape(DEPTH, 1, E))


def _norm_proj_kernel(x_ref, sh_ref, sc_ref, ng_ref, w_ref, y_ref):
    h = _rms(x_ref[0], ng_ref[...]) * (1.0 + sc_ref[0]) + sh_ref[0]
    y_ref[0] = _dot(h.astype(BF16), w_ref[...])


def _norm_proj(x, sh, sc, ng, w):
    B, S, D = x.shape
    N = w.shape[1]
    return pl.pallas_call(
        _norm_proj_kernel,
        out_shape=jax.ShapeDtypeStruct((B, S, N), F32),
        grid=(B, S // ROW_TILE),
        in_specs=[pl.BlockSpec((1, ROW_TILE, D), lambda b, i: (b, i, 0)),
                  pl.BlockSpec((1, 1, D), lambda b, i: (b, 0, 0)),
                  pl.BlockSpec((1, 1, D), lambda b, i: (b, 0, 0)),
                  pl.BlockSpec((1, D), lambda b, i: (0, 0)),
                  pl.BlockSpec((D, N), lambda b, i: (0, 0))],
        out_specs=pl.BlockSpec((1, ROW_TILE, N), lambda b, i: (b, i, 0)),
        compiler_params=_cparams(("arbitrary", "arbitrary")),
        name="norm_proj",
    )(x, sh, sc, ng, w)


def _out_proj_kernel(x_ref, a_ref, b_ref, wa_ref, wb_ref, g_ref, o_ref):
    y = _dot(a_ref[0].astype(BF16), wa_ref[...]) + _dot(b_ref[0].astype(BF16), wb_ref[...])
    o_ref[0] = x_ref[0] + g_ref[0] * y


def _out_proj(x, a, a_blk, b, b_blk, w, gate):
    B, S, D = x.shape
    half = w.shape[0] // 2
    wa, wb = w[:half], w[half:]
    return pl.pallas_call(
        _out_proj_kernel,
        out_shape=jax.ShapeDtypeStruct((B, S, D), F32),
        grid=(B, S // ROW_TILE),
        in_specs=[pl.BlockSpec((1, ROW_TILE, D), lambda b, i: (b, i, 0)),
                  pl.BlockSpec((1, ROW_TILE, half), lambda b, i: (b, i, a_blk)),
                  pl.BlockSpec((1, ROW_TILE, half), lambda b, i: (b, i, b_blk)),
                  pl.BlockSpec((half, D), lambda b, i: (0, 0)),
                  pl.BlockSpec((half, D), lambda b, i: (0, 0)),
                  pl.BlockSpec((1, 1, D), lambda b, i: (b, 0, 0))],
        out_specs=pl.BlockSpec((1, ROW_TILE, D), lambda b, i: (b, i, 0)),
        compiler_params=_cparams(("arbitrary", "arbitrary")),
        name="out_proj",
    )(x, a, b, wa, wb, gate)


def _conv_kernel(cb_ref, cc_ref, ch_ref, w_ref, o_ref):
    u = cc_ref[0] * ch_ref[0]
    row = lax.broadcasted_iota(jnp.int32, u.shape, 0)
    u1 = jnp.where(row >= 1, pltpu.roll(u, 1, axis=0), 0.0)
    u2 = jnp.where(row >= 2, pltpu.roll(u, 2, axis=0), 0.0)
    w = w_ref[...]
    o_ref[0] = cb_ref[0] * (w[0:1] * u2 + w[1:2] * u1 + w[2:3] * u)


def _short_conv(y, conv_w):
    B, S, _ = y.shape
    nblk = CONV_CH // LANES
    w = jnp.zeros((8, CONV_CH), F32).at[:CONV_K].set(conv_w)
    return pl.pallas_call(
        _conv_kernel,
        out_shape=jax.ShapeDtypeStruct((B, S, CONV_CH), F32),
        grid=(B, nblk),
        in_specs=[pl.BlockSpec((1, S, LANES), lambda b, j: (b, 0, j)),
                  pl.BlockSpec((1, S, LANES), lambda b, j: (b, 0, nblk + j)),
                  pl.BlockSpec((1, S, LANES), lambda b, j: (b, 0, 2 * nblk + j)),
                  pl.BlockSpec((8, LANES), lambda b, j: (0, j))],
        out_specs=pl.BlockSpec((1, S, LANES), lambda b, j: (b, 0, j)),
        compiler_params=_cparams(("arbitrary", "arbitrary")),
        name="short_conv",
    )(y, y, y, w)


def _cumsum_rows(x):
    n = x.shape[0]
    row = lax.broadcasted_iota(jnp.int32, x.shape, 0)
    sh = 1
    while sh < n:
        x = x + jnp.where(row >= sh, pltpu.roll(x, sh, axis=0), 0.0)
        sh *= 2
    return x


def _softmax_step_ones(carry, s, v):
    m, acc = carry
    m_new = jnp.maximum(m, jnp.max(s, axis=-1, keepdims=True))
    p = jnp.exp(s - m_new)
    return m_new, jnp.exp(m - m_new) * acc + _dot(p.astype(BF16), v)


def _fox_kernel(q_ref, k_ref, v_ref, f_ref, bf_ref, qg_ref, kg_ref, o_ref,
                kn_scr, vb_scr, cum_scr, cumt_scr):
    qi = pl.program_id(1)
    S = k_ref.shape[1]
    T = q_ref.shape[1]
    nk = S // T

    @pl.when(qi == 0)
    def _prep():
        ones_col = (lax.broadcasted_iota(jnp.int32, (S, LANES - HEAD_DIM), 1) == 0).astype(BF16)
        for h in range(FOX_HEADS):
            cols = slice(h * HEAD_DIM, (h + 1) * HEAD_DIM)
            kn_scr[:, cols] = _rms(k_ref[0, :, cols], kg_ref[...]).astype(BF16)
            vb_scr[:, h * LANES:h * LANES + HEAD_DIM] = v_ref[0, :, cols].astype(BF16)
            vb_scr[:, h * LANES + HEAD_DIM:(h + 1) * LANES] = ones_col
        logf = jax.nn.log_sigmoid(f_ref[0] + bf_ref[...])
        cum = _cumsum_rows(logf)
        cum_scr[...] = cum
        for kt in range(nk):
            cumt_scr[kt] = cum[kt * T:(kt + 1) * T, :].T[0:8, :]

    causal = jnp.where(lax.broadcasted_iota(jnp.int32, (T, T), 0) >= lax.broadcasted_iota(jnp.int32, (T, T), 1),
                       0.0, NEG_INF)
    cq = cum_scr[pl.ds(pl.multiple_of(qi * T, T), T), :]
    heads = []
    for h in range(FOX_HEADS):
        cols = slice(h * HEAD_DIM, (h + 1) * HEAD_DIM)
        heads.append((cols, (_rms(q_ref[0, :, cols], qg_ref[...]) * SCALE).astype(BF16), cq[:, h:h + 1]))

    def step(kt, carry, diag):
        ks = pl.ds(pl.multiple_of(kt * T, T), T)
        ctile = cumt_scr[kt]
        out = []
        for h, (cols, qh, ct) in enumerate(heads):
            s = _dot_nt(qh, kn_scr[ks, cols]) + (ct - ctile[h:h + 1, :])
            if diag:
                s = s + causal
            out.append(_softmax_step_ones(carry[h], s, vb_scr[ks, h * LANES:(h + 1) * LANES]))
        return tuple(out)

    init = tuple((jnp.full((T, 1), NEG_INF, F32), jnp.zeros((T, LANES), F32)) for _ in range(FOX_HEADS))
    carry = lax.fori_loop(0, qi, lambda kt, c: step(kt, c, False), init)
    carry = step(qi, carry, True)
    for h, (cols, _, _) in enumerate(heads):
        o_ref[0, :, cols] = carry[h][1][:, :HEAD_DIM] / carry[h][1][:, HEAD_DIM:HEAD_DIM + 1]


def _fox_attention(y, b_f, q_g, k_g):
    B, S, _ = y.shape
    T = FOX_TILE
    qblk = 3 * CONV_CH // FOX_WIDTH
    fblk = (3 * CONV_CH + 3 * FOX_WIDTH) // LANES
    bf = jnp.zeros((1, LANES), F32).at[0, :FOX_HEADS].set(b_f)
    return pl.pallas_call(
        _fox_kernel,
        out_shape=jax.ShapeDtypeStruct((B, S, FOX_WIDTH), F32),
        grid=(B, S // T),
        in_specs=[pl.BlockSpec((1, T, FOX_WIDTH), lambda b, i: (b, i, qblk)),
                  pl.BlockSpec((1, S, FOX_WIDTH), lambda b, i: (b, 0, qblk + 1)),
                  pl.BlockSpec((1, S, FOX_WIDTH), lambda b, i: (b, 0, qblk + 2)),
                  pl.BlockSpec((1, S, LANES), lambda b, i: (b, 0, fblk)),
                  pl.BlockSpec((1, LANES), lambda b, i: (0, 0)),
                  pl.BlockSpec((1, HEAD_DIM), lambda b, i: (0, 0)),
                  pl.BlockSpec((1, HEAD_DIM), lambda b, i: (0, 0))],
        out_specs=pl.BlockSpec((1, T, FOX_WIDTH), lambda b, i: (b, i, 0)),
        scratch_shapes=[pltpu.VMEM((S, FOX_WIDTH), BF16),
                        pltpu.VMEM((S, FOX_HEADS * LANES), BF16),
                        pltpu.VMEM((S, LANES), F32),
                        pltpu.VMEM((S // T, 8, T), F32)],
        compiler_params=_cparams(("arbitrary", "arbitrary")),
        name="fox_attention",
    )(y, y, y, y, bf, q_g.reshape(1, HEAD_DIM), k_g.reshape(1, HEAD_DIM))


def _extract_top(work, iota, k, exact):
    nrows = work.shape[0]
    vals = []
    for _ in range(k):
        m = jnp.max(work, axis=0, keepdims=True)
        if exact:
            idx = jnp.min(jnp.where(work == m, iota, nrows), axis=0, keepdims=True)
            work = jnp.where(iota == idx, -jnp.inf, work)
        else:
            work = jnp.where(work == m, -jnp.inf, work)
        vals.append(m)
    return vals, work


def _removed(work):
    return jnp.sum(jnp.where(work == -jnp.inf, 1.0, 0.0), axis=0, keepdims=True)


def _peer_kernel(x_ref, sh_ref, sc_ref, gt_ref, ng_ref, wqh_ref, wql_ref, keys_ref, u_ref, vt_ref, o_ref,
                 ht_scr, qt_scr, sp_scr, s0_scr, e0_scr, s1_scr, e1_scr, tau_scr, a_scr, p_scr, acc_scr):
    c = pl.program_id(2)
    nc = pl.num_programs(2)
    tn = x_ref.shape[1]
    nlt = tn // LANES
    half = PEER_QDIM // 2

    @pl.when(c == 0)
    def _prep():
        h = _rms(x_ref[0], ng_ref[...]) * (1.0 + sc_ref[0]) + sh_ref[0]
        hh = h.astype(BF16)
        hl = (h - hh.astype(F32)).astype(BF16)
        cw = 2 * LANES
        for k in range(h.shape[1] // cw):
            cs = slice(k * cw, (k + 1) * cw)
            q = _dot(hh, wqh_ref[:, cs]) + (_dot(hh, wql_ref[:, cs]) + _dot(hl, wqh_ref[:, cs]))
            qt_scr[cs, :] = q.T
            ht_scr[cs, :] = h[:, cs].T.astype(BF16)
        acc_scr[...] = jnp.zeros_like(acc_scr)
        iota_k = lax.broadcasted_iota(jnp.int32, (PEER_NKEYS, LANES), 0)
        iota_c = lax.broadcasted_iota(jnp.int32, (PEER_TOPK + 8 * 8, LANES), 0)

        n_pad = sum(8 - PEER_TOPK // (r0 + 1) for r0 in range(2, 8))

        def select(hd, lt, exact):
            lanes = slice(lt * LANES, (lt + 1) * LANES)
            s0 = sp_scr[0, :, lanes]
            s1 = sp_scr[1, :, lanes]
            sv0, w0 = _extract_top(s0, iota_k, PEER_TOPK, exact)
            sv1, w1 = _extract_top(s1, iota_k, PEER_TOPK, exact)
            s0m = jnp.where(w0 == -jnp.inf, s0, -jnp.inf)
            s1m = jnp.where(w1 == -jnp.inf, s1, -jnp.inf)
            sv0_all = jnp.concatenate(sv0, axis=0)
            sv1_all = jnp.concatenate(sv1, axis=0)
            parts = [sv0[0] + sv1_all, sv0[1] + sv1_all[:8]]
            for r0 in range(2, 8):
                parts.append(jnp.where(iota_c[:8] < PEER_TOPK // (r0 + 1), sv0[r0] + sv1_all[:8], -jnp.inf))
            parts.append(sv0_all[8:] + sv1[0])
            cv, wc = _extract_top(jnp.concatenate(parts, axis=0), iota_c, PEER_TOPK, exact)
            z = cv[0] - cv[0]
            for r in range(PEER_TOPK):
                z = z + jnp.exp(cv[r] - cv[0])
            s0_scr[hd, :, lanes] = s0m
            s1_scr[hd, :, lanes] = s1m
            e0_scr[hd, :, lanes] = jnp.exp(s0m - sv0[0])
            e1_scr[hd, :, lanes] = jnp.exp(s1m - sv1[0]) / z
            tau_scr[hd, :, lanes] = jnp.broadcast_to(cv[PEER_TOPK - 1], (8, LANES))
            return ((_removed(w0) != PEER_TOPK) | (_removed(w1) != PEER_TOPK)
                    | (_removed(wc) != PEER_TOPK + n_pad))

        def head_body(hd, carry):
            for p in range(2):
                qs = qt_scr[pl.ds(pl.multiple_of((hd * 2 + p) * half, half), half), :]
                sp_scr[p] = _dot(keys_ref[p], qs, HIGHEST)
            tied = jnp.zeros((1, LANES), F32)
            for lt in range(nlt):
                tied = jnp.maximum(tied, jnp.where(select(hd, lt, False), 1.0, 0.0))

            @pl.when(jnp.max(tied) > 0.0)
            def _redo():
                for lt in range(nlt):
                    select(hd, lt, True)
            return carry

        lax.fori_loop(0, PEER_HEADS, head_body, 0)

    a_scr[...] = jax.nn.gelu(_dot(u_ref[...], ht_scr[...]), approximate=True)
    nsub = 8
    nrq = 32

    def weights_tile(rq, carry, lt, part):
        lanes = slice(lt * LANES, (lt + 1) * LANES)
        jr = pl.ds(pl.multiple_of(rq * nrq, nrq), nrq)
        i0 = pl.ds(pl.multiple_of(c * (PEER_EC // PEER_NKEYS) + part * nsub, nsub), nsub)
        w = [jnp.zeros((nrq, LANES), F32) for _ in range(nsub)]
        for hd in range(PEER_HEADS):
            s1 = s1_scr[hd, jr, lanes]
            e1 = e1_scr[hd, jr, lanes]
            s0 = s0_scr[hd, i0, lanes]
            e0 = e0_scr[hd, i0, lanes]
            tau = tau_scr[hd, 0:1, lanes]
            for ii in range(nsub):
                keep = s1 + s0[ii:ii + 1] >= tau
                w[ii] = w[ii] + jnp.where(keep, e1 * e0[ii:ii + 1], 0.0)
        for ii in range(nsub):
            rows = pl.ds(pl.multiple_of((part * nsub + ii) * PEER_NKEYS + rq * nrq, nrq), nrq)
            p_scr[rows, lanes] = (w[ii] * a_scr[rows, lanes]).astype(BF16)
        return carry

    for part in range(PEER_EC // PEER_NKEYS // nsub):
        for lt in range(nlt):
            lax.fori_loop(0, PEER_NKEYS // nrq, functools.partial(weights_tile, lt=lt, part=part), 0)
    acc_scr[...] += _dot(vt_ref[...], p_scr[...])

    @pl.when(c == nc - 1)
    def _fin():
        o_ref[0] = x_ref[0] + gt_ref[0] * acc_scr[...].T


def _peer(x, sh, sc, gate, ng, w_q, keys, u_b, vt_b):
    B, S, D = x.shape
    tn = PEER_TN
    nc = PEER_EXPERTS // PEER_EC
    tab = pltpu.VMEM((PEER_HEADS, PEER_NKEYS, tn), F32)
    wq_hi = w_q.astype(BF16)
    wq_lo = (w_q - wq_hi.astype(F32)).astype(BF16)
    wspec = pl.BlockSpec((D, PEER_HEADS * PEER_QDIM), lambda b, i, c: (0, 0), pipeline_mode=pl.Buffered(1))
    return pl.pallas_call(
        _peer_kernel,
        out_shape=jax.ShapeDtypeStruct((B, S, D), F32),
        grid=(B, S // tn, nc),
        in_specs=[pl.BlockSpec((1, tn, D), lambda b, i, c: (b, i, 0), pipeline_mode=pl.Buffered(1)),
                  pl.BlockSpec((1, 1, D), lambda b, i, c: (b, 0, 0)),
                  pl.BlockSpec((1, 1, D), lambda b, i, c: (b, 0, 0)),
                  pl.BlockSpec((1, 1, D), lambda b, i, c: (b, 0, 0)),
                  pl.BlockSpec((1, D), lambda b, i, c: (0, 0)),
                  wspec, wspec,
                  pl.BlockSpec((2, PEER_NKEYS, PEER_QDIM // 2), lambda b, i, c: (0, 0, 0)),
                  pl.BlockSpec((PEER_EC, D), lambda b, i, c: (c, 0)),
                  pl.BlockSpec((D, PEER_EC), lambda b, i, c: (0, c))],
        out_specs=pl.BlockSpec((1, tn, D), lambda b, i, c: (b, i, 0)),
        scratch_shapes=[pltpu.VMEM((D, tn), BF16),
                        pltpu.VMEM((PEER_HEADS * PEER_QDIM, tn), F32),
                        pltpu.VMEM((2, PEER_NKEYS, tn), F32),
                        tab, tab, tab, tab,
                        pltpu.VMEM((PEER_HEADS, 8, tn), F32),
                        pltpu.VMEM((PEER_EC, tn), F32),
                        pltpu.VMEM((PEER_EC, tn), BF16),
                        pltpu.VMEM((D, tn), F32)],
        compiler_params=_cparams(("arbitrary", "arbitrary", "arbitrary")),
        name="peer",
    )(x, sh, sc, gate, ng, wq_hi, wq_lo, keys, u_b, vt_b)


def _rel_bucket(dist):
    n = jnp.maximum(dist, 0)
    max_exact = REL_BUCKETS // 2
    nf = jnp.maximum(n, 1).astype(F32)
    large = max_exact + (jnp.log(nf / max_exact) / math.log(REL_MAX_DIST / max_exact)
                         * (REL_BUCKETS - max_exact)).astype(jnp.int32)
    large = jnp.minimum(large, REL_BUCKETS - 1)
    return jnp.where(n < max_exact, n, large)


def _bias_lookup(bucket, tab_ref, h):
    bias = jnp.zeros(bucket.shape, F32)
    for b in range(REL_BUCKETS):
        bias = jnp.where(bucket == b, tab_ref[b, h], bias)
    return bias


N_BIAS_TILES = 5


def _relbias_kernel(tab_ref, o_ref, c_ref):
    h = pl.program_id(0)
    T = o_ref.shape[2]
    a = lax.broadcasted_iota(jnp.int32, (T, T), 0)
    b = lax.broadcasted_iota(jnp.int32, (T, T), 1)
    last = tab_ref[REL_BUCKETS - 1, h]
    o_ref[0, 0] = jnp.where(a >= b, _bias_lookup(_rel_bucket(a - b), tab_ref, h), NEG_INF)
    o_ref[0, 1] = _bias_lookup(_rel_bucket(a - b + T), tab_ref, h)
    o_ref[0, 2] = jnp.full((T, T), last, F32)
    o_ref[0, 3] = jnp.where(a < b, last, NEG_INF)
    o_ref[0, 4] = jnp.full((T, T), NEG_INF, F32)
    dist = a - ((b - T // 2) * CMP_STRIDE + CMP_LEN - 1)
    c_ref[0] = jnp.where(dist >= 0, _bias_lookup(_rel_bucket(dist), tab_ref, h), last)


def _relbias(rel_table):
    T = ATT_TILE
    assert 2 * T - (T - 1) >= REL_MAX_DIST
    assert WINDOW % T == 0
    return pl.pallas_call(
        _relbias_kernel,
        out_shape=(jax.ShapeDtypeStruct((NSA_HEADS, N_BIAS_TILES, T, T), F32),
                   jax.ShapeDtypeStruct((NSA_HEADS, T, T), F32)),
        grid=(NSA_HEADS,),
        in_specs=[pl.BlockSpec(memory_space=pltpu.SMEM)],
        out_specs=(pl.BlockSpec((1, N_BIAS_TILES, T, T), lambda h: (h, 0, 0, 0)),
                   pl.BlockSpec((1, T, T), lambda h: (h, 0, 0))),
        compiler_params=_cparams(("arbitrary",)),
        name="rel_bias_tiles",
    )(rel_table)


def _cmp_kernel(x_ref, pos_ref, w1_ref, w2_ref, kg_ref, o_ref):
    j = pl.program_id(1)
    half = CMP_STRIDE * HEAD_DIM
    x = x_ref[0, 0]
    pos = pos_ref[0]
    pa = _dot((x + pos[:, :half]).astype(BF16), w1_ref[0, :half, :])
    pb = _dot((x + pos[:, half:]).astype(BF16), w1_ref[0, half:, :])
    pre = pa + pltpu.roll(pb, pb.shape[0] - 1, axis=0)
    out = _dot(jax.nn.gelu(pre, approximate=True).astype(BF16), w2_ref[0])
    o_ref[0, 0] = jnp.where(j < NSA_GROUPS, _rms(out, kg_ref[...]), out)


def _nsa_compress(xr, pos, w1, w2, kg0):
    B, _, M, W = xr.shape
    G = NSA_GROUPS
    return pl.pallas_call(
        _cmp_kernel,
        out_shape=jax.ShapeDtypeStruct((B, 2 * G, M, HEAD_DIM), F32),
        grid=(B, 2 * G),
        in_specs=[pl.BlockSpec((1, 1, M, W), lambda b, j: (b, j, 0, 0)),
                  pl.BlockSpec((1, 1, 2 * W), lambda b, j: (j // G, 0, 0)),
                  pl.BlockSpec((1, 2 * W, HEAD_DIM), lambda b, j: (j // G, 0, 0)),
                  pl.BlockSpec((1, HEAD_DIM, HEAD_DIM), lambda b, j: (j // G, 0, 0)),
                  pl.BlockSpec((1, HEAD_DIM), lambda b, j: (0, 0))],
        out_specs=pl.BlockSpec((1, 1, M, HEAD_DIM), lambda b, j: (b, j, 0, 0)),
        compiler_params=_cparams(("arbitrary", "arbitrary")),
        name="nsa_compress",
    )(xr, pos.reshape(2, 1, 2 * W), w1.astype(BF16), w2.astype(BF16), kg0.reshape(1, HEAD_DIM))


def _cmp_attn_kernel(q_ref, kv_ref, ov_ref, cb_ref, qg_ref, oc_ref, sel_ref, qn_ref):
    qi = pl.program_id(1)
    T = q_ref.shape[1]
    NC = kv_ref.shape[2]
    G, J = NSA_GROUPS, NSA_HPG
    NS = sel_ref.shape[2] // G
    assert T == NC == LANES and T % CMP_STRIDE == 0 and G * NS == LANES
    t = qi * T + lax.broadcasted_iota(jnp.int32, (T, NC), 0)
    lane = lax.broadcasted_iota(jnp.int32, (T, NC), 1)
    vis = t - (lane * CMP_STRIDE + CMP_LEN - 1) >= 0
    visf = vis.astype(F32)
    shift = (qi * (T // CMP_STRIDE) + T // 2) % LANES
    imp = jnp.zeros((T, LANES), F32)
    for g in range(G):
        kc = kv_ref[0, g].astype(BF16)
        vc = kv_ref[0, G + g].astype(BF16)
        psum = jnp.zeros((T, NC), F32)
        for j in range(J):
            h = g * J + j
            cols = slice(h * HEAD_DIM, (h + 1) * HEAD_DIM)
            qh = (_rms(q_ref[0, :, cols], qg_ref[...]) * SCALE).astype(BF16)
            qn_ref[0, :, cols] = qh
            s = _dot_nt(qh, kc) + pltpu.roll(cb_ref[h], shift, axis=1)
            s = jnp.where(vis, s, NEG_INF)
            e = jnp.exp(s - jnp.max(s, axis=-1, keepdims=True))
            p = e / jnp.sum(e, axis=-1, keepdims=True) * visf
            oc_ref[0, :, cols] = _dot(p.astype(BF16), vc)
            psum = psum + p
        imp = imp + _dot(psum, ov_ref[g], HIGHEST)
    mm = lane % NS
    cur = t // SLC_LEN
    forced = (mm == 0) | (mm == cur) | (mm == cur - 1)
    imp = jnp.where(forced, FORCE_SCORE, jnp.where(mm * SLC_LEN <= t, imp, -1.0))
    rank = jnp.zeros((T, LANES), F32)
    for r in range(1, NS):
        wrapped = mm >= NS - r
        other = jnp.where(wrapped, pltpu.roll(imp, NS - r, axis=1), pltpu.roll(imp, LANES - r, axis=1))
        rank = rank + jnp.where(other > imp, 1.0, jnp.where(other == imp, wrapped.astype(F32), 0.0))
    sel_ref[0] = jnp.where(rank < SLC_TOPN, jnp.where(imp >= 0.0, 1.0, 0.0), 0.0)


def _nsa_cmp_attn(y, kcvc, overlap, cmp_bias, q_g):
    B, S, _ = y.shape
    T = ATT_TILE
    NC = kcvc.shape[2]
    return pl.pallas_call(
        _cmp_attn_kernel,
        out_shape=(jax.ShapeDtypeStruct((B, S, NSA_WIDTH), F32),
                   jax.ShapeDtypeStruct((B, S, LANES), F32),
                   jax.ShapeDtypeStruct((B, S, NSA_WIDTH), BF16)),
        grid=(B, S // T),
        in_specs=[pl.BlockSpec((1, T, NSA_WIDTH), lambda b, i: (b, i, 0)),
                  pl.BlockSpec((1, 2 * NSA_GROUPS, NC, HEAD_DIM), lambda b, i: (b, 0, 0, 0)),
                  pl.BlockSpec(overlap.shape, lambda b, i: (0, 0, 0)),
                  pl.BlockSpec(cmp_bias.shape, lambda b, i: (0, 0, 0)),
                  pl.BlockSpec((1, HEAD_DIM), lambda b, i: (0, 0))],
        out_specs=(pl.BlockSpec((1, T, NSA_WIDTH), lambda b, i: (b, i, 0)),
                   pl.BlockSpec((1, T, LANES), lambda b, i: (b, i, 0)),
                   pl.BlockSpec((1, T, NSA_WIDTH), lambda b, i: (b, i, 0))),
        compiler_params=_cparams(("arbitrary", "arbitrary")),
        name="nsa_cmp_attn",
    )(y, kcvc, overlap, cmp_bias, q_g.reshape(1, HEAD_DIM))


def _kv_prep_kernel(ks_ref, kw_ref, kg1_ref, kg2_ref, ksn_ref, vs_ref, kwn_ref, vw_ref):
    KV = NSA_KV
    rows = ks_ref.shape[1]
    ones_col = (lax.broadcasted_iota(jnp.int32, (rows, LANES - HEAD_DIM), 1) == 0).astype(BF16)
    for g in range(NSA_GROUPS):
        cols = slice(g * HEAD_DIM, (g + 1) * HEAD_DIM)
        vcols = slice(KV + g * HEAD_DIM, KV + (g + 1) * HEAD_DIM)
        ksn_ref[0, :, cols] = _rms(ks_ref[0, :, cols], kg1_ref[...]).astype(BF16)
        kwn_ref[0, :, cols] = _rms(kw_ref[0, :, cols], kg2_ref[...]).astype(BF16)
        vs_ref[0, :, g * LANES:g * LANES + HEAD_DIM] = ks_ref[0, :, vcols].astype(BF16)
        vw_ref[0, :, g * LANES:g * LANES + HEAD_DIM] = kw_ref[0, :, vcols].astype(BF16)
        vs_ref[0, :, g * LANES + HEAD_DIM:(g + 1) * LANES] = ones_col
        vw_ref[0, :, g * LANES + HEAD_DIM:(g + 1) * LANES] = ones_col


def _nsa_kv_prep(y, k_g):
    B, S, _ = y.shape
    ksblk = (NSA_WIDTH + 2 * NSA_KV) // (2 * NSA_KV)
    kout = jax.ShapeDtypeStruct((B, S, NSA_KV), BF16)
    vout = jax.ShapeDtypeStruct((B, S, NSA_GROUPS * LANES), BF16)
    kspec = pl.BlockSpec((1, ROW_TILE, NSA_KV), lambda b, i: (b, i, 0))
    vspec = pl.BlockSpec((1, ROW_TILE, NSA_GROUPS * LANES), lambda b, i: (b, i, 0))
    return pl.pallas_call(
        _kv_prep_kernel,
        out_shape=(kout, vout, kout, vout),
        grid=(B, S // ROW_TILE),
        in_specs=[pl.BlockSpec((1, ROW_TILE, 2 * NSA_KV), lambda b, i: (b, i, ksblk)),
                  pl.BlockSpec((1, ROW_TILE, 2 * NSA_KV), lambda b, i: (b, i, ksblk + 1)),
                  pl.BlockSpec((1, HEAD_DIM), lambda b, i: (0, 0)),
                  pl.BlockSpec((1, HEAD_DIM), lambda b, i: (0, 0))],
        out_specs=(kspec, vspec, kspec, vspec),
        compiler_params=_cparams(("arbitrary", "arbitrary")),
        name="nsa_kv_prep",
    )(y, y, k_g[1].reshape(1, HEAD_DIM), k_g[2].reshape(1, HEAD_DIM))


def _sel_win_kernel(qn_ref, ks_ref, vs_ref, kw_ref, vw_ref, gl_ref, sel_ref, oc_ref, bias_ref, e_ref, bg_ref,
                    o_ref, mk_scr):
    qi = pl.program_id(1)
    T = qn_ref.shape[1]
    S = ks_ref.shape[1]
    npair = S // (2 * T)
    G, J = NSA_GROUPS, NSA_HPG
    NS = e_ref.shape[0]
    wt = WINDOW // T

    def sel_tile(r):
        return jnp.where(r < 0, 4, jnp.minimum(r, 2))

    def win_tile(r):
        return jnp.where(r < 0, 4, jnp.where(r <= 1, r, jnp.where(r < wt, 2, jnp.where(r == wt, 3, 4))))

    def bias4(g, i0, i1):
        return jnp.concatenate(
            [jnp.concatenate([bias_ref[g * J + j, i0], bias_ref[g * J + j, i1]], axis=1) for j in range(J)],
            axis=0)

    q4 = []
    for g in range(G):
        selg = sel_ref[0, :, g * NS:(g + 1) * NS].astype(BF16)
        for kp in range(npair):
            mk_scr[g, kp] = (_dot(selg, e_ref[:, kp * 2 * T:(kp + 1) * 2 * T]) - 1.0) * (-NEG_INF)
        q4.append(jnp.concatenate(
            [qn_ref[0, :, (g * J + j) * HEAD_DIM:(g * J + j + 1) * HEAD_DIM] for j in range(J)], axis=0))

    def sel_step(kp, carry, g):
        ks = pl.ds(pl.multiple_of(kp * 2 * T, 2 * T), 2 * T)
        r0 = qi - 2 * kp
        s = _dot_nt(q4[g], ks_ref[0, ks, g * HEAD_DIM:(g + 1) * HEAD_DIM])
        s = s + bias4(g, sel_tile(r0), sel_tile(r0 - 1)) + jnp.concatenate([mk_scr[g, kp]] * J, axis=0)
        return _softmax_step_ones(carry, s, vs_ref[0, ks, g * LANES:(g + 1) * LANES])

    def win_step(kp, carry, g):
        ks = pl.ds(pl.multiple_of(kp * 2 * T, 2 * T), 2 * T)
        r0 = qi - 2 * kp
        s = _dot_nt(q4[g], kw_ref[0, ks, g * HEAD_DIM:(g + 1) * HEAD_DIM])
        s = s + bias4(g, win_tile(r0), win_tile(r0 - 1))
        return _softmax_step_ones(carry, s, vw_ref[0, ks, g * LANES:(g + 1) * LANES])

    init = tuple((jnp.full((J * T, 1), NEG_INF, F32), jnp.zeros((J * T, LANES), F32)) for _ in range(G))
    kw0 = jnp.maximum(qi - wt, 0) // 2
    cs = lax.fori_loop(0, kw0, lambda kp, c: tuple(sel_step(kp, c[g], g) for g in range(G)), init)
    cs, cw = lax.fori_loop(
        kw0, qi // 2 + 1,
        lambda kp, c: (tuple(sel_step(kp, c[0][g], g) for g in range(G)),
                       tuple(win_step(kp, c[1][g], g) for g in range(G))),
        (cs, init))

    gates = jax.nn.sigmoid(gl_ref[0] + bg_ref[...])
    for g in range(G):
        o_s = cs[g][1][:, :HEAD_DIM] / cs[g][1][:, HEAD_DIM:HEAD_DIM + 1]
        o_w = cw[g][1][:, :HEAD_DIM] / cw[g][1][:, HEAD_DIM:HEAD_DIM + 1]
        for j in range(J):
            h = g * J + j
            cols = slice(h * HEAD_DIM, (h + 1) * HEAD_DIM)
            rows = slice(j * T, (j + 1) * T)
            o_ref[0, :, cols] = (gates[:, 3 * h:3 * h + 1] * oc_ref[0, :, cols]
                                 + gates[:, 3 * h + 1:3 * h + 2] * o_s[rows]
                                 + gates[:, 3 * h + 2:3 * h + 3] * o_w[rows])


def _nsa_sel_win(qn, y, kv, sel, oc, bias_tiles, expand, b_gate):
    B, S, _ = y.shape
    T = ATT_TILE
    glblk = (NSA_WIDTH + 6 * NSA_KV) // LANES
    bg = jnp.zeros((1, LANES), F32).at[0, :3 * NSA_HEADS].set(b_gate)
    kspec = pl.BlockSpec((1, S, NSA_KV), lambda b, i: (b, 0, 0))
    vspec = pl.BlockSpec((1, S, NSA_GROUPS * LANES), lambda b, i: (b, 0, 0))
    return pl.pallas_call(
        _sel_win_kernel,
        out_shape=jax.ShapeDtypeStruct((B, S, NSA_WIDTH), F32),
        grid=(B, S // T),
        in_specs=[pl.BlockSpec((1, T, NSA_WIDTH), lambda b, i: (b, i, 0)),
                  kspec, vspec, kspec, vspec,
                  pl.BlockSpec((1, T, LANES), lambda b, i: (b, i, glblk)),
                  pl.BlockSpec((1, T, sel.shape[2]), lambda b, i: (b, i, 0)),
                  pl.BlockSpec((1, T, NSA_WIDTH), lambda b, i: (b, i, 0)),
                  pl.BlockSpec(bias_tiles.shape, lambda b, i: (0, 0, 0, 0)),
                  pl.BlockSpec(expand.shape, lambda b, i: (0, 0)),
                  pl.BlockSpec((1, LANES), lambda b, i: (0, 0))],
        out_specs=pl.BlockSpec((1, T, NSA_WIDTH), lambda b, i: (b, i, 0)),
        scratch_shapes=[pltpu.VMEM((NSA_GROUPS, S // (2 * T), T, 2 * T), F32)],
        compiler_params=_cparams(("arbitrary", "arbitrary")),
        name="nsa_sel_win",
    )(qn, *kv, y, sel, oc, bias_tiles, expand, bg)


def _odd_mixer(x, sh, sc, gate, ng, w_in, b_gate, q_g, k_g, cmp_pos, cmp_w1, cmp_w2, bias_tiles, cmp_bias,
               w_out):
    B, S, _ = x.shape
    G = NSA_GROUPS
    w = jnp.zeros((D_MODEL, ODD_IN_PAD), BF16).at[:, :ODD_IN].set(w_in.astype(BF16))
    y = _norm_proj(x, sh, sc, ng, w)
    xr = y[:, :, NSA_WIDTH:NSA_WIDTH + 2 * NSA_KV].reshape(B, S, 2, G, HEAD_DIM)
    xr = xr.transpose(0, 2, 3, 1, 4).reshape(B, 2 * G, S // CMP_STRIDE, CMP_STRIDE * HEAD_DIM)
    kcvc = _nsa_compress(xr, cmp_pos.reshape(2, CMP_LEN * HEAD_DIM), cmp_w1, cmp_w2, k_g[0])
    n_slc = S // SLC_LEN
    starts = jnp.arange(S // CMP_STRIDE) * CMP_STRIDE
    bstart = jnp.arange(n_slc) * SLC_LEN
    real = (jnp.arange(S // CMP_STRIDE) < (S - CMP_LEN) // CMP_STRIDE + 1)[:, None]
    overlap = (real & (starts[:, None] < bstart[None, :] + SLC_LEN)
               & (starts[:, None] + CMP_LEN > bstart[None, :])).astype(F32)
    overlap = jnp.stack([jnp.pad(overlap, ((0, 0), (g * n_slc, (G - 1 - g) * n_slc))) for g in range(G)])
    expand = (jnp.arange(S)[None, :] // SLC_LEN == jnp.arange(n_slc)[:, None]).astype(BF16)
    oc, sel, qn = _nsa_cmp_attn(y, kcvc, overlap, cmp_bias, q_g)
    o = _nsa_sel_win(qn, y, _nsa_kv_prep(y, k_g), sel, oc, bias_tiles, expand, b_gate)
    return _out_proj(x, o, 0, o, 1, w_out.astype(BF16), gate)


def _even_mixer(x, sh, sc, gate, ng, w_in, b_f, conv_w, q_g, k_g, w_out):
    w = jnp.zeros((D_MODEL, EVEN_IN_PAD), BF16).at[:, :EVEN_IN].set(w_in.astype(BF16))
    y = _norm_proj(x, sh, sc, ng, w)
    conv = _short_conv(y, conv_w)
    attn = _fox_attention(y, b_f, q_g, k_g)
    return _out_proj(x, conv, 0, attn, 0, w_out.astype(BF16), gate)


def kernel(x, c, ada_w, ada_b, norm_g, even_w_in, even_b_f, even_conv_w, even_q_g, even_k_g, even_w_out, odd_w_in, odd_b_gate, odd_q_g, odd_k_g, odd_cmp_pos, odd_cmp_w1, odd_cmp_w2, odd_w_out, rel_table, peer_w_q, peer_keys, peer_u, peer_v):
    B = x.shape[0]
    mods = _ada(c, ada_w, ada_b)
    bias_tiles, cmp_bias = _relbias(rel_table)
    for layer in range(DEPTH):
        sh1, sc1, g1, sh2, sc2, g2 = [m.reshape(B, 1, D_MODEL) for m in jnp.split(mods[layer], 6, axis=-1)]
        i = layer // 2
        ng1 = norm_g[layer, 0].reshape(1, D_MODEL)
        ng2 = norm_g[layer, 1].reshape(1, D_MODEL)
        if layer % 2 == 0:
            x = _even_mixer(x, sh1, sc1, g1, ng1, even_w_in[i], even_b_f[i], even_conv_w[i], even_q_g[i],
                            even_k_g[i], even_w_out[i])
        else:
            x = _odd_mixer(x, sh1, sc1, g1, ng1, odd_w_in[i], odd_b_gate[i], odd_q_g[i], odd_k_g[i],
                           odd_cmp_pos[i], odd_cmp_w1[i], odd_cmp_w2[i], bias_tiles, cmp_bias, odd_w_out[i])
        x = _peer(x, sh2, sc2, g2, ng2, peer_w_q[layer], peer_keys[layer],
                  peer_u[layer].astype(BF16), peer_v[layer].T.astype(BF16))
    return x
```

```python
import functools
import math

import jax
import jax.numpy as jnp
from jax import lax
from jax.experimental import pallas as pl
from jax.experimental.pallas import tpu as pltpu

D_MODEL = 1024
DEPTH = 4
HEAD_DIM = 64
CONV_CH = 512
CONV_K = 3
FOX_HEADS = 8
FOX_WIDTH = FOX_HEADS * HEAD_DIM
EVEN_IN = 3 * CONV_CH + 3 * FOX_WIDTH + FOX_HEADS
NSA_GROUPS = 4
NSA_HPG = 4
NSA_HEADS = NSA_GROUPS * NSA_HPG
NSA_WIDTH = NSA_HEADS * HEAD_DIM
NSA_KV = NSA_GROUPS * HEAD_DIM
CMP_LEN = 32
CMP_STRIDE = 16
SLC_LEN = 64
SLC_TOPN = 16
WINDOW = 512
ODD_IN = NSA_WIDTH + 6 * NSA_KV + 3 * NSA_HEADS
REL_BUCKETS = 32
REL_MAX_DIST = 128
PEER_HEADS = 8
PEER_NKEYS = 128
PEER_EXPERTS = PEER_NKEYS * PEER_NKEYS
PEER_QDIM = 128
PEER_TOPK = 16
EPS = 1e-6
NEG_INF = -1e30
FORCE_SCORE = 1e6

LANES = 128
F32 = jnp.float32
BF16 = jnp.bfloat16
HIGHEST = lax.Precision.HIGHEST
VMEM_LIMIT = 56 * 1024 * 1024

EVEN_IN_PAD = 3200
ODD_IN_PAD = 2688
ROW_TILE = 512
ATT_TILE = 128
FOX_TILE = 256
PEER_TN = 512
PEER_EC = 2048
SCALE = HEAD_DIM ** -0.5


def _cparams(sem, flags=None):
    return pltpu.CompilerParams(dimension_semantics=sem, vmem_limit_bytes=VMEM_LIMIT, flags=flags)


def _dot(a, b, precision=None):
    return jnp.dot(a, b, precision=precision, preferred_element_type=F32)


def _dot_nt(a, b, precision=None):
    return lax.dot_general(a, b, (((1,), (1,)), ((), ())), precision=precision,
                           preferred_element_type=F32)


def _rms(x, g):
    return x * lax.rsqrt(jnp.mean(x * x, axis=-1, keepdims=True) + EPS) * g


def _ada_kernel(c_ref, w_ref, b_ref, o_ref):
    c = c_ref[...]
    sc = c * jax.nn.sigmoid(c)
    o_ref[0] = _dot(sc, w_ref[0], HIGHEST) + b_ref[0]


def _ada(c, ada_w, ada_b):
    B, D = c.shape
    E = ada_w.shape[2]
    tn = 2048
    return pl.pallas_call(
        _ada_kernel,
        out_shape=jax.ShapeDtypeStruct((DEPTH, B, E), F32),
        grid=(DEPTH, E // tn),
        in_specs=[pl.BlockSpec((B, D), lambda l, j: (0, 0)),
                  pl.BlockSpec((1, D, tn), lambda l, j: (l, 0, j)),
                  pl.BlockSpec((1, 1, tn), lambda l, j: (l, 0, j))],
        out_specs=pl.BlockSpec((1, B, tn), lambda l, j: (l, 0, j)),
        compiler_params=_cparams(("arbitrary", "arbitrary")),
        name="ada_mod",
    )(c, ada_w, ada_b.reshape(DEPTH, 1, E))


def _norm_proj_kernel(x_ref, sh_ref, sc_ref, ng_ref, w_ref, y_ref):
    h = _rms(x_ref[0], ng_ref[...]) * (1.0 + sc_ref[0]) + sh_ref[0]
    y_ref[0] = _dot(h.astype(BF16), w_ref[...])


def _norm_proj(x, sh, sc, ng, w):
    B, S, D = x.shape
    N = w.shape[1]
    return pl.pallas_call(
        _norm_proj_kernel,
        out_shape=jax.ShapeDtypeStruct((B, S, N), F32),
        grid=(B, S // ROW_TILE),
        in_specs=[pl.BlockSpec((1, ROW_TILE, D), lambda b, i: (b, i, 0)),
                  pl.BlockSpec((1, 1, D), lambda b, i: (b, 0, 0)),
                  pl.BlockSpec((1, 1, D), lambda b, i: (b, 0, 0)),
                  pl.BlockSpec((1, D), lambda b, i: (0, 0)),
                  pl.BlockSpec((D, N), lambda b, i: (0, 0))],
        out_specs=pl.BlockSpec((1, ROW_TILE, N), lambda b, i: (b, i, 0)),
        compiler_params=_cparams(("arbitrary", "arbitrary")),
        name="norm_proj",
    )(x, sh, sc, ng, w)


def _out_proj_kernel(x_ref, a_ref, b_ref, wa_ref, wb_ref, g_ref, o_ref):
    y = _dot(a_ref[0].astype(BF16), wa_ref[...]) + _dot(b_ref[0].astype(BF16), wb_ref[...])
    o_ref[0] = x_ref[0] + g_ref[0] * y


def _out_proj(x, a, a_blk, b, b_blk, w, gate):
    B, S, D = x.shape
    half = w.shape[0] // 2
    wa, wb = w[:half], w[half:]
    return pl.pallas_call(
        _out_proj_kernel,
        out_shape=jax.ShapeDtypeStruct((B, S, D), F32),
        grid=(B, S // ROW_TILE),
        in_specs=[pl.BlockSpec((1, ROW_TILE, D), lambda b, i: (b, i, 0)),
                  pl.BlockSpec((1, ROW_TILE, half), lambda b, i: (b, i, a_blk)),
                  pl.BlockSpec((1, ROW_TILE, half), lambda b, i: (b, i, b_blk)),
                  pl.BlockSpec((half, D), lambda b, i: (0, 0)),
                  pl.BlockSpec((half, D), lambda b, i: (0, 0)),
                  pl.BlockSpec((1, 1, D), lambda b, i: (b, 0, 0))],
        out_specs=pl.BlockSpec((1, ROW_TILE, D), lambda b, i: (b, i, 0)),
        compiler_params=_cparams(("arbitrary", "arbitrary")),
        name="out_proj",
    )(x, a, b, wa, wb, gate)


def _conv_kernel(cb_ref, cc_ref, ch_ref, w_ref, o_ref):
    u = cc_ref[0] * ch_ref[0]
    row = lax.broadcasted_iota(jnp.int32, u.shape, 0)
    u1 = jnp.where(row >= 1, pltpu.roll(u, 1, axis=0), 0.0)
    u2 = jnp.where(row >= 2, pltpu.roll(u, 2, axis=0), 0.0)
    w = w_ref[...]
    o_ref[0] = cb_ref[0] * (w[0:1] * u2 + w[1:2] * u1 + w[2:3] * u)


def _short_conv(y, conv_w):
    B, S, _ = y.shape
    nblk = CONV_CH // LANES
    w = jnp.zeros((8, CONV_CH), F32).at[:CONV_K].set(conv_w)
    return pl.pallas_call(
        _conv_kernel,
        out_shape=jax.ShapeDtypeStruct((B, S, CONV_CH), F32),
        grid=(B, nblk),
        in_specs=[pl.BlockSpec((1, S, LANES), lambda b, j: (b, 0, j)),
                  pl.BlockSpec((1, S, LANES), lambda b, j: (b, 0, nblk + j)),
                  pl.BlockSpec((1, S, LANES), lambda b, j: (b, 0, 2 * nblk + j)),
                  pl.BlockSpec((8, LANES), lambda b, j: (0, j))],
        out_specs=pl.BlockSpec((1, S, LANES), lambda b, j: (b, 0, j)),
        compiler_params=_cparams(("arbitrary", "arbitrary")),
        name="short_conv",
    )(y, y, y, w)


def _cumsum_rows(x):
    n = x.shape[0]
    row = lax.broadcasted_iota(jnp.int32, x.shape, 0)
    sh = 1
    while sh < n:
        x = x + jnp.where(row >= sh, pltpu.roll(x, sh, axis=0), 0.0)
        sh *= 2
    return x


FOX_KEY_PARTS = 4


def _fox_kernel(q_ref, k_ref, v_ref, f_ref, bf_ref, qg_ref, kg_ref, o_ref,
                kn_scr, vb_scr, cum_scr, cumrow_scr):
    qi = pl.program_id(1)
    S = k_ref.shape[1]
    T = q_ref.shape[1]
    nq = S // T

    @pl.when(qi == 0)
    def _prep():
        ones_col = (lax.broadcasted_iota(jnp.int32, (S, LANES - HEAD_DIM), 1) == 0).astype(BF16)
        for h in range(FOX_HEADS):
            cols = slice(h * HEAD_DIM, (h + 1) * HEAD_DIM)
            kn_scr[:, cols] = _rms(k_ref[0, :, cols], kg_ref[...]).astype(BF16)
            vb_scr[:, h * LANES:h * LANES + HEAD_DIM] = v_ref[0, :, cols].astype(BF16)
            vb_scr[:, h * LANES + HEAD_DIM:(h + 1) * LANES] = ones_col
        logf = jax.nn.log_sigmoid(f_ref[0] + bf_ref[...])
        cum = _cumsum_rows(logf)
        cum_scr[...] = cum
        for kt in range(S // T):
            cumrow_scr[:, kt * T:(kt + 1) * T] = cum[kt * T:(kt + 1) * T, :].T[0:8, :]

    cq = cum_scr[pl.ds(pl.multiple_of(qi * T, T), T), :]

    def attend(nkeys):
        row = qi * T + lax.broadcasted_iota(jnp.int32, (T, nkeys), 0)
        causal = jnp.where(lax.broadcasted_iota(jnp.int32, (T, nkeys), 1) <= row, 0.0, NEG_INF)
        for h in range(FOX_HEADS):
            cols = slice(h * HEAD_DIM, (h + 1) * HEAD_DIM)
            qh = (_rms(q_ref[0, :, cols], qg_ref[...]) * SCALE).astype(BF16)
            s = _dot_nt(qh, kn_scr[0:nkeys, cols]) + ((cq[:, h:h + 1] - cumrow_scr[h:h + 1, 0:nkeys]) + causal)
            p = jnp.exp(s - jnp.max(s, axis=-1, keepdims=True))
            acc = _dot(p.astype(BF16), vb_scr[0:nkeys, h * LANES:(h + 1) * LANES])
            o_ref[0, :, cols] = acc[:, :HEAD_DIM] / acc[:, HEAD_DIM:HEAD_DIM + 1]

    parts = min(FOX_KEY_PARTS, nq)
    per = nq // parts
    for part in range(parts):
        pl.when((qi >= part * per) & (qi < (part + 1) * per))(functools.partial(attend, (part + 1) * (S // parts)))


def _fox_attention(y, b_f, q_g, k_g):
    B, S, _ = y.shape
    T = FOX_TILE
    qblk = 3 * CONV_CH // FOX_WIDTH
    fblk = (3 * CONV_CH + 3 * FOX_WIDTH) // LANES
    bf = jnp.zeros((1, LANES), F32).at[0, :FOX_HEADS].set(b_f)
    return pl.pallas_call(
        _fox_kernel,
        out_shape=jax.ShapeDtypeStruct((B, S, FOX_WIDTH), F32),
        grid=(B, S // T),
        in_specs=[pl.BlockSpec((1, T, FOX_WIDTH), lambda b, i: (b, i, qblk)),
                  pl.BlockSpec((1, S, FOX_WIDTH), lambda b, i: (b, 0, qblk + 1)),
                  pl.BlockSpec((1, S, FOX_WIDTH), lambda b, i: (b, 0, qblk + 2)),
                  pl.BlockSpec((1, S, LANES), lambda b, i: (b, 0, fblk)),
                  pl.BlockSpec((1, LANES), lambda b, i: (0, 0)),
                  pl.BlockSpec((1, HEAD_DIM), lambda b, i: (0, 0)),
                  pl.BlockSpec((1, HEAD_DIM), lambda b, i: (0, 0))],
        out_specs=pl.BlockSpec((1, T, FOX_WIDTH), lambda b, i: (b, i, 0)),
        scratch_shapes=[pltpu.VMEM((S, FOX_WIDTH), BF16),
                        pltpu.VMEM((S, FOX_HEADS * LANES), BF16),
                        pltpu.VMEM((S, LANES), F32),
                        pltpu.VMEM((8, S), F32)],
        compiler_params=_cparams(("arbitrary", "arbitrary")),
        name="fox_attention",
    )(y, y, y, y, bf, q_g.reshape(1, HEAD_DIM), k_g.reshape(1, HEAD_DIM))


def _extract_top(work, iota, k, exact):
    nrows = work.shape[0]
    vals = []
    for _ in range(k):
        m = jnp.max(work, axis=0, keepdims=True)
        if exact:
            idx = jnp.min(jnp.where(work == m, iota, nrows), axis=0, keepdims=True)
            work = jnp.where(iota == idx, -jnp.inf, work)
        else:
            work = jnp.where(work == m, -jnp.inf, work)
        vals.append(m)
    return vals, work


def _removed(work):
    return jnp.sum(jnp.where(work == -jnp.inf, 1.0, 0.0), axis=0, keepdims=True)


def _peer_kernel(x_ref, sh_ref, sc_ref, gt_ref, ng_ref, wqh_ref, wql_ref, keys_ref, u_ref, vt_ref, o_ref,
                 ht_scr, qt_scr, sp_scr, s0_scr, e0_scr, s1_scr, e1_scr, tau_scr, a_scr, p_scr, acc_scr):
    c = pl.program_id(2)
    nc = pl.num_programs(2)
    tn = x_ref.shape[1]
    nlt = tn // LANES
    half = PEER_QDIM // 2

    @pl.when(c == 0)
    def _prep():
        h = _rms(x_ref[0], ng_ref[...]) * (1.0 + sc_ref[0]) + sh_ref[0]
        hh = h.astype(BF16)
        hl = (h - hh.astype(F32)).astype(BF16)
        cw = 2 * LANES
        for k in range(h.shape[1] // cw):
            cs = slice(k * cw, (k + 1) * cw)
            q = _dot(hh, wqh_ref[:, cs]) + (_dot(hh, wql_ref[:, cs]) + _dot(hl, wqh_ref[:, cs]))
            qt_scr[cs, :] = q.T
            ht_scr[cs, :] = h[:, cs].T.astype(BF16)
        acc_scr[...] = jnp.zeros_like(acc_scr)
        iota_k = lax.broadcasted_iota(jnp.int32, (PEER_NKEYS, LANES), 0)
        iota_c = lax.broadcasted_iota(jnp.int32, (PEER_TOPK + 8 * 8, LANES), 0)

        n_pad = sum(8 - PEER_TOPK // (r0 + 1) for r0 in range(2, 8))

        def select(hd, lt, exact):
            lanes = slice(lt * LANES, (lt + 1) * LANES)
            s0 = sp_scr[0, :, lanes]
            s1 = sp_scr[1, :, lanes]
            sv0, w0 = _extract_top(s0, iota_k, PEER_TOPK, exact)
            sv1, w1 = _extract_top(s1, iota_k, PEER_TOPK, exact)
            s0m = jnp.where(w0 == -jnp.inf, s0, -jnp.inf)
            s1m = jnp.where(w1 == -jnp.inf, s1, -jnp.inf)
            sv0_all = jnp.concatenate(sv0, axis=0)
            sv1_all = jnp.concatenate(sv1, axis=0)
            parts = [sv0[0] + sv1_all, sv0[1] + sv1_all[:8]]
            for r0 in range(2, 8):
                parts.append(jnp.where(iota_c[:8] < PEER_TOPK // (r0 + 1), sv0[r0] + sv1_all[:8], -jnp.inf))
            parts.append(sv0_all[8:] + sv1[0])
            cv, wc = _extract_top(jnp.concatenate(parts, axis=0), iota_c, PEER_TOPK, exact)
            z = cv[0] - cv[0]
            for r in range(PEER_TOPK):
                z = z + jnp.exp(cv[r] - cv[0])
            s0_scr[hd, :, lanes] = s0m
            s1_scr[hd, :, lanes] = s1m
            e0_scr[hd, :, lanes] = jnp.exp(s0m - sv0[0])
            e1_scr[hd, :, lanes] = jnp.exp(s1m - sv1[0]) / z
            tau_scr[hd, :, lanes] = jnp.broadcast_to(cv[PEER_TOPK - 1], (8, LANES))
            return ((_removed(w0) != PEER_TOPK) | (_removed(w1) != PEER_TOPK)
                    | (_removed(wc) != PEER_TOPK + n_pad))

        def head_body(hd, carry):
            for p in range(2):
                qs = qt_scr[pl.ds(pl.multiple_of((hd * 2 + p) * half, half), half), :]
                sp_scr[p] = _dot(keys_ref[p], qs, HIGHEST)
            tied = jnp.zeros((1, LANES), F32)
            for lt in range(nlt):
                tied = jnp.maximum(tied, jnp.where(select(hd, lt, False), 1.0, 0.0))

            @pl.when(jnp.max(tied) > 0.0)
            def _redo():
                for lt in range(nlt):
                    select(hd, lt, True)
            return carry

        lax.fori_loop(0, PEER_HEADS, head_body, 0)

    a_scr[...] = jax.nn.gelu(_dot(u_ref[...], ht_scr[...]), approximate=True)
    nsub = 8
    nrq = 32

    def weights_tile(rq, carry, lt, part):
        lanes = slice(lt * LANES, (lt + 1) * LANES)
        jr = pl.ds(pl.multiple_of(rq * nrq, nrq), nrq)
        i0 = pl.ds(pl.multiple_of(c * (PEER_EC // PEER_NKEYS) + part * nsub, nsub), nsub)
        w = [jnp.zeros((nrq, LANES), F32) for _ in range(nsub)]
        for hd in range(PEER_HEADS):
            s1 = s1_scr[hd, jr, lanes]
            e1 = e1_scr[hd, jr, lanes]
            s0 = s0_scr[hd, i0, lanes]
            e0 = e0_scr[hd, i0, lanes]
            tau = tau_scr[hd, 0:1, lanes]
            for ii in range(nsub):
                keep = s1 + s0[ii:ii + 1] >= tau
                w[ii] = w[ii] + jnp.where(keep, e1 * e0[ii:ii + 1], 0.0)
        for ii in range(nsub):
            rows = pl.ds(pl.multiple_of((part * nsub + ii) * PEER_NKEYS + rq * nrq, nrq), nrq)
            p_scr[rows, lanes] = (w[ii] * a_scr[rows, lanes]).astype(BF16)
        return carry

    for part in range(PEER_EC // PEER_NKEYS // nsub):
        for lt in range(nlt):
            lax.fori_loop(0, PEER_NKEYS // nrq, functools.partial(weights_tile, lt=lt, part=part), 0)
    acc_scr[...] += _dot(vt_ref[...], p_scr[...])

    @pl.when(c == nc - 1)
    def _fin():
        o_ref[0] = x_ref[0] + gt_ref[0] * acc_scr[...].T


def _peer(x, sh, sc, gate, ng, w_q, keys, u_b, vt_b):
    B, S, D = x.shape
    tn = PEER_TN
    nc = PEER_EXPERTS // PEER_EC
    tab = pltpu.VMEM((PEER_HEADS, PEER_NKEYS, tn), F32)
    wq_hi = w_q.astype(BF16)
    wq_lo = (w_q - wq_hi.astype(F32)).astype(BF16)
    wspec = pl.BlockSpec((D, PEER_HEADS * PEER_QDIM), lambda b, i, c: (0, 0), pipeline_mode=pl.Buffered(1))
    return pl.pallas_call(
        _peer_kernel,
        out_shape=jax.ShapeDtypeStruct((B, S, D), F32),
        grid=(B, S // tn, nc),
        in_specs=[pl.BlockSpec((1, tn, D), lambda b, i, c: (b, i, 0), pipeline_mode=pl.Buffered(1)),
                  pl.BlockSpec((1, 1, D), lambda b, i, c: (b, 0, 0)),
                  pl.BlockSpec((1, 1, D), lambda b, i, c: (b, 0, 0)),
                  pl.BlockSpec((1, 1, D), lambda b, i, c: (b, 0, 0)),
                  pl.BlockSpec((1, D), lambda b, i, c: (0, 0)),
                  wspec, wspec,
                  pl.BlockSpec((2, PEER_NKEYS, PEER_QDIM // 2), lambda b, i, c: (0, 0, 0)),
                  pl.BlockSpec((PEER_EC, D), lambda b, i, c: (c, 0)),
                  pl.BlockSpec((D, PEER_EC), lambda b, i, c: (0, c))],
        out_specs=pl.BlockSpec((1, tn, D), lambda b, i, c: (b, i, 0)),
        scratch_shapes=[pltpu.VMEM((D, tn), BF16),
                        pltpu.VMEM((PEER_HEADS * PEER_QDIM, tn), F32),
                        pltpu.VMEM((2, PEER_NKEYS, tn), F32),
                        tab, tab, tab, tab,
                        pltpu.VMEM((PEER_HEADS, 8, tn), F32),
                        pltpu.VMEM((PEER_EC, tn), F32),
                        pltpu.VMEM((PEER_EC, tn), BF16),
                        pltpu.VMEM((D, tn), F32)],
        compiler_params=_cparams(("arbitrary", "arbitrary", "arbitrary")),
        name="peer",
    )(x, sh, sc, gate, ng, wq_hi, wq_lo, keys, u_b, vt_b)


def _rel_bucket(dist):
    n = jnp.maximum(dist, 0)
    max_exact = REL_BUCKETS // 2
    nf = jnp.maximum(n, 1).astype(F32)
    large = max_exact + (jnp.log(nf / max_exact) / math.log(REL_MAX_DIST / max_exact)
                         * (REL_BUCKETS - max_exact)).astype(jnp.int32)
    large = jnp.minimum(large, REL_BUCKETS - 1)
    return jnp.where(n < max_exact, n, large)


def _bias_lookup(bucket, tab_ref, h):
    bias = jnp.zeros(bucket.shape, F32)
    for b in range(REL_BUCKETS):
        bias = jnp.where(bucket == b, tab_ref[b, h], bias)
    return bias


N_BIAS_TILES = 5


def _relbias_kernel(tab_ref, o_ref, c_ref):
    h = pl.program_id(0)
    T = o_ref.shape[2]
    a = lax.broadcasted_iota(jnp.int32, (T, T), 0)
    b = lax.broadcasted_iota(jnp.int32, (T, T), 1)
    last = tab_ref[REL_BUCKETS - 1, h]
    o_ref[0, 0] = jnp.where(a >= b, _bias_lookup(_rel_bucket(a - b), tab_ref, h), NEG_INF)
    o_ref[0, 1] = _bias_lookup(_rel_bucket(a - b + T), tab_ref, h)
    o_ref[0, 2] = jnp.full((T, T), last, F32)
    o_ref[0, 3] = jnp.where(a < b, last, NEG_INF)
    o_ref[0, 4] = jnp.full((T, T), NEG_INF, F32)
    dist = a - ((b - T // 2) * CMP_STRIDE + CMP_LEN - 1)
    c_ref[0] = jnp.where(dist >= 0, _bias_lookup(_rel_bucket(dist), tab_ref, h), last)


def _relbias(rel_table):
    T = ATT_TILE
    assert 2 * T - (T - 1) >= REL_MAX_DIST
    assert WINDOW % T == 0
    return pl.pallas_call(
        _relbias_kernel,
        out_shape=(jax.ShapeDtypeStruct((NSA_HEADS, N_BIAS_TILES, T, T), F32),
                   jax.ShapeDtypeStruct((NSA_HEADS, T, T), F32)),
        grid=(NSA_HEADS,),
        in_specs=[pl.BlockSpec(memory_space=pltpu.SMEM)],
        out_specs=(pl.BlockSpec((1, N_BIAS_TILES, T, T), lambda h: (h, 0, 0, 0)),
                   pl.BlockSpec((1, T, T), lambda h: (h, 0, 0))),
        compiler_params=_cparams(("arbitrary",)),
        name="rel_bias_tiles",
    )(rel_table)


def _cmp_kernel(x_ref, pos_ref, w1_ref, w2_ref, kg_ref, o_ref):
    j = pl.program_id(1)
    half = CMP_STRIDE * HEAD_DIM
    x = x_ref[0, 0]
    pos = pos_ref[0]
    pa = _dot((x + pos[:, :half]).astype(BF16), w1_ref[0, :half, :])
    pb = _dot((x + pos[:, half:]).astype(BF16), w1_ref[0, half:, :])
    pre = pa + pltpu.roll(pb, pb.shape[0] - 1, axis=0)
    out = _dot(jax.nn.gelu(pre, approximate=True).astype(BF16), w2_ref[0])
    o_ref[0, 0] = jnp.where(j < NSA_GROUPS, _rms(out, kg_ref[...]), out)


def _nsa_compress(xr, pos, w1, w2, kg0):
    B, _, M, W = xr.shape
    G = NSA_GROUPS
    return pl.pallas_call(
        _cmp_kernel,
        out_shape=jax.ShapeDtypeStruct((B, 2 * G, M, HEAD_DIM), F32),
        grid=(B, 2 * G),
        in_specs=[pl.BlockSpec((1, 1, M, W), lambda b, j: (b, j, 0, 0)),
                  pl.BlockSpec((1, 1, 2 * W), lambda b, j: (j // G, 0, 0)),
                  pl.BlockSpec((1, 2 * W, HEAD_DIM), lambda b, j: (j // G, 0, 0)),
                  pl.BlockSpec((1, HEAD_DIM, HEAD_DIM), lambda b, j: (j // G, 0, 0)),
                  pl.BlockSpec((1, HEAD_DIM), lambda b, j: (0, 0))],
        out_specs=pl.BlockSpec((1, 1, M, HEAD_DIM), lambda b, j: (b, j, 0, 0)),
        compiler_params=_cparams(("arbitrary", "arbitrary")),
        name="nsa_compress",
    )(xr, pos.reshape(2, 1, 2 * W), w1.astype(BF16), w2.astype(BF16), kg0.reshape(1, HEAD_DIM))


def _cmp_attn_kernel(q_ref, kv_ref, ov_ref, cb_ref, qg_ref, oc_ref, sel_ref, qn_ref):
    qi = pl.program_id(1)
    T = q_ref.shape[1]
    NC = kv_ref.shape[2]
    G, J = NSA_GROUPS, NSA_HPG
    NS = sel_ref.shape[2] // G
    assert T == NC == LANES and T % CMP_STRIDE == 0 and G * NS == LANES
    t = qi * T + lax.broadcasted_iota(jnp.int32, (T, NC), 0)
    lane = lax.broadcasted_iota(jnp.int32, (T, NC), 1)
    vis = t - (lane * CMP_STRIDE + CMP_LEN - 1) >= 0
    visf = vis.astype(F32)
    shift = (qi * (T // CMP_STRIDE) + T // 2) % LANES
    imp = jnp.zeros((T, LANES), F32)
    for g in range(G):
        kc = kv_ref[0, g].astype(BF16)
        vc = kv_ref[0, G + g].astype(BF16)
        psum = jnp.zeros((T, NC), F32)
        for j in range(J):
            h = g * J + j
            cols = slice(h * HEAD_DIM, (h + 1) * HEAD_DIM)
            qh = (_rms(q_ref[0, :, cols], qg_ref[...]) * SCALE).astype(BF16)
            qn_ref[0, :, cols] = qh
            s = _dot_nt(qh, kc) + pltpu.roll(cb_ref[h], shift, axis=1)
            s = jnp.where(vis, s, NEG_INF)
            e = jnp.exp(s - jnp.max(s, axis=-1, keepdims=True))
            p = e / jnp.sum(e, axis=-1, keepdims=True) * visf
            oc_ref[0, :, cols] = _dot(p.astype(BF16), vc)
            psum = psum + p
        imp = imp + _dot(psum, ov_ref[g], HIGHEST)
    mm = lane % NS
    cur = t // SLC_LEN
    forced = (mm == 0) | (mm == cur) | (mm == cur - 1)
    imp = jnp.where(forced, FORCE_SCORE, jnp.where(mm * SLC_LEN <= t, imp, -1.0))
    rank = jnp.zeros((T, LANES), F32)
    for r in range(1, NS):
        wrapped = mm >= NS - r
        other = jnp.where(wrapped, pltpu.roll(imp, NS - r, axis=1), pltpu.roll(imp, LANES - r, axis=1))
        rank = rank + jnp.where(other > imp, 1.0, jnp.where(other == imp, wrapped.astype(F32), 0.0))
    sel_ref[0] = jnp.where(rank < SLC_TOPN, jnp.where(imp >= 0.0, 1.0, 0.0), 0.0)


def _nsa_cmp_attn(y, kcvc, overlap, cmp_bias, q_g):
    B, S, _ = y.shape
    T = ATT_TILE
    NC = kcvc.shape[2]
    return pl.pallas_call(
        _cmp_attn_kernel,
        out_shape=(jax.ShapeDtypeStruct((B, S, NSA_WIDTH), F32),
                   jax.ShapeDtypeStruct((B, S, LANES), F32),
                   jax.ShapeDtypeStruct((B, S, NSA_WIDTH), BF16)),
        grid=(B, S // T),
        in_specs=[pl.BlockSpec((1, T, NSA_WIDTH), lambda b, i: (b, i, 0)),
                  pl.BlockSpec((1, 2 * NSA_GROUPS, NC, HEAD_DIM), lambda b, i: (b, 0, 0, 0)),
                  pl.BlockSpec(overlap.shape, lambda b, i: (0, 0, 0)),
                  pl.BlockSpec(cmp_bias.shape, lambda b, i: (0, 0, 0)),
                  pl.BlockSpec((1, HEAD_DIM), lambda b, i: (0, 0))],
        out_specs=(pl.BlockSpec((1, T, NSA_WIDTH), lambda b, i: (b, i, 0)),
                   pl.BlockSpec((1, T, LANES), lambda b, i: (b, i, 0)),
                   pl.BlockSpec((1, T, NSA_WIDTH), lambda b, i: (b, i, 0))),
        compiler_params=_cparams(("arbitrary", "arbitrary")),
        name="nsa_cmp_attn",
    )(y, kcvc, overlap, cmp_bias, q_g.reshape(1, HEAD_DIM))


def _kv_prep_kernel(ks_ref, kw_ref, kg1_ref, kg2_ref, ksn_ref, vs_ref, kwn_ref, vw_ref):
    KV = NSA_KV
    rows = ks_ref.shape[1]
    ones_col = (lax.broadcasted_iota(jnp.int32, (rows, LANES - HEAD_DIM), 1) == 0).astype(BF16)
    for g in range(NSA_GROUPS):
        cols = slice(g * HEAD_DIM, (g + 1) * HEAD_DIM)
        vcols = slice(KV + g * HEAD_DIM, KV + (g + 1) * HEAD_DIM)
        ksn_ref[0, :, cols] = _rms(ks_ref[0, :, cols], kg1_ref[...]).astype(BF16)
        kwn_ref[0, :, cols] = _rms(kw_ref[0, :, cols], kg2_ref[...]).astype(BF16)
        vs_ref[0, :, g * LANES:g * LANES + HEAD_DIM] = ks_ref[0, :, vcols].astype(BF16)
        vw_ref[0, :, g * LANES:g * LANES + HEAD_DIM] = kw_ref[0, :, vcols].astype(BF16)
        vs_ref[0, :, g * LANES + HEAD_DIM:(g + 1) * LANES] = ones_col
        vw_ref[0, :, g * LANES + HEAD_DIM:(g + 1) * LANES] = ones_col


def _nsa_kv_prep(y, k_g):
    B, S, _ = y.shape
    ksblk = (NSA_WIDTH + 2 * NSA_KV) // (2 * NSA_KV)
    kout = jax.ShapeDtypeStruct((B, S, NSA_KV), BF16)
    vout = jax.ShapeDtypeStruct((B, S, NSA_GROUPS * LANES), BF16)
    kspec = pl.BlockSpec((1, ROW_TILE, NSA_KV), lambda b, i: (b, i, 0))
    vspec = pl.BlockSpec((1, ROW_TILE, NSA_GROUPS * LANES), lambda b, i: (b, i, 0))
    return pl.pallas_call(
        _kv_prep_kernel,
        out_shape=(kout, vout, kout, vout),
        grid=(B, S // ROW_TILE),
        in_specs=[pl.BlockSpec((1, ROW_TILE, 2 * NSA_KV), lambda b, i: (b, i, ksblk)),
                  pl.BlockSpec((1, ROW_TILE, 2 * NSA_KV), lambda b, i: (b, i, ksblk + 1)),
                  pl.BlockSpec((1, HEAD_DIM), lambda b, i: (0, 0)),
                  pl.BlockSpec((1, HEAD_DIM), lambda b, i: (0, 0))],
        out_specs=(kspec, vspec, kspec, vspec),
        compiler_params=_cparams(("arbitrary", "arbitrary")),
        name="nsa_kv_prep",
    )(y, y, k_g[1].reshape(1, HEAD_DIM), k_g[2].reshape(1, HEAD_DIM))


SEL_KEY_PARTS = 4


def _sel_win_kernel(qn_ref, ks_ref, vs_ref, kw_ref, vw_ref, gl_ref, sel_ref, oc_ref, bias_ref, e_ref, bg_ref,
                    o_ref):
    qi = pl.program_id(1)
    T = qn_ref.shape[1]
    S = ks_ref.shape[1]
    nk = S // T
    G, J = NSA_GROUPS, NSA_HPG
    NS = e_ref.shape[0]
    wt = WINDOW // T
    nwin = wt + 1
    assert nk % SEL_KEY_PARTS == 0 and nk >= nwin

    def sel_tile(r):
        return jnp.where(r < 0, 4, jnp.minimum(r, 2))

    def win_tile(r):
        return jnp.where(r < 0, 4, jnp.where(r <= 1, r, jnp.where(r < wt, 2, jnp.where(r == wt, 3, 4))))

    def softmax_pv(s, v):
        p = jnp.exp(s - jnp.max(s, axis=-1, keepdims=True))
        acc = _dot(p.astype(BF16), v)
        return acc[:, :HEAD_DIM] / acc[:, HEAD_DIM:HEAD_DIM + 1]

    gates = jax.nn.sigmoid(gl_ref[0] + bg_ref[...])
    k0 = jnp.maximum(qi - wt, 0)
    wrows = pl.ds(pl.multiple_of(k0 * T, T), nwin * T)

    def attend(nkt):
        nkeys = nkt * T
        for g in range(G):
            gcols = slice(g * HEAD_DIM, (g + 1) * HEAD_DIM)
            vcols = slice(g * LANES, (g + 1) * LANES)
            q4 = jnp.concatenate(
                [qn_ref[0, :, (g * J + j) * HEAD_DIM:(g * J + j + 1) * HEAD_DIM] for j in range(J)], axis=0)
            selg = sel_ref[0, :, g * NS:(g + 1) * NS].astype(BF16)
            mask = (_dot(selg, e_ref[:, :nkeys]) - 1.0) * (-NEG_INF)
            bias = jnp.concatenate(
                [jnp.concatenate([bias_ref[g * J + j, sel_tile(qi - kt)] for kt in range(nkt)], axis=1) + mask
                 for j in range(J)], axis=0)
            o_s = softmax_pv(_dot_nt(q4, ks_ref[0, :nkeys, gcols]) + bias, vs_ref[0, :nkeys, vcols])
            wbias = jnp.concatenate(
                [jnp.concatenate([bias_ref[g * J + j, win_tile(qi - k0 - t)] for t in range(nwin)], axis=1)
                 for j in range(J)], axis=0)
            o_w = softmax_pv(_dot_nt(q4, kw_ref[0, wrows, gcols]) + wbias, vw_ref[0, wrows, vcols])
            for j in range(J):
                h = g * J + j
                cols = slice(h * HEAD_DIM, (h + 1) * HEAD_DIM)
                rows = slice(j * T, (j + 1) * T)
                o_ref[0, :, cols] = (gates[:, 3 * h:3 * h + 1] * oc_ref[0, :, cols]
                                     + gates[:, 3 * h + 1:3 * h + 2] * o_s[rows]
                                     + gates[:, 3 * h + 2:3 * h + 3] * o_w[rows])

    per = nk // SEL_KEY_PARTS
    for part in range(SEL_KEY_PARTS):
        pl.when((qi >= part * per) & (qi < (part + 1) * per))(functools.partial(attend, (part + 1) * per))


def _nsa_sel_win(qn, y, kv, sel, oc, bias_tiles, expand, b_gate):
    B, S, _ = y.shape
    T = ATT_TILE
    glblk = (NSA_WIDTH + 6 * NSA_KV) // LANES
    bg = jnp.zeros((1, LANES), F32).at[0, :3 * NSA_HEADS].set(b_gate)
    kspec = pl.BlockSpec((1, S, NSA_KV), lambda b, i: (b, 0, 0))
    vspec = pl.BlockSpec((1, S, NSA_GROUPS * LANES), lambda b, i: (b, 0, 0))
    return pl.pallas_call(
        _sel_win_kernel,
        out_shape=jax.ShapeDtypeStruct((B, S, NSA_WIDTH), F32),
        grid=(B, S // T),
        in_specs=[pl.BlockSpec((1, T, NSA_WIDTH), lambda b, i: (b, i, 0)),
                  kspec, vspec, kspec, vspec,
                  pl.BlockSpec((1, T, LANES), lambda b, i: (b, i, glblk)),
                  pl.BlockSpec((1, T, sel.shape[2]), lambda b, i: (b, i, 0)),
                  pl.BlockSpec((1, T, NSA_WIDTH), lambda b, i: (b, i, 0)),
                  pl.BlockSpec(bias_tiles.shape, lambda b, i: (0, 0, 0, 0)),
                  pl.BlockSpec(expand.shape, lambda b, i: (0, 0)),
                  pl.BlockSpec((1, LANES), lambda b, i: (0, 0))],
        out_specs=pl.BlockSpec((1, T, NSA_WIDTH), lambda b, i: (b, i, 0)),
        compiler_params=_cparams(("arbitrary", "arbitrary")),
        name="nsa_sel_win",
    )(qn, *kv, y, sel, oc, bias_tiles, expand, bg)


def _odd_mixer(x, sh, sc, gate, ng, w_in, b_gate, q_g, k_g, cmp_pos, cmp_w1, cmp_w2, bias_tiles, cmp_bias,
               w_out):
    B, S, _ = x.shape
    G = NSA_GROUPS
    w = jnp.zeros((D_MODEL, ODD_IN_PAD), BF16).at[:, :ODD_IN].set(w_in.astype(BF16))
    y = _norm_proj(x, sh, sc, ng, w)
    xr = y[:, :, NSA_WIDTH:NSA_WIDTH + 2 * NSA_KV].reshape(B, S, 2, G, HEAD_DIM)
    xr = xr.transpose(0, 2, 3, 1, 4).reshape(B, 2 * G, S // CMP_STRIDE, CMP_STRIDE * HEAD_DIM)
    kcvc = _nsa_compress(xr, cmp_pos.reshape(2, CMP_LEN * HEAD_DIM), cmp_w1, cmp_w2, k_g[0])
    n_slc = S // SLC_LEN
    starts = jnp.arange(S // CMP_STRIDE) * CMP_STRIDE
    bstart = jnp.arange(n_slc) * SLC_LEN
    real = (jnp.arange(S // CMP_STRIDE) < (S - CMP_LEN) // CMP_STRIDE + 1)[:, None]
    overlap = (real & (starts[:, None] < bstart[None, :] + SLC_LEN)
               & (starts[:, None] + CMP_LEN > bstart[None, :])).astype(F32)
    overlap = jnp.stack([jnp.pad(overlap, ((0, 0), (g * n_slc, (G - 1 - g) * n_slc))) for g in range(G)])
    expand = (jnp.arange(S)[None, :] // SLC_LEN == jnp.arange(n_slc)[:, None]).astype(BF16)
    oc, sel, qn = _nsa_cmp_attn(y, kcvc, overlap, cmp_bias, q_g)
    o = _nsa_sel_win(qn, y, _nsa_kv_prep(y, k_g), sel, oc, bias_tiles, expand, b_gate)
    return _out_proj(x, o, 0, o, 1, w_out.astype(BF16), gate)


def _even_mixer(x, sh, sc, gate, ng, w_in, b_f, conv_w, q_g, k_g, w_out):
    w = jnp.zeros((D_MODEL, EVEN_IN_PAD), BF16).at[:, :EVEN_IN].set(w_in.astype(BF16))
    y = _norm_proj(x, sh, sc, ng, w)
    conv = _short_conv(y, conv_w)
    attn = _fox_attention(y, b_f, q_g, k_g)
    return _out_proj(x, conv, 0, attn, 0, w_out.astype(BF16), gate)


def kernel(x, c, ada_w, ada_b, norm_g, even_w_in, even_b_f, even_conv_w, even_q_g, even_k_g, even_w_out, odd_w_in, odd_b_gate, odd_q_g, odd_k_g, odd_cmp_pos, odd_cmp_w1, odd_cmp_w2, odd_w_out, rel_table, peer_w_q, peer_keys, peer_u, peer_v):
    B = x.shape[0]
    mods = _ada(c, ada_w, ada_b)
    bias_tiles, cmp_bias = _relbias(rel_table)
    for layer in range(DEPTH):
        sh1, sc1, g1, sh2, sc2, g2 = [m.reshape(B, 1, D_MODEL) for m in jnp.split(mods[layer], 6, axis=-1)]
        i = layer // 2
        ng1 = norm_g[layer, 0].reshape(1, D_MODEL)
        ng2 = norm_g[layer, 1].reshape(1, D_MODEL)
        if layer % 2 == 0:
            x = _even_mixer(x, sh1, sc1, g1, ng1, even_w_in[i], even_b_f[i], even_conv_w[i], even_q_g[i],
                            even_k_g[i], even_w_out[i])
        else:
            x = _odd_mixer(x, sh1, sc1, g1, ng1, odd_w_in[i], odd_b_gate[i], odd_q_g[i], odd_k_g[i],
                           odd_cmp_pos[i], odd_cmp_w1[i], odd_cmp_w2[i], bias_tiles, cmp_bias, odd_w_out[i])
        x = _peer(x, sh2, sc2, g2, ng2, peer_w_q[layer], peer_keys[layer],
                  peer_u[layer].astype(BF16), peer_v[layer].T.astype(BF16))
    return x
```

```python
import functools
import math

import jax
import jax.numpy as jnp
from jax import lax
from jax.experimental import pallas as pl
from jax.experimental.pallas import tpu as pltpu

D_MODEL = 1024
DEPTH = 4
HEAD_DIM = 64
CONV_CH = 512
CONV_K = 3
FOX_HEADS = 8
FOX_WIDTH = FOX_HEADS * HEAD_DIM
EVEN_IN = 3 * CONV_CH + 3 * FOX_WIDTH + FOX_HEADS
NSA_GROUPS = 4
NSA_HPG = 4
NSA_HEADS = NSA_GROUPS * NSA_HPG
NSA_WIDTH = NSA_HEADS * HEAD_DIM
NSA_KV = NSA_GROUPS * HEAD_DIM
CMP_LEN = 32
CMP_STRIDE = 16
SLC_LEN = 64
SLC_TOPN = 16
WINDOW = 512
ODD_IN = NSA_WIDTH + 6 * NSA_KV + 3 * NSA_HEADS
REL_BUCKETS = 32
REL_MAX_DIST = 128
PEER_HEADS = 8
PEER_NKEYS = 128
PEER_EXPERTS = PEER_NKEYS * PEER_NKEYS
PEER_QDIM = 128
PEER_TOPK = 16
EPS = 1e-6
NEG_INF = -1e30
FORCE_SCORE = 1e6

LANES = 128
F32 = jnp.float32
BF16 = jnp.bfloat16
HIGHEST = lax.Precision.HIGHEST
VMEM_LIMIT = 56 * 1024 * 1024

EVEN_IN_PAD = 3200
ODD_IN_PAD = 2688
ROW_TILE = 512
ATT_TILE = 128
FOX_TILE = 256
PEER_TN = 512
PEER_EC = 2048
SCALE = HEAD_DIM ** -0.5


def _cparams(sem, flags=None):
    return pltpu.CompilerParams(dimension_semantics=sem, vmem_limit_bytes=VMEM_LIMIT, flags=flags)


def _dot(a, b, precision=None):
    return jnp.dot(a, b, precision=precision, preferred_element_type=F32)


def _dot_nt(a, b, precision=None):
    return lax.dot_general(a, b, (((1,), (1,)), ((), ())), precision=precision,
                           preferred_element_type=F32)


def _rms(x, g):
    return x * lax.rsqrt(jnp.mean(x * x, axis=-1, keepdims=True) + EPS) * g


def _ada_kernel(c_ref, w_ref, b_ref, o_ref):
    c = c_ref[...]
    sc = c * jax.nn.sigmoid(c)
    o_ref[0] = _dot(sc, w_ref[0], HIGHEST) + b_ref[0]


def _ada(c, ada_w, ada_b):
    B, D = c.shape
    E = ada_w.shape[2]
    tn = 2048
    return pl.pallas_call(
        _ada_kernel,
        out_shape=jax.ShapeDtypeStruct((DEPTH, B, E), F32),
        grid=(DEPTH, E // tn),
        in_specs=[pl.BlockSpec((B, D), lambda l, j: (0, 0)),
                  pl.BlockSpec((1, D, tn), lambda l, j: (l, 0, j)),
                  pl.BlockSpec((1, 1, tn), lambda l, j: (l, 0, j))],
        out_specs=pl.BlockSpec((1, B, tn), lambda l, j: (l, 0, j)),
        compiler_params=_cparams(("arbitrary", "arbitrary")),
        name="ada_mod",
    )(c, ada_w, ada_b.reshape(DEPTH, 1, E))


def _norm_proj_kernel(x_ref, sh_ref, sc_ref, ng_ref, w_ref, y_ref):
    h = _rms(x_ref[0], ng_ref[...]) * (1.0 + sc_ref[0]) + sh_ref[0]
    y_ref[0] = _dot(h.astype(BF16), w_ref[...])


def _norm_proj(x, sh, sc, ng, w):
    B, S, D = x.shape
    N = w.shape[1]
    return pl.pallas_call(
        _norm_proj_kernel,
        out_shape=jax.ShapeDtypeStruct((B, S, N), F32),
        grid=(B, S // ROW_TILE),
        in_specs=[pl.BlockSpec((1, ROW_TILE, D), lambda b, i: (b, i, 0)),
                  pl.BlockSpec((1, 1, D), lambda b, i: (b, 0, 0)),
                  pl.BlockSpec((1, 1, D), lambda b, i: (b, 0, 0)),
                  pl.BlockSpec((1, D), lambda b, i: (0, 0)),
                  pl.BlockSpec((D, N), lambda b, i: (0, 0))],
        out_specs=pl.BlockSpec((1, ROW_TILE, N), lambda b, i: (b, i, 0)),
        compiler_params=_cparams(("arbitrary", "arbitrary")),
        name="norm_proj",
    )(x, sh, sc, ng, w)


def _out_proj_kernel(x_ref, a_ref, b_ref, wa_ref, wb_ref, g_ref, o_ref):
    y = _dot(a_ref[0].astype(BF16), wa_ref[...]) + _dot(b_ref[0].astype(BF16), wb_ref[...])
    o_ref[0] = x_ref[0] + g_ref[0] * y


def _out_proj(x, a, a_blk, b, b_blk, w, gate):
    B, S, D = x.shape
    half = w.shape[0] // 2
    wa, wb = w[:half], w[half:]
    return pl.pallas_call(
        _out_proj_kernel,
        out_shape=jax.ShapeDtypeStruct((B, S, D), F32),
        grid=(B, S // ROW_TILE),
        in_specs=[pl.BlockSpec((1, ROW_TILE, D), lambda b, i: (b, i, 0)),
                  pl.BlockSpec((1, ROW_TILE, half), lambda b, i: (b, i, a_blk)),
                  pl.BlockSpec((1, ROW_TILE, half), lambda b, i: (b, i, b_blk)),
                  pl.BlockSpec((half, D), lambda b, i: (0, 0)),
                  pl.BlockSpec((half, D), lambda b, i: (0, 0)),
                  pl.BlockSpec((1, 1, D), lambda b, i: (b, 0, 0))],
        out_specs=pl.BlockSpec((1, ROW_TILE, D), lambda b, i: (b, i, 0)),
        compiler_params=_cparams(("arbitrary", "arbitrary")),
        name="out_proj",
    )(x, a, b, wa, wb, gate)


def _conv_kernel(cb_ref, cc_ref, ch_ref, w_ref, o_ref):
    u = cc_ref[0] * ch_ref[0]
    row = lax.broadcasted_iota(jnp.int32, u.shape, 0)
    u1 = jnp.where(row >= 1, pltpu.roll(u, 1, axis=0), 0.0)
    u2 = jnp.where(row >= 2, pltpu.roll(u, 2, axis=0), 0.0)
    w = w_ref[...]
    o_ref[0] = cb_ref[0] * (w[0:1] * u2 + w[1:2] * u1 + w[2:3] * u)


def _short_conv(y, conv_w):
    B, S, _ = y.shape
    nblk = CONV_CH // LANES
    w = jnp.zeros((8, CONV_CH), F32).at[:CONV_K].set(conv_w)
    return pl.pallas_call(
        _conv_kernel,
        out_shape=jax.ShapeDtypeStruct((B, S, CONV_CH), F32),
        grid=(B, nblk),
        in_specs=[pl.BlockSpec((1, S, LANES), lambda b, j: (b, 0, j)),
                  pl.BlockSpec((1, S, LANES), lambda b, j: (b, 0, nblk + j)),
                  pl.BlockSpec((1, S, LANES), lambda b, j: (b, 0, 2 * nblk + j)),
                  pl.BlockSpec((8, LANES), lambda b, j: (0, j))],
        out_specs=pl.BlockSpec((1, S, LANES), lambda b, j: (b, 0, j)),
        compiler_params=_cparams(("arbitrary", "arbitrary")),
        name="short_conv",
    )(y, y, y, w)


def _cumsum_rows(x):
    n = x.shape[0]
    row = lax.broadcasted_iota(jnp.int32, x.shape, 0)
    sh = 1
    while sh < n:
        x = x + jnp.where(row >= sh, pltpu.roll(x, sh, axis=0), 0.0)
        sh *= 2
    return x


FOX_KEY_PARTS = 4


def _fox_kernel(q_ref, k_ref, v_ref, f_ref, bf_ref, qg_ref, kg_ref, o_ref,
                kn_scr, vb_scr, cum_scr, cumrow_scr):
    qi = pl.program_id(1)
    S = k_ref.shape[1]
    T = q_ref.shape[1]
    nq = S // T

    @pl.when(qi == 0)
    def _prep():
        ones_col = (lax.broadcasted_iota(jnp.int32, (S, LANES - HEAD_DIM), 1) == 0).astype(BF16)
        for h in range(FOX_HEADS):
            cols = slice(h * HEAD_DIM, (h + 1) * HEAD_DIM)
            kn_scr[:, cols] = _rms(k_ref[0, :, cols], kg_ref[...]).astype(BF16)
            vb_scr[:, h * LANES:h * LANES + HEAD_DIM] = v_ref[0, :, cols].astype(BF16)
            vb_scr[:, h * LANES + HEAD_DIM:(h + 1) * LANES] = ones_col
        logf = jax.nn.log_sigmoid(f_ref[0] + bf_ref[...])
        cum = _cumsum_rows(logf)
        cum_scr[...] = cum
        for kt in range(S // T):
            cumrow_scr[:, kt * T:(kt + 1) * T] = cum[kt * T:(kt + 1) * T, :].T[0:8, :]

    cq = cum_scr[pl.ds(pl.multiple_of(qi * T, T), T), :]

    def attend(nkeys):
        row = qi * T + lax.broadcasted_iota(jnp.int32, (T, nkeys), 0)
        causal = jnp.where(lax.broadcasted_iota(jnp.int32, (T, nkeys), 1) <= row, 0.0, NEG_INF)
        for h in range(FOX_HEADS):
            cols = slice(h * HEAD_DIM, (h + 1) * HEAD_DIM)
            qh = (_rms(q_ref[0, :, cols], qg_ref[...]) * SCALE).astype(BF16)
            s = _dot_nt(qh, kn_scr[0:nkeys, cols]) + ((cq[:, h:h + 1] - cumrow_scr[h:h + 1, 0:nkeys]) + causal)
            p = jnp.exp(s - jnp.max(s, axis=-1, keepdims=True))
            acc = _dot(p.astype(BF16), vb_scr[0:nkeys, h * LANES:(h + 1) * LANES])
            o_ref[0, :, cols] = acc[:, :HEAD_DIM] / acc[:, HEAD_DIM:HEAD_DIM + 1]

    parts = min(FOX_KEY_PARTS, nq)
    per = nq // parts
    for part in range(parts):
        pl.when((qi >= part * per) & (qi < (part + 1) * per))(functools.partial(attend, (part + 1) * (S // parts)))


def _fox_attention(y, b_f, q_g, k_g):
    B, S, _ = y.shape
    T = FOX_TILE
    qblk = 3 * CONV_CH // FOX_WIDTH
    fblk = (3 * CONV_CH + 3 * FOX_WIDTH) // LANES
    bf = jnp.zeros((1, LANES), F32).at[0, :FOX_HEADS].set(b_f)
    return pl.pallas_call(
        _fox_kernel,
        out_shape=jax.ShapeDtypeStruct((B, S, FOX_WIDTH), F32),
        grid=(B, S // T),
        in_specs=[pl.BlockSpec((1, T, FOX_WIDTH), lambda b, i: (b, i, qblk)),
                  pl.BlockSpec((1, S, FOX_WIDTH), lambda b, i: (b, 0, qblk + 1)),
                  pl.BlockSpec((1, S, FOX_WIDTH), lambda b, i: (b, 0, qblk + 2)),
                  pl.BlockSpec((1, S, LANES), lambda b, i: (b, 0, fblk)),
                  pl.BlockSpec((1, LANES), lambda b, i: (0, 0)),
                  pl.BlockSpec((1, HEAD_DIM), lambda b, i: (0, 0)),
                  pl.BlockSpec((1, HEAD_DIM), lambda b, i: (0, 0))],
        out_specs=pl.BlockSpec((1, T, FOX_WIDTH), lambda b, i: (b, i, 0)),
        scratch_shapes=[pltpu.VMEM((S, FOX_WIDTH), BF16),
                        pltpu.VMEM((S, FOX_HEADS * LANES), BF16),
                        pltpu.VMEM((S, LANES), F32),
                        pltpu.VMEM((8, S), F32)],
        compiler_params=_cparams(("arbitrary", "arbitrary")),
        name="fox_attention",
    )(y, y, y, y, bf, q_g.reshape(1, HEAD_DIM), k_g.reshape(1, HEAD_DIM))


def _extract_top(work, iota, k, exact):
    nrows = work.shape[0]
    vals = []
    for _ in range(k):
        m = jnp.max(work, axis=0, keepdims=True)
        if exact:
            idx = jnp.min(jnp.where(work == m, iota, nrows), axis=0, keepdims=True)
            work = jnp.where(iota == idx, -jnp.inf, work)
        else:
            work = jnp.where(work == m, -jnp.inf, work)
        vals.append(m)
    return vals, work


def _removed(work):
    return jnp.sum(jnp.where(work == -jnp.inf, 1.0, 0.0), axis=0, keepdims=True)


def _peer_kernel(x_ref, sh_ref, sc_ref, gt_ref, ng_ref, wqh_ref, wql_ref, keys_ref, u_ref, vt_ref, o_ref,
                 ht_scr, qt_scr, sp_scr, s0_scr, e0_scr, s1_scr, e1_scr, tau_scr, a_scr, p_scr, acc_scr):
    c = pl.program_id(2)
    nc = pl.num_programs(2)
    tn = x_ref.shape[1]
    nlt = tn // LANES
    half = PEER_QDIM // 2

    @pl.when(c == 0)
    def _prep():
        h = _rms(x_ref[0], ng_ref[...]) * (1.0 + sc_ref[0]) + sh_ref[0]
        hh = h.astype(BF16)
        hl = (h - hh.astype(F32)).astype(BF16)
        cw = 2 * LANES
        for k in range(h.shape[1] // cw):
            cs = slice(k * cw, (k + 1) * cw)
            q = _dot(hh, wqh_ref[:, cs]) + (_dot(hh, wql_ref[:, cs]) + _dot(hl, wqh_ref[:, cs]))
            qt_scr[cs, :] = q.T
            ht_scr[cs, :] = h[:, cs].T.astype(BF16)
        acc_scr[...] = jnp.zeros_like(acc_scr)
        iota_k = lax.broadcasted_iota(jnp.int32, (PEER_NKEYS, LANES), 0)
        iota_c = lax.broadcasted_iota(jnp.int32, (PEER_TOPK + 8 * 8, LANES), 0)

        n_pad = sum(8 - PEER_TOPK // (r0 + 1) for r0 in range(2, 8))

        def select(hd, lt, exact):
            lanes = slice(lt * LANES, (lt + 1) * LANES)
            s0 = sp_scr[0, :, lanes]
            s1 = sp_scr[1, :, lanes]
            sv0, w0 = _extract_top(s0, iota_k, PEER_TOPK, exact)
            sv1, w1 = _extract_top(s1, iota_k, PEER_TOPK, exact)
            s0m = jnp.where(w0 == -jnp.inf, s0, -jnp.inf)
            s1m = jnp.where(w1 == -jnp.inf, s1, -jnp.inf)
            sv0_all = jnp.concatenate(sv0, axis=0)
            sv1_all = jnp.concatenate(sv1, axis=0)
            parts = [sv0[0] + sv1_all, sv0[1] + sv1_all[:8]]
            for r0 in range(2, 8):
                parts.append(jnp.where(iota_c[:8] < PEER_TOPK // (r0 + 1), sv0[r0] + sv1_all[:8], -jnp.inf))
            parts.append(sv0_all[8:] + sv1[0])
            cv, wc = _extract_top(jnp.concatenate(parts, axis=0), iota_c, PEER_TOPK, exact)
            z = cv[0] - cv[0]
            for r in range(PEER_TOPK):
                z = z + jnp.exp(cv[r] - cv[0])
            s0_scr[hd, lt] = s0m
            s1_scr[hd, lt] = s1m
            e0_scr[hd, lt] = jnp.exp(s0m - sv0[0])
            e1_scr[hd, lt] = jnp.exp(s1m - sv1[0]) / z
            tau_scr[hd, lt] = jnp.broadcast_to(cv[PEER_TOPK - 1], (8, LANES))
            return ((_removed(w0) != PEER_TOPK) | (_removed(w1) != PEER_TOPK)
                    | (_removed(wc) != PEER_TOPK + n_pad))

        def head_body(hd, carry):
            for p in range(2):
                qs = qt_scr[pl.ds(pl.multiple_of((hd * 2 + p) * half, half), half), :]
                sp_scr[p] = _dot(keys_ref[p], qs, HIGHEST)
            tied = jnp.zeros((1, LANES), F32)
            for lt in range(nlt):
                tied = jnp.maximum(tied, jnp.where(select(hd, lt, False), 1.0, 0.0))

            @pl.when(jnp.max(tied) > 0.0)
            def _redo():
                for lt in range(nlt):
                    select(hd, lt, True)
            return carry

        lax.fori_loop(0, PEER_HEADS, head_body, 0)

    nsub = 8
    nrq = 32
    nparts = PEER_EC // PEER_NKEYS // nsub
    for part in range(nparts):
        er = slice(part * nsub * PEER_NKEYS, (part + 1) * nsub * PEER_NKEYS)
        a_scr[er, :] = jax.nn.gelu(_dot(u_ref[er, :], ht_scr[...]), approximate=True)

    def weights_tile(part, lt, rq, prev):
        lanes = slice(lt * LANES, (lt + 1) * LANES)
        jr = slice(rq * nrq, (rq + 1) * nrq)
        i0 = pl.ds(pl.multiple_of(c * (PEER_EC // PEER_NKEYS) + part * nsub, nsub), nsub)
        zero = 0.0 if prev is None else jnp.where(prev != prev, prev, 0.0)
        w = [jnp.zeros((nrq, LANES), F32) for _ in range(nsub)]
        for hd in range(PEER_HEADS):
            s1 = s1_scr[hd, lt, jr, :]
            e1 = e1_scr[hd, lt, jr, :]
            s0 = s0_scr[hd, lt, i0, :] + zero
            e0 = e0_scr[hd, lt, i0, :] + zero
            tau = tau_scr[hd, lt, 0:1, :]
            for ii in range(nsub):
                keep = s1 + s0[ii:ii + 1] >= tau
                w[ii] = w[ii] + jnp.where(keep, e1 * e0[ii:ii + 1], 0.0)
        for ii in range(nsub):
            rows = slice((part * nsub + ii) * PEER_NKEYS + rq * nrq, (part * nsub + ii) * PEER_NKEYS + (rq + 1) * nrq)
            w[ii] = w[ii] * a_scr[rows, lanes]
            p_scr[rows, lanes] = w[ii].astype(BF16)
        return w[nsub - 1][0:1, :]

    prev = None
    for part in range(nparts):
        for lt in range(nlt):
            for rq in range(PEER_NKEYS // nrq):
                prev = weights_tile(part, lt, rq, prev)
    acc_scr[...] += _dot(vt_ref[...], p_scr[...])

    @pl.when(c == nc - 1)
    def _fin():
        o_ref[0] = x_ref[0] + gt_ref[0] * acc_scr[...].T


def _peer(x, sh, sc, gate, ng, w_q, keys, u_b, vt_b):
    B, S, D = x.shape
    tn = PEER_TN
    nc = PEER_EXPERTS // PEER_EC
    tab = pltpu.VMEM((PEER_HEADS, tn // LANES, PEER_NKEYS, LANES), F32)
    wq_hi = w_q.astype(BF16)
    wq_lo = (w_q - wq_hi.astype(F32)).astype(BF16)
    wspec = pl.BlockSpec((D, PEER_HEADS * PEER_QDIM), lambda b, i, c: (0, 0), pipeline_mode=pl.Buffered(1))
    return pl.pallas_call(
        _peer_kernel,
        out_shape=jax.ShapeDtypeStruct((B, S, D), F32),
        grid=(B, S // tn, nc),
        in_specs=[pl.BlockSpec((1, tn, D), lambda b, i, c: (b, i, 0), pipeline_mode=pl.Buffered(1)),
                  pl.BlockSpec((1, 1, D), lambda b, i, c: (b, 0, 0)),
                  pl.BlockSpec((1, 1, D), lambda b, i, c: (b, 0, 0)),
                  pl.BlockSpec((1, 1, D), lambda b, i, c: (b, 0, 0)),
                  pl.BlockSpec((1, D), lambda b, i, c: (0, 0)),
                  wspec, wspec,
                  pl.BlockSpec((2, PEER_NKEYS, PEER_QDIM // 2), lambda b, i, c: (0, 0, 0)),
                  pl.BlockSpec((PEER_EC, D), lambda b, i, c: (c, 0)),
                  pl.BlockSpec((D, PEER_EC), lambda b, i, c: (0, c))],
        out_specs=pl.BlockSpec((1, tn, D), lambda b, i, c: (b, i, 0)),
        scratch_shapes=[pltpu.VMEM((D, tn), BF16),
                        pltpu.VMEM((PEER_HEADS * PEER_QDIM, tn), F32),
                        pltpu.VMEM((2, PEER_NKEYS, tn), F32),
                        tab, tab, tab, tab,
                        pltpu.VMEM((PEER_HEADS, tn // LANES, 8, LANES), F32),
                        pltpu.VMEM((PEER_EC, tn), F32),
                        pltpu.VMEM((PEER_EC, tn), BF16),
                        pltpu.VMEM((D, tn), F32)],
        compiler_params=_cparams(("arbitrary", "arbitrary", "arbitrary")),
        name="peer",
    )(x, sh, sc, gate, ng, wq_hi, wq_lo, keys, u_b, vt_b)


def _rel_bucket(dist):
    n = jnp.maximum(dist, 0)
    max_exact = REL_BUCKETS // 2
    nf = jnp.maximum(n, 1).astype(F32)
    large = max_exact + (jnp.log(nf / max_exact) / math.log(REL_MAX_DIST / max_exact)
                         * (REL_BUCKETS - max_exact)).astype(jnp.int32)
    large = jnp.minimum(large, REL_BUCKETS - 1)
    return jnp.where(n < max_exact, n, large)


def _bias_lookup(bucket, tab_ref, h):
    bias = jnp.zeros(bucket.shape, F32)
    for b in range(REL_BUCKETS):
        bias = jnp.where(bucket == b, tab_ref[b, h], bias)
    return bias


N_BIAS_TILES = 5


def _relbias_kernel(tab_ref, o_ref, c_ref):
    h = pl.program_id(0)
    T = o_ref.shape[2]
    a = lax.broadcasted_iota(jnp.int32, (T, T), 0)
    b = lax.broadcasted_iota(jnp.int32, (T, T), 1)
    last = tab_ref[REL_BUCKETS - 1, h]
    o_ref[0, 0] = jnp.where(a >= b, _bias_lookup(_rel_bucket(a - b), tab_ref, h), NEG_INF)
    o_ref[0, 1] = _bias_lookup(_rel_bucket(a - b + T), tab_ref, h)
    o_ref[0, 2] = jnp.full((T, T), last, F32)
    o_ref[0, 3] = jnp.where(a < b, last, NEG_INF)
    o_ref[0, 4] = jnp.full((T, T), NEG_INF, F32)
    dist = a - ((b - T // 2) * CMP_STRIDE + CMP_LEN - 1)
    c_ref[0] = jnp.where(dist >= 0, _bias_lookup(_rel_bucket(dist), tab_ref, h), last)


def _relbias(rel_table):
    T = ATT_TILE
    assert 2 * T - (T - 1) >= REL_MAX_DIST
    assert WINDOW % T == 0
    return pl.pallas_call(
        _relbias_kernel,
        out_shape=(jax.ShapeDtypeStruct((NSA_HEADS, N_BIAS_TILES, T, T), F32),
                   jax.ShapeDtypeStruct((NSA_HEADS, T, T), F32)),
        grid=(NSA_HEADS,),
        in_specs=[pl.BlockSpec(memory_space=pltpu.SMEM)],
        out_specs=(pl.BlockSpec((1, N_BIAS_TILES, T, T), lambda h: (h, 0, 0, 0)),
                   pl.BlockSpec((1, T, T), lambda h: (h, 0, 0))),
        compiler_params=_cparams(("arbitrary",)),
        name="rel_bias_tiles",
    )(rel_table)


def _cmp_kernel(x_ref, pos_ref, w1_ref, w2_ref, kg_ref, o_ref):
    j = pl.program_id(1)
    half = CMP_STRIDE * HEAD_DIM
    x = x_ref[0, 0]
    pos = pos_ref[0]
    pa = _dot((x + pos[:, :half]).astype(BF16), w1_ref[0, :half, :])
    pb = _dot((x + pos[:, half:]).astype(BF16), w1_ref[0, half:, :])
    pre = pa + pltpu.roll(pb, pb.shape[0] - 1, axis=0)
    out = _dot(jax.nn.gelu(pre, approximate=True).astype(BF16), w2_ref[0])
    o_ref[0, 0] = jnp.where(j < NSA_GROUPS, _rms(out, kg_ref[...]), out)


def _nsa_compress(xr, pos, w1, w2, kg0):
    B, _, M, W = xr.shape
    G = NSA_GROUPS
    return pl.pallas_call(
        _cmp_kernel,
        out_shape=jax.ShapeDtypeStruct((B, 2 * G, M, HEAD_DIM), F32),
        grid=(B, 2 * G),
        in_specs=[pl.BlockSpec((1, 1, M, W), lambda b, j: (b, j, 0, 0)),
                  pl.BlockSpec((1, 1, 2 * W), lambda b, j: (j // G, 0, 0)),
                  pl.BlockSpec((1, 2 * W, HEAD_DIM), lambda b, j: (j // G, 0, 0)),
                  pl.BlockSpec((1, HEAD_DIM, HEAD_DIM), lambda b, j: (j // G, 0, 0)),
                  pl.BlockSpec((1, HEAD_DIM), lambda b, j: (0, 0))],
        out_specs=pl.BlockSpec((1, 1, M, HEAD_DIM), lambda b, j: (b, j, 0, 0)),
        compiler_params=_cparams(("arbitrary", "arbitrary")),
        name="nsa_compress",
    )(xr, pos.reshape(2, 1, 2 * W), w1.astype(BF16), w2.astype(BF16), kg0.reshape(1, HEAD_DIM))


def _cmp_attn_kernel(q_ref, kv_ref, ov_ref, cb_ref, qg_ref, oc_ref, sel_ref, qn_ref):
    qi = pl.program_id(1)
    T = q_ref.shape[1]
    NC = kv_ref.shape[2]
    G, J = NSA_GROUPS, NSA_HPG
    NS = sel_ref.shape[2] // G
    assert T == NC == LANES and T % CMP_STRIDE == 0 and G * NS == LANES
    t = qi * T + lax.broadcasted_iota(jnp.int32, (T, NC), 0)
    lane = lax.broadcasted_iota(jnp.int32, (T, NC), 1)
    vis = t - (lane * CMP_STRIDE + CMP_LEN - 1) >= 0
    visf = vis.astype(F32)
    shift = (qi * (T // CMP_STRIDE) + T // 2) % LANES
    imp = jnp.zeros((T, LANES), F32)
    for g in range(G):
        kc = kv_ref[0, g].astype(BF16)
        vc = kv_ref[0, G + g].astype(BF16)
        psum = jnp.zeros((T, NC), F32)
        for j in range(J):
            h = g * J + j
            cols = slice(h * HEAD_DIM, (h + 1) * HEAD_DIM)
            qh = (_rms(q_ref[0, :, cols], qg_ref[...]) * SCALE).astype(BF16)
            qn_ref[0, :, cols] = qh
            s = _dot_nt(qh, kc) + pltpu.roll(cb_ref[h], shift, axis=1)
            s = jnp.where(vis, s, NEG_INF)
            e = jnp.exp(s - jnp.max(s, axis=-1, keepdims=True))
            p = e / jnp.sum(e, axis=-1, keepdims=True) * visf
            oc_ref[0, :, cols] = _dot(p.astype(BF16), vc)
            psum = psum + p
        imp = imp + _dot(psum, ov_ref[g], HIGHEST)
    mm = lane % NS
    cur = t // SLC_LEN
    forced = (mm == 0) | (mm == cur) | (mm == cur - 1)
    imp = jnp.where(forced, FORCE_SCORE, jnp.where(mm * SLC_LEN <= t, imp, -1.0))
    rank = jnp.zeros((T, LANES), F32)
    for r in range(1, NS):
        wrapped = mm >= NS - r
        other = jnp.where(wrapped, pltpu.roll(imp, NS - r, axis=1), pltpu.roll(imp, LANES - r, axis=1))
        rank = rank + jnp.where(other > imp, 1.0, jnp.where(other == imp, wrapped.astype(F32), 0.0))
    sel_ref[0] = jnp.where(rank < SLC_TOPN, jnp.where(imp >= 0.0, 1.0, 0.0), 0.0)


def _nsa_cmp_attn(y, kcvc, overlap, cmp_bias, q_g):
    B, S, _ = y.shape
    T = ATT_TILE
    NC = kcvc.shape[2]
    return pl.pallas_call(
        _cmp_attn_kernel,
        out_shape=(jax.ShapeDtypeStruct((B, S, NSA_WIDTH), F32),
                   jax.ShapeDtypeStruct((B, S, LANES), F32),
                   jax.ShapeDtypeStruct((B, S, NSA_WIDTH), BF16)),
        grid=(B, S // T),
        in_specs=[pl.BlockSpec((1, T, NSA_WIDTH), lambda b, i: (b, i, 0)),
                  pl.BlockSpec((1, 2 * NSA_GROUPS, NC, HEAD_DIM), lambda b, i: (b, 0, 0, 0)),
                  pl.BlockSpec(overlap.shape, lambda b, i: (0, 0, 0)),
                  pl.BlockSpec(cmp_bias.shape, lambda b, i: (0, 0, 0)),
                  pl.BlockSpec((1, HEAD_DIM), lambda b, i: (0, 0))],
        out_specs=(pl.BlockSpec((1, T, NSA_WIDTH), lambda b, i: (b, i, 0)),
                   pl.BlockSpec((1, T, LANES), lambda b, i: (b, i, 0)),
                   pl.BlockSpec((1, T, NSA_WIDTH), lambda b, i: (b, i, 0))),
        compiler_params=_cparams(("arbitrary", "arbitrary")),
        name="nsa_cmp_attn",
    )(y, kcvc, overlap, cmp_bias, q_g.reshape(1, HEAD_DIM))


def _kv_prep_kernel(ks_ref, kw_ref, kg1_ref, kg2_ref, ksn_ref, vs_ref, kwn_ref, vw_ref):
    KV = NSA_KV
    rows = ks_ref.shape[1]
    ones_col = (lax.broadcasted_iota(jnp.int32, (rows, LANES - HEAD_DIM), 1) == 0).astype(BF16)
    for g in range(NSA_GROUPS):
        cols = slice(g * HEAD_DIM, (g + 1) * HEAD_DIM)
        vcols = slice(KV + g * HEAD_DIM, KV + (g + 1) * HEAD_DIM)
        ksn_ref[0, :, cols] = _rms(ks_ref[0, :, cols], kg1_ref[...]).astype(BF16)
        kwn_ref[0, :, cols] = _rms(kw_ref[0, :, cols], kg2_ref[...]).astype(BF16)
        vs_ref[0, :, g * LANES:g * LANES + HEAD_DIM] = ks_ref[0, :, vcols].astype(BF16)
        vw_ref[0, :, g * LANES:g * LANES + HEAD_DIM] = kw_ref[0, :, vcols].astype(BF16)
        vs_ref[0, :, g * LANES + HEAD_DIM:(g + 1) * LANES] = ones_col
        vw_ref[0, :, g * LANES + HEAD_DIM:(g + 1) * LANES] = ones_col


def _nsa_kv_prep(y, k_g):
    B, S, _ = y.shape
    ksblk = (NSA_WIDTH + 2 * NSA_KV) // (2 * NSA_KV)
    kout = jax.ShapeDtypeStruct((B, S, NSA_KV), BF16)
    vout = jax.ShapeDtypeStruct((B, S, NSA_GROUPS * LANES), BF16)
    kspec = pl.BlockSpec((1, ROW_TILE, NSA_KV), lambda b, i: (b, i, 0))
    vspec = pl.BlockSpec((1, ROW_TILE, NSA_GROUPS * LANES), lambda b, i: (b, i, 0))
    return pl.pallas_call(
        _kv_prep_kernel,
        out_shape=(kout, vout, kout, vout),
        grid=(B, S // ROW_TILE),
        in_specs=[pl.BlockSpec((1, ROW_TILE, 2 * NSA_KV), lambda b, i: (b, i, ksblk)),
                  pl.BlockSpec((1, ROW_TILE, 2 * NSA_KV), lambda b, i: (b, i, ksblk + 1)),
                  pl.BlockSpec((1, HEAD_DIM), lambda b, i: (0, 0)),
                  pl.BlockSpec((1, HEAD_DIM), lambda b, i: (0, 0))],
        out_specs=(kspec, vspec, kspec, vspec),
        compiler_params=_cparams(("arbitrary", "arbitrary")),
        name="nsa_kv_prep",
    )(y, y, k_g[1].reshape(1, HEAD_DIM), k_g[2].reshape(1, HEAD_DIM))


SEL_KEY_PARTS = 4


def _sel_win_kernel(qn_ref, ks_ref, vs_ref, kw_ref, vw_ref, gl_ref, sel_ref, oc_ref, bias_ref, e_ref, bg_ref,
                    o_ref):
    qi = pl.program_id(1)
    T = qn_ref.shape[1]
    S = ks_ref.shape[1]
    nk = S // T
    G, J = NSA_GROUPS, NSA_HPG
    NS = e_ref.shape[0]
    wt = WINDOW // T
    nwin = wt + 1
    assert nk % SEL_KEY_PARTS == 0 and nk >= nwin

    def sel_tile(r):
        return jnp.where(r < 0, 4, jnp.minimum(r, 2))

    def win_tile(r):
        return jnp.where(r < 0, 4, jnp.where(r <= 1, r, jnp.where(r < wt, 2, jnp.where(r == wt, 3, 4))))

    def softmax_pv(s, v):
        p = jnp.exp(s - jnp.max(s, axis=-1, keepdims=True))
        acc = _dot(p.astype(BF16), v)
        return acc[:, :HEAD_DIM] / acc[:, HEAD_DIM:HEAD_DIM + 1]

    gates = jax.nn.sigmoid(gl_ref[0] + bg_ref[...])
    k0 = jnp.maximum(qi - wt, 0)
    wrows = pl.ds(pl.multiple_of(k0 * T, T), nwin * T)

    def attend(nkt):
        nkeys = nkt * T
        for g in range(G):
            gcols = slice(g * HEAD_DIM, (g + 1) * HEAD_DIM)
            vcols = slice(g * LANES, (g + 1) * LANES)
            q4 = jnp.concatenate(
                [qn_ref[0, :, (g * J + j) * HEAD_DIM:(g * J + j + 1) * HEAD_DIM] for j in range(J)], axis=0)
            selg = sel_ref[0, :, g * NS:(g + 1) * NS].astype(BF16)
            mask = (_dot(selg, e_ref[:, :nkeys]) - 1.0) * (-NEG_INF)
            bias = jnp.concatenate(
                [jnp.concatenate([bias_ref[g * J + j, sel_tile(qi - kt)] for kt in range(nkt)], axis=1) + mask
                 for j in range(J)], axis=0)
            o_s = softmax_pv(_dot_nt(q4, ks_ref[0, :nkeys, gcols]) + bias, vs_ref[0, :nkeys, vcols])
            wbias = jnp.concatenate(
                [jnp.concatenate([bias_ref[g * J + j, win_tile(qi - k0 - t)] for t in range(nwin)], axis=1)
                 for j in range(J)], axis=0)
            o_w = softmax_pv(_dot_nt(q4, kw_ref[0, wrows, gcols]) + wbias, vw_ref[0, wrows, vcols])
            for j in range(J):
                h = g * J + j
                cols = slice(h * HEAD_DIM, (h + 1) * HEAD_DIM)
                rows = slice(j * T, (j + 1) * T)
                o_ref[0, :, cols] = (gates[:, 3 * h:3 * h + 1] * oc_ref[0, :, cols]
                                     + gates[:, 3 * h + 1:3 * h + 2] * o_s[rows]
                                     + gates[:, 3 * h + 2:3 * h + 3] * o_w[rows])

    per = nk // SEL_KEY_PARTS
    for part in range(SEL_KEY_PARTS):
        pl.when((qi >= part * per) & (qi < (part + 1) * per))(functools.partial(attend, (part + 1) * per))


def _nsa_sel_win(qn, y, kv, sel, oc, bias_tiles, expand, b_gate):
    B, S, _ = y.shape
    T = ATT_TILE
    glblk = (NSA_WIDTH + 6 * NSA_KV) // LANES
    bg = jnp.zeros((1, LANES), F32).at[0, :3 * NSA_HEADS].set(b_gate)
    kspec = pl.BlockSpec((1, S, NSA_KV), lambda b, i: (b, 0, 0))
    vspec = pl.BlockSpec((1, S, NSA_GROUPS * LANES), lambda b, i: (b, 0, 0))
    return pl.pallas_call(
        _sel_win_kernel,
        out_shape=jax.ShapeDtypeStruct((B, S, NSA_WIDTH), F32),
        grid=(B, S // T),
        in_specs=[pl.BlockSpec((1, T, NSA_WIDTH), lambda b, i: (b, i, 0)),
                  kspec, vspec, kspec, vspec,
                  pl.BlockSpec((1, T, LANES), lambda b, i: (b, i, glblk)),
                  pl.BlockSpec((1, T, sel.shape[2]), lambda b, i: (b, i, 0)),
                  pl.BlockSpec((1, T, NSA_WIDTH), lambda b, i: (b, i, 0)),
                  pl.BlockSpec(bias_tiles.shape, lambda b, i: (0, 0, 0, 0)),
                  pl.BlockSpec(expand.shape, lambda b, i: (0, 0)),
                  pl.BlockSpec((1, LANES), lambda b, i: (0, 0))],
        out_specs=pl.BlockSpec((1, T, NSA_WIDTH), lambda b, i: (b, i, 0)),
        compiler_params=_cparams(("arbitrary", "arbitrary")),
        name="nsa_sel_win",
    )(qn, *kv, y, sel, oc, bias_tiles, expand, bg)


def _odd_mixer(x, sh, sc, gate, ng, w_in, b_gate, q_g, k_g, cmp_pos, cmp_w1, cmp_w2, bias_tiles, cmp_bias,
               w_out):
    B, S, _ = x.shape
    G = NSA_GROUPS
    w = jnp.zeros((D_MODEL, ODD_IN_PAD), BF16).at[:, :ODD_IN].set(w_in.astype(BF16))
    y = _norm_proj(x, sh, sc, ng, w)
    xr = y[:, :, NSA_WIDTH:NSA_WIDTH + 2 * NSA_KV].reshape(B, S, 2, G, HEAD_DIM)
    xr = xr.transpose(0, 2, 3, 1, 4).reshape(B, 2 * G, S // CMP_STRIDE, CMP_STRIDE * HEAD_DIM)
    kcvc = _nsa_compress(xr, cmp_pos.reshape(2, CMP_LEN * HEAD_DIM), cmp_w1, cmp_w2, k_g[0])
    n_slc = S // SLC_LEN
    starts = jnp.arange(S // CMP_STRIDE) * CMP_STRIDE
    bstart = jnp.arange(n_slc) * SLC_LEN
    real = (jnp.arange(S // CMP_STRIDE) < (S - CMP_LEN) // CMP_STRIDE + 1)[:, None]
    overlap = (real & (starts[:, None] < bstart[None, :] + SLC_LEN)
               & (starts[:, None] + CMP_LEN > bstart[None, :])).astype(F32)
    overlap = jnp.stack([jnp.pad(overlap, ((0, 0), (g * n_slc, (G - 1 - g) * n_slc))) for g in range(G)])
    expand = (jnp.arange(S)[None, :] // SLC_LEN == jnp.arange(n_slc)[:, None]).astype(BF16)
    oc, sel, qn = _nsa_cmp_attn(y, kcvc, overlap, cmp_bias, q_g)
    o = _nsa_sel_win(qn, y, _nsa_kv_prep(y, k_g), sel, oc, bias_tiles, expand, b_gate)
    return _out_proj(x, o, 0, o, 1, w_out.astype(BF16), gate)


def _even_mixer(x, sh, sc, gate, ng, w_in, b_f, conv_w, q_g, k_g, w_out):
    w = jnp.zeros((D_MODEL, EVEN_IN_PAD), BF16).at[:, :EVEN_IN].set(w_in.astype(BF16))
    y = _norm_proj(x, sh, sc, ng, w)
    conv = _short_conv(y, conv_w)
    attn = _fox_attention(y, b_f, q_g, k_g)
    return _out_proj(x, conv, 0, attn, 0, w_out.astype(BF16), gate)


def kernel(x, c, ada_w, ada_b, norm_g, even_w_in, even_b_f, even_conv_w, even_q_g, even_k_g, even_w_out, odd_w_in, odd_b_gate, odd_q_g, odd_k_g, odd_cmp_pos, odd_cmp_w1, odd_cmp_w2, odd_w_out, rel_table, peer_w_q, peer_keys, peer_u, peer_v):
    B = x.shape[0]
    mods = _ada(c, ada_w, ada_b)
    bias_tiles, cmp_bias = _relbias(rel_table)
    for layer in range(DEPTH):
        sh1, sc1, g1, sh2, sc2, g2 = [m.reshape(B, 1, D_MODEL) for m in jnp.split(mods[layer], 6, axis=-1)]
        i = layer // 2
        ng1 = norm_g[layer, 0].reshape(1, D_MODEL)
        ng2 = norm_g[layer, 1].reshape(1, D_MODEL)
        if layer % 2 == 0:
            x = _even_mixer(x, sh1, sc1, g1, ng1, even_w_in[i], even_b_f[i], even_conv_w[i], even_q_g[i],
                            even_k_g[i], even_w_out[i])
        else:
            x = _odd_mixer(x, sh1, sc1, g1, ng1, odd_w_in[i], odd_b_gate[i], odd_q_g[i], odd_k_g[i],
                           odd_cmp_pos[i], odd_cmp_w1[i], odd_cmp_w2[i], bias_tiles, cmp_bias, odd_w_out[i])
        x = _peer(x, sh2, sc2, g2, ng2, peer_w_q[layer], peer_keys[layer],
                  peer_u[layer].astype(BF16), peer_v[layer].T.astype(BF16))
    return x
```

```python
import functools
import math

import jax
import jax.numpy as jnp
from jax import lax
from jax.experimental import pallas as pl
from jax.experimental.pallas import tpu as pltpu

D_MODEL = 1024
DEPTH = 4
HEAD_DIM = 64
CONV_CH = 512
CONV_K = 3
FOX_HEADS = 8
FOX_WIDTH = FOX_HEADS * HEAD_DIM
EVEN_IN = 3 * CONV_CH + 3 * FOX_WIDTH + FOX_HEADS
NSA_GROUPS = 4
NSA_HPG = 4
NSA_HEADS = NSA_GROUPS * NSA_HPG
NSA_WIDTH = NSA_HEADS * HEAD_DIM
NSA_KV = NSA_GROUPS * HEAD_DIM
CMP_LEN = 32
CMP_STRIDE = 16
SLC_LEN = 64
SLC_TOPN = 16
WINDOW = 512
ODD_IN = NSA_WIDTH + 6 * NSA_KV + 3 * NSA_HEADS
REL_BUCKETS = 32
REL_MAX_DIST = 128
PEER_HEADS = 8
PEER_NKEYS = 128
PEER_EXPERTS = PEER_NKEYS * PEER_NKEYS
PEER_QDIM = 128
PEER_TOPK = 16
EPS = 1e-6
NEG_INF = -1e30
FORCE_SCORE = 1e6

LANES = 128
F32 = jnp.float32
BF16 = jnp.bfloat16
HIGHEST = lax.Precision.HIGHEST
VMEM_LIMIT = 56 * 1024 * 1024

EVEN_IN_PAD = 3200
ODD_IN_PAD = 2688
ROW_TILE = 512
ATT_TILE = 128
FOX_TILE = 256
PEER_TN = 512
PEER_EC = 2048
SCALE = HEAD_DIM ** -0.5


def _cparams(sem, flags=None):
    return pltpu.CompilerParams(dimension_semantics=sem, vmem_limit_bytes=VMEM_LIMIT, flags=flags)


def _dot(a, b, precision=None):
    return jnp.dot(a, b, precision=precision, preferred_element_type=F32)


def _dot_nt(a, b, precision=None):
    return lax.dot_general(a, b, (((1,), (1,)), ((), ())), precision=precision,
                           preferred_element_type=F32)


def _rms(x, g):
    return x * lax.rsqrt(jnp.mean(x * x, axis=-1, keepdims=True) + EPS) * g


def _ada_kernel(c_ref, w_ref, b_ref, o_ref):
    c = c_ref[...]
    sc = c * jax.nn.sigmoid(c)
    o_ref[0] = _dot(sc, w_ref[0], HIGHEST) + b_ref[0]


def _ada(c, ada_w, ada_b):
    B, D = c.shape
    E = ada_w.shape[2]
    tn = 2048
    return pl.pallas_call(
        _ada_kernel,
        out_shape=jax.ShapeDtypeStruct((DEPTH, B, E), F32),
        grid=(DEPTH, E // tn),
        in_specs=[pl.BlockSpec((B, D), lambda l, j: (0, 0)),
                  pl.BlockSpec((1, D, tn), lambda l, j: (l, 0, j)),
                  pl.BlockSpec((1, 1, tn), lambda l, j: (l, 0, j))],
        out_specs=pl.BlockSpec((1, B, tn), lambda l, j: (l, 0, j)),
        compiler_params=_cparams(("arbitrary", "arbitrary")),
        name="ada_mod",
    )(c, ada_w, ada_b.reshape(DEPTH, 1, E))


def _norm_proj_kernel(x_ref, sh_ref, sc_ref, ng_ref, w_ref, y_ref):
    h = _rms(x_ref[0], ng_ref[...]) * (1.0 + sc_ref[0]) + sh_ref[0]
    y_ref[0] = _dot(h.astype(BF16), w_ref[...])


def _norm_proj(x, sh, sc, ng, w):
    B, S, D = x.shape
    N = w.shape[1]
    return pl.pallas_call(
        _norm_proj_kernel,
        out_shape=jax.ShapeDtypeStruct((B, S, N), F32),
        grid=(B, S // ROW_TILE),
        in_specs=[pl.BlockSpec((1, ROW_TILE, D), lambda b, i: (b, i, 0)),
                  pl.BlockSpec((1, 1, D), lambda b, i: (b, 0, 0)),
                  pl.BlockSpec((1, 1, D), lambda b, i: (b, 0, 0)),
                  pl.BlockSpec((1, D), lambda b, i: (0, 0)),
                  pl.BlockSpec((D, N), lambda b, i: (0, 0))],
        out_specs=pl.BlockSpec((1, ROW_TILE, N), lambda b, i: (b, i, 0)),
        compiler_params=_cparams(("arbitrary", "arbitrary")),
        name="norm_proj",
    )(x, sh, sc, ng, w)


def _out_proj_kernel(x_ref, a_ref, b_ref, wa_ref, wb_ref, g_ref, o_ref):
    y = _dot(a_ref[0].astype(BF16), wa_ref[...]) + _dot(b_ref[0].astype(BF16), wb_ref[...])
    o_ref[0] = x_ref[0] + g_ref[0] * y


def _out_proj(x, a, a_blk, b, b_blk, w, gate):
    B, S, D = x.shape
    half = w.shape[0] // 2
    wa, wb = w[:half], w[half:]
    return pl.pallas_call(
        _out_proj_kernel,
        out_shape=jax.ShapeDtypeStruct((B, S, D), F32),
        grid=(B, S // ROW_TILE),
        in_specs=[pl.BlockSpec((1, ROW_TILE, D), lambda b, i: (b, i, 0)),
                  pl.BlockSpec((1, ROW_TILE, half), lambda b, i: (b, i, a_blk)),
                  pl.BlockSpec((1, ROW_TILE, half), lambda b, i: (b, i, b_blk)),
                  pl.BlockSpec((half, D), lambda b, i: (0, 0)),
                  pl.BlockSpec((half, D), lambda b, i: (0, 0)),
                  pl.BlockSpec((1, 1, D), lambda b, i: (b, 0, 0))],
        out_specs=pl.BlockSpec((1, ROW_TILE, D), lambda b, i: (b, i, 0)),
        compiler_params=_cparams(("arbitrary", "arbitrary")),
        name="out_proj",
    )(x, a, b, wa, wb, gate)


def _conv_kernel(cb_ref, cc_ref, ch_ref, w_ref, o_ref):
    u = cc_ref[0] * ch_ref[0]
    row = lax.broadcasted_iota(jnp.int32, u.shape, 0)
    u1 = jnp.where(row >= 1, pltpu.roll(u, 1, axis=0), 0.0)
    u2 = jnp.where(row >= 2, pltpu.roll(u, 2, axis=0), 0.0)
    w = w_ref[...]
    o_ref[0] = cb_ref[0] * (w[0:1] * u2 + w[1:2] * u1 + w[2:3] * u)


def _short_conv(y, conv_w):
    B, S, _ = y.shape
    nblk = CONV_CH // LANES
    w = jnp.zeros((8, CONV_CH), F32).at[:CONV_K].set(conv_w)
    return pl.pallas_call(
        _conv_kernel,
        out_shape=jax.ShapeDtypeStruct((B, S, CONV_CH), F32),
        grid=(B, nblk),
        in_specs=[pl.BlockSpec((1, S, LANES), lambda b, j: (b, 0, j)),
                  pl.BlockSpec((1, S, LANES), lambda b, j: (b, 0, nblk + j)),
                  pl.BlockSpec((1, S, LANES), lambda b, j: (b, 0, 2 * nblk + j)),
                  pl.BlockSpec((8, LANES), lambda b, j: (0, j))],
        out_specs=pl.BlockSpec((1, S, LANES), lambda b, j: (b, 0, j)),
        compiler_params=_cparams(("arbitrary", "arbitrary")),
        name="short_conv",
    )(y, y, y, w)


def _cumsum_rows(x):
    n = x.shape[0]
    row = lax.broadcasted_iota(jnp.int32, x.shape, 0)
    sh = 1
    while sh < n:
        x = x + jnp.where(row >= sh, pltpu.roll(x, sh, axis=0), 0.0)
        sh *= 2
    return x


FOX_KEY_PARTS = 4


def _fox_kernel(q_ref, k_ref, v_ref, f_ref, bf_ref, qg_ref, kg_ref, o_ref,
                kn_scr, vb_scr, cum_scr, cumrow_scr):
    qi = pl.program_id(1)
    S = k_ref.shape[1]
    T = q_ref.shape[1]
    nq = S // T

    @pl.when(qi == 0)
    def _prep():
        ones_col = (lax.broadcasted_iota(jnp.int32, (S, LANES - HEAD_DIM), 1) == 0).astype(BF16)
        for h in range(FOX_HEADS):
            cols = slice(h * HEAD_DIM, (h + 1) * HEAD_DIM)
            kn_scr[:, cols] = _rms(k_ref[0, :, cols], kg_ref[...]).astype(BF16)
            vb_scr[:, h * LANES:h * LANES + HEAD_DIM] = v_ref[0, :, cols].astype(BF16)
            vb_scr[:, h * LANES + HEAD_DIM:(h + 1) * LANES] = ones_col
        logf = jax.nn.log_sigmoid(f_ref[0] + bf_ref[...])
        cum = _cumsum_rows(logf)
        cum_scr[...] = cum
        for kt in range(S // T):
            cumrow_scr[:, kt * T:(kt + 1) * T] = cum[kt * T:(kt + 1) * T, :].T[0:8, :]

    cq = cum_scr[pl.ds(pl.multiple_of(qi * T, T), T), :]

    def attend(nkeys):
        row = qi * T + lax.broadcasted_iota(jnp.int32, (T, nkeys), 0)
        causal = jnp.where(lax.broadcasted_iota(jnp.int32, (T, nkeys), 1) <= row, 0.0, NEG_INF)
        for h in range(FOX_HEADS):
            cols = slice(h * HEAD_DIM, (h + 1) * HEAD_DIM)
            qh = (_rms(q_ref[0, :, cols], qg_ref[...]) * SCALE).astype(BF16)
            s = _dot_nt(qh, kn_scr[0:nkeys, cols]) + ((cq[:, h:h + 1] - cumrow_scr[h:h + 1, 0:nkeys]) + causal)
            p = jnp.exp(s - jnp.max(s, axis=-1, keepdims=True))
            acc = _dot(p.astype(BF16), vb_scr[0:nkeys, h * LANES:(h + 1) * LANES])
            o_ref[0, :, cols] = acc[:, :HEAD_DIM] / acc[:, HEAD_DIM:HEAD_DIM + 1]

    parts = min(FOX_KEY_PARTS, nq)
    per = nq // parts
    for part in range(parts):
        pl.when((qi >= part * per) & (qi < (part + 1) * per))(functools.partial(attend, (part + 1) * (S // parts)))


def _fox_attention(y, b_f, q_g, k_g):
    B, S, _ = y.shape
    T = FOX_TILE
    qblk = 3 * CONV_CH // FOX_WIDTH
    fblk = (3 * CONV_CH + 3 * FOX_WIDTH) // LANES
    bf = jnp.zeros((1, LANES), F32).at[0, :FOX_HEADS].set(b_f)
    return pl.pallas_call(
        _fox_kernel,
        out_shape=jax.ShapeDtypeStruct((B, S, FOX_WIDTH), F32),
        grid=(B, S // T),
        in_specs=[pl.BlockSpec((1, T, FOX_WIDTH), lambda b, i: (b, i, qblk)),
                  pl.BlockSpec((1, S, FOX_WIDTH), lambda b, i: (b, 0, qblk + 1)),
                  pl.BlockSpec((1, S, FOX_WIDTH), lambda b, i: (b, 0, qblk + 2)),
                  pl.BlockSpec((1, S, LANES), lambda b, i: (b, 0, fblk)),
                  pl.BlockSpec((1, LANES), lambda b, i: (0, 0)),
                  pl.BlockSpec((1, HEAD_DIM), lambda b, i: (0, 0)),
                  pl.BlockSpec((1, HEAD_DIM), lambda b, i: (0, 0))],
        out_specs=pl.BlockSpec((1, T, FOX_WIDTH), lambda b, i: (b, i, 0)),
        scratch_shapes=[pltpu.VMEM((S, FOX_WIDTH), BF16),
                        pltpu.VMEM((S, FOX_HEADS * LANES), BF16),
                        pltpu.VMEM((S, LANES), F32),
                        pltpu.VMEM((8, S), F32)],
        compiler_params=_cparams(("arbitrary", "arbitrary")),
        name="fox_attention",
    )(y, y, y, y, bf, q_g.reshape(1, HEAD_DIM), k_g.reshape(1, HEAD_DIM))


def _extract_top(work, iota, k, exact):
    nrows = work.shape[0]
    vals = []
    for _ in range(k):
        m = jnp.max(work, axis=0, keepdims=True)
        if exact:
            idx = jnp.min(jnp.where(work == m, iota, nrows), axis=0, keepdims=True)
            work = jnp.where(iota == idx, -jnp.inf, work)
        else:
            work = jnp.where(work == m, -jnp.inf, work)
        vals.append(m)
    return vals, work


def _removed(work):
    return jnp.sum(jnp.where(work == -jnp.inf, 1.0, 0.0), axis=0, keepdims=True)


def _peer_kernel(x_ref, sh_ref, sc_ref, gt_ref, ng_ref, wqh_ref, wql_ref, keys_ref, u_ref, vt_ref, o_ref,
                 ht_scr, qt_scr, sp_scr, s0_scr, e0_scr, s1_scr, e1_scr, tau_scr, a_scr, p_scr, acc_scr):
    c = pl.program_id(2)
    nc = pl.num_programs(2)
    tn = x_ref.shape[1]
    nlt = tn // LANES
    half = PEER_QDIM // 2

    @pl.when(c == 0)
    def _prep():
        h = _rms(x_ref[0], ng_ref[...]) * (1.0 + sc_ref[0]) + sh_ref[0]
        hh = h.astype(BF16)
        hl = (h - hh.astype(F32)).astype(BF16)
        cw = 2 * LANES
        for k in range(h.shape[1] // cw):
            cs = slice(k * cw, (k + 1) * cw)
            q = _dot(hh, wqh_ref[:, cs]) + (_dot(hh, wql_ref[:, cs]) + _dot(hl, wqh_ref[:, cs]))
            qt_scr[cs, :] = q.T
            ht_scr[cs, :] = h[:, cs].T.astype(BF16)
        acc_scr[...] = jnp.zeros_like(acc_scr)
        iota_k = lax.broadcasted_iota(jnp.int32, (PEER_NKEYS, LANES), 0)
        iota_c = lax.broadcasted_iota(jnp.int32, (PEER_TOPK + 8 * 8, LANES), 0)

        n_pad = sum(8 - PEER_TOPK // (r0 + 1) for r0 in range(2, 8))

        def select(hd, lt, exact):
            lanes = slice(lt * LANES, (lt + 1) * LANES)
            s0 = sp_scr[0, :, lanes]
            s1 = sp_scr[1, :, lanes]
            sv0, w0 = _extract_top(s0, iota_k, PEER_TOPK, exact)
            sv1, w1 = _extract_top(s1, iota_k, PEER_TOPK, exact)
            s0m = jnp.where(w0 == -jnp.inf, s0, -jnp.inf)
            s1m = jnp.where(w1 == -jnp.inf, s1, -jnp.inf)
            sv0_all = jnp.concatenate(sv0, axis=0)
            sv1_all = jnp.concatenate(sv1, axis=0)
            parts = [sv0[0] + sv1_all, sv0[1] + sv1_all[:8]]
            for r0 in range(2, 8):
                parts.append(jnp.where(iota_c[:8] < PEER_TOPK // (r0 + 1), sv0[r0] + sv1_all[:8], -jnp.inf))
            parts.append(sv0_all[8:] + sv1[0])
            cv, wc = _extract_top(jnp.concatenate(parts, axis=0), iota_c, PEER_TOPK, exact)
            z = cv[0] - cv[0]
            for r in range(PEER_TOPK):
                z = z + jnp.exp(cv[r] - cv[0])
            s0_scr[hd, lt] = s0m
            s1_scr[hd, lt] = s1m
            e0_scr[hd, lt] = jnp.exp(s0m - sv0[0])
            e1_scr[hd, lt] = jnp.exp(s1m - sv1[0]) / z
            tau_scr[hd, lt] = jnp.broadcast_to(cv[PEER_TOPK - 1], (8, LANES))
            return ((_removed(w0) != PEER_TOPK) | (_removed(w1) != PEER_TOPK)
                    | (_removed(wc) != PEER_TOPK + n_pad))

        def head_body(hd, carry):
            for p in range(2):
                qs = qt_scr[pl.ds(pl.multiple_of((hd * 2 + p) * half, half), half), :]
                sp_scr[p] = _dot(keys_ref[p], qs, HIGHEST)
            tied = jnp.zeros((1, LANES), F32)
            for lt in range(nlt):
                tied = jnp.maximum(tied, jnp.where(select(hd, lt, False), 1.0, 0.0))

            @pl.when(jnp.max(tied) > 0.0)
            def _redo():
                for lt in range(nlt):
                    select(hd, lt, True)
            return carry

        lax.fori_loop(0, PEER_HEADS, head_body, 0)

    nsub = 4
    nrq = 64
    nparts = PEER_EC // PEER_NKEYS // nsub

    def experts(part):
        return slice(part * nsub * PEER_NKEYS, (part + 1) * nsub * PEER_NKEYS)

    for part in range(nparts):
        a_scr[experts(part), :] = jax.nn.gelu(_dot(u_ref[experts(part), :], ht_scr[...]), approximate=True)

    def weights_tile(part, lt, rq, prev):
        jr = slice(rq * nrq, (rq + 1) * nrq)
        blk = part * nsub // 8
        i0 = pl.ds(pl.multiple_of(c * (PEER_EC // PEER_NKEYS) + blk * 8, 8), 8)
        zero = 0.0 if prev is None else jnp.where(prev != prev, prev, 0.0)
        w = [jnp.zeros((nrq, LANES), F32) for _ in range(nsub)]
        for hd in range(PEER_HEADS):
            s1 = s1_scr[hd, lt, jr, :]
            e1 = e1_scr[hd, lt, jr, :]
            s0 = s0_scr[hd, lt, i0, :] + zero
            e0 = e0_scr[hd, lt, i0, :] + zero
            tau = tau_scr[hd, lt, 0:1, :]
            for ii in range(nsub):
                r = part * nsub - blk * 8 + ii
                keep = s1 + s0[r:r + 1] >= tau
                w[ii] = w[ii] + jnp.where(keep, e1 * e0[r:r + 1], 0.0)
        lanes = slice(lt * LANES, (lt + 1) * LANES)
        for ii in range(nsub):
            rows = slice((part * nsub + ii) * PEER_NKEYS + rq * nrq, (part * nsub + ii) * PEER_NKEYS + (rq + 1) * nrq)
            w[ii] = w[ii] * a_scr[rows, lanes]
            p_scr[rows, lanes] = w[ii].astype(BF16)
        return w[nsub - 1][0:1, :]

    prev = None
    for part in range(nparts):
        for lt in range(nlt):
            for rq in range(PEER_NKEYS // nrq):
                prev = weights_tile(part, lt, rq, prev)
        acc_scr[...] += _dot(vt_ref[:, experts(part)], p_scr[experts(part), :])

    @pl.when(c == nc - 1)
    def _fin():
        o_ref[0] = x_ref[0] + gt_ref[0] * acc_scr[...].T


def _peer(x, sh, sc, gate, ng, w_q, keys, u_b, vt_b):
    B, S, D = x.shape
    tn = PEER_TN
    nc = PEER_EXPERTS // PEER_EC
    tab = pltpu.VMEM((PEER_HEADS, tn // LANES, PEER_NKEYS, LANES), F32)
    wq_hi = w_q.astype(BF16)
    wq_lo = (w_q - wq_hi.astype(F32)).astype(BF16)
    wspec = pl.BlockSpec((D, PEER_HEADS * PEER_QDIM), lambda b, i, c: (0, 0), pipeline_mode=pl.Buffered(1))
    return pl.pallas_call(
        _peer_kernel,
        out_shape=jax.ShapeDtypeStruct((B, S, D), F32),
        grid=(B, S // tn, nc),
        in_specs=[pl.BlockSpec((1, tn, D), lambda b, i, c: (b, i, 0), pipeline_mode=pl.Buffered(1)),
                  pl.BlockSpec((1, 1, D), lambda b, i, c: (b, 0, 0)),
                  pl.BlockSpec((1, 1, D), lambda b, i, c: (b, 0, 0)),
                  pl.BlockSpec((1, 1, D), lambda b, i, c: (b, 0, 0)),
                  pl.BlockSpec((1, D), lambda b, i, c: (0, 0)),
                  wspec, wspec,
                  pl.BlockSpec((2, PEER_NKEYS, PEER_QDIM // 2), lambda b, i, c: (0, 0, 0)),
                  pl.BlockSpec((PEER_EC, D), lambda b, i, c: (c, 0)),
                  pl.BlockSpec((D, PEER_EC), lambda b, i, c: (0, c))],
        out_specs=pl.BlockSpec((1, tn, D), lambda b, i, c: (b, i, 0)),
        scratch_shapes=[pltpu.VMEM((D, tn), BF16),
                        pltpu.VMEM((PEER_HEADS * PEER_QDIM, tn), F32),
                        pltpu.VMEM((2, PEER_NKEYS, tn), F32),
                        tab, tab, tab, tab,
                        pltpu.VMEM((PEER_HEADS, tn // LANES, 8, LANES), F32),
                        pltpu.VMEM((PEER_EC, tn), F32),
                        pltpu.VMEM((PEER_EC, tn), BF16),
                        pltpu.VMEM((D, tn), F32)],
        compiler_params=_cparams(("arbitrary", "arbitrary", "arbitrary")),
        name="peer",
    )(x, sh, sc, gate, ng, wq_hi, wq_lo, keys, u_b, vt_b)


def _rel_bucket(dist):
    n = jnp.maximum(dist, 0)
    max_exact = REL_BUCKETS // 2
    nf = jnp.maximum(n, 1).astype(F32)
    large = max_exact + (jnp.log(nf / max_exact) / math.log(REL_MAX_DIST / max_exact)
                         * (REL_BUCKETS - max_exact)).astype(jnp.int32)
    large = jnp.minimum(large, REL_BUCKETS - 1)
    return jnp.where(n < max_exact, n, large)


def _bias_lookup(bucket, tab_ref, h):
    bias = jnp.zeros(bucket.shape, F32)
    for b in range(REL_BUCKETS):
        bias = jnp.where(bucket == b, tab_ref[b, h], bias)
    return bias


N_BIAS_TILES = 5


def _relbias_kernel(tab_ref, o_ref, c_ref):
    h = pl.program_id(0)
    T = o_ref.shape[2]
    a = lax.broadcasted_iota(jnp.int32, (T, T), 0)
    b = lax.broadcasted_iota(jnp.int32, (T, T), 1)
    last = tab_ref[REL_BUCKETS - 1, h]
    o_ref[0, 0] = jnp.where(a >= b, _bias_lookup(_rel_bucket(a - b), tab_ref, h), NEG_INF)
    o_ref[0, 1] = _bias_lookup(_rel_bucket(a - b + T), tab_ref, h)
    o_ref[0, 2] = jnp.full((T, T), last, F32)
    o_ref[0, 3] = jnp.where(a < b, last, NEG_INF)
    o_ref[0, 4] = jnp.full((T, T), NEG_INF, F32)
    dist = a - ((b - T // 2) * CMP_STRIDE + CMP_LEN - 1)
    c_ref[0] = jnp.where(dist >= 0, _bias_lookup(_rel_bucket(dist), tab_ref, h), last)


def _relbias(rel_table):
    T = ATT_TILE
    assert 2 * T - (T - 1) >= REL_MAX_DIST
    assert WINDOW % T == 0
    return pl.pallas_call(
        _relbias_kernel,
        out_shape=(jax.ShapeDtypeStruct((NSA_HEADS, N_BIAS_TILES, T, T), F32),
                   jax.ShapeDtypeStruct((NSA_HEADS, T, T), F32)),
        grid=(NSA_HEADS,),
        in_specs=[pl.BlockSpec(memory_space=pltpu.SMEM)],
        out_specs=(pl.BlockSpec((1, N_BIAS_TILES, T, T), lambda h: (h, 0, 0, 0)),
                   pl.BlockSpec((1, T, T), lambda h: (h, 0, 0))),
        compiler_params=_cparams(("arbitrary",)),
        name="rel_bias_tiles",
    )(rel_table)


def _cmp_kernel(x_ref, pos_ref, w1_ref, w2_ref, kg_ref, o_ref):
    j = pl.program_id(1)
    half = CMP_STRIDE * HEAD_DIM
    x = x_ref[0, 0]
    pos = pos_ref[0]
    pa = _dot((x + pos[:, :half]).astype(BF16), w1_ref[0, :half, :])
    pb = _dot((x + pos[:, half:]).astype(BF16), w1_ref[0, half:, :])
    pre = pa + pltpu.roll(pb, pb.shape[0] - 1, axis=0)
    out = _dot(jax.nn.gelu(pre, approximate=True).astype(BF16), w2_ref[0])
    o_ref[0, 0] = jnp.where(j < NSA_GROUPS, _rms(out, kg_ref[...]), out)


def _nsa_compress(xr, pos, w1, w2, kg0):
    B, _, M, W = xr.shape
    G = NSA_GROUPS
    return pl.pallas_call(
        _cmp_kernel,
        out_shape=jax.ShapeDtypeStruct((B, 2 * G, M, HEAD_DIM), F32),
        grid=(B, 2 * G),
        in_specs=[pl.BlockSpec((1, 1, M, W), lambda b, j: (b, j, 0, 0)),
                  pl.BlockSpec((1, 1, 2 * W), lambda b, j: (j // G, 0, 0)),
                  pl.BlockSpec((1, 2 * W, HEAD_DIM), lambda b, j: (j // G, 0, 0)),
                  pl.BlockSpec((1, HEAD_DIM, HEAD_DIM), lambda b, j: (j // G, 0, 0)),
                  pl.BlockSpec((1, HEAD_DIM), lambda b, j: (0, 0))],
        out_specs=pl.BlockSpec((1, 1, M, HEAD_DIM), lambda b, j: (b, j, 0, 0)),
        compiler_params=_cparams(("arbitrary", "arbitrary")),
        name="nsa_compress",
    )(xr, pos.reshape(2, 1, 2 * W), w1.astype(BF16), w2.astype(BF16), kg0.reshape(1, HEAD_DIM))


def _cmp_attn_kernel(q_ref, kv_ref, ov_ref, cb_ref, qg_ref, oc_ref, sel_ref, qn_ref):
    qi = pl.program_id(1)
    T = q_ref.shape[1]
    NC = kv_ref.shape[2]
    G, J = NSA_GROUPS, NSA_HPG
    NS = sel_ref.shape[2] // G
    assert T == NC == LANES and T % CMP_STRIDE == 0 and G * NS == LANES
    t = qi * T + lax.broadcasted_iota(jnp.int32, (T, NC), 0)
    lane = lax.broadcasted_iota(jnp.int32, (T, NC), 1)
    vis = t - (lane * CMP_STRIDE + CMP_LEN - 1) >= 0
    visf = vis.astype(F32)
    shift = (qi * (T // CMP_STRIDE) + T // 2) % LANES
    imp = jnp.zeros((T, LANES), F32)
    for g in range(G):
        kc = kv_ref[0, g].astype(BF16)
        vc = kv_ref[0, G + g].astype(BF16)
        psum = jnp.zeros((T, NC), F32)
        for j in range(J):
            h = g * J + j
            cols = slice(h * HEAD_DIM, (h + 1) * HEAD_DIM)
            qh = (_rms(q_ref[0, :, cols], qg_ref[...]) * SCALE).astype(BF16)
            qn_ref[0, :, cols] = qh
            s = _dot_nt(qh, kc) + pltpu.roll(cb_ref[h], shift, axis=1)
            s = jnp.where(vis, s, NEG_INF)
            e = jnp.exp(s - jnp.max(s, axis=-1, keepdims=True))
            p = e / jnp.sum(e, axis=-1, keepdims=True) * visf
            oc_ref[0, :, cols] = _dot(p.astype(BF16), vc)
            psum = psum + p
        imp = imp + _dot(psum, ov_ref[g], HIGHEST)
    mm = lane % NS
    cur = t // SLC_LEN
    forced = (mm == 0) | (mm == cur) | (mm == cur - 1)
    imp = jnp.where(forced, FORCE_SCORE, jnp.where(mm * SLC_LEN <= t, imp, -1.0))
    rank = jnp.zeros((T, LANES), F32)
    for r in range(1, NS):
        wrapped = mm >= NS - r
        other = jnp.where(wrapped, pltpu.roll(imp, NS - r, axis=1), pltpu.roll(imp, LANES - r, axis=1))
        rank = rank + jnp.where(other > imp, 1.0, jnp.where(other == imp, wrapped.astype(F32), 0.0))
    sel_ref[0] = jnp.where(rank < SLC_TOPN, jnp.where(imp >= 0.0, 1.0, 0.0), 0.0)


def _nsa_cmp_attn(y, kcvc, overlap, cmp_bias, q_g):
    B, S, _ = y.shape
    T = ATT_TILE
    NC = kcvc.shape[2]
    return pl.pallas_call(
        _cmp_attn_kernel,
        out_shape=(jax.ShapeDtypeStruct((B, S, NSA_WIDTH), F32),
                   jax.ShapeDtypeStruct((B, S, LANES), F32),
                   jax.ShapeDtypeStruct((B, S, NSA_WIDTH), BF16)),
        grid=(B, S // T),
        in_specs=[pl.BlockSpec((1, T, NSA_WIDTH), lambda b, i: (b, i, 0)),
                  pl.BlockSpec((1, 2 * NSA_GROUPS, NC, HEAD_DIM), lambda b, i: (b, 0, 0, 0)),
                  pl.BlockSpec(overlap.shape, lambda b, i: (0, 0, 0)),
                  pl.BlockSpec(cmp_bias.shape, lambda b, i: (0, 0, 0)),
                  pl.BlockSpec((1, HEAD_DIM), lambda b, i: (0, 0))],
        out_specs=(pl.BlockSpec((1, T, NSA_WIDTH), lambda b, i: (b, i, 0)),
                   pl.BlockSpec((1, T, LANES), lambda b, i: (b, i, 0)),
                   pl.BlockSpec((1, T, NSA_WIDTH), lambda b, i: (b, i, 0))),
        compiler_params=_cparams(("arbitrary", "arbitrary")),
        name="nsa_cmp_attn",
    )(y, kcvc, overlap, cmp_bias, q_g.reshape(1, HEAD_DIM))


def _kv_prep_kernel(ks_ref, kw_ref, kg1_ref, kg2_ref, ksn_ref, vs_ref, kwn_ref, vw_ref):
    KV = NSA_KV
    rows = ks_ref.shape[1]
    ones_col = (lax.broadcasted_iota(jnp.int32, (rows, LANES - HEAD_DIM), 1) == 0).astype(BF16)
    for g in range(NSA_GROUPS):
        cols = slice(g * HEAD_DIM, (g + 1) * HEAD_DIM)
        vcols = slice(KV + g * HEAD_DIM, KV + (g + 1) * HEAD_DIM)
        ksn_ref[0, :, cols] = _rms(ks_ref[0, :, cols], kg1_ref[...]).astype(BF16)
        kwn_ref[0, :, cols] = _rms(kw_ref[0, :, cols], kg2_ref[...]).astype(BF16)
        vs_ref[0, :, g * LANES:g * LANES + HEAD_DIM] = ks_ref[0, :, vcols].astype(BF16)
        vw_ref[0, :, g * LANES:g * LANES + HEAD_DIM] = kw_ref[0, :, vcols].astype(BF16)
        vs_ref[0, :, g * LANES + HEAD_DIM:(g + 1) * LANES] = ones_col
        vw_ref[0, :, g * LANES + HEAD_DIM:(g + 1) * LANES] = ones_col


def _nsa_kv_prep(y, k_g):
    B, S, _ = y.shape
    ksblk = (NSA_WIDTH + 2 * NSA_KV) // (2 * NSA_KV)
    kout = jax.ShapeDtypeStruct((B, S, NSA_KV), BF16)
    vout = jax.ShapeDtypeStruct((B, S, NSA_GROUPS * LANES), BF16)
    kspec = pl.BlockSpec((1, ROW_TILE, NSA_KV), lambda b, i: (b, i, 0))
    vspec = pl.BlockSpec((1, ROW_TILE, NSA_GROUPS * LANES), lambda b, i: (b, i, 0))
    return pl.pallas_call(
        _kv_prep_kernel,
        out_shape=(kout, vout, kout, vout),
        grid=(B, S // ROW_TILE),
        in_specs=[pl.BlockSpec((1, ROW_TILE, 2 * NSA_KV), lambda b, i: (b, i, ksblk)),
                  pl.BlockSpec((1, ROW_TILE, 2 * NSA_KV), lambda b, i: (b, i, ksblk + 1)),
                  pl.BlockSpec((1, HEAD_DIM), lambda b, i: (0, 0)),
                  pl.BlockSpec((1, HEAD_DIM), lambda b, i: (0, 0))],
        out_specs=(kspec, vspec, kspec, vspec),
        compiler_params=_cparams(("arbitrary", "arbitrary")),
        name="nsa_kv_prep",
    )(y, y, k_g[1].reshape(1, HEAD_DIM), k_g[2].reshape(1, HEAD_DIM))


SEL_KEY_PARTS = 4


def _sel_win_kernel(qn_ref, ks_ref, vs_ref, kw_ref, vw_ref, gl_ref, sel_ref, oc_ref, bias_ref, e_ref, bg_ref,
                    o_ref):
    qi = pl.program_id(1)
    T = qn_ref.shape[1]
    S = ks_ref.shape[1]
    nk = S // T
    G, J = NSA_GROUPS, NSA_HPG
    NS = e_ref.shape[0]
    wt = WINDOW // T
    nwin = wt + 1
    assert nk % SEL_KEY_PARTS == 0 and nk >= nwin

    def sel_tile(r):
        return jnp.where(r < 0, 4, jnp.minimum(r, 2))

    def win_tile(r):
        return jnp.where(r < 0, 4, jnp.where(r <= 1, r, jnp.where(r < wt, 2, jnp.where(r == wt, 3, 4))))

    def softmax_pv(s, v):
        p = jnp.exp(s - jnp.max(s, axis=-1, keepdims=True))
        acc = _dot(p.astype(BF16), v)
        return acc[:, :HEAD_DIM] / acc[:, HEAD_DIM:HEAD_DIM + 1]

    gates = jax.nn.sigmoid(gl_ref[0] + bg_ref[...])
    k0 = jnp.maximum(qi - wt, 0)
    wrows = pl.ds(pl.multiple_of(k0 * T, T), nwin * T)

    def attend(nkt):
        nkeys = nkt * T
        for g in range(G):
            gcols = slice(g * HEAD_DIM, (g + 1) * HEAD_DIM)
            vcols = slice(g * LANES, (g + 1) * LANES)
            q4 = jnp.concatenate(
                [qn_ref[0, :, (g * J + j) * HEAD_DIM:(g * J + j + 1) * HEAD_DIM] for j in range(J)], axis=0)
            selg = sel_ref[0, :, g * NS:(g + 1) * NS].astype(BF16)
            mask = (_dot(selg, e_ref[:, :nkeys]) - 1.0) * (-NEG_INF)
            bias = jnp.concatenate(
                [jnp.concatenate([bias_ref[g * J + j, sel_tile(qi - kt)] for kt in range(nkt)], axis=1) + mask
                 for j in range(J)], axis=0)
            o_s = softmax_pv(_dot_nt(q4, ks_ref[0, :nkeys, gcols]) + bias, vs_ref[0, :nkeys, vcols])
            wbias = jnp.concatenate(
                [jnp.concatenate([bias_ref[g * J + j, win_tile(qi - k0 - t)] for t in range(nwin)], axis=1)
                 for j in range(J)], axis=0)
            o_w = softmax_pv(_dot_nt(q4, kw_ref[0, wrows, gcols]) + wbias, vw_ref[0, wrows, vcols])
            for j in range(J):
                h = g * J + j
                cols = slice(h * HEAD_DIM, (h + 1) * HEAD_DIM)
                rows = slice(j * T, (j + 1) * T)
                o_ref[0, :, cols] = (gates[:, 3 * h:3 * h + 1] * oc_ref[0, :, cols]
                                     + gates[:, 3 * h + 1:3 * h + 2] * o_s[rows]
                                     + gates[:, 3 * h + 2:3 * h + 3] * o_w[rows])

    per = nk // SEL_KEY_PARTS
    for part in range(SEL_KEY_PARTS):
        pl.when((qi >= part * per) & (qi < (part + 1) * per))(functools.partial(attend, (part + 1) * per))


def _nsa_sel_win(qn, y, kv, sel, oc, bias_tiles, expand, b_gate):
    B, S, _ = y.shape
    T = ATT_TILE
    glblk = (NSA_WIDTH + 6 * NSA_KV) // LANES
    bg = jnp.zeros((1, LANES), F32).at[0, :3 * NSA_HEADS].set(b_gate)
    kspec = pl.BlockSpec((1, S, NSA_KV), lambda b, i: (b, 0, 0))
    vspec = pl.BlockSpec((1, S, NSA_GROUPS * LANES), lambda b, i: (b, 0, 0))
    return pl.pallas_call(
        _sel_win_kernel,
        out_shape=jax.ShapeDtypeStruct((B, S, NSA_WIDTH), F32),
        grid=(B, S // T),
        in_specs=[pl.BlockSpec((1, T, NSA_WIDTH), lambda b, i: (b, i, 0)),
                  kspec, vspec, kspec, vspec,
                  pl.BlockSpec((1, T, LANES), lambda b, i: (b, i, glblk)),
                  pl.BlockSpec((1, T, sel.shape[2]), lambda b, i: (b, i, 0)),
                  pl.BlockSpec((1, T, NSA_WIDTH), lambda b, i: (b, i, 0)),
                  pl.BlockSpec(bias_tiles.shape, lambda b, i: (0, 0, 0, 0)),
                  pl.BlockSpec(expand.shape, lambda b, i: (0, 0)),
                  pl.BlockSpec((1, LANES), lambda b, i: (0, 0))],
        out_specs=pl.BlockSpec((1, T, NSA_WIDTH), lambda b, i: (b, i, 0)),
        compiler_params=_cparams(("arbitrary", "arbitrary")),
        name="nsa_sel_win",
    )(qn, *kv, y, sel, oc, bias_tiles, expand, bg)


def _odd_mixer(x, sh, sc, gate, ng, w_in, b_gate, q_g, k_g, cmp_pos, cmp_w1, cmp_w2, bias_tiles, cmp_bias,
               w_out):
    B, S, _ = x.shape
    G = NSA_GROUPS
    w = jnp.zeros((D_MODEL, ODD_IN_PAD), BF16).at[:, :ODD_IN].set(w_in.astype(BF16))
    y = _norm_proj(x, sh, sc, ng, w)
    xr = y[:, :, NSA_WIDTH:NSA_WIDTH + 2 * NSA_KV].reshape(B, S, 2, G, HEAD_DIM)
    xr = xr.transpose(0, 2, 3, 1, 4).reshape(B, 2 * G, S // CMP_STRIDE, CMP_STRIDE * HEAD_DIM)
    kcvc = _nsa_compress(xr, cmp_pos.reshape(2, CMP_LEN * HEAD_DIM), cmp_w1, cmp_w2, k_g[0])
    n_slc = S // SLC_LEN
    starts = jnp.arange(S // CMP_STRIDE) * CMP_STRIDE
    bstart = jnp.arange(n_slc) * SLC_LEN
    real = (jnp.arange(S // CMP_STRIDE) < (S - CMP_LEN) // CMP_STRIDE + 1)[:, None]
    overlap = (real & (starts[:, None] < bstart[None, :] + SLC_LEN)
               & (starts[:, None] + CMP_LEN > bstart[None, :])).astype(F32)
    overlap = jnp.stack([jnp.pad(overlap, ((0, 0), (g * n_slc, (G - 1 - g) * n_slc))) for g in range(G)])
    expand = (jnp.arange(S)[None, :] // SLC_LEN == jnp.arange(n_slc)[:, None]).astype(BF16)
    oc, sel, qn = _nsa_cmp_attn(y, kcvc, overlap, cmp_bias, q_g)
    o = _nsa_sel_win(qn, y, _nsa_kv_prep(y, k_g), sel, oc, bias_tiles, expand, b_gate)
    return _out_proj(x, o, 0, o, 1, w_out.astype(BF16), gate)


def _even_mixer(x, sh, sc, gate, ng, w_in, b_f, conv_w, q_g, k_g, w_out):
    w = jnp.zeros((D_MODEL, EVEN_IN_PAD), BF16).at[:, :EVEN_IN].set(w_in.astype(BF16))
    y = _norm_proj(x, sh, sc, ng, w)
    conv = _short_conv(y, conv_w)
    attn = _fox_attention(y, b_f, q_g, k_g)
    return _out_proj(x, conv, 0, attn, 0, w_out.astype(BF16), gate)


def kernel(x, c, ada_w, ada_b, norm_g, even_w_in, even_b_f, even_conv_w, even_q_g, even_k_g, even_w_out, odd_w_in, odd_b_gate, odd_q_g, odd_k_g, odd_cmp_pos, odd_cmp_w1, odd_cmp_w2, odd_w_out, rel_table, peer_w_q, peer_keys, peer_u, peer_v):
    B = x.shape[0]
    mods = _ada(c, ada_w, ada_b)
    bias_tiles, cmp_bias = _relbias(rel_table)
    for layer in range(DEPTH):
        sh1, sc1, g1, sh2, sc2, g2 = [m.reshape(B, 1, D_MODEL) for m in jnp.split(mods[layer], 6, axis=-1)]
        i = layer // 2
        ng1 = norm_g[layer, 0].reshape(1, D_MODEL)
        ng2 = norm_g[layer, 1].reshape(1, D_MODEL)
        if layer % 2 == 0:
            x = _even_mixer(x, sh1, sc1, g1, ng1, even_w_in[i], even_b_f[i], even_conv_w[i], even_q_g[i],
                            even_k_g[i], even_w_out[i])
        else:
            x = _odd_mixer(x, sh1, sc1, g1, ng1, odd_w_in[i], odd_b_gate[i], odd_q_g[i], odd_k_g[i],
                           odd_cmp_pos[i], odd_cmp_w1[i], odd_cmp_w2[i], bias_tiles, cmp_bias, odd_w_out[i])
        x = _peer(x, sh2, sc2, g2, ng2, peer_w_q[layer], peer_keys[layer],
                  peer_u[layer].astype(BF16), peer_v[layer].T.astype(BF16))
    return x
```

```python
import functools
import math

import jax
import jax.numpy as jnp
from jax import lax
from jax.experimental import pallas as pl
from jax.experimental.pallas import tpu as pltpu

D_MODEL = 1024
DEPTH = 4
HEAD_DIM = 64
CONV_CH = 512
CONV_K = 3
FOX_HEADS = 8
FOX_WIDTH = FOX_HEADS * HEAD_DIM
EVEN_IN = 3 * CONV_CH + 3 * FOX_WIDTH + FOX_HEADS
NSA_GROUPS = 4
NSA_HPG = 4
NSA_HEADS = NSA_GROUPS * NSA_HPG
NSA_WIDTH = NSA_HEADS * HEAD_DIM
NSA_KV = NSA_GROUPS * HEAD_DIM
CMP_LEN = 32
CMP_STRIDE = 16
SLC_LEN = 64
SLC_TOPN = 16
WINDOW = 512
ODD_IN = NSA_WIDTH + 6 * NSA_KV + 3 * NSA_HEADS
REL_BUCKETS = 32
REL_MAX_DIST = 128
PEER_HEADS = 8
PEER_NKEYS = 128
PEER_EXPERTS = PEER_NKEYS * PEER_NKEYS
PEER_QDIM = 128
PEER_TOPK = 16
EPS = 1e-6
NEG_INF = -1e30
FORCE_SCORE = 1e6

LANES = 128
F32 = jnp.float32
BF16 = jnp.bfloat16
HIGHEST = lax.Precision.HIGHEST
VMEM_LIMIT = 56 * 1024 * 1024

EVEN_IN_PAD = 3200
ODD_IN_PAD = 2688
ROW_TILE = 512
ATT_TILE = 128
FOX_TILE = 256
PEER_TN = 512
PEER_EC = 2048
SCALE = HEAD_DIM ** -0.5


def _cparams(sem, flags=None):
    return pltpu.CompilerParams(dimension_semantics=sem, vmem_limit_bytes=VMEM_LIMIT, flags=flags)


def _dot(a, b, precision=None):
    return jnp.dot(a, b, precision=precision, preferred_element_type=F32)


def _dot_nt(a, b, precision=None):
    return lax.dot_general(a, b, (((1,), (1,)), ((), ())), precision=precision,
                           preferred_element_type=F32)


def _rms(x, g):
    return x * lax.rsqrt(jnp.mean(x * x, axis=-1, keepdims=True) + EPS) * g


def _ada_kernel(c_ref, w_ref, b_ref, o_ref):
    c = c_ref[...]
    sc = c * jax.nn.sigmoid(c)
    o_ref[0] = _dot(sc, w_ref[0], HIGHEST) + b_ref[0]


def _ada(c, ada_w, ada_b):
    B, D = c.shape
    E = ada_w.shape[2]
    tn = 2048
    return pl.pallas_call(
        _ada_kernel,
        out_shape=jax.ShapeDtypeStruct((DEPTH, B, E), F32),
        grid=(DEPTH, E // tn),
        in_specs=[pl.BlockSpec((B, D), lambda l, j: (0, 0)),
                  pl.BlockSpec((1, D, tn), lambda l, j: (l, 0, j)),
                  pl.BlockSpec((1, 1, tn), lambda l, j: (l, 0, j))],
        out_specs=pl.BlockSpec((1, B, tn), lambda l, j: (l, 0, j)),
        compiler_params=_cparams(("arbitrary", "arbitrary")),
        name="ada_mod",
    )(c, ada_w, ada_b.reshape(DEPTH, 1, E))


def _norm_proj_kernel(x_ref, sh_ref, sc_ref, ng_ref, w_ref, y_ref):
    h = _rms(x_ref[0], ng_ref[...]) * (1.0 + sc_ref[0]) + sh_ref[0]
    y_ref[0] = _dot(h.astype(BF16), w_ref[...])


def _norm_proj(x, sh, sc, ng, w):
    B, S, D = x.shape
    N = w.shape[1]
    return pl.pallas_call(
        _norm_proj_kernel,
        out_shape=jax.ShapeDtypeStruct((B, S, N), F32),
        grid=(B, S // ROW_TILE),
        in_specs=[pl.BlockSpec((1, ROW_TILE, D), lambda b, i: (b, i, 0)),
                  pl.BlockSpec((1, 1, D), lambda b, i: (b, 0, 0)),
                  pl.BlockSpec((1, 1, D), lambda b, i: (b, 0, 0)),
                  pl.BlockSpec((1, D), lambda b, i: (0, 0)),
                  pl.BlockSpec((D, N), lambda b, i: (0, 0))],
        out_specs=pl.BlockSpec((1, ROW_TILE, N), lambda b, i: (b, i, 0)),
        compiler_params=_cparams(("arbitrary", "arbitrary")),
        name="norm_proj",
    )(x, sh, sc, ng, w)


def _out_proj_kernel(x_ref, a_ref, b_ref, wa_ref, wb_ref, g_ref, o_ref):
    y = _dot(a_ref[0].astype(BF16), wa_ref[...]) + _dot(b_ref[0].astype(BF16), wb_ref[...])
    o_ref[0] = x_ref[0] + g_ref[0] * y


def _out_proj(x, a, a_blk, b, b_blk, w, gate):
    B, S, D = x.shape
    half = w.shape[0] // 2
    wa, wb = w[:half], w[half:]
    return pl.pallas_call(
        _out_proj_kernel,
        out_shape=jax.ShapeDtypeStruct((B, S, D), F32),
        grid=(B, S // ROW_TILE),
        in_specs=[pl.BlockSpec((1, ROW_TILE, D), lambda b, i: (b, i, 0)),
                  pl.BlockSpec((1, ROW_TILE, half), lambda b, i: (b, i, a_blk)),
                  pl.BlockSpec((1, ROW_TILE, half), lambda b, i: (b, i, b_blk)),
                  pl.BlockSpec((half, D), lambda b, i: (0, 0)),
                  pl.BlockSpec((half, D), lambda b, i: (0, 0)),
                  pl.BlockSpec((1, 1, D), lambda b, i: (b, 0, 0))],
        out_specs=pl.BlockSpec((1, ROW_TILE, D), lambda b, i: (b, i, 0)),
        compiler_params=_cparams(("arbitrary", "arbitrary")),
        name="out_proj",
    )(x, a, b, wa, wb, gate)


def _conv_kernel(cb_ref, cc_ref, ch_ref, w_ref, o_ref):
    u = cc_ref[0] * ch_ref[0]
    row = lax.broadcasted_iota(jnp.int32, u.shape, 0)
    u1 = jnp.where(row >= 1, pltpu.roll(u, 1, axis=0), 0.0)
    u2 = jnp.where(row >= 2, pltpu.roll(u, 2, axis=0), 0.0)
    w = w_ref[...]
    o_ref[0] = cb_ref[0] * (w[0:1] * u2 + w[1:2] * u1 + w[2:3] * u)


def _short_conv(y, conv_w):
    B, S, _ = y.shape
    nblk = CONV_CH // LANES
    w = jnp.zeros((8, CONV_CH), F32).at[:CONV_K].set(conv_w)
    return pl.pallas_call(
        _conv_kernel,
        out_shape=jax.ShapeDtypeStruct((B, S, CONV_CH), F32),
        grid=(B, nblk),
        in_specs=[pl.BlockSpec((1, S, LANES), lambda b, j: (b, 0, j)),
                  pl.BlockSpec((1, S, LANES), lambda b, j: (b, 0, nblk + j)),
                  pl.BlockSpec((1, S, LANES), lambda b, j: (b, 0, 2 * nblk + j)),
                  pl.BlockSpec((8, LANES), lambda b, j: (0, j))],
        out_specs=pl.BlockSpec((1, S, LANES), lambda b, j: (b, 0, j)),
        compiler_params=_cparams(("arbitrary", "arbitrary")),
        name="short_conv",
    )(y, y, y, w)


def _cumsum_rows(x):
    n = x.shape[0]
    row = lax.broadcasted_iota(jnp.int32, x.shape, 0)
    sh = 1
    while sh < n:
        x = x + jnp.where(row >= sh, pltpu.roll(x, sh, axis=0), 0.0)
        sh *= 2
    return x


FOX_KEY_PARTS = 4


def _fox_kernel(q_ref, k_ref, v_ref, f_ref, bf_ref, qg_ref, kg_ref, o_ref,
                kn_scr, vb_scr, cum_scr, cumrow_scr):
    qi = pl.program_id(1)
    S = k_ref.shape[1]
    T = q_ref.shape[1]
    nq = S // T

    @pl.when(qi == 0)
    def _prep():
        ones_col = (lax.broadcasted_iota(jnp.int32, (S, LANES - HEAD_DIM), 1) == 0).astype(BF16)
        for h in range(FOX_HEADS):
            cols = slice(h * HEAD_DIM, (h + 1) * HEAD_DIM)
            kn_scr[:, cols] = _rms(k_ref[0, :, cols], kg_ref[...]).astype(BF16)
            vb_scr[:, h * LANES:h * LANES + HEAD_DIM] = v_ref[0, :, cols].astype(BF16)
            vb_scr[:, h * LANES + HEAD_DIM:(h + 1) * LANES] = ones_col
        logf = jax.nn.log_sigmoid(f_ref[0] + bf_ref[...])
        cum = _cumsum_rows(logf)
        cum_scr[...] = cum
        for kt in range(S // T):
            cumrow_scr[:, kt * T:(kt + 1) * T] = cum[kt * T:(kt + 1) * T, :].T[0:8, :]

    cq = cum_scr[pl.ds(pl.multiple_of(qi * T, T), T), :]

    def attend(nkeys):
        row = qi * T + lax.broadcasted_iota(jnp.int32, (T, nkeys), 0)
        causal = jnp.where(lax.broadcasted_iota(jnp.int32, (T, nkeys), 1) <= row, 0.0, NEG_INF)
        for h in range(FOX_HEADS):
            cols = slice(h * HEAD_DIM, (h + 1) * HEAD_DIM)
            qh = (_rms(q_ref[0, :, cols], qg_ref[...]) * SCALE).astype(BF16)
            s = _dot_nt(qh, kn_scr[0:nkeys, cols]) + ((cq[:, h:h + 1] - cumrow_scr[h:h + 1, 0:nkeys]) + causal)
            p = jnp.exp(s - jnp.max(s, axis=-1, keepdims=True))
            acc = _dot(p.astype(BF16), vb_scr[0:nkeys, h * LANES:(h + 1) * LANES])
            o_ref[0, :, cols] = acc[:, :HEAD_DIM] / acc[:, HEAD_DIM:HEAD_DIM + 1]

    parts = min(FOX_KEY_PARTS, nq)
    per = nq // parts
    for part in range(parts):
        pl.when((qi >= part * per) & (qi < (part + 1) * per))(functools.partial(attend, (part + 1) * (S // parts)))


def _fox_attention(y, b_f, q_g, k_g):
    B, S, _ = y.shape
    T = FOX_TILE
    qblk = 3 * CONV_CH // FOX_WIDTH
    fblk = (3 * CONV_CH + 3 * FOX_WIDTH) // LANES
    bf = jnp.zeros((1, LANES), F32).at[0, :FOX_HEADS].set(b_f)
    return pl.pallas_call(
        _fox_kernel,
        out_shape=jax.ShapeDtypeStruct((B, S, FOX_WIDTH), F32),
        grid=(B, S // T),
        in_specs=[pl.BlockSpec((1, T, FOX_WIDTH), lambda b, i: (b, i, qblk)),
                  pl.BlockSpec((1, S, FOX_WIDTH), lambda b, i: (b, 0, qblk + 1)),
                  pl.BlockSpec((1, S, FOX_WIDTH), lambda b, i: (b, 0, qblk + 2)),
                  pl.BlockSpec((1, S, LANES), lambda b, i: (b, 0, fblk)),
                  pl.BlockSpec((1, LANES), lambda b, i: (0, 0)),
                  pl.BlockSpec((1, HEAD_DIM), lambda b, i: (0, 0)),
                  pl.BlockSpec((1, HEAD_DIM), lambda b, i: (0, 0))],
        out_specs=pl.BlockSpec((1, T, FOX_WIDTH), lambda b, i: (b, i, 0)),
        scratch_shapes=[pltpu.VMEM((S, FOX_WIDTH), BF16),
                        pltpu.VMEM((S, FOX_HEADS * LANES), BF16),
                        pltpu.VMEM((S, LANES), F32),
                        pltpu.VMEM((8, S), F32)],
        compiler_params=_cparams(("arbitrary", "arbitrary")),
        name="fox_attention",
    )(y, y, y, y, bf, q_g.reshape(1, HEAD_DIM), k_g.reshape(1, HEAD_DIM))


def _extract_top(work, iota, k, exact):
    nrows = work.shape[0]
    vals = []
    for _ in range(k):
        m = jnp.max(work, axis=0, keepdims=True)
        if exact:
            idx = jnp.min(jnp.where(work == m, iota, nrows), axis=0, keepdims=True)
            work = jnp.where(iota == idx, -jnp.inf, work)
        else:
            work = jnp.where(work == m, -jnp.inf, work)
        vals.append(m)
    return vals, work


def _removed(work):
    return jnp.sum(jnp.where(work == -jnp.inf, 1.0, 0.0), axis=0, keepdims=True)


def _peer_kernel(x_ref, sh_ref, sc_ref, gt_ref, ng_ref, wqh_ref, wql_ref, keys_ref, u_ref, vt_ref, o_ref,
                 ht_scr, qt_scr, sp_scr, t0_scr, t1_scr, tau_scr, a_scr, p_scr, acc_scr):
    c = pl.program_id(2)
    nc = pl.num_programs(2)
    tn = x_ref.shape[1]
    nlt = tn // LANES
    half = PEER_QDIM // 2

    @pl.when(c == 0)
    def _prep():
        h = _rms(x_ref[0], ng_ref[...]) * (1.0 + sc_ref[0]) + sh_ref[0]
        hh = h.astype(BF16)
        hl = (h - hh.astype(F32)).astype(BF16)
        cw = 2 * LANES
        for k in range(h.shape[1] // cw):
            cs = slice(k * cw, (k + 1) * cw)
            q = _dot(hh, wqh_ref[:, cs]) + (_dot(hh, wql_ref[:, cs]) + _dot(hl, wqh_ref[:, cs]))
            qt_scr[cs, :] = q.T
            ht_scr[cs, :] = h[:, cs].T.astype(BF16)
        acc_scr[...] = jnp.zeros_like(acc_scr)
        iota_k = lax.broadcasted_iota(jnp.int32, (PEER_NKEYS, LANES), 0)
        iota_c = lax.broadcasted_iota(jnp.int32, (PEER_TOPK + 8 * 8, LANES), 0)

        n_pad = sum(8 - PEER_TOPK // (r0 + 1) for r0 in range(2, 8))

        def select(hd, lt, exact):
            lanes = slice(lt * LANES, (lt + 1) * LANES)
            s0 = sp_scr[0, :, lanes]
            s1 = sp_scr[1, :, lanes]
            sv0, w0 = _extract_top(s0, iota_k, PEER_TOPK, exact)
            sv1, w1 = _extract_top(s1, iota_k, PEER_TOPK, exact)
            s0m = jnp.where(w0 == -jnp.inf, s0, -jnp.inf)
            s1m = jnp.where(w1 == -jnp.inf, s1, -jnp.inf)
            sv0_all = jnp.concatenate(sv0, axis=0)
            sv1_all = jnp.concatenate(sv1, axis=0)
            parts = [sv0[0] + sv1_all, sv0[1] + sv1_all[:8]]
            for r0 in range(2, 8):
                parts.append(jnp.where(iota_c[:8] < PEER_TOPK // (r0 + 1), sv0[r0] + sv1_all[:8], -jnp.inf))
            parts.append(sv0_all[8:] + sv1[0])
            cv, wc = _extract_top(jnp.concatenate(parts, axis=0), iota_c, PEER_TOPK, exact)
            z = cv[0] - cv[0]
            for r in range(PEER_TOPK):
                z = z + jnp.exp(cv[r] - cv[0])
            nrt = PEER_NKEYS // 8
            t0_scr[hd, lt, :, 0] = s0m.reshape(nrt, 8, LANES)
            t0_scr[hd, lt, :, 1] = jnp.exp(s0m - sv0[0]).reshape(nrt, 8, LANES)
            t1_scr[hd, lt, :, 0] = s1m.reshape(nrt, 8, LANES)
            t1_scr[hd, lt, :, 1] = (jnp.exp(s1m - sv1[0]) / z).reshape(nrt, 8, LANES)
            tau_scr[hd, lt] = jnp.broadcast_to(cv[PEER_TOPK - 1], (8, LANES))
            return ((_removed(w0) != PEER_TOPK) | (_removed(w1) != PEER_TOPK)
                    | (_removed(wc) != PEER_TOPK + n_pad))

        def head_body(hd, carry):
            for p in range(2):
                qs = qt_scr[pl.ds(pl.multiple_of((hd * 2 + p) * half, half), half), :]
                sp_scr[p] = _dot(keys_ref[p], qs, HIGHEST)
            tied = jnp.zeros((1, LANES), F32)
            for lt in range(nlt):
                tied = jnp.maximum(tied, jnp.where(select(hd, lt, False), 1.0, 0.0))

            @pl.when(jnp.max(tied) > 0.0)
            def _redo():
                for lt in range(nlt):
                    select(hd, lt, True)
            return carry

        lax.fori_loop(0, PEER_HEADS, head_body, 0)

    nsub = 8
    nrq = 32
    nparts = PEER_EC // PEER_NKEYS // nsub
    for part in range(nparts):
        er = slice(part * nsub * PEER_NKEYS, (part + 1) * nsub * PEER_NKEYS)
        a_scr[er, :] = jax.nn.gelu(_dot(u_ref[er, :], ht_scr[...]), approximate=True)

    def weights_tile(part, lt, rq, prev):
        lanes = slice(lt * LANES, (lt + 1) * LANES)
        jt = slice(rq * nrq // 8, (rq + 1) * nrq // 8)
        blk = c * (PEER_EC // PEER_NKEYS // 8) + part * nsub // 8
        zero = 0.0 if prev is None else jnp.where(prev != prev, prev, 0.0)
        w = [jnp.zeros((nrq, LANES), F32) for _ in range(nsub)]
        for hd in range(PEER_HEADS):
            s1 = t1_scr[hd, lt, jt, 0].reshape(nrq, LANES)
            e1 = t1_scr[hd, lt, jt, 1].reshape(nrq, LANES)
            s0 = t0_scr[hd, lt, blk, 0] + zero
            e0 = t0_scr[hd, lt, blk, 1] + zero
            tau = tau_scr[hd, lt, 0:1, :]
            for ii in range(nsub):
                keep = s1 + s0[ii:ii + 1] >= tau
                w[ii] = w[ii] + jnp.where(keep, e1 * e0[ii:ii + 1], 0.0)
        for ii in range(nsub):
            rows = slice((part * nsub + ii) * PEER_NKEYS + rq * nrq, (part * nsub + ii) * PEER_NKEYS + (rq + 1) * nrq)
            w[ii] = w[ii] * a_scr[rows, lanes]
            p_scr[rows, lanes] = w[ii].astype(BF16)
        return w[nsub - 1][0:1, :]

    prev = None
    for part in range(nparts):
        for lt in range(nlt):
            for rq in range(PEER_NKEYS // nrq):
                prev = weights_tile(part, lt, rq, prev)
    acc_scr[...] += _dot(vt_ref[...], p_scr[...])

    @pl.when(c == nc - 1)
    def _fin():
        o_ref[0] = x_ref[0] + gt_ref[0] * acc_scr[...].T


def _peer(x, sh, sc, gate, ng, w_q, keys, u_b, vt_b):
    B, S, D = x.shape
    tn = PEER_TN
    nc = PEER_EXPERTS // PEER_EC
    tab = pltpu.VMEM((PEER_HEADS, tn // LANES, PEER_NKEYS // 8, 2, 8, LANES), F32)
    wq_hi = w_q.astype(BF16)
    wq_lo = (w_q - wq_hi.astype(F32)).astype(BF16)
    wspec = pl.BlockSpec((D, PEER_HEADS * PEER_QDIM), lambda b, i, c: (0, 0), pipeline_mode=pl.Buffered(1))
    return pl.pallas_call(
        _peer_kernel,
        out_shape=jax.ShapeDtypeStruct((B, S, D), F32),
        grid=(B, S // tn, nc),
        in_specs=[pl.BlockSpec((1, tn, D), lambda b, i, c: (b, i, 0), pipeline_mode=pl.Buffered(1)),
                  pl.BlockSpec((1, 1, D), lambda b, i, c: (b, 0, 0)),
                  pl.BlockSpec((1, 1, D), lambda b, i, c: (b, 0, 0)),
                  pl.BlockSpec((1, 1, D), lambda b, i, c: (b, 0, 0)),
                  pl.BlockSpec((1, D), lambda b, i, c: (0, 0)),
                  wspec, wspec,
                  pl.BlockSpec((2, PEER_NKEYS, PEER_QDIM // 2), lambda b, i, c: (0, 0, 0)),
                  pl.BlockSpec((PEER_EC, D), lambda b, i, c: (c, 0)),
                  pl.BlockSpec((D, PEER_EC), lambda b, i, c: (0, c))],
        out_specs=pl.BlockSpec((1, tn, D), lambda b, i, c: (b, i, 0)),
        scratch_shapes=[pltpu.VMEM((D, tn), BF16),
                        pltpu.VMEM((PEER_HEADS * PEER_QDIM, tn), F32),
                        pltpu.VMEM((2, PEER_NKEYS, tn), F32),
                        tab, tab,
                        pltpu.VMEM((PEER_HEADS, tn // LANES, 8, LANES), F32),
                        pltpu.VMEM((PEER_EC, tn), F32),
                        pltpu.VMEM((PEER_EC, tn), BF16),
                        pltpu.VMEM((D, tn), F32)],
        compiler_params=_cparams(("arbitrary", "arbitrary", "arbitrary")),
        name="peer",
    )(x, sh, sc, gate, ng, wq_hi, wq_lo, keys, u_b, vt_b)


def _rel_bucket(dist):
    n = jnp.maximum(dist, 0)
    max_exact = REL_BUCKETS // 2
    nf = jnp.maximum(n, 1).astype(F32)
    large = max_exact + (jnp.log(nf / max_exact) / math.log(REL_MAX_DIST / max_exact)
                         * (REL_BUCKETS - max_exact)).astype(jnp.int32)
    large = jnp.minimum(large, REL_BUCKETS - 1)
    return jnp.where(n < max_exact, n, large)


def _bias_lookup(bucket, tab_ref, h):
    bias = jnp.zeros(bucket.shape, F32)
    for b in range(REL_BUCKETS):
        bias = jnp.where(bucket == b, tab_ref[b, h], bias)
    return bias


N_BIAS_TILES = 5


def _relbias_kernel(tab_ref, o_ref, c_ref):
    h = pl.program_id(0)
    T = o_ref.shape[2]
    a = lax.broadcasted_iota(jnp.int32, (T, T), 0)
    b = lax.broadcasted_iota(jnp.int32, (T, T), 1)
    last = tab_ref[REL_BUCKETS - 1, h]
    o_ref[0, 0] = jnp.where(a >= b, _bias_lookup(_rel_bucket(a - b), tab_ref, h), NEG_INF)
    o_ref[0, 1] = _bias_lookup(_rel_bucket(a - b + T), tab_ref, h)
    o_ref[0, 2] = jnp.full((T, T), last, F32)
    o_ref[0, 3] = jnp.where(a < b, last, NEG_INF)
    o_ref[0, 4] = jnp.full((T, T), NEG_INF, F32)
    dist = a - ((b - T // 2) * CMP_STRIDE + CMP_LEN - 1)
    c_ref[0] = jnp.where(dist >= 0, _bias_lookup(_rel_bucket(dist), tab_ref, h), last)


def _relbias(rel_table):
    T = ATT_TILE
    assert 2 * T - (T - 1) >= REL_MAX_DIST
    assert WINDOW % T == 0
    return pl.pallas_call(
        _relbias_kernel,
        out_shape=(jax.ShapeDtypeStruct((NSA_HEADS, N_BIAS_TILES, T, T), F32),
                   jax.ShapeDtypeStruct((NSA_HEADS, T, T), F32)),
        grid=(NSA_HEADS,),
        in_specs=[pl.BlockSpec(memory_space=pltpu.SMEM)],
        out_specs=(pl.BlockSpec((1, N_BIAS_TILES, T, T), lambda h: (h, 0, 0, 0)),
                   pl.BlockSpec((1, T, T), lambda h: (h, 0, 0))),
        compiler_params=_cparams(("arbitrary",)),
        name="rel_bias_tiles",
    )(rel_table)


def _cmp_kernel(x_ref, pos_ref, w1_ref, w2_ref, kg_ref, o_ref):
    j = pl.program_id(1)
    half = CMP_STRIDE * HEAD_DIM
    x = x_ref[0, 0]
    pos = pos_ref[0]
    pa = _dot((x + pos[:, :half]).astype(BF16), w1_ref[0, :half, :])
    pb = _dot((x + pos[:, half:]).astype(BF16), w1_ref[0, half:, :])
    pre = pa + pltpu.roll(pb, pb.shape[0] - 1, axis=0)
    out = _dot(jax.nn.gelu(pre, approximate=True).astype(BF16), w2_ref[0])
    o_ref[0, 0] = jnp.where(j < NSA_GROUPS, _rms(out, kg_ref[...]), out)


def _nsa_compress(xr, pos, w1, w2, kg0):
    B, _, M, W = xr.shape
    G = NSA_GROUPS
    return pl.pallas_call(
        _cmp_kernel,
        out_shape=jax.ShapeDtypeStruct((B, 2 * G, M, HEAD_DIM), F32),
        grid=(B, 2 * G),
        in_specs=[pl.BlockSpec((1, 1, M, W), lambda b, j: (b, j, 0, 0)),
                  pl.BlockSpec((1, 1, 2 * W), lambda b, j: (j // G, 0, 0)),
                  pl.BlockSpec((1, 2 * W, HEAD_DIM), lambda b, j: (j // G, 0, 0)),
                  pl.BlockSpec((1, HEAD_DIM, HEAD_DIM), lambda b, j: (j // G, 0, 0)),
                  pl.BlockSpec((1, HEAD_DIM), lambda b, j: (0, 0))],
        out_specs=pl.BlockSpec((1, 1, M, HEAD_DIM), lambda b, j: (b, j, 0, 0)),
        compiler_params=_cparams(("arbitrary", "arbitrary")),
        name="nsa_compress",
    )(xr, pos.reshape(2, 1, 2 * W), w1.astype(BF16), w2.astype(BF16), kg0.reshape(1, HEAD_DIM))


def _cmp_attn_kernel(q_ref, kv_ref, ov_ref, cb_ref, qg_ref, oc_ref, sel_ref, qn_ref):
    qi = pl.program_id(1)
    T = q_ref.shape[1]
    NC = kv_ref.shape[2]
    G, J = NSA_GROUPS, NSA_HPG
    NS = sel_ref.shape[2] // G
    assert T == NC == LANES and T % CMP_STRIDE == 0 and G * NS == LANES
    t = qi * T + lax.broadcasted_iota(jnp.int32, (T, NC), 0)
    lane = lax.broadcasted_iota(jnp.int32, (T, NC), 1)
    vis = t - (lane * CMP_STRIDE + CMP_LEN - 1) >= 0
    visf = vis.astype(F32)
    shift = (qi * (T // CMP_STRIDE) + T // 2) % LANES
    imp = jnp.zeros((T, LANES), F32)
    for g in range(G):
        kc = kv_ref[0, g].astype(BF16)
        vc = kv_ref[0, G + g].astype(BF16)
        psum = jnp.zeros((T, NC), F32)
        for j in range(J):
            h = g * J + j
            cols = slice(h * HEAD_DIM, (h + 1) * HEAD_DIM)
            qh = (_rms(q_ref[0, :, cols], qg_ref[...]) * SCALE).astype(BF16)
            qn_ref[0, :, cols] = qh
            s = _dot_nt(qh, kc) + pltpu.roll(cb_ref[h], shift, axis=1)
            s = jnp.where(vis, s, NEG_INF)
            e = jnp.exp(s - jnp.max(s, axis=-1, keepdims=True))
            p = e / jnp.sum(e, axis=-1, keepdims=True) * visf
            oc_ref[0, :, cols] = _dot(p.astype(BF16), vc)
            psum = psum + p
        imp = imp + _dot(psum, ov_ref[g], HIGHEST)
    mm = lane % NS
    cur = t // SLC_LEN
    forced = (mm == 0) | (mm == cur) | (mm == cur - 1)
    imp = jnp.where(forced, FORCE_SCORE, jnp.where(mm * SLC_LEN <= t, imp, -1.0))
    rank = jnp.zeros((T, LANES), F32)
    for r in range(1, NS):
        wrapped = mm >= NS - r
        other = jnp.where(wrapped, pltpu.roll(imp, NS - r, axis=1), pltpu.roll(imp, LANES - r, axis=1))
        rank = rank + jnp.where(other > imp, 1.0, jnp.where(other == imp, wrapped.astype(F32), 0.0))
    sel_ref[0] = jnp.where(rank < SLC_TOPN, jnp.where(imp >= 0.0, 1.0, 0.0), 0.0)


def _nsa_cmp_attn(y, kcvc, overlap, cmp_bias, q_g):
    B, S, _ = y.shape
    T = ATT_TILE
    NC = kcvc.shape[2]
    return pl.pallas_call(
        _cmp_attn_kernel,
        out_shape=(jax.ShapeDtypeStruct((B, S, NSA_WIDTH), F32),
                   jax.ShapeDtypeStruct((B, S, LANES), F32),
                   jax.ShapeDtypeStruct((B, S, NSA_WIDTH), BF16)),
        grid=(B, S // T),
        in_specs=[pl.BlockSpec((1, T, NSA_WIDTH), lambda b, i: (b, i, 0)),
                  pl.BlockSpec((1, 2 * NSA_GROUPS, NC, HEAD_DIM), lambda b, i: (b, 0, 0, 0)),
                  pl.BlockSpec(overlap.shape, lambda b, i: (0, 0, 0)),
                  pl.BlockSpec(cmp_bias.shape, lambda b, i: (0, 0, 0)),
                  pl.BlockSpec((1, HEAD_DIM), lambda b, i: (0, 0))],
        out_specs=(pl.BlockSpec((1, T, NSA_WIDTH), lambda b, i: (b, i, 0)),
                   pl.BlockSpec((1, T, LANES), lambda b, i: (b, i, 0)),
                   pl.BlockSpec((1, T, NSA_WIDTH), lambda b, i: (b, i, 0))),
        compiler_params=_cparams(("arbitrary", "arbitrary")),
        name="nsa_cmp_attn",
    )(y, kcvc, overlap, cmp_bias, q_g.reshape(1, HEAD_DIM))


def _kv_prep_kernel(ks_ref, kw_ref, kg1_ref, kg2_ref, ksn_ref, vs_ref, kwn_ref, vw_ref):
    KV = NSA_KV
    rows = ks_ref.shape[1]
    ones_col = (lax.broadcasted_iota(jnp.int32, (rows, LANES - HEAD_DIM), 1) == 0).astype(BF16)
    for g in range(NSA_GROUPS):
        cols = slice(g * HEAD_DIM, (g + 1) * HEAD_DIM)
        vcols = slice(KV + g * HEAD_DIM, KV + (g + 1) * HEAD_DIM)
        ksn_ref[0, :, cols] = _rms(ks_ref[0, :, cols], kg1_ref[...]).astype(BF16)
        kwn_ref[0, :, cols] = _rms(kw_ref[0, :, cols], kg2_ref[...]).astype(BF16)
        vs_ref[0, :, g * LANES:g * LANES + HEAD_DIM] = ks_ref[0, :, vcols].astype(BF16)
        vw_ref[0, :, g * LANES:g * LANES + HEAD_DIM] = kw_ref[0, :, vcols].astype(BF16)
        vs_ref[0, :, g * LANES + HEAD_DIM:(g + 1) * LANES] = ones_col
        vw_ref[0, :, g * LANES + HEAD_DIM:(g + 1) * LANES] = ones_col


def _nsa_kv_prep(y, k_g):
    B, S, _ = y.shape
    ksblk = (NSA_WIDTH + 2 * NSA_KV) // (2 * NSA_KV)
    kout = jax.ShapeDtypeStruct((B, S, NSA_KV), BF16)
    vout = jax.ShapeDtypeStruct((B, S, NSA_GROUPS * LANES), BF16)
    kspec = pl.BlockSpec((1, ROW_TILE, NSA_KV), lambda b, i: (b, i, 0))
    vspec = pl.BlockSpec((1, ROW_TILE, NSA_GROUPS * LANES), lambda b, i: (b, i, 0))
    return pl.pallas_call(
        _kv_prep_kernel,
        out_shape=(kout, vout, kout, vout),
        grid=(B, S // ROW_TILE),
        in_specs=[pl.BlockSpec((1, ROW_TILE, 2 * NSA_KV), lambda b, i: (b, i, ksblk)),
                  pl.BlockSpec((1, ROW_TILE, 2 * NSA_KV), lambda b, i: (b, i, ksblk + 1)),
                  pl.BlockSpec((1, HEAD_DIM), lambda b, i: (0, 0)),
                  pl.BlockSpec((1, HEAD_DIM), lambda b, i: (0, 0))],
        out_specs=(kspec, vspec, kspec, vspec),
        compiler_params=_cparams(("arbitrary", "arbitrary")),
        name="nsa_kv_prep",
    )(y, y, k_g[1].reshape(1, HEAD_DIM), k_g[2].reshape(1, HEAD_DIM))


SEL_KEY_PARTS = 4


def _sel_win_kernel(qn_ref, ks_ref, vs_ref, kw_ref, vw_ref, gl_ref, sel_ref, oc_ref, bias_ref, e_ref, bg_ref,
                    o_ref):
    qi = pl.program_id(1)
    T = qn_ref.shape[1]
    S = ks_ref.shape[1]
    nk = S // T
    G, J = NSA_GROUPS, NSA_HPG
    NS = e_ref.shape[0]
    wt = WINDOW // T
    nwin = wt + 1
    assert nk % SEL_KEY_PARTS == 0 and nk >= nwin

    def sel_tile(r):
        return jnp.where(r < 0, 4, jnp.minimum(r, 2))

    def win_tile(r):
        return jnp.where(r < 0, 4, jnp.where(r <= 1, r, jnp.where(r < wt, 2, jnp.where(r == wt, 3, 4))))

    def softmax_pv(s, v):
        p = jnp.exp(s - jnp.max(s, axis=-1, keepdims=True))
        acc = _dot(p.astype(BF16), v)
        return acc[:, :HEAD_DIM] / acc[:, HEAD_DIM:HEAD_DIM + 1]

    gates = jax.nn.sigmoid(gl_ref[0] + bg_ref[...])
    k0 = jnp.maximum(qi - wt, 0)
    wrows = pl.ds(pl.multiple_of(k0 * T, T), nwin * T)

    def attend(nkt):
        nkeys = nkt * T
        for g in range(G):
            gcols = slice(g * HEAD_DIM, (g + 1) * HEAD_DIM)
            vcols = slice(g * LANES, (g + 1) * LANES)
            q4 = jnp.concatenate(
                [qn_ref[0, :, (g * J + j) * HEAD_DIM:(g * J + j + 1) * HEAD_DIM] for j in range(J)], axis=0)
            selg = sel_ref[0, :, g * NS:(g + 1) * NS].astype(BF16)
            mask = (_dot(selg, e_ref[:, :nkeys]) - 1.0) * (-NEG_INF)
            bias = jnp.concatenate(
                [jnp.concatenate([bias_ref[g * J + j, sel_tile(qi - kt)] for kt in range(nkt)], axis=1) + mask
                 for j in range(J)], axis=0)
            o_s = softmax_pv(_dot_nt(q4, ks_ref[0, :nkeys, gcols]) + bias, vs_ref[0, :nkeys, vcols])
            wbias = jnp.concatenate(
                [jnp.concatenate([bias_ref[g * J + j, win_tile(qi - k0 - t)] for t in range(nwin)], axis=1)
                 for j in range(J)], axis=0)
            o_w = softmax_pv(_dot_nt(q4, kw_ref[0, wrows, gcols]) + wbias, vw_ref[0, wrows, vcols])
            for j in range(J):
                h = g * J + j
                cols = slice(h * HEAD_DIM, (h + 1) * HEAD_DIM)
                rows = slice(j * T, (j + 1) * T)
                o_ref[0, :, cols] = (gates[:, 3 * h:3 * h + 1] * oc_ref[0, :, cols]
                                     + gates[:, 3 * h + 1:3 * h + 2] * o_s[rows]
                                     + gates[:, 3 * h + 2:3 * h + 3] * o_w[rows])

    per = nk // SEL_KEY_PARTS
    for part in range(SEL_KEY_PARTS):
        pl.when((qi >= part * per) & (qi < (part + 1) * per))(functools.partial(attend, (part + 1) * per))


def _nsa_sel_win(qn, y, kv, sel, oc, bias_tiles, expand, b_gate):
    B, S, _ = y.shape
    T = ATT_TILE
    glblk = (NSA_WIDTH + 6 * NSA_KV) // LANES
    bg = jnp.zeros((1, LANES), F32).at[0, :3 * NSA_HEADS].set(b_gate)
    kspec = pl.BlockSpec((1, S, NSA_KV), lambda b, i: (b, 0, 0))
    vspec = pl.BlockSpec((1, S, NSA_GROUPS * LANES), lambda b, i: (b, 0, 0))
    return pl.pallas_call(
        _sel_win_kernel,
        out_shape=jax.ShapeDtypeStruct((B, S, NSA_WIDTH), F32),
        grid=(B, S // T),
        in_specs=[pl.BlockSpec((1, T, NSA_WIDTH), lambda b, i: (b, i, 0)),
                  kspec, vspec, kspec, vspec,
                  pl.BlockSpec((1, T, LANES), lambda b, i: (b, i, glblk)),
                  pl.BlockSpec((1, T, sel.shape[2]), lambda b, i: (b, i, 0)),
                  pl.BlockSpec((1, T, NSA_WIDTH), lambda b, i: (b, i, 0)),
                  pl.BlockSpec(bias_tiles.shape, lambda b, i: (0, 0, 0, 0)),
                  pl.BlockSpec(expand.shape, lambda b, i: (0, 0)),
                  pl.BlockSpec((1, LANES), lambda b, i: (0, 0))],
        out_specs=pl.BlockSpec((1, T, NSA_WIDTH), lambda b, i: (b, i, 0)),
        compiler_params=_cparams(("arbitrary", "arbitrary")),
        name="nsa_sel_win",
    )(qn, *kv, y, sel, oc, bias_tiles, expand, bg)


def _odd_mixer(x, sh, sc, gate, ng, w_in, b_gate, q_g, k_g, cmp_pos, cmp_w1, cmp_w2, bias_tiles, cmp_bias,
               w_out):
    B, S, _ = x.shape
    G = NSA_GROUPS
    w = jnp.zeros((D_MODEL, ODD_IN_PAD), BF16).at[:, :ODD_IN].set(w_in.astype(BF16))
    y = _norm_proj(x, sh, sc, ng, w)
    xr = y[:, :, NSA_WIDTH:NSA_WIDTH + 2 * NSA_KV].reshape(B, S, 2, G, HEAD_DIM)
    xr = xr.transpose(0, 2, 3, 1, 4).reshape(B, 2 * G, S // CMP_STRIDE, CMP_STRIDE * HEAD_DIM)
    kcvc = _nsa_compress(xr, cmp_pos.reshape(2, CMP_LEN * HEAD_DIM), cmp_w1, cmp_w2, k_g[0])
    n_slc = S // SLC_LEN
    starts = jnp.arange(S // CMP_STRIDE) * CMP_STRIDE
    bstart = jnp.arange(n_slc) * SLC_LEN
    real = (jnp.arange(S // CMP_STRIDE) < (S - CMP_LEN) // CMP_STRIDE + 1)[:, None]
    overlap = (real & (starts[:, None] < bstart[None, :] + SLC_LEN)
               & (starts[:, None] + CMP_LEN > bstart[None, :])).astype(F32)
    overlap = jnp.stack([jnp.pad(overlap, ((0, 0), (g * n_slc, (G - 1 - g) * n_slc))) for g in range(G)])
    expand = (jnp.arange(S)[None, :] // SLC_LEN == jnp.arange(n_slc)[:, None]).astype(BF16)
    oc, sel, qn = _nsa_cmp_attn(y, kcvc, overlap, cmp_bias, q_g)
    o = _nsa_sel_win(qn, y, _nsa_kv_prep(y, k_g), sel, oc, bias_tiles, expand, b_gate)
    return _out_proj(x, o, 0, o, 1, w_out.astype(BF16), gate)


def _even_mixer(x, sh, sc, gate, ng, w_in, b_f, conv_w, q_g, k_g, w_out):
    w = jnp.zeros((D_MODEL, EVEN_IN_PAD), BF16).at[:, :EVEN_IN].set(w_in.astype(BF16))
    y = _norm_proj(x, sh, sc, ng, w)
    conv = _short_conv(y, conv_w)
    attn = _fox_attention(y, b_f, q_g, k_g)
    return _out_proj(x, conv, 0, attn, 0, w_out.astype(BF16), gate)


def kernel(x, c, ada_w, ada_b, norm_g, even_w_in, even_b_f, even_conv_w, even_q_g, even_k_g, even_w_out, odd_w_in, odd_b_gate, odd_q_g, odd_k_g, odd_cmp_pos, odd_cmp_w1, odd_cmp_w2, odd_w_out, rel_table, peer_w_q, peer_keys, peer_u, peer_v):
    B = x.shape[0]
    mods = _ada(c, ada_w, ada_b)
    bias_tiles, cmp_bias = _relbias(rel_table)
    for layer in range(DEPTH):
        sh1, sc1, g1, sh2, sc2, g2 = [m.reshape(B, 1, D_MODEL) for m in jnp.split(mods[layer], 6, axis=-1)]
        i = layer // 2
        ng1 = norm_g[layer, 0].reshape(1, D_MODEL)
        ng2 = norm_g[layer, 1].reshape(1, D_MODEL)
        if layer % 2 == 0:
            x = _even_mixer(x, sh1, sc1, g1, ng1, even_w_in[i], even_b_f[i], even_conv_w[i], even_q_g[i],
                            even_k_g[i], even_w_out[i])
        else:
            x = _odd_mixer(x, sh1, sc1, g1, ng1, odd_w_in[i], odd_b_gate[i], odd_q_g[i], odd_k_g[i],
                           odd_cmp_pos[i], odd_cmp_w1[i], odd_cmp_w2[i], bias_tiles, cmp_bias, odd_w_out[i])
        x = _peer(x, sh2, sc2, g2, ng2, peer_w_q[layer], peer_keys[layer],
                  peer_u[layer].astype(BF16), peer_v[layer].T.astype(BF16))
    return x
```

```python
import functools
import math

import jax
import jax.numpy as jnp
from jax import lax
from jax.experimental import pallas as pl
from jax.experimental.pallas import tpu as pltpu

D_MODEL = 1024
DEPTH = 4
HEAD_DIM = 64
CONV_CH = 512
CONV_K = 3
FOX_HEADS = 8
FOX_WIDTH = FOX_HEADS * HEAD_DIM
EVEN_IN = 3 * CONV_CH + 3 * FOX_WIDTH + FOX_HEADS
NSA_GROUPS = 4
NSA_HPG = 4
NSA_HEADS = NSA_GROUPS * NSA_HPG
NSA_WIDTH = NSA_HEADS * HEAD_DIM
NSA_KV = NSA_GROUPS * HEAD_DIM
CMP_LEN = 32
CMP_STRIDE = 16
SLC_LEN = 64
SLC_TOPN = 16
WINDOW = 512
ODD_IN = NSA_WIDTH + 6 * NSA_KV + 3 * NSA_HEADS
REL_BUCKETS = 32
REL_MAX_DIST = 128
PEER_HEADS = 8
PEER_NKEYS = 128
PEER_EXPERTS = PEER_NKEYS * PEER_NKEYS
PEER_QDIM = 128
PEER_TOPK = 16
EPS = 1e-6
NEG_INF = -1e30
FORCE_SCORE = 1e6

LANES = 128
F32 = jnp.float32
BF16 = jnp.bfloat16
HIGHEST = lax.Precision.HIGHEST
VMEM_LIMIT = 56 * 1024 * 1024

EVEN_IN_PAD = -(-EVEN_IN // LANES) * LANES
ODD_IN_PAD = -(-ODD_IN // LANES) * LANES
ROW_TILE = 512
ATT_TILE = 128
FOX_TILE = 256
PEER_TN = 512
PEER_EC = 2048
SCALE = HEAD_DIM ** -0.5


def _cparams(sem):
    return pltpu.CompilerParams(dimension_semantics=sem, vmem_limit_bytes=VMEM_LIMIT)


def _dot(a, b, precision=None):
    return jnp.dot(a, b, precision=precision, preferred_element_type=F32)


def _dot_nt(a, b, precision=None):
    return lax.dot_general(a, b, (((1,), (1,)), ((), ())), precision=precision,
                           preferred_element_type=F32)


def _rms(x, g):
    return x * lax.rsqrt(jnp.mean(x * x, axis=-1, keepdims=True) + EPS) * g


def _ada_kernel(c_ref, w_ref, b_ref, o_ref):
    c = c_ref[...]
    sc = c * jax.nn.sigmoid(c)
    o_ref[0] = _dot(sc, w_ref[0], HIGHEST) + b_ref[0]


def _ada(c, ada_w, ada_b):
    B, D = c.shape
    E = ada_w.shape[2]
    tn = 2048
    return pl.pallas_call(
        _ada_kernel,
        out_shape=jax.ShapeDtypeStruct((DEPTH, B, E), F32),
        grid=(DEPTH, E // tn),
        in_specs=[pl.BlockSpec((B, D), lambda l, j: (0, 0)),
                  pl.BlockSpec((1, D, tn), lambda l, j: (l, 0, j)),
                  pl.BlockSpec((1, 1, tn), lambda l, j: (l, 0, j))],
        out_specs=pl.BlockSpec((1, B, tn), lambda l, j: (l, 0, j)),
        compiler_params=_cparams(("arbitrary", "arbitrary")),
        name="ada_mod",
    )(c, ada_w, ada_b.reshape(DEPTH, 1, E))


def _norm_proj_kernel(x_ref, sh_ref, sc_ref, ng_ref, w_ref, y_ref):
    h = _rms(x_ref[0], ng_ref[...]) * (1.0 + sc_ref[0]) + sh_ref[0]
    y_ref[0] = _dot(h.astype(BF16), w_ref[...])


def _norm_proj(x, sh, sc, ng, w):
    B, S, D = x.shape
    N = w.shape[1]
    return pl.pallas_call(
        _norm_proj_kernel,
        out_shape=jax.ShapeDtypeStruct((B, S, N), F32),
        grid=(B, S // ROW_TILE),
        in_specs=[pl.BlockSpec((1, ROW_TILE, D), lambda b, i: (b, i, 0)),
                  pl.BlockSpec((1, 1, D), lambda b, i: (b, 0, 0)),
                  pl.BlockSpec((1, 1, D), lambda b, i: (b, 0, 0)),
                  pl.BlockSpec((1, D), lambda b, i: (0, 0)),
                  pl.BlockSpec((D, N), lambda b, i: (0, 0))],
        out_specs=pl.BlockSpec((1, ROW_TILE, N), lambda b, i: (b, i, 0)),
        compiler_params=_cparams(("arbitrary", "arbitrary")),
        name="norm_proj",
    )(x, sh, sc, ng, w)


def _out_proj_kernel(x_ref, a_ref, b_ref, wa_ref, wb_ref, g_ref, o_ref):
    y = _dot(a_ref[0].astype(BF16), wa_ref[...]) + _dot(b_ref[0].astype(BF16), wb_ref[...])
    o_ref[0] = x_ref[0] + g_ref[0] * y


def _out_proj(x, a, a_blk, b, b_blk, w, gate):
    B, S, D = x.shape
    half = w.shape[0] // 2
    wa, wb = w[:half], w[half:]
    return pl.pallas_call(
        _out_proj_kernel,
        out_shape=jax.ShapeDtypeStruct((B, S, D), F32),
        grid=(B, S // ROW_TILE),
        in_specs=[pl.BlockSpec((1, ROW_TILE, D), lambda b, i: (b, i, 0)),
                  pl.BlockSpec((1, ROW_TILE, half), lambda b, i: (b, i, a_blk)),
                  pl.BlockSpec((1, ROW_TILE, half), lambda b, i: (b, i, b_blk)),
                  pl.BlockSpec((half, D), lambda b, i: (0, 0)),
                  pl.BlockSpec((half, D), lambda b, i: (0, 0)),
                  pl.BlockSpec((1, 1, D), lambda b, i: (b, 0, 0))],
        out_specs=pl.BlockSpec((1, ROW_TILE, D), lambda b, i: (b, i, 0)),
        compiler_params=_cparams(("arbitrary", "arbitrary")),
        name="out_proj",
    )(x, a, b, wa, wb, gate)


def _conv_kernel(cb_ref, cc_ref, ch_ref, w_ref, o_ref):
    u = cc_ref[0] * ch_ref[0]
    row = lax.broadcasted_iota(jnp.int32, u.shape, 0)
    u1 = jnp.where(row >= 1, pltpu.roll(u, 1, axis=0), 0.0)
    u2 = jnp.where(row >= 2, pltpu.roll(u, 2, axis=0), 0.0)
    w = w_ref[...]
    o_ref[0] = cb_ref[0] * (w[0:1] * u2 + w[1:2] * u1 + w[2:3] * u)


def _short_conv(y, conv_w):
    B, S, _ = y.shape
    nblk = CONV_CH // LANES
    w = jnp.zeros((8, CONV_CH), F32).at[:CONV_K].set(conv_w)
    return pl.pallas_call(
        _conv_kernel,
        out_shape=jax.ShapeDtypeStruct((B, S, CONV_CH), F32),
        grid=(B, nblk),
        in_specs=[pl.BlockSpec((1, S, LANES), lambda b, j: (b, 0, j)),
                  pl.BlockSpec((1, S, LANES), lambda b, j: (b, 0, nblk + j)),
                  pl.BlockSpec((1, S, LANES), lambda b, j: (b, 0, 2 * nblk + j)),
                  pl.BlockSpec((8, LANES), lambda b, j: (0, j))],
        out_specs=pl.BlockSpec((1, S, LANES), lambda b, j: (b, 0, j)),
        compiler_params=_cparams(("arbitrary", "arbitrary")),
        name="short_conv",
    )(y, y, y, w)


def _cumsum_rows(x):
    n = x.shape[0]
    row = lax.broadcasted_iota(jnp.int32, x.shape, 0)
    sh = 1
    while sh < n:
        x = x + jnp.where(row >= sh, pltpu.roll(x, sh, axis=0), 0.0)
        sh *= 2
    return x


FOX_KEY_PARTS = 4


def _fox_kernel(q_ref, k_ref, v_ref, f_ref, bf_ref, qg_ref, kg_ref, o_ref,
                kn_scr, vb_scr, cum_scr, cumrow_scr):
    qi = pl.program_id(1)
    S = k_ref.shape[1]
    T = q_ref.shape[1]
    nq = S // T

    @pl.when(qi == 0)
    def _prep():
        ones_col = (lax.broadcasted_iota(jnp.int32, (S, LANES - HEAD_DIM), 1) == 0).astype(BF16)
        for h in range(FOX_HEADS):
            cols = slice(h * HEAD_DIM, (h + 1) * HEAD_DIM)
            kn_scr[:, cols] = _rms(k_ref[0, :, cols], kg_ref[...]).astype(BF16)
            vb_scr[:, h * LANES:h * LANES + HEAD_DIM] = v_ref[0, :, cols].astype(BF16)
            vb_scr[:, h * LANES + HEAD_DIM:(h + 1) * LANES] = ones_col
        logf = jax.nn.log_sigmoid(f_ref[0] + bf_ref[...])
        cum = _cumsum_rows(logf)
        cum_scr[...] = cum
        for kt in range(S // T):
            cumrow_scr[:, kt * T:(kt + 1) * T] = cum[kt * T:(kt + 1) * T, :].T[0:8, :]

    cq = cum_scr[pl.ds(pl.multiple_of(qi * T, T), T), :]

    def attend(nkeys):
        row = qi * T + lax.broadcasted_iota(jnp.int32, (T, nkeys), 0)
        causal = jnp.where(lax.broadcasted_iota(jnp.int32, (T, nkeys), 1) <= row, 0.0, NEG_INF)
        for h in range(FOX_HEADS):
            cols = slice(h * HEAD_DIM, (h + 1) * HEAD_DIM)
            qh = (_rms(q_ref[0, :, cols], qg_ref[...]) * SCALE).astype(BF16)
            s = _dot_nt(qh, kn_scr[0:nkeys, cols]) + ((cq[:, h:h + 1] - cumrow_scr[h:h + 1, 0:nkeys]) + causal)
            p = jnp.exp(s - jnp.max(s, axis=-1, keepdims=True))
            acc = _dot(p.astype(BF16), vb_scr[0:nkeys, h * LANES:(h + 1) * LANES])
            o_ref[0, :, cols] = acc[:, :HEAD_DIM] / acc[:, HEAD_DIM:HEAD_DIM + 1]

    parts = min(FOX_KEY_PARTS, nq)
    per = nq // parts
    for part in range(parts):
        pl.when((qi >= part * per) & (qi < (part + 1) * per))(functools.partial(attend, (part + 1) * (S // parts)))


def _fox_attention(y, b_f, q_g, k_g):
    B, S, _ = y.shape
    T = FOX_TILE
    qblk = 3 * CONV_CH // FOX_WIDTH
    fblk = (3 * CONV_CH + 3 * FOX_WIDTH) // LANES
    bf = jnp.zeros((1, LANES), F32).at[0, :FOX_HEADS].set(b_f)
    return pl.pallas_call(
        _fox_kernel,
        out_shape=jax.ShapeDtypeStruct((B, S, FOX_WIDTH), F32),
        grid=(B, S // T),
        in_specs=[pl.BlockSpec((1, T, FOX_WIDTH), lambda b, i: (b, i, qblk)),
                  pl.BlockSpec((1, S, FOX_WIDTH), lambda b, i: (b, 0, qblk + 1)),
                  pl.BlockSpec((1, S, FOX_WIDTH), lambda b, i: (b, 0, qblk + 2)),
                  pl.BlockSpec((1, S, LANES), lambda b, i: (b, 0, fblk)),
                  pl.BlockSpec((1, LANES), lambda b, i: (0, 0)),
                  pl.BlockSpec((1, HEAD_DIM), lambda b, i: (0, 0)),
                  pl.BlockSpec((1, HEAD_DIM), lambda b, i: (0, 0))],
        out_specs=pl.BlockSpec((1, T, FOX_WIDTH), lambda b, i: (b, i, 0)),
        scratch_shapes=[pltpu.VMEM((S, FOX_WIDTH), BF16),
                        pltpu.VMEM((S, FOX_HEADS * LANES), BF16),
                        pltpu.VMEM((S, LANES), F32),
                        pltpu.VMEM((8, S), F32)],
        compiler_params=_cparams(("arbitrary", "arbitrary")),
        name="fox_attention",
    )(y, y, y, y, bf, q_g.reshape(1, HEAD_DIM), k_g.reshape(1, HEAD_DIM))


def _gelu_tanh(x):
    c0 = math.sqrt(2.0 / math.pi)
    return (0.5 * x) * (1.0 + jnp.tanh(x * (c0 + (c0 * 0.044715) * (x * x))))


def _extract_top(work, iota, k, exact):
    nrows = work.shape[0]
    vals = []
    for _ in range(k):
        m = jnp.max(work, axis=0, keepdims=True)
        if exact:
            idx = jnp.min(jnp.where(work == m, iota, nrows), axis=0, keepdims=True)
            work = jnp.where(iota == idx, -jnp.inf, work)
        else:
            work = jnp.where(work == m, -jnp.inf, work)
        vals.append(m)
    return vals, work


def _removed(work):
    return jnp.sum(jnp.where(work == -jnp.inf, 1.0, 0.0), axis=0, keepdims=True)


def _peer_kernel(x_ref, sh_ref, sc_ref, gt_ref, ng_ref, wqh_ref, wql_ref, keys_ref, u_ref, vt_ref, o_ref,
                 ht_scr, qt_scr, sp_scr, t0_scr, t1_scr, tau_scr, a_scr, p_scr, acc_scr):
    c = pl.program_id(2)
    nc = pl.num_programs(2)
    tn = x_ref.shape[1]
    nlt = tn // LANES
    half = PEER_QDIM // 2

    @pl.when(c == 0)
    def _prep():
        h = _rms(x_ref[0], ng_ref[...]) * (1.0 + sc_ref[0]) + sh_ref[0]
        hh = h.astype(BF16)
        hl = (h - hh.astype(F32)).astype(BF16)
        cw = 2 * LANES
        for k in range(h.shape[1] // cw):
            cs = slice(k * cw, (k + 1) * cw)
            q = _dot(hh, wqh_ref[:, cs]) + (_dot(hh, wql_ref[:, cs]) + _dot(hl, wqh_ref[:, cs]))
            qt_scr[cs, :] = q.T
            ht_scr[cs, :] = h[:, cs].T.astype(BF16)
        acc_scr[...] = jnp.zeros_like(acc_scr)
        iota_k = lax.broadcasted_iota(jnp.int32, (PEER_NKEYS, LANES), 0)
        iota_c = lax.broadcasted_iota(jnp.int32, (PEER_TOPK + 8 * 8, LANES), 0)

        n_pad = sum(8 - PEER_TOPK // (r0 + 1) for r0 in range(2, 8))

        def select(hd, lt, exact):
            lanes = slice(lt * LANES, (lt + 1) * LANES)
            s0 = sp_scr[0, :, lanes]
            s1 = sp_scr[1, :, lanes]
            sv0, w0 = _extract_top(s0, iota_k, PEER_TOPK, exact)
            sv1, w1 = _extract_top(s1, iota_k, PEER_TOPK, exact)
            s0m = jnp.where(w0 == -jnp.inf, s0, -jnp.inf)
            s1m = jnp.where(w1 == -jnp.inf, s1, -jnp.inf)
            sv0_all = jnp.concatenate(sv0, axis=0)
            sv1_all = jnp.concatenate(sv1, axis=0)
            parts = [sv0[0] + sv1_all, sv0[1] + sv1_all[:8]]
            for r0 in range(2, 8):
                parts.append(jnp.where(iota_c[:8] < PEER_TOPK // (r0 + 1), sv0[r0] + sv1_all[:8], -jnp.inf))
            parts.append(sv0_all[8:] + sv1[0])
            cv, wc = _extract_top(jnp.concatenate(parts, axis=0), iota_c, PEER_TOPK, exact)
            z = cv[0] - cv[0]
            for r in range(PEER_TOPK):
                z = z + jnp.exp(cv[r] - cv[0])
            nrt = PEER_NKEYS // 8
            t0_scr[hd, lt, :, 0] = s0m.reshape(nrt, 8, LANES)
            t0_scr[hd, lt, :, 1] = jnp.exp(s0m - sv0[0]).reshape(nrt, 8, LANES)
            t1_scr[hd, lt, :, 0] = s1m.reshape(nrt, 8, LANES)
            t1_scr[hd, lt, :, 1] = (jnp.exp(s1m - sv1[0]) / z).reshape(nrt, 8, LANES)
            tau_scr[hd, lt] = jnp.broadcast_to(cv[PEER_TOPK - 1], (8, LANES))
            return ((_removed(w0) != PEER_TOPK) | (_removed(w1) != PEER_TOPK)
                    | (_removed(wc) != PEER_TOPK + n_pad))

        def head_body(hd, carry):
            for p in range(2):
                qs = qt_scr[pl.ds(pl.multiple_of((hd * 2 + p) * half, half), half), :]
                sp_scr[p] = _dot(keys_ref[p], qs, HIGHEST)
            tied = jnp.zeros((1, LANES), F32)
            for lt in range(nlt):
                tied = jnp.maximum(tied, jnp.where(select(hd, lt, False), 1.0, 0.0))

            @pl.when(jnp.max(tied) > 0.0)
            def _redo():
                for lt in range(nlt):
                    select(hd, lt, True)
            return carry

        lax.fori_loop(0, PEER_HEADS, head_body, 0)

    nsub = 8
    nrq = 32
    nparts = PEER_EC // PEER_NKEYS // nsub
    for part in range(nparts):
        er = slice(part * nsub * PEER_NKEYS, (part + 1) * nsub * PEER_NKEYS)
        for lh in range(nlt // 2):
            lanes2 = slice(lh * 2 * LANES, (lh + 1) * 2 * LANES)
            a_scr[er, lanes2] = _gelu_tanh(_dot(u_ref[er, :], ht_scr[:, lanes2]))

    def weights_tile(part, lt, rq, prev):
        lanes = slice(lt * LANES, (lt + 1) * LANES)
        jt = slice(rq * nrq // 8, (rq + 1) * nrq // 8)
        blk = c * (PEER_EC // PEER_NKEYS // 8) + part * nsub // 8
        zero = 0.0 if prev is None else jnp.where(prev != prev, prev, 0.0)
        w = [jnp.zeros((nrq, LANES), F32) for _ in range(nsub)]
        for hd in range(PEER_HEADS):
            s1 = t1_scr[hd, lt, jt, 0].reshape(nrq, LANES)
            e1 = t1_scr[hd, lt, jt, 1].reshape(nrq, LANES)
            s0 = t0_scr[hd, lt, blk, 0] + zero
            e0 = t0_scr[hd, lt, blk, 1] + zero
            tau = tau_scr[hd, lt, 0:1, :]
            for ii in range(nsub):
                keep = s1 + s0[ii:ii + 1] >= tau
                w[ii] = w[ii] + jnp.where(keep, e1 * e0[ii:ii + 1], 0.0)
        for ii in range(nsub):
            rows = slice((part * nsub + ii) * PEER_NKEYS + rq * nrq, (part * nsub + ii) * PEER_NKEYS + (rq + 1) * nrq)
            w[ii] = w[ii] * a_scr[rows, lanes]
            p_scr[rows, lanes] = w[ii].astype(BF16)
        return w[nsub - 1][0:1, :]

    prev = None
    for part in range(nparts):
        for lt in range(nlt):
            for rq in range(PEER_NKEYS // nrq):
                prev = weights_tile(part, lt, rq, prev)
    acc_scr[...] += _dot(vt_ref[...], p_scr[...])

    @pl.when(c == nc - 1)
    def _fin():
        o_ref[0] = x_ref[0] + gt_ref[0] * acc_scr[...].T


def _peer(x, sh, sc, gate, ng, w_q, keys, u_b, vt_b):
    B, S, D = x.shape
    tn = PEER_TN
    nc = PEER_EXPERTS // PEER_EC
    tab = pltpu.VMEM((PEER_HEADS, tn // LANES, PEER_NKEYS // 8, 2, 8, LANES), F32)
    wq_hi = w_q.astype(BF16)
    wq_lo = (w_q - wq_hi.astype(F32)).astype(BF16)
    wspec = pl.BlockSpec((D, PEER_HEADS * PEER_QDIM), lambda b, i, c: (0, 0), pipeline_mode=pl.Buffered(1))
    return pl.pallas_call(
        _peer_kernel,
        out_shape=jax.ShapeDtypeStruct((B, S, D), F32),
        grid=(B, S // tn, nc),
        in_specs=[pl.BlockSpec((1, tn, D), lambda b, i, c: (b, i, 0), pipeline_mode=pl.Buffered(1)),
                  pl.BlockSpec((1, 1, D), lambda b, i, c: (b, 0, 0)),
                  pl.BlockSpec((1, 1, D), lambda b, i, c: (b, 0, 0)),
                  pl.BlockSpec((1, 1, D), lambda b, i, c: (b, 0, 0)),
                  pl.BlockSpec((1, D), lambda b, i, c: (0, 0)),
                  wspec, wspec,
                  pl.BlockSpec((2, PEER_NKEYS, PEER_QDIM // 2), lambda b, i, c: (0, 0, 0)),
                  pl.BlockSpec((PEER_EC, D), lambda b, i, c: (c, 0)),
                  pl.BlockSpec((D, PEER_EC), lambda b, i, c: (0, c))],
        out_specs=pl.BlockSpec((1, tn, D), lambda b, i, c: (b, i, 0)),
        scratch_shapes=[pltpu.VMEM((D, tn), BF16),
                        pltpu.VMEM((PEER_HEADS * PEER_QDIM, tn), F32),
                        pltpu.VMEM((2, PEER_NKEYS, tn), F32),
                        tab, tab,
                        pltpu.VMEM((PEER_HEADS, tn // LANES, 8, LANES), F32),
                        pltpu.VMEM((PEER_EC, tn), F32),
                        pltpu.VMEM((PEER_EC, tn), BF16),
                        pltpu.VMEM((D, tn), F32)],
        compiler_params=_cparams(("arbitrary", "arbitrary", "arbitrary")),
        name="peer",
    )(x, sh, sc, gate, ng, wq_hi, wq_lo, keys, u_b, vt_b)


def _rel_bucket(dist):
    n = jnp.maximum(dist, 0)
    max_exact = REL_BUCKETS // 2
    nf = jnp.maximum(n, 1).astype(F32)
    large = max_exact + (jnp.log(nf / max_exact) / math.log(REL_MAX_DIST / max_exact)
                         * (REL_BUCKETS - max_exact)).astype(jnp.int32)
    large = jnp.minimum(large, REL_BUCKETS - 1)
    return jnp.where(n < max_exact, n, large)


def _bias_lookup(bucket, tab_ref, h):
    bias = jnp.zeros(bucket.shape, F32)
    for b in range(REL_BUCKETS):
        bias = jnp.where(bucket == b, tab_ref[b, h], bias)
    return bias


N_BIAS_TILES = 5


def _relbias_kernel(tab_ref, o_ref, c_ref):
    h = pl.program_id(0)
    T = o_ref.shape[2]
    a = lax.broadcasted_iota(jnp.int32, (T, T), 0)
    b = lax.broadcasted_iota(jnp.int32, (T, T), 1)
    last = tab_ref[REL_BUCKETS - 1, h]
    o_ref[0, 0] = jnp.where(a >= b, _bias_lookup(_rel_bucket(a - b), tab_ref, h), NEG_INF)
    o_ref[0, 1] = _bias_lookup(_rel_bucket(a - b + T), tab_ref, h)
    o_ref[0, 2] = jnp.full((T, T), last, F32)
    o_ref[0, 3] = jnp.where(a < b, last, NEG_INF)
    o_ref[0, 4] = jnp.full((T, T), NEG_INF, F32)
    dist = a - ((b - T // 2) * CMP_STRIDE + CMP_LEN - 1)
    c_ref[0] = jnp.where(dist >= 0, _bias_lookup(_rel_bucket(dist), tab_ref, h), last)


def _relbias(rel_table):
    T = ATT_TILE
    assert 2 * T - (T - 1) >= REL_MAX_DIST
    assert WINDOW % T == 0
    return pl.pallas_call(
        _relbias_kernel,
        out_shape=(jax.ShapeDtypeStruct((NSA_HEADS, N_BIAS_TILES, T, T), F32),
                   jax.ShapeDtypeStruct((NSA_HEADS, T, T), F32)),
        grid=(NSA_HEADS,),
        in_specs=[pl.BlockSpec(memory_space=pltpu.SMEM)],
        out_specs=(pl.BlockSpec((1, N_BIAS_TILES, T, T), lambda h: (h, 0, 0, 0)),
                   pl.BlockSpec((1, T, T), lambda h: (h, 0, 0))),
        compiler_params=_cparams(("arbitrary",)),
        name="rel_bias_tiles",
    )(rel_table)


def _cmp_kernel(x_ref, pos_ref, w1_ref, w2_ref, kg_ref, o_ref):
    j = pl.program_id(1)
    half = CMP_STRIDE * HEAD_DIM
    x = x_ref[0, 0]
    pos = pos_ref[0]
    pa = _dot((x + pos[:, :half]).astype(BF16), w1_ref[0, :half, :])
    pb = _dot((x + pos[:, half:]).astype(BF16), w1_ref[0, half:, :])
    pre = pa + pltpu.roll(pb, pb.shape[0] - 1, axis=0)
    out = _dot(jax.nn.gelu(pre, approximate=True).astype(BF16), w2_ref[0])
    o_ref[0, 0] = jnp.where(j < NSA_GROUPS, _rms(out, kg_ref[...]), out)


def _nsa_compress(xr, pos, w1, w2, kg0):
    B, _, M, W = xr.shape
    G = NSA_GROUPS
    return pl.pallas_call(
        _cmp_kernel,
        out_shape=jax.ShapeDtypeStruct((B, 2 * G, M, HEAD_DIM), F32),
        grid=(B, 2 * G),
        in_specs=[pl.BlockSpec((1, 1, M, W), lambda b, j: (b, j, 0, 0)),
                  pl.BlockSpec((1, 1, 2 * W), lambda b, j: (j // G, 0, 0)),
                  pl.BlockSpec((1, 2 * W, HEAD_DIM), lambda b, j: (j // G, 0, 0)),
                  pl.BlockSpec((1, HEAD_DIM, HEAD_DIM), lambda b, j: (j // G, 0, 0)),
                  pl.BlockSpec((1, HEAD_DIM), lambda b, j: (0, 0))],
        out_specs=pl.BlockSpec((1, 1, M, HEAD_DIM), lambda b, j: (b, j, 0, 0)),
        compiler_params=_cparams(("arbitrary", "arbitrary")),
        name="nsa_compress",
    )(xr, pos.reshape(2, 1, 2 * W), w1.astype(BF16), w2.astype(BF16), kg0.reshape(1, HEAD_DIM))


def _cmp_attn_kernel(q_ref, kv_ref, ov_ref, cb_ref, qg_ref, oc_ref, sel_ref, qn_ref):
    qi = pl.program_id(1)
    T = q_ref.shape[1]
    NC = kv_ref.shape[2]
    G, J = NSA_GROUPS, NSA_HPG
    NS = sel_ref.shape[2] // G
    assert T == NC == LANES and T % CMP_STRIDE == 0 and G * NS == LANES
    t = qi * T + lax.broadcasted_iota(jnp.int32, (T, NC), 0)
    lane = lax.broadcasted_iota(jnp.int32, (T, NC), 1)
    vis = t - (lane * CMP_STRIDE + CMP_LEN - 1) >= 0
    visf = vis.astype(F32)
    shift = (qi * (T // CMP_STRIDE) + T // 2) % LANES
    imp = jnp.zeros((T, LANES), F32)
    for g in range(G):
        kc = kv_ref[0, g].astype(BF16)
        vc = kv_ref[0, G + g].astype(BF16)
        psum = jnp.zeros((T, NC), F32)
        for j in range(J):
            h = g * J + j
            cols = slice(h * HEAD_DIM, (h + 1) * HEAD_DIM)
            qh = (_rms(q_ref[0, :, cols], qg_ref[...]) * SCALE).astype(BF16)
            qn_ref[0, :, cols] = qh
            s = _dot_nt(qh, kc) + pltpu.roll(cb_ref[h], shift, axis=1)
            s = jnp.where(vis, s, NEG_INF)
            e = jnp.exp(s - jnp.max(s, axis=-1, keepdims=True))
            p = e / jnp.sum(e, axis=-1, keepdims=True) * visf
            oc_ref[0, :, cols] = _dot(p.astype(BF16), vc)
            psum = psum + p
        imp = imp + _dot(psum, ov_ref[g], HIGHEST)
    mm = lane % NS
    cur = t // SLC_LEN
    forced = (mm == 0) | (mm == cur) | (mm == cur - 1)
    imp = jnp.where(forced, FORCE_SCORE, jnp.where(mm * SLC_LEN <= t, imp, -1.0))
    rank = jnp.zeros((T, LANES), F32)
    for r in range(1, NS):
        wrapped = mm >= NS - r
        other = jnp.where(wrapped, pltpu.roll(imp, NS - r, axis=1), pltpu.roll(imp, LANES - r, axis=1))
        rank = rank + jnp.where(other > imp, 1.0, jnp.where(other == imp, wrapped.astype(F32), 0.0))
    sel_ref[0] = jnp.where(rank < SLC_TOPN, jnp.where(imp >= 0.0, 1.0, 0.0), 0.0)


def _nsa_cmp_attn(y, kcvc, overlap, cmp_bias, q_g):
    B, S, _ = y.shape
    T = ATT_TILE
    NC = kcvc.shape[2]
    return pl.pallas_call(
        _cmp_attn_kernel,
        out_shape=(jax.ShapeDtypeStruct((B, S, NSA_WIDTH), F32),
                   jax.ShapeDtypeStruct((B, S, LANES), F32),
                   jax.ShapeDtypeStruct((B, S, NSA_WIDTH), BF16)),
        grid=(B, S // T),
        in_specs=[pl.BlockSpec((1, T, NSA_WIDTH), lambda b, i: (b, i, 0)),
                  pl.BlockSpec((1, 2 * NSA_GROUPS, NC, HEAD_DIM), lambda b, i: (b, 0, 0, 0)),
                  pl.BlockSpec(overlap.shape, lambda b, i: (0, 0, 0)),
                  pl.BlockSpec(cmp_bias.shape, lambda b, i: (0, 0, 0)),
                  pl.BlockSpec((1, HEAD_DIM), lambda b, i: (0, 0))],
        out_specs=(pl.BlockSpec((1, T, NSA_WIDTH), lambda b, i: (b, i, 0)),
                   pl.BlockSpec((1, T, LANES), lambda b, i: (b, i, 0)),
                   pl.BlockSpec((1, T, NSA_WIDTH), lambda b, i: (b, i, 0))),
        compiler_params=_cparams(("arbitrary", "arbitrary")),
        name="nsa_cmp_attn",
    )(y, kcvc, overlap, cmp_bias, q_g.reshape(1, HEAD_DIM))


def _kv_prep_kernel(ks_ref, kw_ref, kg1_ref, kg2_ref, ksn_ref, vs_ref, kwn_ref, vw_ref):
    KV = NSA_KV
    rows = ks_ref.shape[1]
    ones_col = (lax.broadcasted_iota(jnp.int32, (rows, LANES - HEAD_DIM), 1) == 0).astype(BF16)
    for g in range(NSA_GROUPS):
        cols = slice(g * HEAD_DIM, (g + 1) * HEAD_DIM)
        vcols = slice(KV + g * HEAD_DIM, KV + (g + 1) * HEAD_DIM)
        ksn_ref[0, :, cols] = _rms(ks_ref[0, :, cols], kg1_ref[...]).astype(BF16)
        kwn_ref[0, :, cols] = _rms(kw_ref[0, :, cols], kg2_ref[...]).astype(BF16)
        vs_ref[0, :, g * LANES:g * LANES + HEAD_DIM] = ks_ref[0, :, vcols].astype(BF16)
        vw_ref[0, :, g * LANES:g * LANES + HEAD_DIM] = kw_ref[0, :, vcols].astype(BF16)
        vs_ref[0, :, g * LANES + HEAD_DIM:(g + 1) * LANES] = ones_col
        vw_ref[0, :, g * LANES + HEAD_DIM:(g + 1) * LANES] = ones_col


def _nsa_kv_prep(y, k_g):
    B, S, _ = y.shape
    ksblk = (NSA_WIDTH + 2 * NSA_KV) // (2 * NSA_KV)
    kout = jax.ShapeDtypeStruct((B, S, NSA_KV), BF16)
    vout = jax.ShapeDtypeStruct((B, S, NSA_GROUPS * LANES), BF16)
    kspec = pl.BlockSpec((1, ROW_TILE, NSA_KV), lambda b, i: (b, i, 0))
    vspec = pl.BlockSpec((1, ROW_TILE, NSA_GROUPS * LANES), lambda b, i: (b, i, 0))
    return pl.pallas_call(
        _kv_prep_kernel,
        out_shape=(kout, vout, kout, vout),
        grid=(B, S // ROW_TILE),
        in_specs=[pl.BlockSpec((1, ROW_TILE, 2 * NSA_KV), lambda b, i: (b, i, ksblk)),
                  pl.BlockSpec((1, ROW_TILE, 2 * NSA_KV), lambda b, i: (b, i, ksblk + 1)),
                  pl.BlockSpec((1, HEAD_DIM), lambda b, i: (0, 0)),
                  pl.BlockSpec((1, HEAD_DIM), lambda b, i: (0, 0))],
        out_specs=(kspec, vspec, kspec, vspec),
        compiler_params=_cparams(("arbitrary", "arbitrary")),
        name="nsa_kv_prep",
    )(y, y, k_g[1].reshape(1, HEAD_DIM), k_g[2].reshape(1, HEAD_DIM))


SEL_KEY_PARTS = 4


def _sel_win_kernel(qn_ref, ks_ref, vs_ref, kw_ref, vw_ref, gl_ref, sel_ref, oc_ref, bias_ref, e_ref, bg_ref,
                    o_ref):
    qi = pl.program_id(1)
    T = qn_ref.shape[1]
    S = ks_ref.shape[1]
    nk = S // T
    G, J = NSA_GROUPS, NSA_HPG
    NS = e_ref.shape[0]
    wt = WINDOW // T
    nwin = wt + 1
    assert nk % SEL_KEY_PARTS == 0 and nk >= nwin

    def sel_tile(r):
        return jnp.where(r < 0, 4, jnp.minimum(r, 2))

    def win_tile(r):
        return jnp.where(r < 0, 4, jnp.where(r <= 1, r, jnp.where(r < wt, 2, jnp.where(r == wt, 3, 4))))

    def softmax_pv(s, v):
        p = jnp.exp(s - jnp.max(s, axis=-1, keepdims=True))
        acc = _dot(p.astype(BF16), v)
        return acc[:, :HEAD_DIM] / acc[:, HEAD_DIM:HEAD_DIM + 1]

    gates = jax.nn.sigmoid(gl_ref[0] + bg_ref[...])
    k0 = jnp.maximum(qi - wt, 0)
    wrows = pl.ds(pl.multiple_of(k0 * T, T), nwin * T)

    def attend(nkt):
        nkeys = nkt * T
        for g in range(G):
            gcols = slice(g * HEAD_DIM, (g + 1) * HEAD_DIM)
            vcols = slice(g * LANES, (g + 1) * LANES)
            q4 = jnp.concatenate(
                [qn_ref[0, :, (g * J + j) * HEAD_DIM:(g * J + j + 1) * HEAD_DIM] for j in range(J)], axis=0)
            selg = sel_ref[0, :, g * NS:(g + 1) * NS].astype(BF16)
            mask = (_dot(selg, e_ref[:, :nkeys]) - 1.0) * (-NEG_INF)
            bias = jnp.concatenate(
                [jnp.concatenate([bias_ref[g * J + j, sel_tile(qi - kt)] for kt in range(nkt)], axis=1) + mask
                 for j in range(J)], axis=0)
            o_s = softmax_pv(_dot_nt(q4, ks_ref[0, :nkeys, gcols]) + bias, vs_ref[0, :nkeys, vcols])
            wbias = jnp.concatenate(
                [jnp.concatenate([bias_ref[g * J + j, win_tile(qi - k0 - t)] for t in range(nwin)], axis=1)
                 for j in range(J)], axis=0)
            o_w = softmax_pv(_dot_nt(q4, kw_ref[0, wrows, gcols]) + wbias, vw_ref[0, wrows, vcols])
            for j in range(J):
                h = g * J + j
                cols = slice(h * HEAD_DIM, (h + 1) * HEAD_DIM)
                rows = slice(j * T, (j + 1) * T)
                o_ref[0, :, cols] = (gates[:, 3 * h:3 * h + 1] * oc_ref[0, :, cols]
                                     + gates[:, 3 * h + 1:3 * h + 2] * o_s[rows]
                                     + gates[:, 3 * h + 2:3 * h + 3] * o_w[rows])

    per = nk // SEL_KEY_PARTS
    for part in range(SEL_KEY_PARTS):
        pl.when((qi >= part * per) & (qi < (part + 1) * per))(functools.partial(attend, (part + 1) * per))


def _nsa_sel_win(qn, y, kv, sel, oc, bias_tiles, expand, b_gate):
    B, S, _ = y.shape
    T = ATT_TILE
    glblk = (NSA_WIDTH + 6 * NSA_KV) // LANES
    bg = jnp.zeros((1, LANES), F32).at[0, :3 * NSA_HEADS].set(b_gate)
    kspec = pl.BlockSpec((1, S, NSA_KV), lambda b, i: (b, 0, 0))
    vspec = pl.BlockSpec((1, S, NSA_GROUPS * LANES), lambda b, i: (b, 0, 0))
    return pl.pallas_call(
        _sel_win_kernel,
        out_shape=jax.ShapeDtypeStruct((B, S, NSA_WIDTH), F32),
        grid=(B, S // T),
        in_specs=[pl.BlockSpec((1, T, NSA_WIDTH), lambda b, i: (b, i, 0)),
                  kspec, vspec, kspec, vspec,
                  pl.BlockSpec((1, T, LANES), lambda b, i: (b, i, glblk)),
                  pl.BlockSpec((1, T, sel.shape[2]), lambda b, i: (b, i, 0)),
                  pl.BlockSpec((1, T, NSA_WIDTH), lambda b, i: (b, i, 0)),
                  pl.BlockSpec(bias_tiles.shape, lambda b, i: (0, 0, 0, 0)),
                  pl.BlockSpec(expand.shape, lambda b, i: (0, 0)),
                  pl.BlockSpec((1, LANES), lambda b, i: (0, 0))],
        out_specs=pl.BlockSpec((1, T, NSA_WIDTH), lambda b, i: (b, i, 0)),
        compiler_params=_cparams(("arbitrary", "arbitrary")),
        name="nsa_sel_win",
    )(qn, *kv, y, sel, oc, bias_tiles, expand, bg)


def _odd_mixer(x, sh, sc, gate, ng, w_in, b_gate, q_g, k_g, cmp_pos, cmp_w1, cmp_w2, bias_tiles, cmp_bias,
               w_out):
    B, S, _ = x.shape
    G = NSA_GROUPS
    w = jnp.zeros((D_MODEL, ODD_IN_PAD), BF16).at[:, :ODD_IN].set(w_in.astype(BF16))
    y = _norm_proj(x, sh, sc, ng, w)
    xr = y[:, :, NSA_WIDTH:NSA_WIDTH + 2 * NSA_KV].reshape(B, S, 2, G, HEAD_DIM)
    xr = xr.transpose(0, 2, 3, 1, 4).reshape(B, 2 * G, S // CMP_STRIDE, CMP_STRIDE * HEAD_DIM)
    kcvc = _nsa_compress(xr, cmp_pos.reshape(2, CMP_LEN * HEAD_DIM), cmp_w1, cmp_w2, k_g[0])
    n_slc = S // SLC_LEN
    starts = jnp.arange(S // CMP_STRIDE) * CMP_STRIDE
    bstart = jnp.arange(n_slc) * SLC_LEN
    real = (jnp.arange(S // CMP_STRIDE) < (S - CMP_LEN) // CMP_STRIDE + 1)[:, None]
    overlap = (real & (starts[:, None] < bstart[None, :] + SLC_LEN)
               & (starts[:, None] + CMP_LEN > bstart[None, :])).astype(F32)
    overlap = jnp.stack([jnp.pad(overlap, ((0, 0), (g * n_slc, (G - 1 - g) * n_slc))) for g in range(G)])
    expand = (jnp.arange(S)[None, :] // SLC_LEN == jnp.arange(n_slc)[:, None]).astype(BF16)
    oc, sel, qn = _nsa_cmp_attn(y, kcvc, overlap, cmp_bias, q_g)
    o = _nsa_sel_win(qn, y, _nsa_kv_prep(y, k_g), sel, oc, bias_tiles, expand, b_gate)
    return _out_proj(x, o, 0, o, 1, w_out.astype(BF16), gate)


def _even_mixer(x, sh, sc, gate, ng, w_in, b_f, conv_w, q_g, k_g, w_out):
    w = jnp.zeros((D_MODEL, EVEN_IN_PAD), BF16).at[:, :EVEN_IN].set(w_in.astype(BF16))
    y = _norm_proj(x, sh, sc, ng, w)
    conv = _short_conv(y, conv_w)
    attn = _fox_attention(y, b_f, q_g, k_g)
    return _out_proj(x, conv, 0, attn, 0, w_out.astype(BF16), gate)


def kernel(x, c, ada_w, ada_b, norm_g, even_w_in, even_b_f, even_conv_w, even_q_g, even_k_g, even_w_out, odd_w_in, odd_b_gate, odd_q_g, odd_k_g, odd_cmp_pos, odd_cmp_w1, odd_cmp_w2, odd_w_out, rel_table, peer_w_q, peer_keys, peer_u, peer_v):
    B = x.shape[0]
    mods = _ada(c, ada_w, ada_b)
    bias_tiles, cmp_bias = _relbias(rel_table)
    for layer in range(DEPTH):
        sh1, sc1, g1, sh2, sc2, g2 = [m.reshape(B, 1, D_MODEL) for m in jnp.split(mods[layer], 6, axis=-1)]
        i = layer // 2
        ng1 = norm_g[layer, 0].reshape(1, D_MODEL)
        ng2 = norm_g[layer, 1].reshape(1, D_MODEL)
        if layer % 2 == 0:
            x = _even_mixer(x, sh1, sc1, g1, ng1, even_w_in[i], even_b_f[i], even_conv_w[i], even_q_g[i],
                            even_k_g[i], even_w_out[i])
        else:
            x = _odd_mixer(x, sh1, sc1, g1, ng1, odd_w_in[i], odd_b_gate[i], odd_q_g[i], odd_k_g[i],
                           odd_cmp_pos[i], odd_cmp_w1[i], odd_cmp_w2[i], bias_tiles, cmp_bias, odd_w_out[i])
        x = _peer(x, sh2, sc2, g2, ng2, peer_w_q[layer], peer_keys[layer],
                  peer_u[layer].astype(BF16), peer_v[layer].T.astype(BF16))
    return x
```

```python
import functools
import math

import jax
import jax.numpy as jnp
from jax import lax
from jax.experimental import pallas as pl
from jax.experimental.pallas import tpu as pltpu

D_MODEL = 1024
DEPTH = 4
HEAD_DIM = 64
CONV_CH = 512
CONV_K = 3
FOX_HEADS = 8
FOX_WIDTH = FOX_HEADS * HEAD_DIM
EVEN_IN = 3 * CONV_CH + 3 * FOX_WIDTH + FOX_HEADS
NSA_GROUPS = 4
NSA_HPG = 4
NSA_HEADS = NSA_GROUPS * NSA_HPG
NSA_WIDTH = NSA_HEADS * HEAD_DIM
NSA_KV = NSA_GROUPS * HEAD_DIM
CMP_LEN = 32
CMP_STRIDE = 16
SLC_LEN = 64
SLC_TOPN = 16
WINDOW = 512
ODD_IN = NSA_WIDTH + 6 * NSA_KV + 3 * NSA_HEADS
REL_BUCKETS = 32
REL_MAX_DIST = 128
PEER_HEADS = 8
PEER_NKEYS = 128
PEER_EXPERTS = PEER_NKEYS * PEER_NKEYS
PEER_QDIM = 128
PEER_TOPK = 16
EPS = 1e-6
NEG_INF = -1e30
FORCE_SCORE = 1e6

LANES = 128
F32 = jnp.float32
BF16 = jnp.bfloat16
HIGHEST = lax.Precision.HIGHEST
VMEM_LIMIT = 56 * 1024 * 1024

EVEN_IN_PAD = -(-EVEN_IN // LANES) * LANES
ODD_IN_PAD = -(-ODD_IN // LANES) * LANES
ROW_TILE = 512
ATT_TILE = 128
FOX_TILE = 256
PEER_TN = 512
PEER_EC = 2048
SCALE = HEAD_DIM ** -0.5


def _cparams(sem):
    return pltpu.CompilerParams(dimension_semantics=sem, vmem_limit_bytes=VMEM_LIMIT)


def _dot(a, b, precision=None):
    return jnp.dot(a, b, precision=precision, preferred_element_type=F32)


def _dot_nt(a, b, precision=None):
    return lax.dot_general(a, b, (((1,), (1,)), ((), ())), precision=precision,
                           preferred_element_type=F32)


def _rms(x, g):
    return x * lax.rsqrt(jnp.mean(x * x, axis=-1, keepdims=True) + EPS) * g


def _ada_kernel(c_ref, w_ref, b_ref, o_ref):
    c = c_ref[...]
    sc = c * jax.nn.sigmoid(c)
    o_ref[0] = _dot(sc, w_ref[0], HIGHEST) + b_ref[0]


def _ada(c, ada_w, ada_b):
    B, D = c.shape
    E = ada_w.shape[2]
    tn = 2048
    return pl.pallas_call(
        _ada_kernel,
        out_shape=jax.ShapeDtypeStruct((DEPTH, B, E), F32),
        grid=(DEPTH, E // tn),
        in_specs=[pl.BlockSpec((B, D), lambda l, j: (0, 0)),
                  pl.BlockSpec((1, D, tn), lambda l, j: (l, 0, j)),
                  pl.BlockSpec((1, 1, tn), lambda l, j: (l, 0, j))],
        out_specs=pl.BlockSpec((1, B, tn), lambda l, j: (l, 0, j)),
        compiler_params=_cparams(("arbitrary", "arbitrary")),
        name="ada_mod",
    )(c, ada_w, ada_b.reshape(DEPTH, 1, E))


def _norm_proj_kernel(x_ref, sh_ref, sc_ref, ng_ref, w_ref, y_ref):
    h = _rms(x_ref[0], ng_ref[...]) * (1.0 + sc_ref[0]) + sh_ref[0]
    y_ref[0] = _dot(h.astype(BF16), w_ref[...])


def _norm_proj(x, sh, sc, ng, w):
    B, S, D = x.shape
    N = w.shape[1]
    return pl.pallas_call(
        _norm_proj_kernel,
        out_shape=jax.ShapeDtypeStruct((B, S, N), F32),
        grid=(B, S // ROW_TILE),
        in_specs=[pl.BlockSpec((1, ROW_TILE, D), lambda b, i: (b, i, 0)),
                  pl.BlockSpec((1, 1, D), lambda b, i: (b, 0, 0)),
                  pl.BlockSpec((1, 1, D), lambda b, i: (b, 0, 0)),
                  pl.BlockSpec((1, D), lambda b, i: (0, 0)),
                  pl.BlockSpec((D, N), lambda b, i: (0, 0))],
        out_specs=pl.BlockSpec((1, ROW_TILE, N), lambda b, i: (b, i, 0)),
        compiler_params=_cparams(("arbitrary", "arbitrary")),
        name="norm_proj",
    )(x, sh, sc, ng, w)


def _out_proj_kernel(x_ref, a_ref, b_ref, wa_ref, wb_ref, g_ref, o_ref):
    y = _dot(a_ref[0].astype(BF16), wa_ref[...]) + _dot(b_ref[0].astype(BF16), wb_ref[...])
    o_ref[0] = x_ref[0] + g_ref[0] * y


def _out_proj(x, a, a_blk, b, b_blk, w, gate):
    B, S, D = x.shape
    half = w.shape[0] // 2
    wa, wb = w[:half], w[half:]
    return pl.pallas_call(
        _out_proj_kernel,
        out_shape=jax.ShapeDtypeStruct((B, S, D), F32),
        grid=(B, S // ROW_TILE),
        in_specs=[pl.BlockSpec((1, ROW_TILE, D), lambda b, i: (b, i, 0)),
                  pl.BlockSpec((1, ROW_TILE, half), lambda b, i: (b, i, a_blk)),
                  pl.BlockSpec((1, ROW_TILE, half), lambda b, i: (b, i, b_blk)),
                  pl.BlockSpec((half, D), lambda b, i: (0, 0)),
                  pl.BlockSpec((half, D), lambda b, i: (0, 0)),
                  pl.BlockSpec((1, 1, D), lambda b, i: (b, 0, 0))],
        out_specs=pl.BlockSpec((1, ROW_TILE, D), lambda b, i: (b, i, 0)),
        compiler_params=_cparams(("arbitrary", "arbitrary")),
        name="out_proj",
    )(x, a, b, wa, wb, gate)


def _conv_kernel(cb_ref, cc_ref, ch_ref, w_ref, o_ref):
    u = cc_ref[0] * ch_ref[0]
    row = lax.broadcasted_iota(jnp.int32, u.shape, 0)
    u1 = jnp.where(row >= 1, pltpu.roll(u, 1, axis=0), 0.0)
    u2 = jnp.where(row >= 2, pltpu.roll(u, 2, axis=0), 0.0)
    w = w_ref[...]
    o_ref[0] = cb_ref[0] * (w[0:1] * u2 + w[1:2] * u1 + w[2:3] * u)


def _short_conv(y, conv_w):
    B, S, _ = y.shape
    nblk = CONV_CH // LANES
    w = jnp.zeros((8, CONV_CH), F32).at[:CONV_K].set(conv_w)
    return pl.pallas_call(
        _conv_kernel,
        out_shape=jax.ShapeDtypeStruct((B, S, CONV_CH), F32),
        grid=(B, nblk),
        in_specs=[pl.BlockSpec((1, S, LANES), lambda b, j: (b, 0, j)),
                  pl.BlockSpec((1, S, LANES), lambda b, j: (b, 0, nblk + j)),
                  pl.BlockSpec((1, S, LANES), lambda b, j: (b, 0, 2 * nblk + j)),
                  pl.BlockSpec((8, LANES), lambda b, j: (0, j))],
        out_specs=pl.BlockSpec((1, S, LANES), lambda b, j: (b, 0, j)),
        compiler_params=_cparams(("arbitrary", "arbitrary")),
        name="short_conv",
    )(y, y, y, w)


def _cumsum_rows(x):
    n = x.shape[0]
    row = lax.broadcasted_iota(jnp.int32, x.shape, 0)
    sh = 1
    while sh < n:
        x = x + jnp.where(row >= sh, pltpu.roll(x, sh, axis=0), 0.0)
        sh *= 2
    return x


FOX_KEY_PARTS = 4


def _fox_kernel(q_ref, k_ref, v_ref, f_ref, bf_ref, qg_ref, kg_ref, o_ref,
                kn_scr, vb_scr, cum_scr, cumrow_scr):
    qi = pl.program_id(1)
    S = k_ref.shape[1]
    T = q_ref.shape[1]
    nq = S // T

    @pl.when(qi == 0)
    def _prep():
        ones_col = (lax.broadcasted_iota(jnp.int32, (S, LANES - HEAD_DIM), 1) == 0).astype(BF16)
        for h in range(FOX_HEADS):
            cols = slice(h * HEAD_DIM, (h + 1) * HEAD_DIM)
            kn_scr[:, cols] = _rms(k_ref[0, :, cols], kg_ref[...]).astype(BF16)
            vb_scr[:, h * LANES:h * LANES + HEAD_DIM] = v_ref[0, :, cols].astype(BF16)
            vb_scr[:, h * LANES + HEAD_DIM:(h + 1) * LANES] = ones_col
        logf = jax.nn.log_sigmoid(f_ref[0] + bf_ref[...])
        cum = _cumsum_rows(logf)
        cum_scr[...] = cum
        for kt in range(S // T):
            cumrow_scr[:, kt * T:(kt + 1) * T] = cum[kt * T:(kt + 1) * T, :].T[0:8, :]

    cq = cum_scr[pl.ds(pl.multiple_of(qi * T, T), T), :]

    def attend(nkeys):
        row = qi * T + lax.broadcasted_iota(jnp.int32, (T, nkeys), 0)
        causal = jnp.where(lax.broadcasted_iota(jnp.int32, (T, nkeys), 1) <= row, 0.0, NEG_INF)
        for h in range(FOX_HEADS):
            cols = slice(h * HEAD_DIM, (h + 1) * HEAD_DIM)
            qh = (_rms(q_ref[0, :, cols], qg_ref[...]) * SCALE).astype(BF16)
            s = _dot_nt(qh, kn_scr[0:nkeys, cols]) + ((cq[:, h:h + 1] - cumrow_scr[h:h + 1, 0:nkeys]) + causal)
            p = jnp.exp(s - jnp.max(s, axis=-1, keepdims=True))
            acc = _dot(p.astype(BF16), vb_scr[0:nkeys, h * LANES:(h + 1) * LANES])
            o_ref[0, :, cols] = acc[:, :HEAD_DIM] / acc[:, HEAD_DIM:HEAD_DIM + 1]

    parts = min(FOX_KEY_PARTS, nq)
    per = nq // parts
    for part in range(parts):
        pl.when((qi >= part * per) & (qi < (part + 1) * per))(functools.partial(attend, (part + 1) * (S // parts)))


def _fox_attention(y, b_f, q_g, k_g):
    B, S, _ = y.shape
    T = FOX_TILE
    qblk = 3 * CONV_CH // FOX_WIDTH
    fblk = (3 * CONV_CH + 3 * FOX_WIDTH) // LANES
    bf = jnp.zeros((1, LANES), F32).at[0, :FOX_HEADS].set(b_f)
    return pl.pallas_call(
        _fox_kernel,
        out_shape=jax.ShapeDtypeStruct((B, S, FOX_WIDTH), F32),
        grid=(B, S // T),
        in_specs=[pl.BlockSpec((1, T, FOX_WIDTH), lambda b, i: (b, i, qblk)),
                  pl.BlockSpec((1, S, FOX_WIDTH), lambda b, i: (b, 0, qblk + 1)),
                  pl.BlockSpec((1, S, FOX_WIDTH), lambda b, i: (b, 0, qblk + 2)),
                  pl.BlockSpec((1, S, LANES), lambda b, i: (b, 0, fblk)),
                  pl.BlockSpec((1, LANES), lambda b, i: (0, 0)),
                  pl.BlockSpec((1, HEAD_DIM), lambda b, i: (0, 0)),
                  pl.BlockSpec((1, HEAD_DIM), lambda b, i: (0, 0))],
        out_specs=pl.BlockSpec((1, T, FOX_WIDTH), lambda b, i: (b, i, 0)),
        scratch_shapes=[pltpu.VMEM((S, FOX_WIDTH), BF16),
                        pltpu.VMEM((S, FOX_HEADS * LANES), BF16),
                        pltpu.VMEM((S, LANES), F32),
                        pltpu.VMEM((8, S), F32)],
        compiler_params=_cparams(("arbitrary", "arbitrary")),
        name="fox_attention",
    )(y, y, y, y, bf, q_g.reshape(1, HEAD_DIM), k_g.reshape(1, HEAD_DIM))


def _gelu_tanh(x):
    c0 = math.sqrt(2.0 / math.pi)
    return (0.5 * x) * (1.0 + jnp.tanh(x * (c0 + (c0 * 0.044715) * (x * x))))


def _extract_top(work, iota, k, exact):
    nrows = work.shape[0]
    vals = []
    for _ in range(k):
        m = jnp.max(work, axis=0, keepdims=True)
        if exact:
            idx = jnp.min(jnp.where(work == m, iota, nrows), axis=0, keepdims=True)
            work = jnp.where(iota == idx, -jnp.inf, work)
        else:
            work = jnp.where(work == m, -jnp.inf, work)
        vals.append(m)
    return vals, work


def _removed(work):
    return jnp.sum(jnp.where(work == -jnp.inf, 1.0, 0.0), axis=0, keepdims=True)


def _peer_kernel(x_ref, sh_ref, sc_ref, gt_ref, ng_ref, wqh_ref, wql_ref, keys_ref, u_ref, vt_ref, o_ref,
                 ht_scr, qt_scr, sp_scr, t0_scr, t1_scr, tau_scr, a_scr, p_scr, acc_scr):
    c = pl.program_id(2)
    nc = pl.num_programs(2)
    tn = x_ref.shape[1]
    nlt = tn // LANES
    half = PEER_QDIM // 2

    @pl.when(c == 0)
    def _prep():
        h = _rms(x_ref[0], ng_ref[...]) * (1.0 + sc_ref[0]) + sh_ref[0]
        hh = h.astype(BF16)
        hl = (h - hh.astype(F32)).astype(BF16)
        cw = 2 * LANES
        for k in range(h.shape[1] // cw):
            cs = slice(k * cw, (k + 1) * cw)
            q = _dot(hh, wqh_ref[:, cs]) + (_dot(hh, wql_ref[:, cs]) + _dot(hl, wqh_ref[:, cs]))
            qt_scr[cs, :] = q.T
            ht_scr[cs, :] = h[:, cs].T.astype(BF16)
        acc_scr[...] = jnp.zeros_like(acc_scr)
        iota_k = lax.broadcasted_iota(jnp.int32, (PEER_NKEYS, LANES), 0)
        iota_c = lax.broadcasted_iota(jnp.int32, (PEER_TOPK + 8 * 8, LANES), 0)

        n_pad = sum(8 - PEER_TOPK // (r0 + 1) for r0 in range(2, 8))

        def select(hd, lt, exact):
            lanes = slice(lt * LANES, (lt + 1) * LANES)
            s0 = sp_scr[0, :, lanes]
            s1 = sp_scr[1, :, lanes]
            sv0, w0 = _extract_top(s0, iota_k, PEER_TOPK, exact)
            sv1, w1 = _extract_top(s1, iota_k, PEER_TOPK, exact)
            s0m = jnp.where(w0 == -jnp.inf, s0, -jnp.inf)
            s1m = jnp.where(w1 == -jnp.inf, s1, -jnp.inf)
            sv0_all = jnp.concatenate(sv0, axis=0)
            sv1_all = jnp.concatenate(sv1, axis=0)
            parts = [sv0[0] + sv1_all, sv0[1] + sv1_all[:8]]
            for r0 in range(2, 8):
                parts.append(jnp.where(iota_c[:8] < PEER_TOPK // (r0 + 1), sv0[r0] + sv1_all[:8], -jnp.inf))
            parts.append(sv0_all[8:] + sv1[0])
            cv, wc = _extract_top(jnp.concatenate(parts, axis=0), iota_c, PEER_TOPK, exact)
            z = cv[0] - cv[0]
            for r in range(PEER_TOPK):
                z = z + jnp.exp(cv[r] - cv[0])
            nrt = PEER_NKEYS // 8
            t0_scr[hd, lt, :, 0] = s0m.reshape(nrt, 8, LANES)
            t0_scr[hd, lt, :, 1] = jnp.exp(s0m - sv0[0]).reshape(nrt, 8, LANES)
            t1_scr[hd, lt, :, 0] = s1m.reshape(nrt, 8, LANES)
            t1_scr[hd, lt, :, 1] = (jnp.exp(s1m - sv1[0]) / z).reshape(nrt, 8, LANES)
            tau_scr[hd, lt] = jnp.broadcast_to(cv[PEER_TOPK - 1], (8, LANES))
            return ((_removed(w0) != PEER_TOPK) | (_removed(w1) != PEER_TOPK)
                    | (_removed(wc) != PEER_TOPK + n_pad))

        def head_body(hd, carry):
            for p in range(2):
                qs = qt_scr[pl.ds(pl.multiple_of((hd * 2 + p) * half, half), half), :]
                sp_scr[p] = _dot(keys_ref[p], qs, HIGHEST)
            tied = jnp.zeros((1, LANES), F32)
            for lt in range(nlt):
                tied = jnp.maximum(tied, jnp.where(select(hd, lt, False), 1.0, 0.0))

            @pl.when(jnp.max(tied) > 0.0)
            def _redo():
                for lt in range(nlt):
                    select(hd, lt, True)
            return carry

        lax.fori_loop(0, PEER_HEADS, head_body, 0)

    nsub = 8
    nrq = 32
    nparts = PEER_EC // PEER_NKEYS // nsub
    for part in range(nparts):
        er = slice(part * nsub * PEER_NKEYS, (part + 1) * nsub * PEER_NKEYS)
        for lh in range(nlt // 2):
            lanes2 = slice(lh * 2 * LANES, (lh + 1) * 2 * LANES)
            a_scr[er, lanes2] = _gelu_tanh(_dot(u_ref[er, :], ht_scr[:, lanes2]))

    def weights_tile(part, lt, rq, prev):
        lanes = slice(lt * LANES, (lt + 1) * LANES)
        jt = slice(rq * nrq // 8, (rq + 1) * nrq // 8)
        blk = c * (PEER_EC // PEER_NKEYS // 8) + part * nsub // 8
        zero = 0.0 if prev is None else jnp.where(prev != prev, prev, 0.0)
        w = [jnp.zeros((nrq, LANES), F32) for _ in range(nsub)]
        for hd in range(PEER_HEADS):
            s1 = t1_scr[hd, lt, jt, 0].reshape(nrq, LANES)
            e1 = t1_scr[hd, lt, jt, 1].reshape(nrq, LANES)
            s0 = t0_scr[hd, lt, blk, 0] + zero
            e0 = t0_scr[hd, lt, blk, 1] + zero
            tau = tau_scr[hd, lt, 0:1, :]
            for ii in range(nsub):
                keep = s1 + s0[ii:ii + 1] >= tau
                w[ii] = w[ii] + jnp.where(keep, e1 * e0[ii:ii + 1], 0.0)
        for ii in range(nsub):
            rows = slice((part * nsub + ii) * PEER_NKEYS + rq * nrq, (part * nsub + ii) * PEER_NKEYS + (rq + 1) * nrq)
            w[ii] = w[ii] * a_scr[rows, lanes]
            p_scr[rows, lanes] = w[ii].astype(BF16)
        return w[nsub - 1][0:1, :]

    prev = None
    for part in range(nparts):
        for lt in range(nlt):
            for rq in range(PEER_NKEYS // nrq):
                prev = weights_tile(part, lt, rq, prev)
    acc_scr[...] += _dot(vt_ref[...], p_scr[...])

    @pl.when(c == nc - 1)
    def _fin():
        o_ref[0] = x_ref[0] + gt_ref[0] * acc_scr[...].T


def _peer(x, sh, sc, gate, ng, w_q, keys, u_b, vt_b):
    B, S, D = x.shape
    tn = PEER_TN
    nc = PEER_EXPERTS // PEER_EC
    tab = pltpu.VMEM((PEER_HEADS, tn // LANES, PEER_NKEYS // 8, 2, 8, LANES), F32)
    wq_hi = w_q.astype(BF16)
    wq_lo = (w_q - wq_hi.astype(F32)).astype(BF16)
    wspec = pl.BlockSpec((D, PEER_HEADS * PEER_QDIM), lambda b, i, c: (0, 0), pipeline_mode=pl.Buffered(1))
    return pl.pallas_call(
        _peer_kernel,
        out_shape=jax.ShapeDtypeStruct((B, S, D), F32),
        grid=(B, S // tn, nc),
        in_specs=[pl.BlockSpec((1, tn, D), lambda b, i, c: (b, i, 0), pipeline_mode=pl.Buffered(1)),
                  pl.BlockSpec((1, 1, D), lambda b, i, c: (b, 0, 0)),
                  pl.BlockSpec((1, 1, D), lambda b, i, c: (b, 0, 0)),
                  pl.BlockSpec((1, 1, D), lambda b, i, c: (b, 0, 0)),
                  pl.BlockSpec((1, D), lambda b, i, c: (0, 0)),
                  wspec, wspec,
                  pl.BlockSpec((2, PEER_NKEYS, PEER_QDIM // 2), lambda b, i, c: (0, 0, 0)),
                  pl.BlockSpec((PEER_EC, D), lambda b, i, c: (c, 0)),
                  pl.BlockSpec((D, PEER_EC), lambda b, i, c: (0, c))],
        out_specs=pl.BlockSpec((1, tn, D), lambda b, i, c: (b, i, 0)),
        scratch_shapes=[pltpu.VMEM((D, tn), BF16),
                        pltpu.VMEM((PEER_HEADS * PEER_QDIM, tn), F32),
                        pltpu.VMEM((2, PEER_NKEYS, tn), F32),
                        tab, tab,
                        pltpu.VMEM((PEER_HEADS, tn // LANES, 8, LANES), F32),
                        pltpu.VMEM((PEER_EC, tn), F32),
                        pltpu.VMEM((PEER_EC, tn), BF16),
                        pltpu.VMEM((D, tn), F32)],
        compiler_params=_cparams(("arbitrary", "arbitrary", "arbitrary")),
        name="peer",
    )(x, sh, sc, gate, ng, wq_hi, wq_lo, keys, u_b, vt_b)


def _rel_bucket(dist):
    n = jnp.maximum(dist, 0)
    max_exact = REL_BUCKETS // 2
    nf = jnp.maximum(n, 1).astype(F32)
    large = max_exact + (jnp.log(nf / max_exact) / math.log(REL_MAX_DIST / max_exact)
                         * (REL_BUCKETS - max_exact)).astype(jnp.int32)
    large = jnp.minimum(large, REL_BUCKETS - 1)
    return jnp.where(n < max_exact, n, large)


def _bias_lookup(bucket, tab_ref, h):
    bias = jnp.zeros(bucket.shape, F32)
    for b in range(REL_BUCKETS):
        bias = jnp.where(bucket == b, tab_ref[b, h], bias)
    return bias


N_BIAS_TILES = 5


def _relbias_kernel(tab_ref, o_ref, c_ref):
    h = pl.program_id(0)
    T = o_ref.shape[2]
    a = lax.broadcasted_iota(jnp.int32, (T, T), 0)
    b = lax.broadcasted_iota(jnp.int32, (T, T), 1)
    last = tab_ref[REL_BUCKETS - 1, h]
    o_ref[0, 0] = jnp.where(a >= b, _bias_lookup(_rel_bucket(a - b), tab_ref, h), NEG_INF)
    o_ref[0, 1] = _bias_lookup(_rel_bucket(a - b + T), tab_ref, h)
    o_ref[0, 2] = jnp.full((T, T), last, F32)
    o_ref[0, 3] = jnp.where(a < b, last, NEG_INF)
    o_ref[0, 4] = jnp.full((T, T), NEG_INF, F32)
    dist = a - ((b - T // 2) * CMP_STRIDE + CMP_LEN - 1)
    c_ref[0] = jnp.where(dist >= 0, _bias_lookup(_rel_bucket(dist), tab_ref, h), last)


def _relbias(rel_table):
    T = ATT_TILE
    assert 2 * T - (T - 1) >= REL_MAX_DIST
    assert WINDOW % T == 0
    return pl.pallas_call(
        _relbias_kernel,
        out_shape=(jax.ShapeDtypeStruct((NSA_HEADS, N_BIAS_TILES, T, T), F32),
                   jax.ShapeDtypeStruct((NSA_HEADS, T, T), F32)),
        grid=(NSA_HEADS,),
        in_specs=[pl.BlockSpec(memory_space=pltpu.SMEM)],
        out_specs=(pl.BlockSpec((1, N_BIAS_TILES, T, T), lambda h: (h, 0, 0, 0)),
                   pl.BlockSpec((1, T, T), lambda h: (h, 0, 0))),
        compiler_params=_cparams(("arbitrary",)),
        name="rel_bias_tiles",
    )(rel_table)


def _cmp_kernel(x_ref, pos_ref, w1_ref, w2_ref, kg_ref, o_ref):
    j = pl.program_id(1)
    half = CMP_STRIDE * HEAD_DIM
    x = x_ref[0, 0]
    pos = pos_ref[0]
    pa = _dot((x + pos[:, :half]).astype(BF16), w1_ref[0, :half, :])
    pb = _dot((x + pos[:, half:]).astype(BF16), w1_ref[0, half:, :])
    pre = pa + pltpu.roll(pb, pb.shape[0] - 1, axis=0)
    out = _dot(jax.nn.gelu(pre, approximate=True).astype(BF16), w2_ref[0])
    o_ref[0, 0] = jnp.where(j < NSA_GROUPS, _rms(out, kg_ref[...]), out)


def _nsa_compress(xr, pos, w1, w2, kg0):
    B, _, M, W = xr.shape
    G = NSA_GROUPS
    return pl.pallas_call(
        _cmp_kernel,
        out_shape=jax.ShapeDtypeStruct((B, 2 * G, M, HEAD_DIM), F32),
        grid=(B, 2 * G),
        in_specs=[pl.BlockSpec((1, 1, M, W), lambda b, j: (b, j, 0, 0)),
                  pl.BlockSpec((1, 1, 2 * W), lambda b, j: (j // G, 0, 0)),
                  pl.BlockSpec((1, 2 * W, HEAD_DIM), lambda b, j: (j // G, 0, 0)),
                  pl.BlockSpec((1, HEAD_DIM, HEAD_DIM), lambda b, j: (j // G, 0, 0)),
                  pl.BlockSpec((1, HEAD_DIM), lambda b, j: (0, 0))],
        out_specs=pl.BlockSpec((1, 1, M, HEAD_DIM), lambda b, j: (b, j, 0, 0)),
        compiler_params=_cparams(("arbitrary", "arbitrary")),
        name="nsa_compress",
    )(xr, pos.reshape(2, 1, 2 * W), w1.astype(BF16), w2.astype(BF16), kg0.reshape(1, HEAD_DIM))


def _cmp_attn_kernel(q_ref, kv_ref, ov_ref, cb_ref, qg_ref, oc_ref, sel_ref, qn_ref):
    qi = pl.program_id(1)
    T = q_ref.shape[1]
    NC = kv_ref.shape[2]
    G, J = NSA_GROUPS, NSA_HPG
    NS = sel_ref.shape[2] // G
    assert T == NC == LANES and T % CMP_STRIDE == 0 and G * NS == LANES
    t = qi * T + lax.broadcasted_iota(jnp.int32, (T, NC), 0)
    lane = lax.broadcasted_iota(jnp.int32, (T, NC), 1)
    vis = t - (lane * CMP_STRIDE + CMP_LEN - 1) >= 0
    visf = vis.astype(F32)
    shift = (qi * (T // CMP_STRIDE) + T // 2) % LANES
    imp = jnp.zeros((T, LANES), F32)
    vis4 = jnp.concatenate([vis] * J, axis=0)
    visf4 = jnp.concatenate([visf] * J, axis=0)
    for g in range(G):
        kc = kv_ref[0, g].astype(BF16)
        vc = kv_ref[0, G + g].astype(BF16)
        qhs = []
        for j in range(J):
            cols = slice((g * J + j) * HEAD_DIM, (g * J + j + 1) * HEAD_DIM)
            qhs.append((_rms(q_ref[0, :, cols], qg_ref[...]) * SCALE).astype(BF16))
            qn_ref[0, :, cols] = qhs[j]
        bias = jnp.concatenate([pltpu.roll(cb_ref[g * J + j], shift, axis=1) for j in range(J)], axis=0)
        s = jnp.where(vis4, _dot_nt(jnp.concatenate(qhs, axis=0), kc) + bias, NEG_INF)
        e = jnp.exp(s - jnp.max(s, axis=-1, keepdims=True))
        p = e / jnp.sum(e, axis=-1, keepdims=True) * visf4
        o = _dot(p.astype(BF16), vc)
        psum = p[0:T]
        for j in range(J):
            oc_ref[0, :, (g * J + j) * HEAD_DIM:(g * J + j + 1) * HEAD_DIM] = o[j * T:(j + 1) * T]
            if j:
                psum = psum + p[j * T:(j + 1) * T]
        imp = imp + _dot(psum, ov_ref[g], HIGHEST)
    mm = lane % NS
    cur = t // SLC_LEN
    forced = (mm == 0) | (mm == cur) | (mm == cur - 1)
    imp = jnp.where(forced, FORCE_SCORE, jnp.where(mm * SLC_LEN <= t, imp, -1.0))
    rank = jnp.zeros((T, LANES), F32)
    for r in range(1, NS):
        wrapped = mm >= NS - r
        other = jnp.where(wrapped, pltpu.roll(imp, NS - r, axis=1), pltpu.roll(imp, LANES - r, axis=1))
        rank = rank + jnp.where(other > imp, 1.0, jnp.where(other == imp, wrapped.astype(F32), 0.0))
    sel_ref[0] = jnp.where(rank < SLC_TOPN, jnp.where(imp >= 0.0, 1.0, 0.0), 0.0)


def _nsa_cmp_attn(y, kcvc, overlap, cmp_bias, q_g):
    B, S, _ = y.shape
    T = ATT_TILE
    NC = kcvc.shape[2]
    return pl.pallas_call(
        _cmp_attn_kernel,
        out_shape=(jax.ShapeDtypeStruct((B, S, NSA_WIDTH), F32),
                   jax.ShapeDtypeStruct((B, S, LANES), F32),
                   jax.ShapeDtypeStruct((B, S, NSA_WIDTH), BF16)),
        grid=(B, S // T),
        in_specs=[pl.BlockSpec((1, T, NSA_WIDTH), lambda b, i: (b, i, 0)),
                  pl.BlockSpec((1, 2 * NSA_GROUPS, NC, HEAD_DIM), lambda b, i: (b, 0, 0, 0)),
                  pl.BlockSpec(overlap.shape, lambda b, i: (0, 0, 0)),
                  pl.BlockSpec(cmp_bias.shape, lambda b, i: (0, 0, 0)),
                  pl.BlockSpec((1, HEAD_DIM), lambda b, i: (0, 0))],
        out_specs=(pl.BlockSpec((1, T, NSA_WIDTH), lambda b, i: (b, i, 0)),
                   pl.BlockSpec((1, T, LANES), lambda b, i: (b, i, 0)),
                   pl.BlockSpec((1, T, NSA_WIDTH), lambda b, i: (b, i, 0))),
        compiler_params=_cparams(("arbitrary", "arbitrary")),
        name="nsa_cmp_attn",
    )(y, kcvc, overlap, cmp_bias, q_g.reshape(1, HEAD_DIM))


def _kv_prep_kernel(ks_ref, kw_ref, kg1_ref, kg2_ref, ksn_ref, vs_ref, kwn_ref, vw_ref):
    KV = NSA_KV
    rows = ks_ref.shape[1]
    ones_col = (lax.broadcasted_iota(jnp.int32, (rows, LANES - HEAD_DIM), 1) == 0).astype(BF16)
    for g in range(NSA_GROUPS):
        cols = slice(g * HEAD_DIM, (g + 1) * HEAD_DIM)
        vcols = slice(KV + g * HEAD_DIM, KV + (g + 1) * HEAD_DIM)
        ksn_ref[0, :, cols] = _rms(ks_ref[0, :, cols], kg1_ref[...]).astype(BF16)
        kwn_ref[0, :, cols] = _rms(kw_ref[0, :, cols], kg2_ref[...]).astype(BF16)
        vs_ref[0, :, g * LANES:g * LANES + HEAD_DIM] = ks_ref[0, :, vcols].astype(BF16)
        vw_ref[0, :, g * LANES:g * LANES + HEAD_DIM] = kw_ref[0, :, vcols].astype(BF16)
        vs_ref[0, :, g * LANES + HEAD_DIM:(g + 1) * LANES] = ones_col
        vw_ref[0, :, g * LANES + HEAD_DIM:(g + 1) * LANES] = ones_col


def _nsa_kv_prep(y, k_g):
    B, S, _ = y.shape
    ksblk = (NSA_WIDTH + 2 * NSA_KV) // (2 * NSA_KV)
    kout = jax.ShapeDtypeStruct((B, S, NSA_KV), BF16)
    vout = jax.ShapeDtypeStruct((B, S, NSA_GROUPS * LANES), BF16)
    kspec = pl.BlockSpec((1, ROW_TILE, NSA_KV), lambda b, i: (b, i, 0))
    vspec = pl.BlockSpec((1, ROW_TILE, NSA_GROUPS * LANES), lambda b, i: (b, i, 0))
    return pl.pallas_call(
        _kv_prep_kernel,
        out_shape=(kout, vout, kout, vout),
        grid=(B, S // ROW_TILE),
        in_specs=[pl.BlockSpec((1, ROW_TILE, 2 * NSA_KV), lambda b, i: (b, i, ksblk)),
                  pl.BlockSpec((1, ROW_TILE, 2 * NSA_KV), lambda b, i: (b, i, ksblk + 1)),
                  pl.BlockSpec((1, HEAD_DIM), lambda b, i: (0, 0)),
                  pl.BlockSpec((1, HEAD_DIM), lambda b, i: (0, 0))],
        out_specs=(kspec, vspec, kspec, vspec),
        compiler_params=_cparams(("arbitrary", "arbitrary")),
        name="nsa_kv_prep",
    )(y, y, k_g[1].reshape(1, HEAD_DIM), k_g[2].reshape(1, HEAD_DIM))


SEL_KEY_PARTS = 4


def _sel_win_kernel(qn_ref, ks_ref, vs_ref, kw_ref, vw_ref, gl_ref, sel_ref, oc_ref, bias_ref, e_ref, bg_ref,
                    o_ref):
    qi = pl.program_id(1)
    T = qn_ref.shape[1]
    S = ks_ref.shape[1]
    nk = S // T
    G, J = NSA_GROUPS, NSA_HPG
    NS = e_ref.shape[0]
    wt = WINDOW // T
    nwin = wt + 1
    assert nk % SEL_KEY_PARTS == 0 and nk >= nwin

    def sel_tile(r):
        return jnp.where(r < 0, 4, jnp.minimum(r, 2))

    def win_tile(r):
        return jnp.where(r < 0, 4, jnp.where(r <= 1, r, jnp.where(r < wt, 2, jnp.where(r == wt, 3, 4))))

    def softmax_pv(s, v):
        p = jnp.exp(s - jnp.max(s, axis=-1, keepdims=True))
        acc = _dot(p.astype(BF16), v)
        return acc[:, :HEAD_DIM] / acc[:, HEAD_DIM:HEAD_DIM + 1]

    gates = jax.nn.sigmoid(gl_ref[0] + bg_ref[...])
    k0 = jnp.maximum(qi - wt, 0)
    wrows = pl.ds(pl.multiple_of(k0 * T, T), nwin * T)

    def attend(nkt):
        nkeys = nkt * T
        for g in range(G):
            gcols = slice(g * HEAD_DIM, (g + 1) * HEAD_DIM)
            vcols = slice(g * LANES, (g + 1) * LANES)
            q4 = jnp.concatenate(
                [qn_ref[0, :, (g * J + j) * HEAD_DIM:(g * J + j + 1) * HEAD_DIM] for j in range(J)], axis=0)
            selg = sel_ref[0, :, g * NS:(g + 1) * NS].astype(BF16)
            mask = (_dot(selg, e_ref[:, :nkeys]) - 1.0) * (-NEG_INF)
            bias = jnp.concatenate(
                [jnp.concatenate([bias_ref[g * J + j, sel_tile(qi - kt)] for kt in range(nkt)], axis=1) + mask
                 for j in range(J)], axis=0)
            o_s = softmax_pv(_dot_nt(q4, ks_ref[0, :nkeys, gcols]) + bias, vs_ref[0, :nkeys, vcols])
            wbias = jnp.concatenate(
                [jnp.concatenate([bias_ref[g * J + j, win_tile(qi - k0 - t)] for t in range(nwin)], axis=1)
                 for j in range(J)], axis=0)
            o_w = softmax_pv(_dot_nt(q4, kw_ref[0, wrows, gcols]) + wbias, vw_ref[0, wrows, vcols])
            for j in range(J):
                h = g * J + j
                cols = slice(h * HEAD_DIM, (h + 1) * HEAD_DIM)
                rows = slice(j * T, (j + 1) * T)
                o_ref[0, :, cols] = (gates[:, 3 * h:3 * h + 1] * oc_ref[0, :, cols]
                                     + gates[:, 3 * h + 1:3 * h + 2] * o_s[rows]
                                     + gates[:, 3 * h + 2:3 * h + 3] * o_w[rows])

    per = nk // SEL_KEY_PARTS
    for part in range(SEL_KEY_PARTS):
        pl.when((qi >= part * per) & (qi < (part + 1) * per))(functools.partial(attend, (part + 1) * per))


def _nsa_sel_win(qn, y, kv, sel, oc, bias_tiles, expand, b_gate):
    B, S, _ = y.shape
    T = ATT_TILE
    glblk = (NSA_WIDTH + 6 * NSA_KV) // LANES
    bg = jnp.zeros((1, LANES), F32).at[0, :3 * NSA_HEADS].set(b_gate)
    kspec = pl.BlockSpec((1, S, NSA_KV), lambda b, i: (b, 0, 0))
    vspec = pl.BlockSpec((1, S, NSA_GROUPS * LANES), lambda b, i: (b, 0, 0))
    return pl.pallas_call(
        _sel_win_kernel,
        out_shape=jax.ShapeDtypeStruct((B, S, NSA_WIDTH), F32),
        grid=(B, S // T),
        in_specs=[pl.BlockSpec((1, T, NSA_WIDTH), lambda b, i: (b, i, 0)),
                  kspec, vspec, kspec, vspec,
                  pl.BlockSpec((1, T, LANES), lambda b, i: (b, i, glblk)),
                  pl.BlockSpec((1, T, sel.shape[2]), lambda b, i: (b, i, 0)),
                  pl.BlockSpec((1, T, NSA_WIDTH), lambda b, i: (b, i, 0)),
                  pl.BlockSpec(bias_tiles.shape, lambda b, i: (0, 0, 0, 0)),
                  pl.BlockSpec(expand.shape, lambda b, i: (0, 0)),
                  pl.BlockSpec((1, LANES), lambda b, i: (0, 0))],
        out_specs=pl.BlockSpec((1, T, NSA_WIDTH), lambda b, i: (b, i, 0)),
        compiler_params=_cparams(("arbitrary", "arbitrary")),
        name="nsa_sel_win",
    )(qn, *kv, y, sel, oc, bias_tiles, expand, bg)


def _odd_mixer(x, sh, sc, gate, ng, w_in, b_gate, q_g, k_g, cmp_pos, cmp_w1, cmp_w2, bias_tiles, cmp_bias,
               w_out):
    B, S, _ = x.shape
    G = NSA_GROUPS
    w = jnp.zeros((D_MODEL, ODD_IN_PAD), BF16).at[:, :ODD_IN].set(w_in.astype(BF16))
    y = _norm_proj(x, sh, sc, ng, w)
    xr = y[:, :, NSA_WIDTH:NSA_WIDTH + 2 * NSA_KV].reshape(B, S, 2, G, HEAD_DIM)
    xr = xr.transpose(0, 2, 3, 1, 4).reshape(B, 2 * G, S // CMP_STRIDE, CMP_STRIDE * HEAD_DIM)
    kcvc = _nsa_compress(xr, cmp_pos.reshape(2, CMP_LEN * HEAD_DIM), cmp_w1, cmp_w2, k_g[0])
    n_slc = S // SLC_LEN
    starts = jnp.arange(S // CMP_STRIDE) * CMP_STRIDE
    bstart = jnp.arange(n_slc) * SLC_LEN
    real = (jnp.arange(S // CMP_STRIDE) < (S - CMP_LEN) // CMP_STRIDE + 1)[:, None]
    overlap = (real & (starts[:, None] < bstart[None, :] + SLC_LEN)
               & (starts[:, None] + CMP_LEN > bstart[None, :])).astype(F32)
    overlap = jnp.stack([jnp.pad(overlap, ((0, 0), (g * n_slc, (G - 1 - g) * n_slc))) for g in range(G)])
    expand = (jnp.arange(S)[None, :] // SLC_LEN == jnp.arange(n_slc)[:, None]).astype(BF16)
    oc, sel, qn = _nsa_cmp_attn(y, kcvc, overlap, cmp_bias, q_g)
    o = _nsa_sel_win(qn, y, _nsa_kv_prep(y, k_g), sel, oc, bias_tiles, expand, b_gate)
    return _out_proj(x, o, 0, o, 1, w_out.astype(BF16), gate)


def _even_mixer(x, sh, sc, gate, ng, w_in, b_f, conv_w, q_g, k_g, w_out):
    w = jnp.zeros((D_MODEL, EVEN_IN_PAD), BF16).at[:, :EVEN_IN].set(w_in.astype(BF16))
    y = _norm_proj(x, sh, sc, ng, w)
    conv = _short_conv(y, conv_w)
    attn = _fox_attention(y, b_f, q_g, k_g)
    return _out_proj(x, conv, 0, attn, 0, w_out.astype(BF16), gate)


def kernel(x, c, ada_w, ada_b, norm_g, even_w_in, even_b_f, even_conv_w, even_q_g, even_k_g, even_w_out, odd_w_in, odd_b_gate, odd_q_g, odd_k_g, odd_cmp_pos, odd_cmp_w1, odd_cmp_w2, odd_w_out, rel_table, peer_w_q, peer_keys, peer_u, peer_v):
    B = x.shape[0]
    mods = _ada(c, ada_w, ada_b)
    bias_tiles, cmp_bias = _relbias(rel_table)
    for layer in range(DEPTH):
        sh1, sc1, g1, sh2, sc2, g2 = [m.reshape(B, 1, D_MODEL) for m in jnp.split(mods[layer], 6, axis=-1)]
        i = layer // 2
        ng1 = norm_g[layer, 0].reshape(1, D_MODEL)
        ng2 = norm_g[layer, 1].reshape(1, D_MODEL)
        if layer % 2 == 0:
            x = _even_mixer(x, sh1, sc1, g1, ng1, even_w_in[i], even_b_f[i], even_conv_w[i], even_q_g[i],
                            even_k_g[i], even_w_out[i])
        else:
            x = _odd_mixer(x, sh1, sc1, g1, ng1, odd_w_in[i], odd_b_gate[i], odd_q_g[i], odd_k_g[i],
                           odd_cmp_pos[i], odd_cmp_w1[i], odd_cmp_w2[i], bias_tiles, cmp_bias, odd_w_out[i])
        x = _peer(x, sh2, sc2, g2, ng2, peer_w_q[layer], peer_keys[layer],
                  peer_u[layer].astype(BF16), peer_v[layer].T.astype(BF16))
    return x
```

```python
import functools
import math

import jax
import jax.numpy as jnp
from jax import lax
from jax.experimental import pallas as pl
from jax.experimental.pallas import tpu as pltpu

D_MODEL = 1024
DEPTH = 4
HEAD_DIM = 64
CONV_CH = 512
CONV_K = 3
FOX_HEADS = 8
FOX_WIDTH = FOX_HEADS * HEAD_DIM
EVEN_IN = 3 * CONV_CH + 3 * FOX_WIDTH + FOX_HEADS
NSA_GROUPS = 4
NSA_HPG = 4
NSA_HEADS = NSA_GROUPS * NSA_HPG
NSA_WIDTH = NSA_HEADS * HEAD_DIM
NSA_KV = NSA_GROUPS * HEAD_DIM
CMP_LEN = 32
CMP_STRIDE = 16
SLC_LEN = 64
SLC_TOPN = 16
WINDOW = 512
ODD_IN = NSA_WIDTH + 6 * NSA_KV + 3 * NSA_HEADS
REL_BUCKETS = 32
REL_MAX_DIST = 128
PEER_HEADS = 8
PEER_NKEYS = 128
PEER_EXPERTS = PEER_NKEYS * PEER_NKEYS
PEER_QDIM = 128
PEER_TOPK = 16
EPS = 1e-6
NEG_INF = -1e30
FORCE_SCORE = 1e6

LANES = 128
F32 = jnp.float32
BF16 = jnp.bfloat16
HIGHEST = lax.Precision.HIGHEST
VMEM_LIMIT = 56 * 1024 * 1024

EVEN_IN_PAD = -(-EVEN_IN // LANES) * LANES
ODD_IN_PAD = -(-ODD_IN // LANES) * LANES
ROW_TILE = 512
ATT_TILE = 128
FOX_TILE = 256
PEER_TN = 512
PEER_EC = 2048
SCALE = HEAD_DIM ** -0.5


def _cparams(sem):
    return pltpu.CompilerParams(dimension_semantics=sem, vmem_limit_bytes=VMEM_LIMIT)


def _dot(a, b, precision=None):
    return jnp.dot(a, b, precision=precision, preferred_element_type=F32)


def _dot_nt(a, b, precision=None):
    return lax.dot_general(a, b, (((1,), (1,)), ((), ())), precision=precision,
                           preferred_element_type=F32)


def _rms(x, g):
    return x * lax.rsqrt(jnp.mean(x * x, axis=-1, keepdims=True) + EPS) * g


def _ada_kernel(c_ref, w_ref, b_ref, o_ref):
    c = c_ref[...]
    sc = c * jax.nn.sigmoid(c)
    o_ref[0] = _dot(sc, w_ref[0], HIGHEST) + b_ref[0]


def _ada(c, ada_w, ada_b):
    B, D = c.shape
    E = ada_w.shape[2]
    tn = 2048
    return pl.pallas_call(
        _ada_kernel,
        out_shape=jax.ShapeDtypeStruct((DEPTH, B, E), F32),
        grid=(DEPTH, E // tn),
        in_specs=[pl.BlockSpec((B, D), lambda l, j: (0, 0)),
                  pl.BlockSpec((1, D, tn), lambda l, j: (l, 0, j)),
                  pl.BlockSpec((1, 1, tn), lambda l, j: (l, 0, j))],
        out_specs=pl.BlockSpec((1, B, tn), lambda l, j: (l, 0, j)),
        compiler_params=_cparams(("arbitrary", "arbitrary")),
        name="ada_mod",
    )(c, ada_w, ada_b.reshape(DEPTH, 1, E))


def _norm_proj_kernel(x_ref, sh_ref, sc_ref, ng_ref, w_ref, y_ref):
    h = _rms(x_ref[0], ng_ref[...]) * (1.0 + sc_ref[0]) + sh_ref[0]
    y_ref[0] = _dot(h.astype(BF16), w_ref[...])


def _norm_proj(x, sh, sc, ng, w):
    B, S, D = x.shape
    N = w.shape[1]
    return pl.pallas_call(
        _norm_proj_kernel,
        out_shape=jax.ShapeDtypeStruct((B, S, N), F32),
        grid=(B, S // ROW_TILE),
        in_specs=[pl.BlockSpec((1, ROW_TILE, D), lambda b, i: (b, i, 0)),
                  pl.BlockSpec((1, 1, D), lambda b, i: (b, 0, 0)),
                  pl.BlockSpec((1, 1, D), lambda b, i: (b, 0, 0)),
                  pl.BlockSpec((1, D), lambda b, i: (0, 0)),
                  pl.BlockSpec((D, N), lambda b, i: (0, 0))],
        out_specs=pl.BlockSpec((1, ROW_TILE, N), lambda b, i: (b, i, 0)),
        compiler_params=_cparams(("arbitrary", "arbitrary")),
        name="norm_proj",
    )(x, sh, sc, ng, w)


def _out_proj_kernel(x_ref, a_ref, b_ref, wa_ref, wb_ref, g_ref, o_ref):
    y = _dot(a_ref[0].astype(BF16), wa_ref[...]) + _dot(b_ref[0].astype(BF16), wb_ref[...])
    o_ref[0] = x_ref[0] + g_ref[0] * y


def _out_proj(x, a, a_blk, b, b_blk, w, gate):
    B, S, D = x.shape
    half = w.shape[0] // 2
    wa, wb = w[:half], w[half:]
    return pl.pallas_call(
        _out_proj_kernel,
        out_shape=jax.ShapeDtypeStruct((B, S, D), F32),
        grid=(B, S // ROW_TILE),
        in_specs=[pl.BlockSpec((1, ROW_TILE, D), lambda b, i: (b, i, 0)),
                  pl.BlockSpec((1, ROW_TILE, half), lambda b, i: (b, i, a_blk)),
                  pl.BlockSpec((1, ROW_TILE, half), lambda b, i: (b, i, b_blk)),
                  pl.BlockSpec((half, D), lambda b, i: (0, 0)),
                  pl.BlockSpec((half, D), lambda b, i: (0, 0)),
                  pl.BlockSpec((1, 1, D), lambda b, i: (b, 0, 0))],
        out_specs=pl.BlockSpec((1, ROW_TILE, D), lambda b, i: (b, i, 0)),
        compiler_params=_cparams(("arbitrary", "arbitrary")),
        name="out_proj",
    )(x, a, b, wa, wb, gate)


def _conv_kernel(cb_ref, cc_ref, ch_ref, w_ref, o_ref):
    u = cc_ref[0] * ch_ref[0]
    row = lax.broadcasted_iota(jnp.int32, u.shape, 0)
    u1 = jnp.where(row >= 1, pltpu.roll(u, 1, axis=0), 0.0)
    u2 = jnp.where(row >= 2, pltpu.roll(u, 2, axis=0), 0.0)
    w = w_ref[...]
    o_ref[0] = cb_ref[0] * (w[0:1] * u2 + w[1:2] * u1 + w[2:3] * u)


def _short_conv(y, conv_w):
    B, S, _ = y.shape
    nblk = CONV_CH // LANES
    w = jnp.zeros((8, CONV_CH), F32).at[:CONV_K].set(conv_w)
    return pl.pallas_call(
        _conv_kernel,
        out_shape=jax.ShapeDtypeStruct((B, S, CONV_CH), F32),
        grid=(B, nblk),
        in_specs=[pl.BlockSpec((1, S, LANES), lambda b, j: (b, 0, j)),
                  pl.BlockSpec((1, S, LANES), lambda b, j: (b, 0, nblk + j)),
                  pl.BlockSpec((1, S, LANES), lambda b, j: (b, 0, 2 * nblk + j)),
                  pl.BlockSpec((8, LANES), lambda b, j: (0, j))],
        out_specs=pl.BlockSpec((1, S, LANES), lambda b, j: (b, 0, j)),
        compiler_params=_cparams(("arbitrary", "arbitrary")),
        name="short_conv",
    )(y, y, y, w)


def _cumsum_rows(x):
    n = x.shape[0]
    row = lax.broadcasted_iota(jnp.int32, x.shape, 0)
    sh = 1
    while sh < n:
        x = x + jnp.where(row >= sh, pltpu.roll(x, sh, axis=0), 0.0)
        sh *= 2
    return x


FOX_KEY_PARTS = 4


def _fox_kernel(q_ref, k_ref, v_ref, f_ref, bf_ref, qg_ref, kg_ref, o_ref,
                kn_scr, vb_scr, cum_scr, cumrow_scr):
    qi = pl.program_id(1)
    S = k_ref.shape[1]
    T = q_ref.shape[1]
    nq = S // T

    @pl.when(qi == 0)
    def _prep():
        ones_col = (lax.broadcasted_iota(jnp.int32, (S, LANES - HEAD_DIM), 1) == 0).astype(BF16)
        for h in range(FOX_HEADS):
            cols = slice(h * HEAD_DIM, (h + 1) * HEAD_DIM)
            kn_scr[:, cols] = _rms(k_ref[0, :, cols], kg_ref[...]).astype(BF16)
            vb_scr[:, h * LANES:h * LANES + HEAD_DIM] = v_ref[0, :, cols].astype(BF16)
            vb_scr[:, h * LANES + HEAD_DIM:(h + 1) * LANES] = ones_col
        logf = jax.nn.log_sigmoid(f_ref[0] + bf_ref[...])
        cum = _cumsum_rows(logf)
        cum_scr[...] = cum
        for kt in range(S // T):
            cumrow_scr[:, kt * T:(kt + 1) * T] = cum[kt * T:(kt + 1) * T, :].T[0:8, :]

    cq = cum_scr[pl.ds(pl.multiple_of(qi * T, T), T), :]

    def attend(nkeys):
        row = qi * T + lax.broadcasted_iota(jnp.int32, (T, nkeys), 0)
        causal = jnp.where(lax.broadcasted_iota(jnp.int32, (T, nkeys), 1) <= row, 0.0, NEG_INF)
        for h in range(FOX_HEADS):
            cols = slice(h * HEAD_DIM, (h + 1) * HEAD_DIM)
            qh = (_rms(q_ref[0, :, cols], qg_ref[...]) * SCALE).astype(BF16)
            s = _dot_nt(qh, kn_scr[0:nkeys, cols]) + ((cq[:, h:h + 1] - cumrow_scr[h:h + 1, 0:nkeys]) + causal)
            p = jnp.exp(s - jnp.max(s, axis=-1, keepdims=True))
            acc = _dot(p.astype(BF16), vb_scr[0:nkeys, h * LANES:(h + 1) * LANES])
            o_ref[0, :, cols] = acc[:, :HEAD_DIM] / acc[:, HEAD_DIM:HEAD_DIM + 1]

    parts = min(FOX_KEY_PARTS, nq)
    per = nq // parts
    for part in range(parts):
        pl.when((qi >= part * per) & (qi < (part + 1) * per))(functools.partial(attend, (part + 1) * (S // parts)))


def _fox_attention(y, b_f, q_g, k_g):
    B, S, _ = y.shape
    T = FOX_TILE
    qblk = 3 * CONV_CH // FOX_WIDTH
    fblk = (3 * CONV_CH + 3 * FOX_WIDTH) // LANES
    bf = jnp.zeros((1, LANES), F32).at[0, :FOX_HEADS].set(b_f)
    return pl.pallas_call(
        _fox_kernel,
        out_shape=jax.ShapeDtypeStruct((B, S, FOX_WIDTH), F32),
        grid=(B, S // T),
        in_specs=[pl.BlockSpec((1, T, FOX_WIDTH), lambda b, i: (b, i, qblk)),
                  pl.BlockSpec((1, S, FOX_WIDTH), lambda b, i: (b, 0, qblk + 1)),
                  pl.BlockSpec((1, S, FOX_WIDTH), lambda b, i: (b, 0, qblk + 2)),
                  pl.BlockSpec((1, S, LANES), lambda b, i: (b, 0, fblk)),
                  pl.BlockSpec((1, LANES), lambda b, i: (0, 0)),
                  pl.BlockSpec((1, HEAD_DIM), lambda b, i: (0, 0)),
                  pl.BlockSpec((1, HEAD_DIM), lambda b, i: (0, 0))],
        out_specs=pl.BlockSpec((1, T, FOX_WIDTH), lambda b, i: (b, i, 0)),
        scratch_shapes=[pltpu.VMEM((S, FOX_WIDTH), BF16),
                        pltpu.VMEM((S, FOX_HEADS * LANES), BF16),
                        pltpu.VMEM((S, LANES), F32),
                        pltpu.VMEM((8, S), F32)],
        compiler_params=_cparams(("arbitrary", "arbitrary")),
        name="fox_attention",
    )(y, y, y, y, bf, q_g.reshape(1, HEAD_DIM), k_g.reshape(1, HEAD_DIM))


def _gelu_tanh(x):
    c0 = math.sqrt(2.0 / math.pi)
    return (0.5 * x) * (1.0 + jnp.tanh(x * (c0 + (c0 * 0.044715) * (x * x))))


def _extract_top(work, iota, k, exact):
    nrows = work.shape[0]
    vals = []
    for _ in range(k):
        m = jnp.max(work, axis=0, keepdims=True)
        if exact:
            idx = jnp.min(jnp.where(work == m, iota, nrows), axis=0, keepdims=True)
            work = jnp.where(iota == idx, -jnp.inf, work)
        else:
            work = jnp.where(work == m, -jnp.inf, work)
        vals.append(m)
    return vals, work


def _removed(work):
    return jnp.sum(jnp.where(work == -jnp.inf, 1.0, 0.0), axis=0, keepdims=True)


def _peer_kernel(x_ref, sh_ref, sc_ref, gt_ref, ng_ref, wqh_ref, wql_ref, keys_ref, u_ref, vt_ref, o_ref,
                 ht_scr, qt_scr, sp_scr, t0_scr, t1_scr, tau_scr, a_scr, p_scr, acc_scr):
    c = pl.program_id(2)
    nc = pl.num_programs(2)
    tn = x_ref.shape[1]
    nlt = tn // LANES
    half = PEER_QDIM // 2

    @pl.when(c == 0)
    def _prep():
        h = _rms(x_ref[0], ng_ref[...]) * (1.0 + sc_ref[0]) + sh_ref[0]
        hh = h.astype(BF16)
        hl = (h - hh.astype(F32)).astype(BF16)
        cw = 2 * LANES
        for k in range(h.shape[1] // cw):
            cs = slice(k * cw, (k + 1) * cw)
            q = _dot(hh, wqh_ref[:, cs]) + (_dot(hh, wql_ref[:, cs]) + _dot(hl, wqh_ref[:, cs]))
            qt_scr[cs, :] = q.T
            ht_scr[cs, :] = h[:, cs].T.astype(BF16)
        acc_scr[...] = jnp.zeros_like(acc_scr)
        iota_k = lax.broadcasted_iota(jnp.int32, (PEER_NKEYS, LANES), 0)
        iota_c = lax.broadcasted_iota(jnp.int32, (PEER_TOPK + 8 * 8, LANES), 0)

        n_pad = sum(8 - PEER_TOPK // (r0 + 1) for r0 in range(2, 8))

        def select(hd, lt, exact):
            lanes = slice(lt * LANES, (lt + 1) * LANES)
            s0 = sp_scr[0, :, lanes]
            s1 = sp_scr[1, :, lanes]
            sv0, w0 = _extract_top(s0, iota_k, PEER_TOPK, exact)
            sv1, w1 = _extract_top(s1, iota_k, PEER_TOPK, exact)
            s0m = jnp.where(w0 == -jnp.inf, s0, -jnp.inf)
            s1m = jnp.where(w1 == -jnp.inf, s1, -jnp.inf)
            sv0_all = jnp.concatenate(sv0, axis=0)
            sv1_all = jnp.concatenate(sv1, axis=0)
            parts = [sv0[0] + sv1_all, sv0[1] + sv1_all[:8]]
            for r0 in range(2, 8):
                parts.append(jnp.where(iota_c[:8] < PEER_TOPK // (r0 + 1), sv0[r0] + sv1_all[:8], -jnp.inf))
            parts.append(sv0_all[8:] + sv1[0])
            cv, wc = _extract_top(jnp.concatenate(parts, axis=0), iota_c, PEER_TOPK, exact)
            z = cv[0] - cv[0]
            for r in range(PEER_TOPK):
                z = z + jnp.exp(cv[r] - cv[0])
            nrt = PEER_NKEYS // 8
            t0_scr[hd, lt, :, 0] = s0m.reshape(nrt, 8, LANES)
            t0_scr[hd, lt, :, 1] = jnp.exp(s0m - sv0[0]).reshape(nrt, 8, LANES)
            t1_scr[hd, lt, :, 0] = s1m.reshape(nrt, 8, LANES)
            t1_scr[hd, lt, :, 1] = (jnp.exp(s1m - sv1[0]) / z).reshape(nrt, 8, LANES)
            tau_scr[hd, lt] = jnp.broadcast_to(cv[PEER_TOPK - 1], (8, LANES))
            return ((_removed(w0) != PEER_TOPK) | (_removed(w1) != PEER_TOPK)
                    | (_removed(wc) != PEER_TOPK + n_pad))

        def head_body(hd, carry):
            for p in range(2):
                qs = qt_scr[pl.ds(pl.multiple_of((hd * 2 + p) * half, half), half), :]
                sp_scr[p] = _dot(keys_ref[p], qs, HIGHEST)
            tied = jnp.zeros((1, LANES), F32)
            for lt in range(nlt):
                tied = jnp.maximum(tied, jnp.where(select(hd, lt, False), 1.0, 0.0))

            @pl.when(jnp.max(tied) > 0.0)
            def _redo():
                for lt in range(nlt):
                    select(hd, lt, True)
            return carry

        lax.fori_loop(0, PEER_HEADS, head_body, 0)

    nsub = 8
    nrq = 32
    nparts = PEER_EC // PEER_NKEYS // nsub
    for part in range(nparts):
        er = slice(part * nsub * PEER_NKEYS, (part + 1) * nsub * PEER_NKEYS)
        for lh in range(nlt // 2):
            lanes2 = slice(lh * 2 * LANES, (lh + 1) * 2 * LANES)
            a_scr[er, lanes2] = _gelu_tanh(_dot(u_ref[er, :], ht_scr[:, lanes2]))

    def weights_tile(part, lt, rq, prev):
        lanes = slice(lt * LANES, (lt + 1) * LANES)
        jt = slice(rq * nrq // 8, (rq + 1) * nrq // 8)
        blk = c * (PEER_EC // PEER_NKEYS // 8) + part * nsub // 8
        zero = 0.0 if prev is None else jnp.where(prev != prev, prev, 0.0)
        w = [jnp.zeros((nrq, LANES), F32) for _ in range(nsub)]
        for hd in range(PEER_HEADS):
            s1 = t1_scr[hd, lt, jt, 0].reshape(nrq, LANES)
            e1 = t1_scr[hd, lt, jt, 1].reshape(nrq, LANES)
            s0 = t0_scr[hd, lt, blk, 0] + zero
            e0 = t0_scr[hd, lt, blk, 1] + zero
            tau = tau_scr[hd, lt, 0:1, :]
            for ii in range(nsub):
                keep = s1 + s0[ii:ii + 1] >= tau
                w[ii] = w[ii] + jnp.where(keep, e1 * e0[ii:ii + 1], 0.0)
        for ii in range(nsub):
            rows = slice((part * nsub + ii) * PEER_NKEYS + rq * nrq, (part * nsub + ii) * PEER_NKEYS + (rq + 1) * nrq)
            w[ii] = w[ii] * a_scr[rows, lanes]
            p_scr[rows, lanes] = w[ii].astype(BF16)
        return w[nsub - 1][0:1, :]

    prev = None
    for part in range(nparts):
        for lt in range(nlt):
            for rq in range(PEER_NKEYS // nrq):
                prev = weights_tile(part, lt, rq, prev)
    acc_scr[...] += _dot(vt_ref[...], p_scr[...])

    @pl.when(c == nc - 1)
    def _fin():
        o_ref[0] = x_ref[0] + gt_ref[0] * acc_scr[...].T


def _peer(x, sh, sc, gate, ng, w_q, keys, u_b, vt_b):
    B, S, D = x.shape
    tn = PEER_TN
    nc = PEER_EXPERTS // PEER_EC
    tab = pltpu.VMEM((PEER_HEADS, tn // LANES, PEER_NKEYS // 8, 2, 8, LANES), F32)
    wq_hi = w_q.astype(BF16)
    wq_lo = (w_q - wq_hi.astype(F32)).astype(BF16)
    wspec = pl.BlockSpec((D, PEER_HEADS * PEER_QDIM), lambda b, i, c: (0, 0), pipeline_mode=pl.Buffered(1))
    return pl.pallas_call(
        _peer_kernel,
        out_shape=jax.ShapeDtypeStruct((B, S, D), F32),
        grid=(B, S // tn, nc),
        in_specs=[pl.BlockSpec((1, tn, D), lambda b, i, c: (b, i, 0)),
                  pl.BlockSpec((1, 1, D), lambda b, i, c: (b, 0, 0)),
                  pl.BlockSpec((1, 1, D), lambda b, i, c: (b, 0, 0)),
                  pl.BlockSpec((1, 1, D), lambda b, i, c: (b, 0, 0)),
                  pl.BlockSpec((1, D), lambda b, i, c: (0, 0)),
                  wspec, wspec,
                  pl.BlockSpec((2, PEER_NKEYS, PEER_QDIM // 2), lambda b, i, c: (0, 0, 0)),
                  pl.BlockSpec((PEER_EC, D), lambda b, i, c: (c, 0)),
                  pl.BlockSpec((D, PEER_EC), lambda b, i, c: (0, c))],
        out_specs=pl.BlockSpec((1, tn, D), lambda b, i, c: (b, i, 0)),
        scratch_shapes=[pltpu.VMEM((D, tn), BF16),
                        pltpu.VMEM((PEER_HEADS * PEER_QDIM, tn), F32),
                        pltpu.VMEM((2, PEER_NKEYS, tn), F32),
                        tab, tab,
                        pltpu.VMEM((PEER_HEADS, tn // LANES, 8, LANES), F32),
                        pltpu.VMEM((PEER_EC, tn), F32),
                        pltpu.VMEM((PEER_EC, tn), BF16),
                        pltpu.VMEM((D, tn), F32)],
        compiler_params=_cparams(("arbitrary", "arbitrary", "arbitrary")),
        name="peer",
    )(x, sh, sc, gate, ng, wq_hi, wq_lo, keys, u_b, vt_b)


def _rel_bucket(dist):
    n = jnp.maximum(dist, 0)
    max_exact = REL_BUCKETS // 2
    nf = jnp.maximum(n, 1).astype(F32)
    large = max_exact + (jnp.log(nf / max_exact) / math.log(REL_MAX_DIST / max_exact)
                         * (REL_BUCKETS - max_exact)).astype(jnp.int32)
    large = jnp.minimum(large, REL_BUCKETS - 1)
    return jnp.where(n < max_exact, n, large)


def _bias_lookup(bucket, tab_ref, h):
    bias = jnp.zeros(bucket.shape, F32)
    for b in range(REL_BUCKETS):
        bias = jnp.where(bucket == b, tab_ref[b, h], bias)
    return bias


N_BIAS_TILES = 5


def _relbias_kernel(tab_ref, o_ref, c_ref):
    h = pl.program_id(0)
    T = o_ref.shape[2]
    a = lax.broadcasted_iota(jnp.int32, (T, T), 0)
    b = lax.broadcasted_iota(jnp.int32, (T, T), 1)
    last = tab_ref[REL_BUCKETS - 1, h]
    o_ref[0, 0] = jnp.where(a >= b, _bias_lookup(_rel_bucket(a - b), tab_ref, h), NEG_INF)
    o_ref[0, 1] = _bias_lookup(_rel_bucket(a - b + T), tab_ref, h)
    o_ref[0, 2] = jnp.full((T, T), last, F32)
    o_ref[0, 3] = jnp.where(a < b, last, NEG_INF)
    o_ref[0, 4] = jnp.full((T, T), NEG_INF, F32)
    dist = a - ((b - T // 2) * CMP_STRIDE + CMP_LEN - 1)
    c_ref[0] = jnp.where(dist >= 0, _bias_lookup(_rel_bucket(dist), tab_ref, h), last)


def _relbias(rel_table):
    T = ATT_TILE
    assert 2 * T - (T - 1) >= REL_MAX_DIST
    assert WINDOW % T == 0
    return pl.pallas_call(
        _relbias_kernel,
        out_shape=(jax.ShapeDtypeStruct((NSA_HEADS, N_BIAS_TILES, T, T), F32),
                   jax.ShapeDtypeStruct((NSA_HEADS, T, T), F32)),
        grid=(NSA_HEADS,),
        in_specs=[pl.BlockSpec(memory_space=pltpu.SMEM)],
        out_specs=(pl.BlockSpec((1, N_BIAS_TILES, T, T), lambda h: (h, 0, 0, 0)),
                   pl.BlockSpec((1, T, T), lambda h: (h, 0, 0))),
        compiler_params=_cparams(("arbitrary",)),
        name="rel_bias_tiles",
    )(rel_table)


def _cmp_kernel(x_ref, pos_ref, w1_ref, w2_ref, kg_ref, o_ref):
    j = pl.program_id(1)
    half = CMP_STRIDE * HEAD_DIM
    x = x_ref[0, 0]
    pos = pos_ref[0]
    pa = _dot((x + pos[:, :half]).astype(BF16), w1_ref[0, :half, :])
    pb = _dot((x + pos[:, half:]).astype(BF16), w1_ref[0, half:, :])
    pre = pa + pltpu.roll(pb, pb.shape[0] - 1, axis=0)
    out = _dot(jax.nn.gelu(pre, approximate=True).astype(BF16), w2_ref[0])
    o_ref[0, 0] = jnp.where(j < NSA_GROUPS, _rms(out, kg_ref[...]), out)


def _nsa_compress(xr, pos, w1, w2, kg0):
    B, _, M, W = xr.shape
    G = NSA_GROUPS
    return pl.pallas_call(
        _cmp_kernel,
        out_shape=jax.ShapeDtypeStruct((B, 2 * G, M, HEAD_DIM), F32),
        grid=(B, 2 * G),
        in_specs=[pl.BlockSpec((1, 1, M, W), lambda b, j: (b, j, 0, 0)),
                  pl.BlockSpec((1, 1, 2 * W), lambda b, j: (j // G, 0, 0)),
                  pl.BlockSpec((1, 2 * W, HEAD_DIM), lambda b, j: (j // G, 0, 0)),
                  pl.BlockSpec((1, HEAD_DIM, HEAD_DIM), lambda b, j: (j // G, 0, 0)),
                  pl.BlockSpec((1, HEAD_DIM), lambda b, j: (0, 0))],
        out_specs=pl.BlockSpec((1, 1, M, HEAD_DIM), lambda b, j: (b, j, 0, 0)),
        compiler_params=_cparams(("arbitrary", "arbitrary")),
        name="nsa_compress",
    )(xr, pos.reshape(2, 1, 2 * W), w1.astype(BF16), w2.astype(BF16), kg0.reshape(1, HEAD_DIM))


def _cmp_attn_kernel(q_ref, kv_ref, ov_ref, cb_ref, qg_ref, oc_ref, sel_ref, qn_ref):
    qi = pl.program_id(1)
    T = q_ref.shape[1]
    NC = kv_ref.shape[2]
    G, J = NSA_GROUPS, NSA_HPG
    NS = sel_ref.shape[2] // G
    assert T == NC == LANES and T % CMP_STRIDE == 0 and G * NS == LANES
    t = qi * T + lax.broadcasted_iota(jnp.int32, (T, NC), 0)
    lane = lax.broadcasted_iota(jnp.int32, (T, NC), 1)
    vis = t - (lane * CMP_STRIDE + CMP_LEN - 1) >= 0
    visf = vis.astype(F32)
    shift = (qi * (T // CMP_STRIDE) + T // 2) % LANES
    imp = jnp.zeros((T, LANES), F32)
    vis4 = jnp.concatenate([vis] * J, axis=0)
    visf4 = jnp.concatenate([visf] * J, axis=0)
    for g in range(G):
        kc = kv_ref[0, g].astype(BF16)
        vc = kv_ref[0, G + g].astype(BF16)
        qhs = []
        for j in range(J):
            cols = slice((g * J + j) * HEAD_DIM, (g * J + j + 1) * HEAD_DIM)
            qhs.append((_rms(q_ref[0, :, cols], qg_ref[...]) * SCALE).astype(BF16))
            qn_ref[0, :, cols] = qhs[j]
        bias = jnp.concatenate([pltpu.roll(cb_ref[g * J + j], shift, axis=1) for j in range(J)], axis=0)
        s = jnp.where(vis4, _dot_nt(jnp.concatenate(qhs, axis=0), kc) + bias, NEG_INF)
        e = jnp.exp(s - jnp.max(s, axis=-1, keepdims=True))
        p = e / jnp.sum(e, axis=-1, keepdims=True) * visf4
        o = _dot(p.astype(BF16), vc)
        psum = p[0:T]
        for j in range(J):
            oc_ref[0, :, (g * J + j) * HEAD_DIM:(g * J + j + 1) * HEAD_DIM] = o[j * T:(j + 1) * T]
            if j:
                psum = psum + p[j * T:(j + 1) * T]
        imp = imp + _dot(psum, ov_ref[g], HIGHEST)
    mm = lane % NS
    cur = t // SLC_LEN
    forced = (mm == 0) | (mm == cur) | (mm == cur - 1)
    imp = jnp.where(forced, FORCE_SCORE, jnp.where(mm * SLC_LEN <= t, imp, -1.0))
    rank = jnp.zeros((T, LANES), F32)
    for r in range(1, NS):
        wrapped = mm >= NS - r
        other = jnp.where(wrapped, pltpu.roll(imp, NS - r, axis=1), pltpu.roll(imp, LANES - r, axis=1))
        rank = rank + jnp.where(other > imp, 1.0, jnp.where(other == imp, wrapped.astype(F32), 0.0))
    sel_ref[0] = jnp.where(rank < SLC_TOPN, jnp.where(imp >= 0.0, 1.0, 0.0), 0.0)


def _nsa_cmp_attn(y, kcvc, overlap, cmp_bias, q_g):
    B, S, _ = y.shape
    T = ATT_TILE
    NC = kcvc.shape[2]
    return pl.pallas_call(
        _cmp_attn_kernel,
        out_shape=(jax.ShapeDtypeStruct((B, S, NSA_WIDTH), F32),
                   jax.ShapeDtypeStruct((B, S, LANES), F32),
                   jax.ShapeDtypeStruct((B, S, NSA_WIDTH), BF16)),
        grid=(B, S // T),
        in_specs=[pl.BlockSpec((1, T, NSA_WIDTH), lambda b, i: (b, i, 0)),
                  pl.BlockSpec((1, 2 * NSA_GROUPS, NC, HEAD_DIM), lambda b, i: (b, 0, 0, 0)),
                  pl.BlockSpec(overlap.shape, lambda b, i: (0, 0, 0)),
                  pl.BlockSpec(cmp_bias.shape, lambda b, i: (0, 0, 0)),
                  pl.BlockSpec((1, HEAD_DIM), lambda b, i: (0, 0))],
        out_specs=(pl.BlockSpec((1, T, NSA_WIDTH), lambda b, i: (b, i, 0)),
                   pl.BlockSpec((1, T, LANES), lambda b, i: (b, i, 0)),
                   pl.BlockSpec((1, T, NSA_WIDTH), lambda b, i: (b, i, 0))),
        compiler_params=_cparams(("arbitrary", "arbitrary")),
        name="nsa_cmp_attn",
    )(y, kcvc, overlap, cmp_bias, q_g.reshape(1, HEAD_DIM))


def _kv_prep_kernel(ks_ref, kw_ref, kg1_ref, kg2_ref, ksn_ref, vs_ref, kwn_ref, vw_ref):
    KV = NSA_KV
    rows = ks_ref.shape[1]
    ones_col = (lax.broadcasted_iota(jnp.int32, (rows, LANES - HEAD_DIM), 1) == 0).astype(BF16)
    for g in range(NSA_GROUPS):
        cols = slice(g * HEAD_DIM, (g + 1) * HEAD_DIM)
        vcols = slice(KV + g * HEAD_DIM, KV + (g + 1) * HEAD_DIM)
        ksn_ref[0, :, cols] = _rms(ks_ref[0, :, cols], kg1_ref[...]).astype(BF16)
        kwn_ref[0, :, cols] = _rms(kw_ref[0, :, cols], kg2_ref[...]).astype(BF16)
        vs_ref[0, :, g * LANES:g * LANES + HEAD_DIM] = ks_ref[0, :, vcols].astype(BF16)
        vw_ref[0, :, g * LANES:g * LANES + HEAD_DIM] = kw_ref[0, :, vcols].astype(BF16)
        vs_ref[0, :, g * LANES + HEAD_DIM:(g + 1) * LANES] = ones_col
        vw_ref[0, :, g * LANES + HEAD_DIM:(g + 1) * LANES] = ones_col


def _nsa_kv_prep(y, k_g):
    B, S, _ = y.shape
    ksblk = (NSA_WIDTH + 2 * NSA_KV) // (2 * NSA_KV)
    kout = jax.ShapeDtypeStruct((B, S, NSA_KV), BF16)
    vout = jax.ShapeDtypeStruct((B, S, NSA_GROUPS * LANES), BF16)
    kspec = pl.BlockSpec((1, ROW_TILE, NSA_KV), lambda b, i: (b, i, 0))
    vspec = pl.BlockSpec((1, ROW_TILE, NSA_GROUPS * LANES), lambda b, i: (b, i, 0))
    return pl.pallas_call(
        _kv_prep_kernel,
        out_shape=(kout, vout, kout, vout),
        grid=(B, S // ROW_TILE),
        in_specs=[pl.BlockSpec((1, ROW_TILE, 2 * NSA_KV), lambda b, i: (b, i, ksblk)),
                  pl.BlockSpec((1, ROW_TILE, 2 * NSA_KV), lambda b, i: (b, i, ksblk + 1)),
                  pl.BlockSpec((1, HEAD_DIM), lambda b, i: (0, 0)),
                  pl.BlockSpec((1, HEAD_DIM), lambda b, i: (0, 0))],
        out_specs=(kspec, vspec, kspec, vspec),
        compiler_params=_cparams(("arbitrary", "arbitrary")),
        name="nsa_kv_prep",
    )(y, y, k_g[1].reshape(1, HEAD_DIM), k_g[2].reshape(1, HEAD_DIM))


SEL_KEY_PARTS = 4


def _sel_win_kernel(qn_ref, ks_ref, vs_ref, kw_ref, vw_ref, gl_ref, sel_ref, oc_ref, bias_ref, e_ref, bg_ref,
                    o_ref):
    qi = pl.program_id(1)
    T = qn_ref.shape[1]
    S = ks_ref.shape[1]
    nk = S // T
    G, J = NSA_GROUPS, NSA_HPG
    NS = e_ref.shape[0]
    wt = WINDOW // T
    nwin = wt + 1
    assert nk % SEL_KEY_PARTS == 0 and nk >= nwin

    def sel_tile(r):
        return jnp.where(r < 0, 4, jnp.minimum(r, 2))

    def win_tile(r):
        return jnp.where(r < 0, 4, jnp.where(r <= 1, r, jnp.where(r < wt, 2, jnp.where(r == wt, 3, 4))))

    def softmax_pv(s, v):
        p = jnp.exp(s - jnp.max(s, axis=-1, keepdims=True))
        acc = _dot(p.astype(BF16), v)
        return acc[:, :HEAD_DIM] / acc[:, HEAD_DIM:HEAD_DIM + 1]

    gates = jax.nn.sigmoid(gl_ref[0] + bg_ref[...])
    k0 = jnp.maximum(qi - wt, 0)
    wrows = pl.ds(pl.multiple_of(k0 * T, T), nwin * T)

    def attend(nkt):
        nkeys = nkt * T
        for g in range(G):
            gcols = slice(g * HEAD_DIM, (g + 1) * HEAD_DIM)
            vcols = slice(g * LANES, (g + 1) * LANES)
            q4 = jnp.concatenate(
                [qn_ref[0, :, (g * J + j) * HEAD_DIM:(g * J + j + 1) * HEAD_DIM] for j in range(J)], axis=0)
            selg = sel_ref[0, :, g * NS:(g + 1) * NS].astype(BF16)
            mask = (_dot(selg, e_ref[:, :nkeys]) - 1.0) * (-NEG_INF)
            bias = jnp.concatenate(
                [jnp.concatenate([bias_ref[g * J + j, sel_tile(qi - kt)] for kt in range(nkt)], axis=1) + mask
                 for j in range(J)], axis=0)
            o_s = softmax_pv(_dot_nt(q4, ks_ref[0, :nkeys, gcols]) + bias, vs_ref[0, :nkeys, vcols])
            wbias = jnp.concatenate(
                [jnp.concatenate([bias_ref[g * J + j, win_tile(qi - k0 - t)] for t in range(nwin)], axis=1)
                 for j in range(J)], axis=0)
            o_w = softmax_pv(_dot_nt(q4, kw_ref[0, wrows, gcols]) + wbias, vw_ref[0, wrows, vcols])
            for j in range(J):
                h = g * J + j
                cols = slice(h * HEAD_DIM, (h + 1) * HEAD_DIM)
                rows = slice(j * T, (j + 1) * T)
                o_ref[0, :, cols] = (gates[:, 3 * h:3 * h + 1] * oc_ref[0, :, cols]
                                     + gates[:, 3 * h + 1:3 * h + 2] * o_s[rows]
                                     + gates[:, 3 * h + 2:3 * h + 3] * o_w[rows])

    per = nk // SEL_KEY_PARTS
    for part in range(SEL_KEY_PARTS):
        pl.when((qi >= part * per) & (qi < (part + 1) * per))(functools.partial(attend, (part + 1) * per))


def _nsa_sel_win(qn, y, kv, sel, oc, bias_tiles, expand, b_gate):
    B, S, _ = y.shape
    T = ATT_TILE
    glblk = (NSA_WIDTH + 6 * NSA_KV) // LANES
    bg = jnp.zeros((1, LANES), F32).at[0, :3 * NSA_HEADS].set(b_gate)
    kspec = pl.BlockSpec((1, S, NSA_KV), lambda b, i: (b, 0, 0))
    vspec = pl.BlockSpec((1, S, NSA_GROUPS * LANES), lambda b, i: (b, 0, 0))
    return pl.pallas_call(
        _sel_win_kernel,
        out_shape=jax.ShapeDtypeStruct((B, S, NSA_WIDTH), F32),
        grid=(B, S // T),
        in_specs=[pl.BlockSpec((1, T, NSA_WIDTH), lambda b, i: (b, i, 0)),
                  kspec, vspec, kspec, vspec,
                  pl.BlockSpec((1, T, LANES), lambda b, i: (b, i, glblk)),
                  pl.BlockSpec((1, T, sel.shape[2]), lambda b, i: (b, i, 0)),
                  pl.BlockSpec((1, T, NSA_WIDTH), lambda b, i: (b, i, 0)),
                  pl.BlockSpec(bias_tiles.shape, lambda b, i: (0, 0, 0, 0)),
                  pl.BlockSpec(expand.shape, lambda b, i: (0, 0)),
                  pl.BlockSpec((1, LANES), lambda b, i: (0, 0))],
        out_specs=pl.BlockSpec((1, T, NSA_WIDTH), lambda b, i: (b, i, 0)),
        compiler_params=_cparams(("arbitrary", "arbitrary")),
        name="nsa_sel_win",
    )(qn, *kv, y, sel, oc, bias_tiles, expand, bg)


def _odd_mixer(x, sh, sc, gate, ng, w_in, b_gate, q_g, k_g, cmp_pos, cmp_w1, cmp_w2, bias_tiles, cmp_bias,
               w_out):
    B, S, _ = x.shape
    G = NSA_GROUPS
    w = jnp.zeros((D_MODEL, ODD_IN_PAD), BF16).at[:, :ODD_IN].set(w_in.astype(BF16))
    y = _norm_proj(x, sh, sc, ng, w)
    xr = y[:, :, NSA_WIDTH:NSA_WIDTH + 2 * NSA_KV].reshape(B, S, 2, G, HEAD_DIM)
    xr = xr.transpose(0, 2, 3, 1, 4).reshape(B, 2 * G, S // CMP_STRIDE, CMP_STRIDE * HEAD_DIM)
    kcvc = _nsa_compress(xr, cmp_pos.reshape(2, CMP_LEN * HEAD_DIM), cmp_w1, cmp_w2, k_g[0])
    n_slc = S // SLC_LEN
    starts = jnp.arange(S // CMP_STRIDE) * CMP_STRIDE
    bstart = jnp.arange(n_slc) * SLC_LEN
    real = (jnp.arange(S // CMP_STRIDE) < (S - CMP_LEN) // CMP_STRIDE + 1)[:, None]
    overlap = (real & (starts[:, None] < bstart[None, :] + SLC_LEN)
               & (starts[:, None] + CMP_LEN > bstart[None, :])).astype(F32)
    overlap = jnp.stack([jnp.pad(overlap, ((0, 0), (g * n_slc, (G - 1 - g) * n_slc))) for g in range(G)])
    expand = (jnp.arange(S)[None, :] // SLC_LEN == jnp.arange(n_slc)[:, None]).astype(BF16)
    oc, sel, qn = _nsa_cmp_attn(y, kcvc, overlap, cmp_bias, q_g)
    o = _nsa_sel_win(qn, y, _nsa_kv_prep(y, k_g), sel, oc, bias_tiles, expand, b_gate)
    return _out_proj(x, o, 0, o, 1, w_out.astype(BF16), gate)


def _even_mixer(x, sh, sc, gate, ng, w_in, b_f, conv_w, q_g, k_g, w_out):
    w = jnp.zeros((D_MODEL, EVEN_IN_PAD), BF16).at[:, :EVEN_IN].set(w_in.astype(BF16))
    y = _norm_proj(x, sh, sc, ng, w)
    conv = _short_conv(y, conv_w)
    attn = _fox_attention(y, b_f, q_g, k_g)
    return _out_proj(x, conv, 0, attn, 0, w_out.astype(BF16), gate)


def kernel(x, c, ada_w, ada_b, norm_g, even_w_in, even_b_f, even_conv_w, even_q_g, even_k_g, even_w_out, odd_w_in, odd_b_gate, odd_q_g, odd_k_g, odd_cmp_pos, odd_cmp_w1, odd_cmp_w2, odd_w_out, rel_table, peer_w_q, peer_keys, peer_u, peer_v):
    B = x.shape[0]
    mods = _ada(c, ada_w, ada_b)
    bias_tiles, cmp_bias = _relbias(rel_table)
    for layer in range(DEPTH):
        sh1, sc1, g1, sh2, sc2, g2 = [m.reshape(B, 1, D_MODEL) for m in jnp.split(mods[layer], 6, axis=-1)]
        i = layer // 2
        ng1 = norm_g[layer, 0].reshape(1, D_MODEL)
        ng2 = norm_g[layer, 1].reshape(1, D_MODEL)
        if layer % 2 == 0:
            x = _even_mixer(x, sh1, sc1, g1, ng1, even_w_in[i], even_b_f[i], even_conv_w[i], even_q_g[i],
                            even_k_g[i], even_w_out[i])
        else:
            x = _odd_mixer(x, sh1, sc1, g1, ng1, odd_w_in[i], odd_b_gate[i], odd_q_g[i], odd_k_g[i],
                           odd_cmp_pos[i], odd_cmp_w1[i], odd_cmp_w2[i], bias_tiles, cmp_bias, odd_w_out[i])
        x = _peer(x, sh2, sc2, g2, ng2, peer_w_q[layer], peer_keys[layer],
                  peer_u[layer].astype(BF16), peer_v[layer].T.astype(BF16))
    return x
```

```python
import functools
import math

import jax
import jax.numpy as jnp
from jax import lax
from jax.experimental import pallas as pl
from jax.experimental.pallas import tpu as pltpu

D_MODEL = 1024
DEPTH = 4
HEAD_DIM = 64
CONV_CH = 512
CONV_K = 3
FOX_HEADS = 8
FOX_WIDTH = FOX_HEADS * HEAD_DIM
EVEN_IN = 3 * CONV_CH + 3 * FOX_WIDTH + FOX_HEADS
NSA_GROUPS = 4
NSA_HPG = 4
NSA_HEADS = NSA_GROUPS * NSA_HPG
NSA_WIDTH = NSA_HEADS * HEAD_DIM
NSA_KV = NSA_GROUPS * HEAD_DIM
CMP_LEN = 32
CMP_STRIDE = 16
SLC_LEN = 64
SLC_TOPN = 16
WINDOW = 512
ODD_IN = NSA_WIDTH + 6 * NSA_KV + 3 * NSA_HEADS
REL_BUCKETS = 32
REL_MAX_DIST = 128
PEER_HEADS = 8
PEER_NKEYS = 128
PEER_EXPERTS = PEER_NKEYS * PEER_NKEYS
PEER_QDIM = 128
PEER_TOPK = 16
EPS = 1e-6
NEG_INF = -1e30
FORCE_SCORE = 1e6

LANES = 128
F32 = jnp.float32
BF16 = jnp.bfloat16
HIGHEST = lax.Precision.HIGHEST
VMEM_LIMIT = 56 * 1024 * 1024

EVEN_IN_PAD = -(-EVEN_IN // LANES) * LANES
ODD_IN_PAD = -(-ODD_IN // LANES) * LANES
ROW_TILE = 512
ATT_TILE = 128
FOX_TILE = 256
PEER_TN = 512
PEER_EC = 2048
SCALE = HEAD_DIM ** -0.5


def _cparams(sem):
    return pltpu.CompilerParams(dimension_semantics=sem, vmem_limit_bytes=VMEM_LIMIT)


def _dot(a, b, precision=None):
    return jnp.dot(a, b, precision=precision, preferred_element_type=F32)


def _dot_nt(a, b, precision=None):
    return lax.dot_general(a, b, (((1,), (1,)), ((), ())), precision=precision,
                           preferred_element_type=F32)


def _rms(x, g):
    return x * lax.rsqrt(jnp.mean(x * x, axis=-1, keepdims=True) + EPS) * g


def _ada_kernel(c_ref, w_ref, b_ref, o_ref):
    c = c_ref[...]
    sc = c * jax.nn.sigmoid(c)
    o_ref[0] = _dot(sc, w_ref[0], HIGHEST) + b_ref[0]


def _ada(c, ada_w, ada_b):
    B, D = c.shape
    E = ada_w.shape[2]
    tn = 2048
    return pl.pallas_call(
        _ada_kernel,
        out_shape=jax.ShapeDtypeStruct((DEPTH, B, E), F32),
        grid=(DEPTH, E // tn),
        in_specs=[pl.BlockSpec((B, D), lambda l, j: (0, 0)),
                  pl.BlockSpec((1, D, tn), lambda l, j: (l, 0, j)),
                  pl.BlockSpec((1, 1, tn), lambda l, j: (l, 0, j))],
        out_specs=pl.BlockSpec((1, B, tn), lambda l, j: (l, 0, j)),
        compiler_params=_cparams(("arbitrary", "arbitrary")),
        name="ada_mod",
    )(c, ada_w, ada_b.reshape(DEPTH, 1, E))


def _norm_proj_kernel(x_ref, sh_ref, sc_ref, ng_ref, w_ref, y_ref):
    h = _rms(x_ref[0], ng_ref[...]) * (1.0 + sc_ref[0]) + sh_ref[0]
    y_ref[0] = _dot(h.astype(BF16), w_ref[...])


def _norm_proj(x, sh, sc, ng, w):
    B, S, D = x.shape
    N = w.shape[1]
    return pl.pallas_call(
        _norm_proj_kernel,
        out_shape=jax.ShapeDtypeStruct((B, S, N), F32),
        grid=(B, S // ROW_TILE),
        in_specs=[pl.BlockSpec((1, ROW_TILE, D), lambda b, i: (b, i, 0)),
                  pl.BlockSpec((1, 1, D), lambda b, i: (b, 0, 0)),
                  pl.BlockSpec((1, 1, D), lambda b, i: (b, 0, 0)),
                  pl.BlockSpec((1, D), lambda b, i: (0, 0)),
                  pl.BlockSpec((D, N), lambda b, i: (0, 0))],
        out_specs=pl.BlockSpec((1, ROW_TILE, N), lambda b, i: (b, i, 0)),
        compiler_params=_cparams(("arbitrary", "arbitrary")),
        name="norm_proj",
    )(x, sh, sc, ng, w)


def _out_proj_kernel(x_ref, a_ref, b_ref, wa_ref, wb_ref, g_ref, o_ref):
    y = _dot(a_ref[0].astype(BF16), wa_ref[...]) + _dot(b_ref[0].astype(BF16), wb_ref[...])
    o_ref[0] = x_ref[0] + g_ref[0] * y


def _out_proj(x, a, a_blk, b, b_blk, w, gate):
    B, S, D = x.shape
    half = w.shape[0] // 2
    wa, wb = w[:half], w[half:]
    return pl.pallas_call(
        _out_proj_kernel,
        out_shape=jax.ShapeDtypeStruct((B, S, D), F32),
        grid=(B, S // ROW_TILE),
        in_specs=[pl.BlockSpec((1, ROW_TILE, D), lambda b, i: (b, i, 0)),
                  pl.BlockSpec((1, ROW_TILE, half), lambda b, i: (b, i, a_blk)),
                  pl.BlockSpec((1, ROW_TILE, half), lambda b, i: (b, i, b_blk)),
                  pl.BlockSpec((half, D), lambda b, i: (0, 0)),
                  pl.BlockSpec((half, D), lambda b, i: (0, 0)),
                  pl.BlockSpec((1, 1, D), lambda b, i: (b, 0, 0))],
        out_specs=pl.BlockSpec((1, ROW_TILE, D), lambda b, i: (b, i, 0)),
        compiler_params=_cparams(("arbitrary", "arbitrary")),
        name="out_proj",
    )(x, a, b, wa, wb, gate)


def _conv_kernel(cb_ref, cc_ref, ch_ref, w_ref, o_ref):
    u = cc_ref[0] * ch_ref[0]
    row = lax.broadcasted_iota(jnp.int32, u.shape, 0)
    u1 = jnp.where(row >= 1, pltpu.roll(u, 1, axis=0), 0.0)
    u2 = jnp.where(row >= 2, pltpu.roll(u, 2, axis=0), 0.0)
    w = w_ref[...]
    o_ref[0] = cb_ref[0] * (w[0:1] * u2 + w[1:2] * u1 + w[2:3] * u)


def _short_conv(y, conv_w):
    B, S, _ = y.shape
    nblk = CONV_CH // LANES
    w = jnp.zeros((8, CONV_CH), F32).at[:CONV_K].set(conv_w)
    return pl.pallas_call(
        _conv_kernel,
        out_shape=jax.ShapeDtypeStruct((B, S, CONV_CH), F32),
        grid=(B, nblk),
        in_specs=[pl.BlockSpec((1, S, LANES), lambda b, j: (b, 0, j)),
                  pl.BlockSpec((1, S, LANES), lambda b, j: (b, 0, nblk + j)),
                  pl.BlockSpec((1, S, LANES), lambda b, j: (b, 0, 2 * nblk + j)),
                  pl.BlockSpec((8, LANES), lambda b, j: (0, j))],
        out_specs=pl.BlockSpec((1, S, LANES), lambda b, j: (b, 0, j)),
        compiler_params=_cparams(("arbitrary", "arbitrary")),
        name="short_conv",
    )(y, y, y, w)


def _cumsum_rows(x):
    n = x.shape[0]
    row = lax.broadcasted_iota(jnp.int32, x.shape, 0)
    sh = 1
    while sh < n:
        x = x + jnp.where(row >= sh, pltpu.roll(x, sh, axis=0), 0.0)
        sh *= 2
    return x


FOX_KEY_PARTS = 4


def _fox_kernel(q_ref, k_ref, v_ref, f_ref, bf_ref, qg_ref, kg_ref, o_ref,
                kn_scr, vb_scr, cum_scr, cumrow_scr):
    qi = pl.program_id(1)
    S = k_ref.shape[1]
    T = q_ref.shape[1]
    nq = S // T

    @pl.when(qi == 0)
    def _prep():
        ones_col = (lax.broadcasted_iota(jnp.int32, (S, LANES - HEAD_DIM), 1) == 0).astype(BF16)
        for h in range(FOX_HEADS):
            cols = slice(h * HEAD_DIM, (h + 1) * HEAD_DIM)
            kn_scr[:, cols] = _rms(k_ref[0, :, cols], kg_ref[...]).astype(BF16)
            vb_scr[:, h * LANES:h * LANES + HEAD_DIM] = v_ref[0, :, cols].astype(BF16)
            vb_scr[:, h * LANES + HEAD_DIM:(h + 1) * LANES] = ones_col
        logf = jax.nn.log_sigmoid(f_ref[0] + bf_ref[...])
        cum = _cumsum_rows(logf)
        cum_scr[...] = cum
        for kt in range(S // T):
            cumrow_scr[:, kt * T:(kt + 1) * T] = cum[kt * T:(kt + 1) * T, :].T[0:8, :]

    cq = cum_scr[pl.ds(pl.multiple_of(qi * T, T), T), :]

    def attend(nkeys):
        row = qi * T + lax.broadcasted_iota(jnp.int32, (T, nkeys), 0)
        causal = jnp.where(lax.broadcasted_iota(jnp.int32, (T, nkeys), 1) <= row, 0.0, NEG_INF)
        for h in range(FOX_HEADS):
            cols = slice(h * HEAD_DIM, (h + 1) * HEAD_DIM)
            qh = (_rms(q_ref[0, :, cols], qg_ref[...]) * SCALE).astype(BF16)
            s = _dot_nt(qh, kn_scr[0:nkeys, cols]) + ((cq[:, h:h + 1] - cumrow_scr[h:h + 1, 0:nkeys]) + causal)
            p = jnp.exp(s - jnp.max(s, axis=-1, keepdims=True))
            acc = _dot(p.astype(BF16), vb_scr[0:nkeys, h * LANES:(h + 1) * LANES])
            o_ref[0, :, cols] = acc[:, :HEAD_DIM] / acc[:, HEAD_DIM:HEAD_DIM + 1]

    parts = min(FOX_KEY_PARTS, nq)
    per = nq // parts
    for part in range(parts):
        pl.when((qi >= part * per) & (qi < (part + 1) * per))(functools.partial(attend, (part + 1) * (S // parts)))


def _fox_attention(y, b_f, q_g, k_g):
    B, S, _ = y.shape
    T = FOX_TILE
    qblk = 3 * CONV_CH // FOX_WIDTH
    fblk = (3 * CONV_CH + 3 * FOX_WIDTH) // LANES
    bf = jnp.zeros((1, LANES), F32).at[0, :FOX_HEADS].set(b_f)
    return pl.pallas_call(
        _fox_kernel,
        out_shape=jax.ShapeDtypeStruct((B, S, FOX_WIDTH), F32),
        grid=(B, S // T),
        in_specs=[pl.BlockSpec((1, T, FOX_WIDTH), lambda b, i: (b, i, qblk)),
                  pl.BlockSpec((1, S, FOX_WIDTH), lambda b, i: (b, 0, qblk + 1)),
                  pl.BlockSpec((1, S, FOX_WIDTH), lambda b, i: (b, 0, qblk + 2)),
                  pl.BlockSpec((1, S, LANES), lambda b, i: (b, 0, fblk)),
                  pl.BlockSpec((1, LANES), lambda b, i: (0, 0)),
                  pl.BlockSpec((1, HEAD_DIM), lambda b, i: (0, 0)),
                  pl.BlockSpec((1, HEAD_DIM), lambda b, i: (0, 0))],
        out_specs=pl.BlockSpec((1, T, FOX_WIDTH), lambda b, i: (b, i, 0)),
        scratch_shapes=[pltpu.VMEM((S, FOX_WIDTH), BF16),
                        pltpu.VMEM((S, FOX_HEADS * LANES), BF16),
                        pltpu.VMEM((S, LANES), F32),
                        pltpu.VMEM((8, S), F32)],
        compiler_params=_cparams(("arbitrary", "arbitrary")),
        name="fox_attention",
    )(y, y, y, y, bf, q_g.reshape(1, HEAD_DIM), k_g.reshape(1, HEAD_DIM))


def _gelu_tanh(x):
    c0 = math.sqrt(2.0 / math.pi)
    return (0.5 * x) * (1.0 + jnp.tanh(x * (c0 + (c0 * 0.044715) * (x * x))))


def _extract_top(work, iota, k, exact):
    nrows = work.shape[0]
    vals = []
    for _ in range(k):
        m = jnp.max(work, axis=0, keepdims=True)
        if exact:
            idx = jnp.min(jnp.where(work == m, iota, nrows), axis=0, keepdims=True)
            work = jnp.where(iota == idx, -jnp.inf, work)
        else:
            work = jnp.where(work == m, -jnp.inf, work)
        vals.append(m)
    return vals, work


def _removed(work):
    return jnp.sum(jnp.where(work == -jnp.inf, 1.0, 0.0), axis=0, keepdims=True)


def _peer_kernel(x_ref, sh_ref, sc_ref, gt_ref, ng_ref, wqh_ref, wql_ref, keys_ref, u_ref, vt_ref, o_ref,
                 ht_scr, qt_scr, sp_scr, t0_scr, t1_scr, tau_scr, a_scr, p_scr, acc_scr):
    c = pl.program_id(2)
    nc = pl.num_programs(2)
    tn = x_ref.shape[1]
    nlt = tn // LANES
    half = PEER_QDIM // 2

    @pl.when(c == 0)
    def _prep():
        h = _rms(x_ref[0], ng_ref[...]) * (1.0 + sc_ref[0]) + sh_ref[0]
        hh = h.astype(BF16)
        hl = (h - hh.astype(F32)).astype(BF16)
        cw = 2 * LANES
        for k in range(h.shape[1] // cw):
            cs = slice(k * cw, (k + 1) * cw)
            q = _dot(hh, wqh_ref[:, cs]) + (_dot(hh, wql_ref[:, cs]) + _dot(hl, wqh_ref[:, cs]))
            qt_scr[cs, :] = q.T
            ht_scr[cs, :] = h[:, cs].T.astype(BF16)
        acc_scr[...] = jnp.zeros_like(acc_scr)
        iota_k = lax.broadcasted_iota(jnp.int32, (PEER_NKEYS, LANES), 0)
        iota_c = lax.broadcasted_iota(jnp.int32, (PEER_TOPK + 8 * 8, LANES), 0)

        n_pad = sum(8 - PEER_TOPK // (r0 + 1) for r0 in range(2, 8))

        def select(hd, lt, exact):
            lanes = slice(lt * LANES, (lt + 1) * LANES)
            s0 = sp_scr[0, :, lanes]
            s1 = sp_scr[1, :, lanes]
            sv0, w0 = _extract_top(s0, iota_k, PEER_TOPK, exact)
            sv1, w1 = _extract_top(s1, iota_k, PEER_TOPK, exact)
            s0m = jnp.where(w0 == -jnp.inf, s0, -jnp.inf)
            s1m = jnp.where(w1 == -jnp.inf, s1, -jnp.inf)
            sv0_all = jnp.concatenate(sv0, axis=0)
            sv1_all = jnp.concatenate(sv1, axis=0)
            parts = [sv0[0] + sv1_all, sv0[1] + sv1_all[:8]]
            for r0 in range(2, 8):
                parts.append(jnp.where(iota_c[:8] < PEER_TOPK // (r0 + 1), sv0[r0] + sv1_all[:8], -jnp.inf))
            parts.append(sv0_all[8:] + sv1[0])
            cv, wc = _extract_top(jnp.concatenate(parts, axis=0), iota_c, PEER_TOPK, exact)
            z = cv[0] - cv[0]
            for r in range(PEER_TOPK):
                z = z + jnp.exp(cv[r] - cv[0])
            nrt = PEER_NKEYS // 8
            t0_scr[hd, lt, :, 0] = s0m.reshape(nrt, 8, LANES)
            t0_scr[hd, lt, :, 1] = jnp.exp(s0m - sv0[0]).reshape(nrt, 8, LANES)
            t1_scr[hd, lt, :, 0] = s1m.reshape(nrt, 8, LANES)
            t1_scr[hd, lt, :, 1] = (jnp.exp(s1m - sv1[0]) / z).reshape(nrt, 8, LANES)
            tau_scr[hd, lt] = jnp.broadcast_to(cv[PEER_TOPK - 1], (8, LANES))
            return ((_removed(w0) != PEER_TOPK) | (_removed(w1) != PEER_TOPK)
                    | (_removed(wc) != PEER_TOPK + n_pad))

        def head_body(hd, carry):
            for p in range(2):
                qs = qt_scr[pl.ds(pl.multiple_of((hd * 2 + p) * half, half), half), :]
                sp_scr[p] = _dot(keys_ref[p], qs, HIGHEST)
            tied = jnp.zeros((1, LANES), F32)
            for lt in range(nlt):
                tied = jnp.maximum(tied, jnp.where(select(hd, lt, False), 1.0, 0.0))

            @pl.when(jnp.max(tied) > 0.0)
            def _redo():
                for lt in range(nlt):
                    select(hd, lt, True)
            return carry

        lax.fori_loop(0, PEER_HEADS, head_body, 0)

    nsub = 8
    nrq = 32
    nparts = PEER_EC // PEER_NKEYS // nsub
    for part in range(nparts):
        er = slice(part * nsub * PEER_NKEYS, (part + 1) * nsub * PEER_NKEYS)
        for lh in range(nlt // 2):
            lanes2 = slice(lh * 2 * LANES, (lh + 1) * 2 * LANES)
            a_scr[er, lanes2] = _gelu_tanh(_dot(u_ref[er, :], ht_scr[:, lanes2]))

    def weights_tile(part, lt, rq, prev):
        lanes = slice(lt * LANES, (lt + 1) * LANES)
        jt = slice(rq * nrq // 8, (rq + 1) * nrq // 8)
        blk = c * (PEER_EC // PEER_NKEYS // 8) + part * nsub // 8
        zero = 0.0 if prev is None else jnp.where(prev != prev, prev, 0.0)
        w = [jnp.zeros((nrq, LANES), F32) for _ in range(nsub)]
        for hd in range(PEER_HEADS):
            s1 = t1_scr[hd, lt, jt, 0].reshape(nrq, LANES)
            e1 = t1_scr[hd, lt, jt, 1].reshape(nrq, LANES)
            s0 = t0_scr[hd, lt, blk, 0] + zero
            e0 = t0_scr[hd, lt, blk, 1] + zero
            tau = tau_scr[hd, lt, 0:1, :]
            for ii in range(nsub):
                keep = s1 + s0[ii:ii + 1] >= tau
                w[ii] = w[ii] + jnp.where(keep, e1 * e0[ii:ii + 1], 0.0)
        for ii in range(nsub):
            rows = slice((part * nsub + ii) * PEER_NKEYS + rq * nrq, (part * nsub + ii) * PEER_NKEYS + (rq + 1) * nrq)
            w[ii] = w[ii] * a_scr[rows, lanes]
            p_scr[rows, lanes] = w[ii].astype(BF16)
        return w[nsub - 1][0:1, :]

    done = [None, None]
    for part in range(nparts):
        for lt in range(nlt):
            for rq in range(PEER_NKEYS // nrq):
                done.append(weights_tile(part, lt, rq, done[-2]))
    acc_scr[...] += _dot(vt_ref[...], p_scr[...])

    @pl.when(c == nc - 1)
    def _fin():
        o_ref[0] = x_ref[0] + gt_ref[0] * acc_scr[...].T


def _peer(x, sh, sc, gate, ng, w_q, keys, u_b, vt_b):
    B, S, D = x.shape
    tn = PEER_TN
    nc = PEER_EXPERTS // PEER_EC
    tab = pltpu.VMEM((PEER_HEADS, tn // LANES, PEER_NKEYS // 8, 2, 8, LANES), F32)
    wq_hi = w_q.astype(BF16)
    wq_lo = (w_q - wq_hi.astype(F32)).astype(BF16)
    wspec = pl.BlockSpec((D, PEER_HEADS * PEER_QDIM), lambda b, i, c: (0, 0), pipeline_mode=pl.Buffered(1))
    return pl.pallas_call(
        _peer_kernel,
        out_shape=jax.ShapeDtypeStruct((B, S, D), F32),
        grid=(B, S // tn, nc),
        in_specs=[pl.BlockSpec((1, tn, D), lambda b, i, c: (b, i, 0)),
                  pl.BlockSpec((1, 1, D), lambda b, i, c: (b, 0, 0)),
                  pl.BlockSpec((1, 1, D), lambda b, i, c: (b, 0, 0)),
                  pl.BlockSpec((1, 1, D), lambda b, i, c: (b, 0, 0)),
                  pl.BlockSpec((1, D), lambda b, i, c: (0, 0)),
                  wspec, wspec,
                  pl.BlockSpec((2, PEER_NKEYS, PEER_QDIM // 2), lambda b, i, c: (0, 0, 0)),
                  pl.BlockSpec((PEER_EC, D), lambda b, i, c: (c, 0)),
                  pl.BlockSpec((D, PEER_EC), lambda b, i, c: (0, c))],
        out_specs=pl.BlockSpec((1, tn, D), lambda b, i, c: (b, i, 0)),
        scratch_shapes=[pltpu.VMEM((D, tn), BF16),
                        pltpu.VMEM((PEER_HEADS * PEER_QDIM, tn), F32),
                        pltpu.VMEM((2, PEER_NKEYS, tn), F32),
                        tab, tab,
                        pltpu.VMEM((PEER_HEADS, tn // LANES, 8, LANES), F32),
                        pltpu.VMEM((PEER_EC, tn), F32),
                        pltpu.VMEM((PEER_EC, tn), BF16),
                        pltpu.VMEM((D, tn), F32)],
        compiler_params=_cparams(("arbitrary", "arbitrary", "arbitrary")),
        name="peer",
    )(x, sh, sc, gate, ng, wq_hi, wq_lo, keys, u_b, vt_b)


def _rel_bucket(dist):
    n = jnp.maximum(dist, 0)
    max_exact = REL_BUCKETS // 2
    nf = jnp.maximum(n, 1).astype(F32)
    large = max_exact + (jnp.log(nf / max_exact) / math.log(REL_MAX_DIST / max_exact)
                         * (REL_BUCKETS - max_exact)).astype(jnp.int32)
    large = jnp.minimum(large, REL_BUCKETS - 1)
    return jnp.where(n < max_exact, n, large)


def _bias_lookup(bucket, tab_ref, h):
    bias = jnp.zeros(bucket.shape, F32)
    for b in range(REL_BUCKETS):
        bias = jnp.where(bucket == b, tab_ref[b, h], bias)
    return bias


N_BIAS_TILES = 5


def _relbias_kernel(tab_ref, o_ref, c_ref):
    h = pl.program_id(0)
    T = o_ref.shape[2]
    a = lax.broadcasted_iota(jnp.int32, (T, T), 0)
    b = lax.broadcasted_iota(jnp.int32, (T, T), 1)
    last = tab_ref[REL_BUCKETS - 1, h]
    o_ref[0, 0] = jnp.where(a >= b, _bias_lookup(_rel_bucket(a - b), tab_ref, h), NEG_INF)
    o_ref[0, 1] = _bias_lookup(_rel_bucket(a - b + T), tab_ref, h)
    o_ref[0, 2] = jnp.full((T, T), last, F32)
    o_ref[0, 3] = jnp.where(a < b, last, NEG_INF)
    o_ref[0, 4] = jnp.full((T, T), NEG_INF, F32)
    dist = a - ((b - T // 2) * CMP_STRIDE + CMP_LEN - 1)
    c_ref[0] = jnp.where(dist >= 0, _bias_lookup(_rel_bucket(dist), tab_ref, h), last)


def _relbias(rel_table):
    T = ATT_TILE
    assert 2 * T - (T - 1) >= REL_MAX_DIST
    assert WINDOW % T == 0
    return pl.pallas_call(
        _relbias_kernel,
        out_shape=(jax.ShapeDtypeStruct((NSA_HEADS, N_BIAS_TILES, T, T), F32),
                   jax.ShapeDtypeStruct((NSA_HEADS, T, T), F32)),
        grid=(NSA_HEADS,),
        in_specs=[pl.BlockSpec(memory_space=pltpu.SMEM)],
        out_specs=(pl.BlockSpec((1, N_BIAS_TILES, T, T), lambda h: (h, 0, 0, 0)),
                   pl.BlockSpec((1, T, T), lambda h: (h, 0, 0))),
        compiler_params=_cparams(("arbitrary",)),
        name="rel_bias_tiles",
    )(rel_table)


def _cmp_kernel(x_ref, pos_ref, w1_ref, w2_ref, kg_ref, o_ref):
    j = pl.program_id(1)
    half = CMP_STRIDE * HEAD_DIM
    x = x_ref[0, 0]
    pos = pos_ref[0]
    pa = _dot((x + pos[:, :half]).astype(BF16), w1_ref[0, :half, :])
    pb = _dot((x + pos[:, half:]).astype(BF16), w1_ref[0, half:, :])
    pre = pa + pltpu.roll(pb, pb.shape[0] - 1, axis=0)
    out = _dot(jax.nn.gelu(pre, approximate=True).astype(BF16), w2_ref[0])
    o_ref[0, 0] = jnp.where(j < NSA_GROUPS, _rms(out, kg_ref[...]), out)


def _nsa_compress(xr, pos, w1, w2, kg0):
    B, _, M, W = xr.shape
    G = NSA_GROUPS
    return pl.pallas_call(
        _cmp_kernel,
        out_shape=jax.ShapeDtypeStruct((B, 2 * G, M, HEAD_DIM), F32),
        grid=(B, 2 * G),
        in_specs=[pl.BlockSpec((1, 1, M, W), lambda b, j: (b, j, 0, 0)),
                  pl.BlockSpec((1, 1, 2 * W), lambda b, j: (j // G, 0, 0)),
                  pl.BlockSpec((1, 2 * W, HEAD_DIM), lambda b, j: (j // G, 0, 0)),
                  pl.BlockSpec((1, HEAD_DIM, HEAD_DIM), lambda b, j: (j // G, 0, 0)),
                  pl.BlockSpec((1, HEAD_DIM), lambda b, j: (0, 0))],
        out_specs=pl.BlockSpec((1, 1, M, HEAD_DIM), lambda b, j: (b, j, 0, 0)),
        compiler_params=_cparams(("arbitrary", "arbitrary")),
        name="nsa_compress",
    )(xr, pos.reshape(2, 1, 2 * W), w1.astype(BF16), w2.astype(BF16), kg0.reshape(1, HEAD_DIM))


def _cmp_attn_kernel(q_ref, kv_ref, ov_ref, cb_ref, qg_ref, oc_ref, sel_ref, qn_ref):
    qi = pl.program_id(1)
    T = q_ref.shape[1]
    NC = kv_ref.shape[2]
    G, J = NSA_GROUPS, NSA_HPG
    NS = sel_ref.shape[2] // G
    assert T == NC == LANES and T % CMP_STRIDE == 0 and G * NS == LANES
    t = qi * T + lax.broadcasted_iota(jnp.int32, (T, NC), 0)
    lane = lax.broadcasted_iota(jnp.int32, (T, NC), 1)
    vis = t - (lane * CMP_STRIDE + CMP_LEN - 1) >= 0
    visf = vis.astype(F32)
    shift = (qi * (T // CMP_STRIDE) + T // 2) % LANES
    imp = jnp.zeros((T, LANES), F32)
    vis4 = jnp.concatenate([vis] * J, axis=0)
    visf4 = jnp.concatenate([visf] * J, axis=0)
    for g in range(G):
        kc = kv_ref[0, g].astype(BF16)
        vc = kv_ref[0, G + g].astype(BF16)
        qhs = []
        for j in range(J):
            cols = slice((g * J + j) * HEAD_DIM, (g * J + j + 1) * HEAD_DIM)
            qhs.append((_rms(q_ref[0, :, cols], qg_ref[...]) * SCALE).astype(BF16))
            qn_ref[0, :, cols] = qhs[j]
        bias = jnp.concatenate([pltpu.roll(cb_ref[g * J + j], shift, axis=1) for j in range(J)], axis=0)
        s = jnp.where(vis4, _dot_nt(jnp.concatenate(qhs, axis=0), kc) + bias, NEG_INF)
        e = jnp.exp(s - jnp.max(s, axis=-1, keepdims=True))
        p = e / jnp.sum(e, axis=-1, keepdims=True) * visf4
        o = _dot(p.astype(BF16), vc)
        psum = p[0:T]
        for j in range(J):
            oc_ref[0, :, (g * J + j) * HEAD_DIM:(g * J + j + 1) * HEAD_DIM] = o[j * T:(j + 1) * T]
            if j:
                psum = psum + p[j * T:(j + 1) * T]
        imp = imp + _dot(psum, ov_ref[g], HIGHEST)
    mm = lane % NS
    cur = t // SLC_LEN
    forced = (mm == 0) | (mm == cur) | (mm == cur - 1)
    imp = jnp.where(forced, FORCE_SCORE, jnp.where(mm * SLC_LEN <= t, imp, -1.0))
    rank = jnp.zeros((T, LANES), F32)
    for r in range(1, NS):
        wrapped = mm >= NS - r
        other = jnp.where(wrapped, pltpu.roll(imp, NS - r, axis=1), pltpu.roll(imp, LANES - r, axis=1))
        rank = rank + jnp.where(other > imp, 1.0, jnp.where(other == imp, wrapped.astype(F32), 0.0))
    sel_ref[0] = jnp.where(rank < SLC_TOPN, jnp.where(imp >= 0.0, 1.0, 0.0), 0.0)


def _nsa_cmp_attn(y, kcvc, overlap, cmp_bias, q_g):
    B, S, _ = y.shape
    T = ATT_TILE
    NC = kcvc.shape[2]
    return pl.pallas_call(
        _cmp_attn_kernel,
        out_shape=(jax.ShapeDtypeStruct((B, S, NSA_WIDTH), F32),
                   jax.ShapeDtypeStruct((B, S, LANES), F32),
                   jax.ShapeDtypeStruct((B, S, NSA_WIDTH), BF16)),
        grid=(B, S // T),
        in_specs=[pl.BlockSpec((1, T, NSA_WIDTH), lambda b, i: (b, i, 0)),
                  pl.BlockSpec((1, 2 * NSA_GROUPS, NC, HEAD_DIM), lambda b, i: (b, 0, 0, 0)),
                  pl.BlockSpec(overlap.shape, lambda b, i: (0, 0, 0)),
                  pl.BlockSpec(cmp_bias.shape, lambda b, i: (0, 0, 0)),
                  pl.BlockSpec((1, HEAD_DIM), lambda b, i: (0, 0))],
        out_specs=(pl.BlockSpec((1, T, NSA_WIDTH), lambda b, i: (b, i, 0)),
                   pl.BlockSpec((1, T, LANES), lambda b, i: (b, i, 0)),
                   pl.BlockSpec((1, T, NSA_WIDTH), lambda b, i: (b, i, 0))),
        compiler_params=_cparams(("arbitrary", "arbitrary")),
        name="nsa_cmp_attn",
    )(y, kcvc, overlap, cmp_bias, q_g.reshape(1, HEAD_DIM))


def _kv_prep_kernel(ks_ref, kw_ref, kg1_ref, kg2_ref, ksn_ref, vs_ref, kwn_ref, vw_ref):
    KV = NSA_KV
    rows = ks_ref.shape[1]
    ones_col = (lax.broadcasted_iota(jnp.int32, (rows, LANES - HEAD_DIM), 1) == 0).astype(BF16)
    for g in range(NSA_GROUPS):
        cols = slice(g * HEAD_DIM, (g + 1) * HEAD_DIM)
        vcols = slice(KV + g * HEAD_DIM, KV + (g + 1) * HEAD_DIM)
        ksn_ref[0, :, cols] = _rms(ks_ref[0, :, cols], kg1_ref[...]).astype(BF16)
        kwn_ref[0, :, cols] = _rms(kw_ref[0, :, cols], kg2_ref[...]).astype(BF16)
        vs_ref[0, :, g * LANES:g * LANES + HEAD_DIM] = ks_ref[0, :, vcols].astype(BF16)
        vw_ref[0, :, g * LANES:g * LANES + HEAD_DIM] = kw_ref[0, :, vcols].astype(BF16)
        vs_ref[0, :, g * LANES + HEAD_DIM:(g + 1) * LANES] = ones_col
        vw_ref[0, :, g * LANES + HEAD_DIM:(g + 1) * LANES] = ones_col


def _nsa_kv_prep(y, k_g):
    B, S, _ = y.shape
    ksblk = (NSA_WIDTH + 2 * NSA_KV) // (2 * NSA_KV)
    kout = jax.ShapeDtypeStruct((B, S, NSA_KV), BF16)
    vout = jax.ShapeDtypeStruct((B, S, NSA_GROUPS * LANES), BF16)
    kspec = pl.BlockSpec((1, ROW_TILE, NSA_KV), lambda b, i: (b, i, 0))
    vspec = pl.BlockSpec((1, ROW_TILE, NSA_GROUPS * LANES), lambda b, i: (b, i, 0))
    return pl.pallas_call(
        _kv_prep_kernel,
        out_shape=(kout, vout, kout, vout),
        grid=(B, S // ROW_TILE),
        in_specs=[pl.BlockSpec((1, ROW_TILE, 2 * NSA_KV), lambda b, i: (b, i, ksblk)),
                  pl.BlockSpec((1, ROW_TILE, 2 * NSA_KV), lambda b, i: (b, i, ksblk + 1)),
                  pl.BlockSpec((1, HEAD_DIM), lambda b, i: (0, 0)),
                  pl.BlockSpec((1, HEAD_DIM), lambda b, i: (0, 0))],
        out_specs=(kspec, vspec, kspec, vspec),
        compiler_params=_cparams(("arbitrary", "arbitrary")),
        name="nsa_kv_prep",
    )(y, y, k_g[1].reshape(1, HEAD_DIM), k_g[2].reshape(1, HEAD_DIM))


SEL_KEY_PARTS = 4


def _sel_win_kernel(qn_ref, ks_ref, vs_ref, kw_ref, vw_ref, gl_ref, sel_ref, oc_ref, bias_ref, e_ref, bg_ref,
                    o_ref):
    qi = pl.program_id(1)
    T = qn_ref.shape[1]
    S = ks_ref.shape[1]
    nk = S // T
    G, J = NSA_GROUPS, NSA_HPG
    NS = e_ref.shape[0]
    wt = WINDOW // T
    nwin = wt + 1
    assert nk % SEL_KEY_PARTS == 0 and nk >= nwin

    def sel_tile(r):
        return jnp.where(r < 0, 4, jnp.minimum(r, 2))

    def win_tile(r):
        return jnp.where(r < 0, 4, jnp.where(r <= 1, r, jnp.where(r < wt, 2, jnp.where(r == wt, 3, 4))))

    def softmax_pv(s, v):
        p = jnp.exp(s - jnp.max(s, axis=-1, keepdims=True))
        acc = _dot(p.astype(BF16), v)
        return acc[:, :HEAD_DIM] / acc[:, HEAD_DIM:HEAD_DIM + 1]

    gates = jax.nn.sigmoid(gl_ref[0] + bg_ref[...])
    k0 = jnp.maximum(qi - wt, 0)
    wrows = pl.ds(pl.multiple_of(k0 * T, T), nwin * T)

    def attend(nkt):
        nkeys = nkt * T
        for g in range(G):
            gcols = slice(g * HEAD_DIM, (g + 1) * HEAD_DIM)
            vcols = slice(g * LANES, (g + 1) * LANES)
            q4 = jnp.concatenate(
                [qn_ref[0, :, (g * J + j) * HEAD_DIM:(g * J + j + 1) * HEAD_DIM] for j in range(J)], axis=0)
            selg = sel_ref[0, :, g * NS:(g + 1) * NS].astype(BF16)
            mask = (_dot(selg, e_ref[:, :nkeys]) - 1.0) * (-NEG_INF)
            bias = jnp.concatenate(
                [jnp.concatenate([bias_ref[g * J + j, sel_tile(qi - kt)] for kt in range(nkt)], axis=1) + mask
                 for j in range(J)], axis=0)
            o_s = softmax_pv(_dot_nt(q4, ks_ref[0, :nkeys, gcols]) + bias, vs_ref[0, :nkeys, vcols])
            wbias = jnp.concatenate(
                [jnp.concatenate([bias_ref[g * J + j, win_tile(qi - k0 - t)] for t in range(nwin)], axis=1)
                 for j in range(J)], axis=0)
            o_w = softmax_pv(_dot_nt(q4, kw_ref[0, wrows, gcols]) + wbias, vw_ref[0, wrows, vcols])
            for j in range(J):
                h = g * J + j
                cols = slice(h * HEAD_DIM, (h + 1) * HEAD_DIM)
                rows = slice(j * T, (j + 1) * T)
                o_ref[0, :, cols] = (gates[:, 3 * h:3 * h + 1] * oc_ref[0, :, cols]
                                     + gates[:, 3 * h + 1:3 * h + 2] * o_s[rows]
                                     + gates[:, 3 * h + 2:3 * h + 3] * o_w[rows])

    per = nk // SEL_KEY_PARTS
    for part in range(SEL_KEY_PARTS):
        pl.when((qi >= part * per) & (qi < (part + 1) * per))(functools.partial(attend, (part + 1) * per))


def _nsa_sel_win(qn, y, kv, sel, oc, bias_tiles, expand, b_gate):
    B, S, _ = y.shape
    T = ATT_TILE
    glblk = (NSA_WIDTH + 6 * NSA_KV) // LANES
    bg = jnp.zeros((1, LANES), F32).at[0, :3 * NSA_HEADS].set(b_gate)
    kspec = pl.BlockSpec((1, S, NSA_KV), lambda b, i: (b, 0, 0))
    vspec = pl.BlockSpec((1, S, NSA_GROUPS * LANES), lambda b, i: (b, 0, 0))
    return pl.pallas_call(
        _sel_win_kernel,
        out_shape=jax.ShapeDtypeStruct((B, S, NSA_WIDTH), F32),
        grid=(B, S // T),
        in_specs=[pl.BlockSpec((1, T, NSA_WIDTH), lambda b, i: (b, i, 0)),
                  kspec, vspec, kspec, vspec,
                  pl.BlockSpec((1, T, LANES), lambda b, i: (b, i, glblk)),
                  pl.BlockSpec((1, T, sel.shape[2]), lambda b, i: (b, i, 0)),
                  pl.BlockSpec((1, T, NSA_WIDTH), lambda b, i: (b, i, 0)),
                  pl.BlockSpec(bias_tiles.shape, lambda b, i: (0, 0, 0, 0)),
                  pl.BlockSpec(expand.shape, lambda b, i: (0, 0)),
                  pl.BlockSpec((1, LANES), lambda b, i: (0, 0))],
        out_specs=pl.BlockSpec((1, T, NSA_WIDTH), lambda b, i: (b, i, 0)),
        compiler_params=_cparams(("arbitrary", "arbitrary")),
        name="nsa_sel_win",
    )(qn, *kv, y, sel, oc, bias_tiles, expand, bg)


def _odd_mixer(x, sh, sc, gate, ng, w_in, b_gate, q_g, k_g, cmp_pos, cmp_w1, cmp_w2, bias_tiles, cmp_bias,
               w_out):
    B, S, _ = x.shape
    G = NSA_GROUPS
    w = jnp.zeros((D_MODEL, ODD_IN_PAD), BF16).at[:, :ODD_IN].set(w_in.astype(BF16))
    y = _norm_proj(x, sh, sc, ng, w)
    xr = y[:, :, NSA_WIDTH:NSA_WIDTH + 2 * NSA_KV].reshape(B, S, 2, G, HEAD_DIM)
    xr = xr.transpose(0, 2, 3, 1, 4).reshape(B, 2 * G, S // CMP_STRIDE, CMP_STRIDE * HEAD_DIM)
    kcvc = _nsa_compress(xr, cmp_pos.reshape(2, CMP_LEN * HEAD_DIM), cmp_w1, cmp_w2, k_g[0])
    n_slc = S // SLC_LEN
    starts = jnp.arange(S // CMP_STRIDE) * CMP_STRIDE
    bstart = jnp.arange(n_slc) * SLC_LEN
    real = (jnp.arange(S // CMP_STRIDE) < (S - CMP_LEN) // CMP_STRIDE + 1)[:, None]
    overlap = (real & (starts[:, None] < bstart[None, :] + SLC_LEN)
               & (starts[:, None] + CMP_LEN > bstart[None, :])).astype(F32)
    overlap = jnp.stack([jnp.pad(overlap, ((0, 0), (g * n_slc, (G - 1 - g) * n_slc))) for g in range(G)])
    expand = (jnp.arange(S)[None, :] // SLC_LEN == jnp.arange(n_slc)[:, None]).astype(BF16)
    oc, sel, qn = _nsa_cmp_attn(y, kcvc, overlap, cmp_bias, q_g)
    o = _nsa_sel_win(qn, y, _nsa_kv_prep(y, k_g), sel, oc, bias_tiles, expand, b_gate)
    return _out_proj(x, o, 0, o, 1, w_out.astype(BF16), gate)


def _even_mixer(x, sh, sc, gate, ng, w_in, b_f, conv_w, q_g, k_g, w_out):
    w = jnp.zeros((D_MODEL, EVEN_IN_PAD), BF16).at[:, :EVEN_IN].set(w_in.astype(BF16))
    y = _norm_proj(x, sh, sc, ng, w)
    conv = _short_conv(y, conv_w)
    attn = _fox_attention(y, b_f, q_g, k_g)
    return _out_proj(x, conv, 0, attn, 0, w_out.astype(BF16), gate)


def kernel(x, c, ada_w, ada_b, norm_g, even_w_in, even_b_f, even_conv_w, even_q_g, even_k_g, even_w_out, odd_w_in, odd_b_gate, odd_q_g, odd_k_g, odd_cmp_pos, odd_cmp_w1, odd_cmp_w2, odd_w_out, rel_table, peer_w_q, peer_keys, peer_u, peer_v):
    B = x.shape[0]
    mods = _ada(c, ada_w, ada_b)
    bias_tiles, cmp_bias = _relbias(rel_table)
    for layer in range(DEPTH):
        sh1, sc1, g1, sh2, sc2, g2 = [m.reshape(B, 1, D_MODEL) for m in jnp.split(mods[layer], 6, axis=-1)]
        i = layer // 2
        ng1 = norm_g[layer, 0].reshape(1, D_MODEL)
        ng2 = norm_g[layer, 1].reshape(1, D_MODEL)
        if layer % 2 == 0:
            x = _even_mixer(x, sh1, sc1, g1, ng1, even_w_in[i], even_b_f[i], even_conv_w[i], even_q_g[i],
                            even_k_g[i], even_w_out[i])
        else:
            x = _odd_mixer(x, sh1, sc1, g1, ng1, odd_w_in[i], odd_b_gate[i], odd_q_g[i], odd_k_g[i],
                           odd_cmp_pos[i], odd_cmp_w1[i], odd_cmp_w2[i], bias_tiles, cmp_bias, odd_w_out[i])
        x = _peer(x, sh2, sc2, g2, ng2, peer_w_q[layer], peer_keys[layer],
                  peer_u[layer].astype(BF16), peer_v[layer].T.astype(BF16))
    return x
```

```python
import functools
import math

import jax
import jax.numpy as jnp
from jax import lax
from jax.experimental import pallas as pl
from jax.experimental.pallas import tpu as pltpu

D_MODEL = 1024
DEPTH = 4
HEAD_DIM = 64
CONV_CH = 512
CONV_K = 3
FOX_HEADS = 8
FOX_WIDTH = FOX_HEADS * HEAD_DIM
EVEN_IN = 3 * CONV_CH + 3 * FOX_WIDTH + FOX_HEADS
NSA_GROUPS = 4
NSA_HPG = 4
NSA_HEADS = NSA_GROUPS * NSA_HPG
NSA_WIDTH = NSA_HEADS * HEAD_DIM
NSA_KV = NSA_GROUPS * HEAD_DIM
CMP_LEN = 32
CMP_STRIDE = 16
SLC_LEN = 64
SLC_TOPN = 16
WINDOW = 512
ODD_IN = NSA_WIDTH + 6 * NSA_KV + 3 * NSA_HEADS
REL_BUCKETS = 32
REL_MAX_DIST = 128
PEER_HEADS = 8
PEER_NKEYS = 128
PEER_EXPERTS = PEER_NKEYS * PEER_NKEYS
PEER_QDIM = 128
PEER_TOPK = 16
EPS = 1e-6
NEG_INF = -1e30
FORCE_SCORE = 1e6

LANES = 128
F32 = jnp.float32
BF16 = jnp.bfloat16
HIGHEST = lax.Precision.HIGHEST
VMEM_LIMIT = 56 * 1024 * 1024

EVEN_IN_PAD = -(-EVEN_IN // LANES) * LANES
ODD_IN_PAD = -(-ODD_IN // LANES) * LANES
ROW_TILE = 512
ATT_TILE = 128
FOX_TILE = 256
PEER_TN = 512
PEER_EC = 2048
SCALE = HEAD_DIM ** -0.5


def _cparams(sem):
    return pltpu.CompilerParams(dimension_semantics=sem, vmem_limit_bytes=VMEM_LIMIT)


def _dot(a, b, precision=None):
    return jnp.dot(a, b, precision=precision, preferred_element_type=F32)


def _dot_nt(a, b, precision=None):
    return lax.dot_general(a, b, (((1,), (1,)), ((), ())), precision=precision,
                           preferred_element_type=F32)


def _rms(x, g):
    return x * lax.rsqrt(jnp.mean(x * x, axis=-1, keepdims=True) + EPS) * g


def _ada_kernel(c_ref, w_ref, b_ref, o_ref):
    c = c_ref[...]
    sc = c * jax.nn.sigmoid(c)
    o_ref[0] = _dot(sc, w_ref[0], HIGHEST) + b_ref[0]


def _ada(c, ada_w, ada_b):
    B, D = c.shape
    E = ada_w.shape[2]
    tn = 2048
    return pl.pallas_call(
        _ada_kernel,
        out_shape=jax.ShapeDtypeStruct((DEPTH, B, E), F32),
        grid=(DEPTH, E // tn),
        in_specs=[pl.BlockSpec((B, D), lambda l, j: (0, 0)),
                  pl.BlockSpec((1, D, tn), lambda l, j: (l, 0, j)),
                  pl.BlockSpec((1, 1, tn), lambda l, j: (l, 0, j))],
        out_specs=pl.BlockSpec((1, B, tn), lambda l, j: (l, 0, j)),
        compiler_params=_cparams(("arbitrary", "arbitrary")),
        name="ada_mod",
    )(c, ada_w, ada_b.reshape(DEPTH, 1, E))


def _norm_proj_kernel(x_ref, sh_ref, sc_ref, ng_ref, w_ref, y_ref):
    h = _rms(x_ref[0], ng_ref[...]) * (1.0 + sc_ref[0]) + sh_ref[0]
    y_ref[0] = _dot(h.astype(BF16), w_ref[...])


def _norm_proj(x, sh, sc, ng, w):
    B, S, D = x.shape
    N = w.shape[1]
    return pl.pallas_call(
        _norm_proj_kernel,
        out_shape=jax.ShapeDtypeStruct((B, S, N), F32),
        grid=(B, S // ROW_TILE),
        in_specs=[pl.BlockSpec((1, ROW_TILE, D), lambda b, i: (b, i, 0)),
                  pl.BlockSpec((1, 1, D), lambda b, i: (b, 0, 0)),
                  pl.BlockSpec((1, 1, D), lambda b, i: (b, 0, 0)),
                  pl.BlockSpec((1, D), lambda b, i: (0, 0)),
                  pl.BlockSpec((D, N), lambda b, i: (0, 0))],
        out_specs=pl.BlockSpec((1, ROW_TILE, N), lambda b, i: (b, i, 0)),
        compiler_params=_cparams(("arbitrary", "arbitrary")),
        name="norm_proj",
    )(x, sh, sc, ng, w)


def _out_proj_kernel(x_ref, a_ref, b_ref, wa_ref, wb_ref, g_ref, o_ref):
    y = _dot(a_ref[0].astype(BF16), wa_ref[...]) + _dot(b_ref[0].astype(BF16), wb_ref[...])
    o_ref[0] = x_ref[0] + g_ref[0] * y


def _out_proj(x, a, a_blk, b, b_blk, w, gate):
    B, S, D = x.shape
    half = w.shape[0] // 2
    wa, wb = w[:half], w[half:]
    return pl.pallas_call(
        _out_proj_kernel,
        out_shape=jax.ShapeDtypeStruct((B, S, D), F32),
        grid=(B, S // ROW_TILE),
        in_specs=[pl.BlockSpec((1, ROW_TILE, D), lambda b, i: (b, i, 0)),
                  pl.BlockSpec((1, ROW_TILE, half), lambda b, i: (b, i, a_blk)),
                  pl.BlockSpec((1, ROW_TILE, half), lambda b, i: (b, i, b_blk)),
                  pl.BlockSpec((half, D), lambda b, i: (0, 0)),
                  pl.BlockSpec((half, D), lambda b, i: (0, 0)),
                  pl.BlockSpec((1, 1, D), lambda b, i: (b, 0, 0))],
        out_specs=pl.BlockSpec((1, ROW_TILE, D), lambda b, i: (b, i, 0)),
        compiler_params=_cparams(("arbitrary", "arbitrary")),
        name="out_proj",
    )(x, a, b, wa, wb, gate)


def _conv_kernel(cb_ref, cc_ref, ch_ref, w_ref, o_ref):
    u = cc_ref[0] * ch_ref[0]
    row = lax.broadcasted_iota(jnp.int32, u.shape, 0)
    u1 = jnp.where(row >= 1, pltpu.roll(u, 1, axis=0), 0.0)
    u2 = jnp.where(row >= 2, pltpu.roll(u, 2, axis=0), 0.0)
    w = w_ref[...]
    o_ref[0] = cb_ref[0] * (w[0:1] * u2 + w[1:2] * u1 + w[2:3] * u)


def _short_conv(y, conv_w):
    B, S, _ = y.shape
    nblk = CONV_CH // LANES
    w = jnp.zeros((8, CONV_CH), F32).at[:CONV_K].set(conv_w)
    return pl.pallas_call(
        _conv_kernel,
        out_shape=jax.ShapeDtypeStruct((B, S, CONV_CH), F32),
        grid=(B, nblk),
        in_specs=[pl.BlockSpec((1, S, LANES), lambda b, j: (b, 0, j)),
                  pl.BlockSpec((1, S, LANES), lambda b, j: (b, 0, nblk + j)),
                  pl.BlockSpec((1, S, LANES), lambda b, j: (b, 0, 2 * nblk + j)),
                  pl.BlockSpec((8, LANES), lambda b, j: (0, j))],
        out_specs=pl.BlockSpec((1, S, LANES), lambda b, j: (b, 0, j)),
        compiler_params=_cparams(("arbitrary", "arbitrary")),
        name="short_conv",
    )(y, y, y, w)


def _cumsum_rows(x):
    n = x.shape[0]
    row = lax.broadcasted_iota(jnp.int32, x.shape, 0)
    sh = 1
    while sh < n:
        x = x + jnp.where(row >= sh, pltpu.roll(x, sh, axis=0), 0.0)
        sh *= 2
    return x


FOX_KEY_PARTS = 4


def _fox_kernel(q_ref, k_ref, v_ref, f_ref, bf_ref, qg_ref, kg_ref, o_ref,
                kn_scr, vb_scr, cum_scr, cumrow_scr):
    qi = pl.program_id(1)
    S = k_ref.shape[1]
    T = q_ref.shape[1]
    nq = S // T

    @pl.when(qi == 0)
    def _prep():
        ones_col = (lax.broadcasted_iota(jnp.int32, (S, LANES - HEAD_DIM), 1) == 0).astype(BF16)
        for h in range(FOX_HEADS):
            cols = slice(h * HEAD_DIM, (h + 1) * HEAD_DIM)
            kn_scr[:, cols] = _rms(k_ref[0, :, cols], kg_ref[...]).astype(BF16)
            vb_scr[:, h * LANES:h * LANES + HEAD_DIM] = v_ref[0, :, cols].astype(BF16)
            vb_scr[:, h * LANES + HEAD_DIM:(h + 1) * LANES] = ones_col
        logf = jax.nn.log_sigmoid(f_ref[0] + bf_ref[...])
        cum = _cumsum_rows(logf)
        cum_scr[...] = cum
        for kt in range(S // T):
            cumrow_scr[:, kt * T:(kt + 1) * T] = cum[kt * T:(kt + 1) * T, :].T[0:8, :]

    cq = cum_scr[pl.ds(pl.multiple_of(qi * T, T), T), :]

    def attend(nkeys):
        row = qi * T + lax.broadcasted_iota(jnp.int32, (T, nkeys), 0)
        causal = jnp.where(lax.broadcasted_iota(jnp.int32, (T, nkeys), 1) <= row, 0.0, NEG_INF)
        for h in range(FOX_HEADS):
            cols = slice(h * HEAD_DIM, (h + 1) * HEAD_DIM)
            qh = (_rms(q_ref[0, :, cols], qg_ref[...]) * SCALE).astype(BF16)
            s = _dot_nt(qh, kn_scr[0:nkeys, cols]) + ((cq[:, h:h + 1] - cumrow_scr[h:h + 1, 0:nkeys]) + causal)
            p = jnp.exp(s - jnp.max(s, axis=-1, keepdims=True))
            acc = _dot(p.astype(BF16), vb_scr[0:nkeys, h * LANES:(h + 1) * LANES])
            o_ref[0, :, cols] = acc[:, :HEAD_DIM] / acc[:, HEAD_DIM:HEAD_DIM + 1]

    parts = min(FOX_KEY_PARTS, nq)
    per = nq // parts
    for part in range(parts):
        pl.when((qi >= part * per) & (qi < (part + 1) * per))(functools.partial(attend, (part + 1) * (S // parts)))


def _fox_attention(y, b_f, q_g, k_g):
    B, S, _ = y.shape
    T = FOX_TILE
    qblk = 3 * CONV_CH // FOX_WIDTH
    fblk = (3 * CONV_CH + 3 * FOX_WIDTH) // LANES
    bf = jnp.zeros((1, LANES), F32).at[0, :FOX_HEADS].set(b_f)
    return pl.pallas_call(
        _fox_kernel,
        out_shape=jax.ShapeDtypeStruct((B, S, FOX_WIDTH), F32),
        grid=(B, S // T),
        in_specs=[pl.BlockSpec((1, T, FOX_WIDTH), lambda b, i: (b, i, qblk)),
                  pl.BlockSpec((1, S, FOX_WIDTH), lambda b, i: (b, 0, qblk + 1)),
                  pl.BlockSpec((1, S, FOX_WIDTH), lambda b, i: (b, 0, qblk + 2)),
                  pl.BlockSpec((1, S, LANES), lambda b, i: (b, 0, fblk)),
                  pl.BlockSpec((1, LANES), lambda b, i: (0, 0)),
                  pl.BlockSpec((1, HEAD_DIM), lambda b, i: (0, 0)),
                  pl.BlockSpec((1, HEAD_DIM), lambda b, i: (0, 0))],
        out_specs=pl.BlockSpec((1, T, FOX_WIDTH), lambda b, i: (b, i, 0)),
        scratch_shapes=[pltpu.VMEM((S, FOX_WIDTH), BF16),
                        pltpu.VMEM((S, FOX_HEADS * LANES), BF16),
                        pltpu.VMEM((S, LANES), F32),
                        pltpu.VMEM((8, S), F32)],
        compiler_params=_cparams(("arbitrary", "arbitrary")),
        name="fox_attention",
    )(y, y, y, y, bf, q_g.reshape(1, HEAD_DIM), k_g.reshape(1, HEAD_DIM))


def _gelu_tanh(x):
    c0 = math.sqrt(2.0 / math.pi)
    return (0.5 * x) * (1.0 + jnp.tanh(x * (c0 + (c0 * 0.044715) * (x * x))))


def _extract_top(work, iota, k, exact):
    nrows = work.shape[0]
    vals = []
    for _ in range(k):
        m = jnp.max(work, axis=0, keepdims=True)
        if exact:
            idx = jnp.min(jnp.where(work == m, iota, nrows), axis=0, keepdims=True)
            work = jnp.where(iota == idx, -jnp.inf, work)
        else:
            work = jnp.where(work == m, -jnp.inf, work)
        vals.append(m)
    return vals, work


def _removed(work):
    return jnp.sum(jnp.where(work == -jnp.inf, 1.0, 0.0), axis=0, keepdims=True)


def _peer_kernel(x_ref, sh_ref, sc_ref, gt_ref, ng_ref, wqh_ref, wql_ref, keys_ref, u_ref, vt_ref, o_ref,
                 ht_scr, qt_scr, sp_scr, t0_scr, t1_scr, tau_scr, a_scr, p_scr, acc_scr):
    c = pl.program_id(2)
    nc = pl.num_programs(2)
    tn = x_ref.shape[1]
    nlt = tn // LANES
    half = PEER_QDIM // 2

    @pl.when(c == 0)
    def _prep():
        h = _rms(x_ref[0], ng_ref[...]) * (1.0 + sc_ref[0]) + sh_ref[0]
        hh = h.astype(BF16)
        hl = (h - hh.astype(F32)).astype(BF16)
        cw = 2 * LANES
        for k in range(h.shape[1] // cw):
            cs = slice(k * cw, (k + 1) * cw)
            q = _dot(hh, wqh_ref[:, cs]) + (_dot(hh, wql_ref[:, cs]) + _dot(hl, wqh_ref[:, cs]))
            qt_scr[cs, :] = q.T
            ht_scr[cs, :] = h[:, cs].T.astype(BF16)
        acc_scr[...] = jnp.zeros_like(acc_scr)
        iota_k = lax.broadcasted_iota(jnp.int32, (PEER_NKEYS, LANES), 0)
        iota_c = lax.broadcasted_iota(jnp.int32, (PEER_TOPK + 8 * 8, LANES), 0)

        n_pad = sum(8 - PEER_TOPK // (r0 + 1) for r0 in range(2, 8))

        def select(hd, lt, exact):
            lanes = slice(lt * LANES, (lt + 1) * LANES)
            s0 = sp_scr[0, :, lanes]
            s1 = sp_scr[1, :, lanes]
            sv0, w0 = _extract_top(s0, iota_k, PEER_TOPK, exact)
            sv1, w1 = _extract_top(s1, iota_k, PEER_TOPK, exact)
            s0m = jnp.where(w0 == -jnp.inf, s0, -jnp.inf)
            s1m = jnp.where(w1 == -jnp.inf, s1, -jnp.inf)
            sv0_all = jnp.concatenate(sv0, axis=0)
            sv1_all = jnp.concatenate(sv1, axis=0)
            parts = [sv0[0] + sv1_all, sv0[1] + sv1_all[:8]]
            for r0 in range(2, 8):
                parts.append(jnp.where(iota_c[:8] < PEER_TOPK // (r0 + 1), sv0[r0] + sv1_all[:8], -jnp.inf))
            parts.append(sv0_all[8:] + sv1[0])
            cv, wc = _extract_top(jnp.concatenate(parts, axis=0), iota_c, PEER_TOPK, exact)
            z = cv[0] - cv[0]
            for r in range(PEER_TOPK):
                z = z + jnp.exp(cv[r] - cv[0])
            nrt = PEER_NKEYS // 8
            t0_scr[hd, lt, :, 0] = s0m.reshape(nrt, 8, LANES)
            t0_scr[hd, lt, :, 1] = jnp.exp(s0m - sv0[0]).reshape(nrt, 8, LANES)
            t1_scr[hd, lt, :, 0] = s1m.reshape(nrt, 8, LANES)
            t1_scr[hd, lt, :, 1] = (jnp.exp(s1m - sv1[0]) / z).reshape(nrt, 8, LANES)
            tau_scr[hd, lt] = jnp.broadcast_to(cv[PEER_TOPK - 1], (8, LANES))
            return ((_removed(w0) != PEER_TOPK) | (_removed(w1) != PEER_TOPK)
                    | (_removed(wc) != PEER_TOPK + n_pad))

        def head_body(hd, carry):
            for p in range(2):
                qs = qt_scr[pl.ds(pl.multiple_of((hd * 2 + p) * half, half), half), :]
                sp_scr[p] = _dot(keys_ref[p], qs, HIGHEST)
            tied = jnp.zeros((1, LANES), F32)
            for lt in range(nlt):
                tied = jnp.maximum(tied, jnp.where(select(hd, lt, False), 1.0, 0.0))

            @pl.when(jnp.max(tied) > 0.0)
            def _redo():
                for lt in range(nlt):
                    select(hd, lt, True)
            return carry

        lax.fori_loop(0, PEER_HEADS, head_body, 0)

    nsub = 8
    nrq = 32
    nparts = PEER_EC // PEER_NKEYS // nsub
    for part in range(nparts):
        er = slice(part * nsub * PEER_NKEYS, (part + 1) * nsub * PEER_NKEYS)
        for lh in range(nlt // 2):
            lanes2 = slice(lh * 2 * LANES, (lh + 1) * 2 * LANES)
            a_scr[er, lanes2] = _gelu_tanh(_dot(u_ref[er, :], ht_scr[:, lanes2]))

    def weights_tile(part, lt, rq, prev):
        lanes = slice(lt * LANES, (lt + 1) * LANES)
        jt = slice(rq * nrq // 8, (rq + 1) * nrq // 8)
        blk = c * (PEER_EC // PEER_NKEYS // 8) + part * nsub // 8
        zero = 0.0 if prev is None else jnp.where(prev != prev, prev, 0.0)
        w = [jnp.zeros((nrq, LANES), F32) for _ in range(nsub)]
        for hd in range(PEER_HEADS):
            s1 = t1_scr[hd, lt, jt, 0].reshape(nrq, LANES)
            e1 = t1_scr[hd, lt, jt, 1].reshape(nrq, LANES)
            s0 = t0_scr[hd, lt, blk, 0] + zero
            e0 = t0_scr[hd, lt, blk, 1] + zero
            tau = tau_scr[hd, lt, 0:1, :]
            for ii in range(nsub):
                keep = s1 + s0[ii:ii + 1] >= tau
                w[ii] = w[ii] + jnp.where(keep, e1 * e0[ii:ii + 1], 0.0)
        for ii in range(nsub):
            rows = slice((part * nsub + ii) * PEER_NKEYS + rq * nrq, (part * nsub + ii) * PEER_NKEYS + (rq + 1) * nrq)
            w[ii] = w[ii] * a_scr[rows, lanes]
            p_scr[rows, lanes] = w[ii].astype(BF16)
        return w[nsub - 1][0:1, :]

    done = [None] * 3
    for part in range(nparts):
        for lt in range(nlt):
            for rq in range(PEER_NKEYS // nrq):
                done.append(weights_tile(part, lt, rq, done[-3]))
    acc_scr[...] += _dot(vt_ref[...], p_scr[...])

    @pl.when(c == nc - 1)
    def _fin():
        o_ref[0] = x_ref[0] + gt_ref[0] * acc_scr[...].T


def _peer(x, sh, sc, gate, ng, w_q, keys, u_b, vt_b):
    B, S, D = x.shape
    tn = PEER_TN
    nc = PEER_EXPERTS // PEER_EC
    tab = pltpu.VMEM((PEER_HEADS, tn // LANES, PEER_NKEYS // 8, 2, 8, LANES), F32)
    wq_hi = w_q.astype(BF16)
    wq_lo = (w_q - wq_hi.astype(F32)).astype(BF16)
    wspec = pl.BlockSpec((D, PEER_HEADS * PEER_QDIM), lambda b, i, c: (0, 0), pipeline_mode=pl.Buffered(1))
    return pl.pallas_call(
        _peer_kernel,
        out_shape=jax.ShapeDtypeStruct((B, S, D), F32),
        grid=(B, S // tn, nc),
        in_specs=[pl.BlockSpec((1, tn, D), lambda b, i, c: (b, i, 0)),
                  pl.BlockSpec((1, 1, D), lambda b, i, c: (b, 0, 0)),
                  pl.BlockSpec((1, 1, D), lambda b, i, c: (b, 0, 0)),
                  pl.BlockSpec((1, 1, D), lambda b, i, c: (b, 0, 0)),
                  pl.BlockSpec((1, D), lambda b, i, c: (0, 0)),
                  wspec, wspec,
                  pl.BlockSpec((2, PEER_NKEYS, PEER_QDIM // 2), lambda b, i, c: (0, 0, 0)),
                  pl.BlockSpec((PEER_EC, D), lambda b, i, c: (c, 0)),
                  pl.BlockSpec((D, PEER_EC), lambda b, i, c: (0, c))],
        out_specs=pl.BlockSpec((1, tn, D), lambda b, i, c: (b, i, 0)),
        scratch_shapes=[pltpu.VMEM((D, tn), BF16),
                        pltpu.VMEM((PEER_HEADS * PEER_QDIM, tn), F32),
                        pltpu.VMEM((2, PEER_NKEYS, tn), F32),
                        tab, tab,
                        pltpu.VMEM((PEER_HEADS, tn // LANES, 8, LANES), F32),
                        pltpu.VMEM((PEER_EC, tn), F32),
                        pltpu.VMEM((PEER_EC, tn), BF16),
                        pltpu.VMEM((D, tn), F32)],
        compiler_params=_cparams(("arbitrary", "arbitrary", "arbitrary")),
        name="peer",
    )(x, sh, sc, gate, ng, wq_hi, wq_lo, keys, u_b, vt_b)


def _rel_bucket(dist):
    n = jnp.maximum(dist, 0)
    max_exact = REL_BUCKETS // 2
    nf = jnp.maximum(n, 1).astype(F32)
    large = max_exact + (jnp.log(nf / max_exact) / math.log(REL_MAX_DIST / max_exact)
                         * (REL_BUCKETS - max_exact)).astype(jnp.int32)
    large = jnp.minimum(large, REL_BUCKETS - 1)
    return jnp.where(n < max_exact, n, large)


def _bias_lookup(bucket, tab_ref, h):
    bias = jnp.zeros(bucket.shape, F32)
    for b in range(REL_BUCKETS):
        bias = jnp.where(bucket == b, tab_ref[b, h], bias)
    return bias


N_BIAS_TILES = 5


def _relbias_kernel(tab_ref, o_ref, c_ref):
    h = pl.program_id(0)
    T = o_ref.shape[2]
    a = lax.broadcasted_iota(jnp.int32, (T, T), 0)
    b = lax.broadcasted_iota(jnp.int32, (T, T), 1)
    last = tab_ref[REL_BUCKETS - 1, h]
    o_ref[0, 0] = jnp.where(a >= b, _bias_lookup(_rel_bucket(a - b), tab_ref, h), NEG_INF)
    o_ref[0, 1] = _bias_lookup(_rel_bucket(a - b + T), tab_ref, h)
    o_ref[0, 2] = jnp.full((T, T), last, F32)
    o_ref[0, 3] = jnp.where(a < b, last, NEG_INF)
    o_ref[0, 4] = jnp.full((T, T), NEG_INF, F32)
    dist = a - ((b - T // 2) * CMP_STRIDE + CMP_LEN - 1)
    c_ref[0] = jnp.where(dist >= 0, _bias_lookup(_rel_bucket(dist), tab_ref, h), last)


def _relbias(rel_table):
    T = ATT_TILE
    assert 2 * T - (T - 1) >= REL_MAX_DIST
    assert WINDOW % T == 0
    return pl.pallas_call(
        _relbias_kernel,
        out_shape=(jax.ShapeDtypeStruct((NSA_HEADS, N_BIAS_TILES, T, T), F32),
                   jax.ShapeDtypeStruct((NSA_HEADS, T, T), F32)),
        grid=(NSA_HEADS,),
        in_specs=[pl.BlockSpec(memory_space=pltpu.SMEM)],
        out_specs=(pl.BlockSpec((1, N_BIAS_TILES, T, T), lambda h: (h, 0, 0, 0)),
                   pl.BlockSpec((1, T, T), lambda h: (h, 0, 0))),
        compiler_params=_cparams(("arbitrary",)),
        name="rel_bias_tiles",
    )(rel_table)


def _cmp_kernel(x_ref, pos_ref, w1_ref, w2_ref, kg_ref, o_ref):
    j = pl.program_id(1)
    half = CMP_STRIDE * HEAD_DIM
    x = x_ref[0, 0]
    pos = pos_ref[0]
    pa = _dot((x + pos[:, :half]).astype(BF16), w1_ref[0, :half, :])
    pb = _dot((x + pos[:, half:]).astype(BF16), w1_ref[0, half:, :])
    pre = pa + pltpu.roll(pb, pb.shape[0] - 1, axis=0)
    out = _dot(jax.nn.gelu(pre, approximate=True).astype(BF16), w2_ref[0])
    o_ref[0, 0] = jnp.where(j < NSA_GROUPS, _rms(out, kg_ref[...]), out)


def _nsa_compress(xr, pos, w1, w2, kg0):
    B, _, M, W = xr.shape
    G = NSA_GROUPS
    return pl.pallas_call(
        _cmp_kernel,
        out_shape=jax.ShapeDtypeStruct((B, 2 * G, M, HEAD_DIM), F32),
        grid=(B, 2 * G),
        in_specs=[pl.BlockSpec((1, 1, M, W), lambda b, j: (b, j, 0, 0)),
                  pl.BlockSpec((1, 1, 2 * W), lambda b, j: (j // G, 0, 0)),
                  pl.BlockSpec((1, 2 * W, HEAD_DIM), lambda b, j: (j // G, 0, 0)),
                  pl.BlockSpec((1, HEAD_DIM, HEAD_DIM), lambda b, j: (j // G, 0, 0)),
                  pl.BlockSpec((1, HEAD_DIM), lambda b, j: (0, 0))],
        out_specs=pl.BlockSpec((1, 1, M, HEAD_DIM), lambda b, j: (b, j, 0, 0)),
        compiler_params=_cparams(("arbitrary", "arbitrary")),
        name="nsa_compress",
    )(xr, pos.reshape(2, 1, 2 * W), w1.astype(BF16), w2.astype(BF16), kg0.reshape(1, HEAD_DIM))


def _cmp_attn_kernel(q_ref, kv_ref, ov_ref, cb_ref, qg_ref, oc_ref, sel_ref, qn_ref):
    qi = pl.program_id(1)
    T = q_ref.shape[1]
    NC = kv_ref.shape[2]
    G, J = NSA_GROUPS, NSA_HPG
    NS = sel_ref.shape[2] // G
    assert T == NC == LANES and T % CMP_STRIDE == 0 and G * NS == LANES
    t = qi * T + lax.broadcasted_iota(jnp.int32, (T, NC), 0)
    lane = lax.broadcasted_iota(jnp.int32, (T, NC), 1)
    vis = t - (lane * CMP_STRIDE + CMP_LEN - 1) >= 0
    visf = vis.astype(F32)
    shift = (qi * (T // CMP_STRIDE) + T // 2) % LANES
    imp = jnp.zeros((T, LANES), F32)
    vis4 = jnp.concatenate([vis] * J, axis=0)
    visf4 = jnp.concatenate([visf] * J, axis=0)
    for g in range(G):
        kc = kv_ref[0, g].astype(BF16)
        vc = kv_ref[0, G + g].astype(BF16)
        qhs = []
        for j in range(J):
            cols = slice((g * J + j) * HEAD_DIM, (g * J + j + 1) * HEAD_DIM)
            qhs.append((_rms(q_ref[0, :, cols], qg_ref[...]) * SCALE).astype(BF16))
            qn_ref[0, :, cols] = qhs[j]
        bias = jnp.concatenate([pltpu.roll(cb_ref[g * J + j], shift, axis=1) for j in range(J)], axis=0)
        s = jnp.where(vis4, _dot_nt(jnp.concatenate(qhs, axis=0), kc) + bias, NEG_INF)
        e = jnp.exp(s - jnp.max(s, axis=-1, keepdims=True))
        p = e / jnp.sum(e, axis=-1, keepdims=True) * visf4
        o = _dot(p.astype(BF16), vc)
        psum = p[0:T]
        for j in range(J):
            oc_ref[0, :, (g * J + j) * HEAD_DIM:(g * J + j + 1) * HEAD_DIM] = o[j * T:(j + 1) * T]
            if j:
                psum = psum + p[j * T:(j + 1) * T]
        imp = imp + _dot(psum, ov_ref[g], HIGHEST)
    mm = lane % NS
    cur = t // SLC_LEN
    forced = (mm == 0) | (mm == cur) | (mm == cur - 1)
    imp = jnp.where(forced, FORCE_SCORE, jnp.where(mm * SLC_LEN <= t, imp, -1.0))
    rank = jnp.zeros((T, LANES), F32)
    for r in range(1, NS):
        wrapped = mm >= NS - r
        other = jnp.where(wrapped, pltpu.roll(imp, NS - r, axis=1), pltpu.roll(imp, LANES - r, axis=1))
        rank = rank + jnp.where(other > imp, 1.0, jnp.where(other == imp, wrapped.astype(F32), 0.0))
    sel_ref[0] = jnp.where(rank < SLC_TOPN, jnp.where(imp >= 0.0, 1.0, 0.0), 0.0)


def _nsa_cmp_attn(y, kcvc, overlap, cmp_bias, q_g):
    B, S, _ = y.shape
    T = ATT_TILE
    NC = kcvc.shape[2]
    return pl.pallas_call(
        _cmp_attn_kernel,
        out_shape=(jax.ShapeDtypeStruct((B, S, NSA_WIDTH), F32),
                   jax.ShapeDtypeStruct((B, S, LANES), F32),
                   jax.ShapeDtypeStruct((B, S, NSA_WIDTH), BF16)),
        grid=(B, S // T),
        in_specs=[pl.BlockSpec((1, T, NSA_WIDTH), lambda b, i: (b, i, 0)),
                  pl.BlockSpec((1, 2 * NSA_GROUPS, NC, HEAD_DIM), lambda b, i: (b, 0, 0, 0)),
                  pl.BlockSpec(overlap.shape, lambda b, i: (0, 0, 0)),
                  pl.BlockSpec(cmp_bias.shape, lambda b, i: (0, 0, 0)),
                  pl.BlockSpec((1, HEAD_DIM), lambda b, i: (0, 0))],
        out_specs=(pl.BlockSpec((1, T, NSA_WIDTH), lambda b, i: (b, i, 0)),
                   pl.BlockSpec((1, T, LANES), lambda b, i: (b, i, 0)),
                   pl.BlockSpec((1, T, NSA_WIDTH), lambda b, i: (b, i, 0))),
        compiler_params=_cparams(("arbitrary", "arbitrary")),
        name="nsa_cmp_attn",
    )(y, kcvc, overlap, cmp_bias, q_g.reshape(1, HEAD_DIM))


def _kv_prep_kernel(ks_ref, kw_ref, kg1_ref, kg2_ref, ksn_ref, vs_ref, kwn_ref, vw_ref):
    KV = NSA_KV
    rows = ks_ref.shape[1]
    ones_col = (lax.broadcasted_iota(jnp.int32, (rows, LANES - HEAD_DIM), 1) == 0).astype(BF16)
    for g in range(NSA_GROUPS):
        cols = slice(g * HEAD_DIM, (g + 1) * HEAD_DIM)
        vcols = slice(KV + g * HEAD_DIM, KV + (g + 1) * HEAD_DIM)
        ksn_ref[0, :, cols] = _rms(ks_ref[0, :, cols], kg1_ref[...]).astype(BF16)
        kwn_ref[0, :, cols] = _rms(kw_ref[0, :, cols], kg2_ref[...]).astype(BF16)
        vs_ref[0, :, g * LANES:g * LANES + HEAD_DIM] = ks_ref[0, :, vcols].astype(BF16)
        vw_ref[0, :, g * LANES:g * LANES + HEAD_DIM] = kw_ref[0, :, vcols].astype(BF16)
        vs_ref[0, :, g * LANES + HEAD_DIM:(g + 1) * LANES] = ones_col
        vw_ref[0, :, g * LANES + HEAD_DIM:(g + 1) * LANES] = ones_col


def _nsa_kv_prep(y, k_g):
    B, S, _ = y.shape
    ksblk = (NSA_WIDTH + 2 * NSA_KV) // (2 * NSA_KV)
    kout = jax.ShapeDtypeStruct((B, S, NSA_KV), BF16)
    vout = jax.ShapeDtypeStruct((B, S, NSA_GROUPS * LANES), BF16)
    kspec = pl.BlockSpec((1, ROW_TILE, NSA_KV), lambda b, i: (b, i, 0))
    vspec = pl.BlockSpec((1, ROW_TILE, NSA_GROUPS * LANES), lambda b, i: (b, i, 0))
    return pl.pallas_call(
        _kv_prep_kernel,
        out_shape=(kout, vout, kout, vout),
        grid=(B, S // ROW_TILE),
        in_specs=[pl.BlockSpec((1, ROW_TILE, 2 * NSA_KV), lambda b, i: (b, i, ksblk)),
                  pl.BlockSpec((1, ROW_TILE, 2 * NSA_KV), lambda b, i: (b, i, ksblk + 1)),
                  pl.BlockSpec((1, HEAD_DIM), lambda b, i: (0, 0)),
                  pl.BlockSpec((1, HEAD_DIM), lambda b, i: (0, 0))],
        out_specs=(kspec, vspec, kspec, vspec),
        compiler_params=_cparams(("arbitrary", "arbitrary")),
        name="nsa_kv_prep",
    )(y, y, k_g[1].reshape(1, HEAD_DIM), k_g[2].reshape(1, HEAD_DIM))


SEL_KEY_PARTS = 4


def _sel_win_kernel(qn_ref, ks_ref, vs_ref, kw_ref, vw_ref, gl_ref, sel_ref, oc_ref, bias_ref, e_ref, bg_ref,
                    o_ref):
    qi = pl.program_id(1)
    T = qn_ref.shape[1]
    S = ks_ref.shape[1]
    nk = S // T
    G, J = NSA_GROUPS, NSA_HPG
    NS = e_ref.shape[0]
    wt = WINDOW // T
    nwin = wt + 1
    assert nk % SEL_KEY_PARTS == 0 and nk >= nwin

    def sel_tile(r):
        return jnp.where(r < 0, 4, jnp.minimum(r, 2))

    def win_tile(r):
        return jnp.where(r < 0, 4, jnp.where(r <= 1, r, jnp.where(r < wt, 2, jnp.where(r == wt, 3, 4))))

    def softmax_pv(s, v):
        p = jnp.exp(s - jnp.max(s, axis=-1, keepdims=True))
        acc = _dot(p.astype(BF16), v)
        return acc[:, :HEAD_DIM] / acc[:, HEAD_DIM:HEAD_DIM + 1]

    gates = jax.nn.sigmoid(gl_ref[0] + bg_ref[...])
    k0 = jnp.maximum(qi - wt, 0)
    wrows = pl.ds(pl.multiple_of(k0 * T, T), nwin * T)

    def attend(nkt):
        nkeys = nkt * T
        for g in range(G):
            gcols = slice(g * HEAD_DIM, (g + 1) * HEAD_DIM)
            vcols = slice(g * LANES, (g + 1) * LANES)
            q4 = jnp.concatenate(
                [qn_ref[0, :, (g * J + j) * HEAD_DIM:(g * J + j + 1) * HEAD_DIM] for j in range(J)], axis=0)
            selg = sel_ref[0, :, g * NS:(g + 1) * NS].astype(BF16)
            mask = (_dot(selg, e_ref[:, :nkeys]) - 1.0) * (-NEG_INF)
            bias = jnp.concatenate(
                [jnp.concatenate([bias_ref[g * J + j, sel_tile(qi - kt)] for kt in range(nkt)], axis=1) + mask
                 for j in range(J)], axis=0)
            o_s = softmax_pv(_dot_nt(q4, ks_ref[0, :nkeys, gcols]) + bias, vs_ref[0, :nkeys, vcols])
            wbias = jnp.concatenate(
                [jnp.concatenate([bias_ref[g * J + j, win_tile(qi - k0 - t)] for t in range(nwin)], axis=1)
                 for j in range(J)], axis=0)
            o_w = softmax_pv(_dot_nt(q4, kw_ref[0, wrows, gcols]) + wbias, vw_ref[0, wrows, vcols])
            for j in range(J):
                h = g * J + j
                cols = slice(h * HEAD_DIM, (h + 1) * HEAD_DIM)
                rows = slice(j * T, (j + 1) * T)
                o_ref[0, :, cols] = (gates[:, 3 * h:3 * h + 1] * oc_ref[0, :, cols]
                                     + gates[:, 3 * h + 1:3 * h + 2] * o_s[rows]
                                     + gates[:, 3 * h + 2:3 * h + 3] * o_w[rows])

    per = nk // SEL_KEY_PARTS
    for part in range(SEL_KEY_PARTS):
        pl.when((qi >= part * per) & (qi < (part + 1) * per))(functools.partial(attend, (part + 1) * per))


def _nsa_sel_win(qn, y, kv, sel, oc, bias_tiles, expand, b_gate):
    B, S, _ = y.shape
    T = ATT_TILE
    glblk = (NSA_WIDTH + 6 * NSA_KV) // LANES
    bg = jnp.zeros((1, LANES), F32).at[0, :3 * NSA_HEADS].set(b_gate)
    kspec = pl.BlockSpec((1, S, NSA_KV), lambda b, i: (b, 0, 0))
    vspec = pl.BlockSpec((1, S, NSA_GROUPS * LANES), lambda b, i: (b, 0, 0))
    return pl.pallas_call(
        _sel_win_kernel,
        out_shape=jax.ShapeDtypeStruct((B, S, NSA_WIDTH), F32),
        grid=(B, S // T),
        in_specs=[pl.BlockSpec((1, T, NSA_WIDTH), lambda b, i: (b, i, 0)),
                  kspec, vspec, kspec, vspec,
                  pl.BlockSpec((1, T, LANES), lambda b, i: (b, i, glblk)),
                  pl.BlockSpec((1, T, sel.shape[2]), lambda b, i: (b, i, 0)),
                  pl.BlockSpec((1, T, NSA_WIDTH), lambda b, i: (b, i, 0)),
                  pl.BlockSpec(bias_tiles.shape, lambda b, i: (0, 0, 0, 0)),
                  pl.BlockSpec(expand.shape, lambda b, i: (0, 0)),
                  pl.BlockSpec((1, LANES), lambda b, i: (0, 0))],
        out_specs=pl.BlockSpec((1, T, NSA_WIDTH), lambda b, i: (b, i, 0)),
        compiler_params=_cparams(("arbitrary", "arbitrary")),
        name="nsa_sel_win",
    )(qn, *kv, y, sel, oc, bias_tiles, expand, bg)


def _odd_mixer(x, sh, sc, gate, ng, w_in, b_gate, q_g, k_g, cmp_pos, cmp_w1, cmp_w2, bias_tiles, cmp_bias,
               w_out):
    B, S, _ = x.shape
    G = NSA_GROUPS
    w = jnp.zeros((D_MODEL, ODD_IN_PAD), BF16).at[:, :ODD_IN].set(w_in.astype(BF16))
    y = _norm_proj(x, sh, sc, ng, w)
    xr = y[:, :, NSA_WIDTH:NSA_WIDTH + 2 * NSA_KV].reshape(B, S, 2, G, HEAD_DIM)
    xr = xr.transpose(0, 2, 3, 1, 4).reshape(B, 2 * G, S // CMP_STRIDE, CMP_STRIDE * HEAD_DIM)
    kcvc = _nsa_compress(xr, cmp_pos.reshape(2, CMP_LEN * HEAD_DIM), cmp_w1, cmp_w2, k_g[0])
    n_slc = S // SLC_LEN
    starts = jnp.arange(S // CMP_STRIDE) * CMP_STRIDE
    bstart = jnp.arange(n_slc) * SLC_LEN
    real = (jnp.arange(S // CMP_STRIDE) < (S - CMP_LEN) // CMP_STRIDE + 1)[:, None]
    overlap = (real & (starts[:, None] < bstart[None, :] + SLC_LEN)
               & (starts[:, None] + CMP_LEN > bstart[None, :])).astype(F32)
    overlap = jnp.stack([jnp.pad(overlap, ((0, 0), (g * n_slc, (G - 1 - g) * n_slc))) for g in range(G)])
    expand = (jnp.arange(S)[None, :] // SLC_LEN == jnp.arange(n_slc)[:, None]).astype(BF16)
    oc, sel, qn = _nsa_cmp_attn(y, kcvc, overlap, cmp_bias, q_g)
    o = _nsa_sel_win(qn, y, _nsa_kv_prep(y, k_g), sel, oc, bias_tiles, expand, b_gate)
    return _out_proj(x, o, 0, o, 1, w_out.astype(BF16), gate)


def _even_mixer(x, sh, sc, gate, ng, w_in, b_f, conv_w, q_g, k_g, w_out):
    w = jnp.zeros((D_MODEL, EVEN_IN_PAD), BF16).at[:, :EVEN_IN].set(w_in.astype(BF16))
    y = _norm_proj(x, sh, sc, ng, w)
    conv = _short_conv(y, conv_w)
    attn = _fox_attention(y, b_f, q_g, k_g)
    return _out_proj(x, conv, 0, attn, 0, w_out.astype(BF16), gate)


def kernel(x, c, ada_w, ada_b, norm_g, even_w_in, even_b_f, even_conv_w, even_q_g, even_k_g, even_w_out, odd_w_in, odd_b_gate, odd_q_g, odd_k_g, odd_cmp_pos, odd_cmp_w1, odd_cmp_w2, odd_w_out, rel_table, peer_w_q, peer_keys, peer_u, peer_v):
    B = x.shape[0]
    mods = _ada(c, ada_w, ada_b)
    bias_tiles, cmp_bias = _relbias(rel_table)
    for layer in range(DEPTH):
        sh1, sc1, g1, sh2, sc2, g2 = [m.reshape(B, 1, D_MODEL) for m in jnp.split(mods[layer], 6, axis=-1)]
        i = layer // 2
        ng1 = norm_g[layer, 0].reshape(1, D_MODEL)
        ng2 = norm_g[layer, 1].reshape(1, D_MODEL)
        if layer % 2 == 0:
            x = _even_mixer(x, sh1, sc1, g1, ng1, even_w_in[i], even_b_f[i], even_conv_w[i], even_q_g[i],
                            even_k_g[i], even_w_out[i])
        else:
            x = _odd_mixer(x, sh1, sc1, g1, ng1, odd_w_in[i], odd_b_gate[i], odd_q_g[i], odd_k_g[i],
                           odd_cmp_pos[i], odd_cmp_w1[i], odd_cmp_w2[i], bias_tiles, cmp_bias, odd_w_out[i])
        x = _peer(x, sh2, sc2, g2, ng2, peer_w_q[layer], peer_keys[layer],
                  peer_u[layer].astype(BF16), peer_v[layer].T.astype(BF16))
    return x
```

```python
import functools
import math

import jax
import jax.numpy as jnp
from jax import lax
from jax.experimental import pallas as pl
from jax.experimental.pallas import tpu as pltpu

D_MODEL = 1024
DEPTH = 4
HEAD_DIM = 64
CONV_CH = 512
CONV_K = 3
FOX_HEADS = 8
FOX_WIDTH = FOX_HEADS * HEAD_DIM
EVEN_IN = 3 * CONV_CH + 3 * FOX_WIDTH + FOX_HEADS
NSA_GROUPS = 4
NSA_HPG = 4
NSA_HEADS = NSA_GROUPS * NSA_HPG
NSA_WIDTH = NSA_HEADS * HEAD_DIM
NSA_KV = NSA_GROUPS * HEAD_DIM
CMP_LEN = 32
CMP_STRIDE = 16
SLC_LEN = 64
SLC_TOPN = 16
WINDOW = 512
ODD_IN = NSA_WIDTH + 6 * NSA_KV + 3 * NSA_HEADS
REL_BUCKETS = 32
REL_MAX_DIST = 128
PEER_HEADS = 8
PEER_NKEYS = 128
PEER_EXPERTS = PEER_NKEYS * PEER_NKEYS
PEER_QDIM = 128
PEER_TOPK = 16
EPS = 1e-6
NEG_INF = -1e30
FORCE_SCORE = 1e6

LANES = 128
F32 = jnp.float32
BF16 = jnp.bfloat16
HIGHEST = lax.Precision.HIGHEST
VMEM_LIMIT = 56 * 1024 * 1024

EVEN_IN_PAD = -(-EVEN_IN // LANES) * LANES
ODD_IN_PAD = -(-ODD_IN // LANES) * LANES
ROW_TILE = 512
ATT_TILE = 128
FOX_TILE = 256
PEER_TN = 512
PEER_EC = 2048
SCALE = HEAD_DIM ** -0.5


def _cparams(sem):
    return pltpu.CompilerParams(dimension_semantics=sem, vmem_limit_bytes=VMEM_LIMIT)


def _dot(a, b, precision=None):
    return jnp.dot(a, b, precision=precision, preferred_element_type=F32)


def _dot_nt(a, b, precision=None):
    return lax.dot_general(a, b, (((1,), (1,)), ((), ())), precision=precision,
                           preferred_element_type=F32)


def _rms(x, g):
    return x * lax.rsqrt(jnp.mean(x * x, axis=-1, keepdims=True) + EPS) * g


def _ada_kernel(c_ref, w_ref, b_ref, o_ref):
    c = c_ref[...]
    sc = c * jax.nn.sigmoid(c)
    o_ref[0] = _dot(sc, w_ref[0], HIGHEST) + b_ref[0]


def _ada(c, ada_w, ada_b):
    B, D = c.shape
    E = ada_w.shape[2]
    tn = 2048
    return pl.pallas_call(
        _ada_kernel,
        out_shape=jax.ShapeDtypeStruct((DEPTH, B, E), F32),
        grid=(DEPTH, E // tn),
        in_specs=[pl.BlockSpec((B, D), lambda l, j: (0, 0)),
                  pl.BlockSpec((1, D, tn), lambda l, j: (l, 0, j)),
                  pl.BlockSpec((1, 1, tn), lambda l, j: (l, 0, j))],
        out_specs=pl.BlockSpec((1, B, tn), lambda l, j: (l, 0, j)),
        compiler_params=_cparams(("arbitrary", "arbitrary")),
        name="ada_mod",
    )(c, ada_w, ada_b.reshape(DEPTH, 1, E))


def _norm_proj_kernel(x_ref, sh_ref, sc_ref, ng_ref, w_ref, y_ref):
    h = _rms(x_ref[0], ng_ref[...]) * (1.0 + sc_ref[0]) + sh_ref[0]
    y_ref[0] = _dot(h.astype(BF16), w_ref[...])


def _norm_proj(x, sh, sc, ng, w):
    B, S, D = x.shape
    N = w.shape[1]
    return pl.pallas_call(
        _norm_proj_kernel,
        out_shape=jax.ShapeDtypeStruct((B, S, N), F32),
        grid=(B, S // ROW_TILE),
        in_specs=[pl.BlockSpec((1, ROW_TILE, D), lambda b, i: (b, i, 0)),
                  pl.BlockSpec((1, 1, D), lambda b, i: (b, 0, 0)),
                  pl.BlockSpec((1, 1, D), lambda b, i: (b, 0, 0)),
                  pl.BlockSpec((1, D), lambda b, i: (0, 0)),
                  pl.BlockSpec((D, N), lambda b, i: (0, 0))],
        out_specs=pl.BlockSpec((1, ROW_TILE, N), lambda b, i: (b, i, 0)),
        compiler_params=_cparams(("arbitrary", "arbitrary")),
        name="norm_proj",
    )(x, sh, sc, ng, w)


def _out_proj_kernel(x_ref, a_ref, b_ref, wa_ref, wb_ref, g_ref, o_ref):
    y = _dot(a_ref[0].astype(BF16), wa_ref[...]) + _dot(b_ref[0].astype(BF16), wb_ref[...])
    o_ref[0] = x_ref[0] + g_ref[0] * y


def _out_proj(x, a, a_blk, b, b_blk, w, gate):
    B, S, D = x.shape
    half = w.shape[0] // 2
    wa, wb = w[:half], w[half:]
    return pl.pallas_call(
        _out_proj_kernel,
        out_shape=jax.ShapeDtypeStruct((B, S, D), F32),
        grid=(B, S // ROW_TILE),
        in_specs=[pl.BlockSpec((1, ROW_TILE, D), lambda b, i: (b, i, 0)),
                  pl.BlockSpec((1, ROW_TILE, half), lambda b, i: (b, i, a_blk)),
                  pl.BlockSpec((1, ROW_TILE, half), lambda b, i: (b, i, b_blk)),
                  pl.BlockSpec((half, D), lambda b, i: (0, 0)),
                  pl.BlockSpec((half, D), lambda b, i: (0, 0)),
                  pl.BlockSpec((1, 1, D), lambda b, i: (b, 0, 0))],
        out_specs=pl.BlockSpec((1, ROW_TILE, D), lambda b, i: (b, i, 0)),
        compiler_params=_cparams(("arbitrary", "arbitrary")),
        name="out_proj",
    )(x, a, b, wa, wb, gate)


def _conv_kernel(cb_ref, cc_ref, ch_ref, w_ref, o_ref):
    u = cc_ref[0] * ch_ref[0]
    row = lax.broadcasted_iota(jnp.int32, u.shape, 0)
    u1 = jnp.where(row >= 1, pltpu.roll(u, 1, axis=0), 0.0)
    u2 = jnp.where(row >= 2, pltpu.roll(u, 2, axis=0), 0.0)
    w = w_ref[...]
    o_ref[0] = cb_ref[0] * (w[0:1] * u2 + w[1:2] * u1 + w[2:3] * u)


def _short_conv(y, conv_w):
    B, S, _ = y.shape
    nblk = CONV_CH // LANES
    w = jnp.zeros((8, CONV_CH), F32).at[:CONV_K].set(conv_w)
    return pl.pallas_call(
        _conv_kernel,
        out_shape=jax.ShapeDtypeStruct((B, S, CONV_CH), F32),
        grid=(B, nblk),
        in_specs=[pl.BlockSpec((1, S, LANES), lambda b, j: (b, 0, j)),
                  pl.BlockSpec((1, S, LANES), lambda b, j: (b, 0, nblk + j)),
                  pl.BlockSpec((1, S, LANES), lambda b, j: (b, 0, 2 * nblk + j)),
                  pl.BlockSpec((8, LANES), lambda b, j: (0, j))],
        out_specs=pl.BlockSpec((1, S, LANES), lambda b, j: (b, 0, j)),
        compiler_params=_cparams(("arbitrary", "arbitrary")),
        name="short_conv",
    )(y, y, y, w)


def _cumsum_rows(x):
    n = x.shape[0]
    row = lax.broadcasted_iota(jnp.int32, x.shape, 0)
    sh = 1
    while sh < n:
        x = x + jnp.where(row >= sh, pltpu.roll(x, sh, axis=0), 0.0)
        sh *= 2
    return x


FOX_KEY_PARTS = 4


def _fox_kernel(q_ref, k_ref, v_ref, f_ref, bf_ref, qg_ref, kg_ref, o_ref,
                kn_scr, vb_scr, cum_scr, cumrow_scr):
    qi = pl.program_id(1)
    S = k_ref.shape[1]
    T = q_ref.shape[1]
    nq = S // T

    @pl.when(qi == 0)
    def _prep():
        ones_col = (lax.broadcasted_iota(jnp.int32, (S, LANES - HEAD_DIM), 1) == 0).astype(BF16)
        for h in range(FOX_HEADS):
            cols = slice(h * HEAD_DIM, (h + 1) * HEAD_DIM)
            kn_scr[:, cols] = _rms(k_ref[0, :, cols], kg_ref[...]).astype(BF16)
            vb_scr[:, h * LANES:h * LANES + HEAD_DIM] = v_ref[0, :, cols].astype(BF16)
            vb_scr[:, h * LANES + HEAD_DIM:(h + 1) * LANES] = ones_col
        logf = jax.nn.log_sigmoid(f_ref[0] + bf_ref[...])
        cum = _cumsum_rows(logf)
        cum_scr[...] = cum
        for kt in range(S // T):
            cumrow_scr[:, kt * T:(kt + 1) * T] = cum[kt * T:(kt + 1) * T, :].T[0:8, :]

    cq = cum_scr[pl.ds(pl.multiple_of(qi * T, T), T), :]

    def attend(nkeys):
        row = qi * T + lax.broadcasted_iota(jnp.int32, (T, nkeys), 0)
        causal = jnp.where(lax.broadcasted_iota(jnp.int32, (T, nkeys), 1) <= row, 0.0, NEG_INF)
        for h in range(FOX_HEADS):
            cols = slice(h * HEAD_DIM, (h + 1) * HEAD_DIM)
            qh = (_rms(q_ref[0, :, cols], qg_ref[...]) * SCALE).astype(BF16)
            s = _dot_nt(qh, kn_scr[0:nkeys, cols]) + ((cq[:, h:h + 1] - cumrow_scr[h:h + 1, 0:nkeys]) + causal)
            p = jnp.exp(s - jnp.max(s, axis=-1, keepdims=True))
            acc = _dot(p.astype(BF16), vb_scr[0:nkeys, h * LANES:(h + 1) * LANES])
            o_ref[0, :, cols] = acc[:, :HEAD_DIM] / acc[:, HEAD_DIM:HEAD_DIM + 1]

    parts = min(FOX_KEY_PARTS, nq)
    per = nq // parts
    for part in range(parts):
        pl.when((qi >= part * per) & (qi < (part + 1) * per))(functools.partial(attend, (part + 1) * (S // parts)))


def _fox_attention(y, b_f, q_g, k_g):
    B, S, _ = y.shape
    T = FOX_TILE
    qblk = 3 * CONV_CH // FOX_WIDTH
    fblk = (3 * CONV_CH + 3 * FOX_WIDTH) // LANES
    bf = jnp.zeros((1, LANES), F32).at[0, :FOX_HEADS].set(b_f)
    return pl.pallas_call(
        _fox_kernel,
        out_shape=jax.ShapeDtypeStruct((B, S, FOX_WIDTH), F32),
        grid=(B, S // T),
        in_specs=[pl.BlockSpec((1, T, FOX_WIDTH), lambda b, i: (b, i, qblk)),
                  pl.BlockSpec((1, S, FOX_WIDTH), lambda b, i: (b, 0, qblk + 1)),
                  pl.BlockSpec((1, S, FOX_WIDTH), lambda b, i: (b, 0, qblk + 2)),
                  pl.BlockSpec((1, S, LANES), lambda b, i: (b, 0, fblk)),
                  pl.BlockSpec((1, LANES), lambda b, i: (0, 0)),
                  pl.BlockSpec((1, HEAD_DIM), lambda b, i: (0, 0)),
                  pl.BlockSpec((1, HEAD_DIM), lambda b, i: (0, 0))],
        out_specs=pl.BlockSpec((1, T, FOX_WIDTH), lambda b, i: (b, i, 0)),
        scratch_shapes=[pltpu.VMEM((S, FOX_WIDTH), BF16),
                        pltpu.VMEM((S, FOX_HEADS * LANES), BF16),
                        pltpu.VMEM((S, LANES), F32),
                        pltpu.VMEM((8, S), F32)],
        compiler_params=_cparams(("arbitrary", "arbitrary")),
        name="fox_attention",
    )(y, y, y, y, bf, q_g.reshape(1, HEAD_DIM), k_g.reshape(1, HEAD_DIM))


def _gelu_tanh(x):
    c0 = math.sqrt(2.0 / math.pi)
    return (0.5 * x) * (1.0 + jnp.tanh(x * (c0 + (c0 * 0.044715) * (x * x))))


def _extract_top(work, iota, k, exact):
    nrows = work.shape[0]
    vals = []
    for _ in range(k):
        m = jnp.max(work, axis=0, keepdims=True)
        if exact:
            idx = jnp.min(jnp.where(work == m, iota, nrows), axis=0, keepdims=True)
            work = jnp.where(iota == idx, -jnp.inf, work)
        else:
            work = jnp.where(work == m, -jnp.inf, work)
        vals.append(m)
    return vals, work


def _removed(work):
    return jnp.sum(jnp.where(work == -jnp.inf, 1.0, 0.0), axis=0, keepdims=True)


def _peer_kernel(x_ref, sh_ref, sc_ref, gt_ref, ng_ref, wqh_ref, wql_ref, keys_ref, u_ref, vt_ref, o_ref,
                 ht_scr, qt_scr, sp_scr, t0_scr, t1_scr, tau_scr, a_scr, p_scr, acc_scr):
    c = pl.program_id(2)
    nc = pl.num_programs(2)
    tn = x_ref.shape[1]
    nlt = tn // LANES
    half = PEER_QDIM // 2

    @pl.when(c == 0)
    def _prep():
        h = _rms(x_ref[0], ng_ref[...]) * (1.0 + sc_ref[0]) + sh_ref[0]
        hh = h.astype(BF16)
        hl = (h - hh.astype(F32)).astype(BF16)
        cw = 2 * LANES
        for k in range(h.shape[1] // cw):
            cs = slice(k * cw, (k + 1) * cw)
            q = _dot(hh, wqh_ref[:, cs]) + (_dot(hh, wql_ref[:, cs]) + _dot(hl, wqh_ref[:, cs]))
            qt_scr[cs, :] = q.T
            ht_scr[cs, :] = h[:, cs].T.astype(BF16)
        acc_scr[...] = jnp.zeros_like(acc_scr)
        iota_k = lax.broadcasted_iota(jnp.int32, (PEER_NKEYS, LANES), 0)
        iota_c = lax.broadcasted_iota(jnp.int32, (PEER_TOPK + 8 * 8, LANES), 0)

        n_pad = sum(8 - PEER_TOPK // (r0 + 1) for r0 in range(2, 8))

        def select(hd, lt, exact):
            lanes = slice(lt * LANES, (lt + 1) * LANES)
            s0 = sp_scr[0, :, lanes]
            s1 = sp_scr[1, :, lanes]
            sv0, w0 = _extract_top(s0, iota_k, PEER_TOPK, exact)
            sv1, w1 = _extract_top(s1, iota_k, PEER_TOPK, exact)
            s0m = jnp.where(w0 == -jnp.inf, s0, -jnp.inf)
            s1m = jnp.where(w1 == -jnp.inf, s1, -jnp.inf)
            sv0_all = jnp.concatenate(sv0, axis=0)
            sv1_all = jnp.concatenate(sv1, axis=0)
            parts = [sv0[0] + sv1_all, sv0[1] + sv1_all[:8]]
            for r0 in range(2, 8):
                parts.append(jnp.where(iota_c[:8] < PEER_TOPK // (r0 + 1), sv0[r0] + sv1_all[:8], -jnp.inf))
            parts.append(sv0_all[8:] + sv1[0])
            cv, wc = _extract_top(jnp.concatenate(parts, axis=0), iota_c, PEER_TOPK, exact)
            z = cv[0] - cv[0]
            for r in range(PEER_TOPK):
                z = z + jnp.exp(cv[r] - cv[0])
            nrt = PEER_NKEYS // 8
            t0_scr[hd, lt, :, 0] = s0m.reshape(nrt, 8, LANES)
            t0_scr[hd, lt, :, 1] = jnp.exp(s0m - sv0[0]).reshape(nrt, 8, LANES)
            t1_scr[hd, lt, :, 0] = s1m.reshape(nrt, 8, LANES)
            t1_scr[hd, lt, :, 1] = (jnp.exp(s1m - sv1[0]) / z).reshape(nrt, 8, LANES)
            tau_scr[hd, lt] = jnp.broadcast_to(cv[PEER_TOPK - 1], (8, LANES))
            return ((_removed(w0) != PEER_TOPK) | (_removed(w1) != PEER_TOPK)
                    | (_removed(wc) != PEER_TOPK + n_pad))

        def head_body(hd, carry):
            for p in range(2):
                qs = qt_scr[pl.ds(pl.multiple_of((hd * 2 + p) * half, half), half), :]
                sp_scr[p] = _dot(keys_ref[p], qs, HIGHEST)
            tied = jnp.zeros((1, LANES), F32)
            for lt in range(nlt):
                tied = jnp.maximum(tied, jnp.where(select(hd, lt, False), 1.0, 0.0))

            @pl.when(jnp.max(tied) > 0.0)
            def _redo():
                for lt in range(nlt):
                    select(hd, lt, True)
            return carry

        lax.fori_loop(0, PEER_HEADS, head_body, 0)

    nsub = 8
    nrq = 32
    nparts = PEER_EC // PEER_NKEYS // nsub
    for part in range(nparts):
        er = slice(part * nsub * PEER_NKEYS, (part + 1) * nsub * PEER_NKEYS)
        for lh in range(nlt // 2):
            lanes2 = slice(lh * 2 * LANES, (lh + 1) * 2 * LANES)
            a_scr[er, lanes2] = _gelu_tanh(_dot(u_ref[er, :], ht_scr[:, lanes2]))

    def weights_tile(part, lt, rq, prev):
        lanes = slice(lt * LANES, (lt + 1) * LANES)
        jt = slice(rq * nrq // 8, (rq + 1) * nrq // 8)
        blk = c * (PEER_EC // PEER_NKEYS // 8) + part * nsub // 8
        zero = 0.0 if prev is None else jnp.where(prev != prev, prev, 0.0)
        w = [jnp.zeros((nrq, LANES), F32) for _ in range(nsub)]
        for hd in range(PEER_HEADS):
            s1 = t1_scr[hd, lt, jt, 0].reshape(nrq, LANES)
            e1 = t1_scr[hd, lt, jt, 1].reshape(nrq, LANES)
            s0 = t0_scr[hd, lt, blk, 0] + zero
            e0 = t0_scr[hd, lt, blk, 1] + zero
            tau = tau_scr[hd, lt, 0:1, :]
            for ii in range(nsub):
                keep = s1 + s0[ii:ii + 1] >= tau
                w[ii] = w[ii] + jnp.where(keep, e1 * e0[ii:ii + 1], 0.0)
        for ii in range(nsub):
            rows = slice((part * nsub + ii) * PEER_NKEYS + rq * nrq, (part * nsub + ii) * PEER_NKEYS + (rq + 1) * nrq)
            w[ii] = w[ii] * a_scr[rows, lanes]
            p_scr[rows, lanes] = w[ii].astype(BF16)
        return w[nsub - 1][0:1, :]

    in_flight = 4
    done = [None] * in_flight
    for part in range(nparts):
        for lt in range(nlt):
            for rq in range(PEER_NKEYS // nrq):
                done.append(weights_tile(part, lt, rq, done[-in_flight]))
    acc_scr[...] += _dot(vt_ref[...], p_scr[...])

    @pl.when(c == nc - 1)
    def _fin():
        o_ref[0] = x_ref[0] + gt_ref[0] * acc_scr[...].T


def _peer(x, sh, sc, gate, ng, w_q, keys, u_b, vt_b):
    B, S, D = x.shape
    tn = PEER_TN
    nc = PEER_EXPERTS // PEER_EC
    tab = pltpu.VMEM((PEER_HEADS, tn // LANES, PEER_NKEYS // 8, 2, 8, LANES), F32)
    wq_hi = w_q.astype(BF16)
    wq_lo = (w_q - wq_hi.astype(F32)).astype(BF16)
    wspec = pl.BlockSpec((D, PEER_HEADS * PEER_QDIM), lambda b, i, c: (0, 0), pipeline_mode=pl.Buffered(1))
    return pl.pallas_call(
        _peer_kernel,
        out_shape=jax.ShapeDtypeStruct((B, S, D), F32),
        grid=(B, S // tn, nc),
        in_specs=[pl.BlockSpec((1, tn, D), lambda b, i, c: (b, i, 0)),
                  pl.BlockSpec((1, 1, D), lambda b, i, c: (b, 0, 0)),
                  pl.BlockSpec((1, 1, D), lambda b, i, c: (b, 0, 0)),
                  pl.BlockSpec((1, 1, D), lambda b, i, c: (b, 0, 0)),
                  pl.BlockSpec((1, D), lambda b, i, c: (0, 0)),
                  wspec, wspec,
                  pl.BlockSpec((2, PEER_NKEYS, PEER_QDIM // 2), lambda b, i, c: (0, 0, 0)),
                  pl.BlockSpec((PEER_EC, D), lambda b, i, c: (c, 0)),
                  pl.BlockSpec((D, PEER_EC), lambda b, i, c: (0, c))],
        out_specs=pl.BlockSpec((1, tn, D), lambda b, i, c: (b, i, 0)),
        scratch_shapes=[pltpu.VMEM((D, tn), BF16),
                        pltpu.VMEM((PEER_HEADS * PEER_QDIM, tn), F32),
                        pltpu.VMEM((2, PEER_NKEYS, tn), F32),
                        tab, tab,
                        pltpu.VMEM((PEER_HEADS, tn // LANES, 8, LANES), F32),
                        pltpu.VMEM((PEER_EC, tn), F32),
                        pltpu.VMEM((PEER_EC, tn), BF16),
                        pltpu.VMEM((D, tn), F32)],
        compiler_params=_cparams(("arbitrary", "arbitrary", "arbitrary")),
        name="peer",
    )(x, sh, sc, gate, ng, wq_hi, wq_lo, keys, u_b, vt_b)


def _rel_bucket(dist):
    n = jnp.maximum(dist, 0)
    max_exact = REL_BUCKETS // 2
    nf = jnp.maximum(n, 1).astype(F32)
    large = max_exact + (jnp.log(nf / max_exact) / math.log(REL_MAX_DIST / max_exact)
                         * (REL_BUCKETS - max_exact)).astype(jnp.int32)
    large = jnp.minimum(large, REL_BUCKETS - 1)
    return jnp.where(n < max_exact, n, large)


def _bias_lookup(bucket, tab_ref, h):
    bias = jnp.zeros(bucket.shape, F32)
    for b in range(REL_BUCKETS):
        bias = jnp.where(bucket == b, tab_ref[b, h], bias)
    return bias


N_BIAS_TILES = 5


def _relbias_kernel(tab_ref, o_ref, c_ref):
    h = pl.program_id(0)
    T = o_ref.shape[2]
    a = lax.broadcasted_iota(jnp.int32, (T, T), 0)
    b = lax.broadcasted_iota(jnp.int32, (T, T), 1)
    last = tab_ref[REL_BUCKETS - 1, h]
    o_ref[0, 0] = jnp.where(a >= b, _bias_lookup(_rel_bucket(a - b), tab_ref, h), NEG_INF)
    o_ref[0, 1] = _bias_lookup(_rel_bucket(a - b + T), tab_ref, h)
    o_ref[0, 2] = jnp.full((T, T), last, F32)
    o_ref[0, 3] = jnp.where(a < b, last, NEG_INF)
    o_ref[0, 4] = jnp.full((T, T), NEG_INF, F32)
    dist = a - ((b - T // 2) * CMP_STRIDE + CMP_LEN - 1)
    c_ref[0] = jnp.where(dist >= 0, _bias_lookup(_rel_bucket(dist), tab_ref, h), last)


def _relbias(rel_table):
    T = ATT_TILE
    assert 2 * T - (T - 1) >= REL_MAX_DIST
    assert WINDOW % T == 0
    return pl.pallas_call(
        _relbias_kernel,
        out_shape=(jax.ShapeDtypeStruct((NSA_HEADS, N_BIAS_TILES, T, T), F32),
                   jax.ShapeDtypeStruct((NSA_HEADS, T, T), F32)),
        grid=(NSA_HEADS,),
        in_specs=[pl.BlockSpec(memory_space=pltpu.SMEM)],
        out_specs=(pl.BlockSpec((1, N_BIAS_TILES, T, T), lambda h: (h, 0, 0, 0)),
                   pl.BlockSpec((1, T, T), lambda h: (h, 0, 0))),
        compiler_params=_cparams(("arbitrary",)),
        name="rel_bias_tiles",
    )(rel_table)


def _cmp_kernel(x_ref, pos_ref, w1_ref, w2_ref, kg_ref, o_ref):
    j = pl.program_id(1)
    half = CMP_STRIDE * HEAD_DIM
    x = x_ref[0, 0]
    pos = pos_ref[0]
    pa = _dot((x + pos[:, :half]).astype(BF16), w1_ref[0, :half, :])
    pb = _dot((x + pos[:, half:]).astype(BF16), w1_ref[0, half:, :])
    pre = pa + pltpu.roll(pb, pb.shape[0] - 1, axis=0)
    out = _dot(jax.nn.gelu(pre, approximate=True).astype(BF16), w2_ref[0])
    o_ref[0, 0] = jnp.where(j < NSA_GROUPS, _rms(out, kg_ref[...]), out)


def _nsa_compress(xr, pos, w1, w2, kg0):
    B, _, M, W = xr.shape
    G = NSA_GROUPS
    return pl.pallas_call(
        _cmp_kernel,
        out_shape=jax.ShapeDtypeStruct((B, 2 * G, M, HEAD_DIM), F32),
        grid=(B, 2 * G),
        in_specs=[pl.BlockSpec((1, 1, M, W), lambda b, j: (b, j, 0, 0)),
                  pl.BlockSpec((1, 1, 2 * W), lambda b, j: (j // G, 0, 0)),
                  pl.BlockSpec((1, 2 * W, HEAD_DIM), lambda b, j: (j // G, 0, 0)),
                  pl.BlockSpec((1, HEAD_DIM, HEAD_DIM), lambda b, j: (j // G, 0, 0)),
                  pl.BlockSpec((1, HEAD_DIM), lambda b, j: (0, 0))],
        out_specs=pl.BlockSpec((1, 1, M, HEAD_DIM), lambda b, j: (b, j, 0, 0)),
        compiler_params=_cparams(("arbitrary", "arbitrary")),
        name="nsa_compress",
    )(xr, pos.reshape(2, 1, 2 * W), w1.astype(BF16), w2.astype(BF16), kg0.reshape(1, HEAD_DIM))


def _cmp_attn_kernel(q_ref, kv_ref, ov_ref, cb_ref, qg_ref, oc_ref, sel_ref, qn_ref):
    qi = pl.program_id(1)
    T = q_ref.shape[1]
    NC = kv_ref.shape[2]
    G, J = NSA_GROUPS, NSA_HPG
    NS = sel_ref.shape[2] // G
    assert T == NC == LANES and T % CMP_STRIDE == 0 and G * NS == LANES
    t = qi * T + lax.broadcasted_iota(jnp.int32, (T, NC), 0)
    lane = lax.broadcasted_iota(jnp.int32, (T, NC), 1)
    vis = t - (lane * CMP_STRIDE + CMP_LEN - 1) >= 0
    visf = vis.astype(F32)
    shift = (qi * (T // CMP_STRIDE) + T // 2) % LANES
    imp = jnp.zeros((T, LANES), F32)
    vis4 = jnp.concatenate([vis] * J, axis=0)
    visf4 = jnp.concatenate([visf] * J, axis=0)
    for g in range(G):
        kc = kv_ref[0, g].astype(BF16)
        vc = kv_ref[0, G + g].astype(BF16)
        qhs = []
        for j in range(J):
            cols = slice((g * J + j) * HEAD_DIM, (g * J + j + 1) * HEAD_DIM)
            qhs.append((_rms(q_ref[0, :, cols], qg_ref[...]) * SCALE).astype(BF16))
            qn_ref[0, :, cols] = qhs[j]
        bias = jnp.concatenate([pltpu.roll(cb_ref[g * J + j], shift, axis=1) for j in range(J)], axis=0)
        s = jnp.where(vis4, _dot_nt(jnp.concatenate(qhs, axis=0), kc) + bias, NEG_INF)
        e = jnp.exp(s - jnp.max(s, axis=-1, keepdims=True))
        p = e / jnp.sum(e, axis=-1, keepdims=True) * visf4
        o = _dot(p.astype(BF16), vc)
        psum = p[0:T]
        for j in range(J):
            oc_ref[0, :, (g * J + j) * HEAD_DIM:(g * J + j + 1) * HEAD_DIM] = o[j * T:(j + 1) * T]
            if j:
                psum = psum + p[j * T:(j + 1) * T]
        imp = imp + _dot(psum, ov_ref[g], HIGHEST)
    mm = lane % NS
    cur = t // SLC_LEN
    forced = (mm == 0) | (mm == cur) | (mm == cur - 1)
    imp = jnp.where(forced, FORCE_SCORE, jnp.where(mm * SLC_LEN <= t, imp, -1.0))
    rank = jnp.zeros((T, LANES), F32)
    for r in range(1, NS):
        wrapped = mm >= NS - r
        other = jnp.where(wrapped, pltpu.roll(imp, NS - r, axis=1), pltpu.roll(imp, LANES - r, axis=1))
        rank = rank + jnp.where(other > imp, 1.0, jnp.where(other == imp, wrapped.astype(F32), 0.0))
    sel_ref[0] = jnp.where(rank < SLC_TOPN, jnp.where(imp >= 0.0, 1.0, 0.0), 0.0)


def _nsa_cmp_attn(y, kcvc, overlap, cmp_bias, q_g):
    B, S, _ = y.shape
    T = ATT_TILE
    NC = kcvc.shape[2]
    return pl.pallas_call(
        _cmp_attn_kernel,
        out_shape=(jax.ShapeDtypeStruct((B, S, NSA_WIDTH), F32),
                   jax.ShapeDtypeStruct((B, S, LANES), F32),
                   jax.ShapeDtypeStruct((B, S, NSA_WIDTH), BF16)),
        grid=(B, S // T),
        in_specs=[pl.BlockSpec((1, T, NSA_WIDTH), lambda b, i: (b, i, 0)),
                  pl.BlockSpec((1, 2 * NSA_GROUPS, NC, HEAD_DIM), lambda b, i: (b, 0, 0, 0)),
                  pl.BlockSpec(overlap.shape, lambda b, i: (0, 0, 0)),
                  pl.BlockSpec(cmp_bias.shape, lambda b, i: (0, 0, 0)),
                  pl.BlockSpec((1, HEAD_DIM), lambda b, i: (0, 0))],
        out_specs=(pl.BlockSpec((1, T, NSA_WIDTH), lambda b, i: (b, i, 0)),
                   pl.BlockSpec((1, T, LANES), lambda b, i: (b, i, 0)),
                   pl.BlockSpec((1, T, NSA_WIDTH), lambda b, i: (b, i, 0))),
        compiler_params=_cparams(("arbitrary", "arbitrary")),
        name="nsa_cmp_attn",
    )(y, kcvc, overlap, cmp_bias, q_g.reshape(1, HEAD_DIM))


def _kv_prep_kernel(ks_ref, kw_ref, kg1_ref, kg2_ref, ksn_ref, vs_ref, kwn_ref, vw_ref):
    KV = NSA_KV
    rows = ks_ref.shape[1]
    ones_col = (lax.broadcasted_iota(jnp.int32, (rows, LANES - HEAD_DIM), 1) == 0).astype(BF16)
    for g in range(NSA_GROUPS):
        cols = slice(g * HEAD_DIM, (g + 1) * HEAD_DIM)
        vcols = slice(KV + g * HEAD_DIM, KV + (g + 1) * HEAD_DIM)
        ksn_ref[0, :, cols] = _rms(ks_ref[0, :, cols], kg1_ref[...]).astype(BF16)
        kwn_ref[0, :, cols] = _rms(kw_ref[0, :, cols], kg2_ref[...]).astype(BF16)
        vs_ref[0, :, g * LANES:g * LANES + HEAD_DIM] = ks_ref[0, :, vcols].astype(BF16)
        vw_ref[0, :, g * LANES:g * LANES + HEAD_DIM] = kw_ref[0, :, vcols].astype(BF16)
        vs_ref[0, :, g * LANES + HEAD_DIM:(g + 1) * LANES] = ones_col
        vw_ref[0, :, g * LANES + HEAD_DIM:(g + 1) * LANES] = ones_col


def _nsa_kv_prep(y, k_g):
    B, S, _ = y.shape
    ksblk = (NSA_WIDTH + 2 * NSA_KV) // (2 * NSA_KV)
    kout = jax.ShapeDtypeStruct((B, S, NSA_KV), BF16)
    vout = jax.ShapeDtypeStruct((B, S, NSA_GROUPS * LANES), BF16)
    kspec = pl.BlockSpec((1, ROW_TILE, NSA_KV), lambda b, i: (b, i, 0))
    vspec = pl.BlockSpec((1, ROW_TILE, NSA_GROUPS * LANES), lambda b, i: (b, i, 0))
    return pl.pallas_call(
        _kv_prep_kernel,
        out_shape=(kout, vout, kout, vout),
        grid=(B, S // ROW_TILE),
        in_specs=[pl.BlockSpec((1, ROW_TILE, 2 * NSA_KV), lambda b, i: (b, i, ksblk)),
                  pl.BlockSpec((1, ROW_TILE, 2 * NSA_KV), lambda b, i: (b, i, ksblk + 1)),
                  pl.BlockSpec((1, HEAD_DIM), lambda b, i: (0, 0)),
                  pl.BlockSpec((1, HEAD_DIM), lambda b, i: (0, 0))],
        out_specs=(kspec, vspec, kspec, vspec),
        compiler_params=_cparams(("arbitrary", "arbitrary")),
        name="nsa_kv_prep",
    )(y, y, k_g[1].reshape(1, HEAD_DIM), k_g[2].reshape(1, HEAD_DIM))


SEL_KEY_PARTS = 4


def _sel_win_kernel(qn_ref, ks_ref, vs_ref, kw_ref, vw_ref, gl_ref, sel_ref, oc_ref, bias_ref, e_ref, bg_ref,
                    o_ref):
    qi = pl.program_id(1)
    T = qn_ref.shape[1]
    S = ks_ref.shape[1]
    nk = S // T
    G, J = NSA_GROUPS, NSA_HPG
    NS = e_ref.shape[0]
    wt = WINDOW // T
    nwin = wt + 1
    assert nk % SEL_KEY_PARTS == 0 and nk >= nwin

    def sel_tile(r):
        return jnp.where(r < 0, 4, jnp.minimum(r, 2))

    def win_tile(r):
        return jnp.where(r < 0, 4, jnp.where(r <= 1, r, jnp.where(r < wt, 2, jnp.where(r == wt, 3, 4))))

    def softmax_pv(s, v):
        p = jnp.exp(s - jnp.max(s, axis=-1, keepdims=True))
        acc = _dot(p.astype(BF16), v)
        return acc[:, :HEAD_DIM] / acc[:, HEAD_DIM:HEAD_DIM + 1]

    gates = jax.nn.sigmoid(gl_ref[0] + bg_ref[...])
    k0 = jnp.maximum(qi - wt, 0)
    wrows = pl.ds(pl.multiple_of(k0 * T, T), nwin * T)

    def attend(nkt):
        nkeys = nkt * T
        for g in range(G):
            gcols = slice(g * HEAD_DIM, (g + 1) * HEAD_DIM)
            vcols = slice(g * LANES, (g + 1) * LANES)
            q4 = jnp.concatenate(
                [qn_ref[0, :, (g * J + j) * HEAD_DIM:(g * J + j + 1) * HEAD_DIM] for j in range(J)], axis=0)
            selg = sel_ref[0, :, g * NS:(g + 1) * NS].astype(BF16)
            mask = (_dot(selg, e_ref[:, :nkeys]) - 1.0) * (-NEG_INF)
            bias = jnp.concatenate(
                [jnp.concatenate([bias_ref[g * J + j, sel_tile(qi - kt)] for kt in range(nkt)], axis=1) + mask
                 for j in range(J)], axis=0)
            o_s = softmax_pv(_dot_nt(q4, ks_ref[0, :nkeys, gcols]) + bias, vs_ref[0, :nkeys, vcols])
            wbias = jnp.concatenate(
                [jnp.concatenate([bias_ref[g * J + j, win_tile(qi - k0 - t)] for t in range(nwin)], axis=1)
                 for j in range(J)], axis=0)
            o_w = softmax_pv(_dot_nt(q4, kw_ref[0, wrows, gcols]) + wbias, vw_ref[0, wrows, vcols])
            for j in range(J):
                h = g * J + j
                cols = slice(h * HEAD_DIM, (h + 1) * HEAD_DIM)
                rows = slice(j * T, (j + 1) * T)
                o_ref[0, :, cols] = (gates[:, 3 * h:3 * h + 1] * oc_ref[0, :, cols]
                                     + gates[:, 3 * h + 1:3 * h + 2] * o_s[rows]
                                     + gates[:, 3 * h + 2:3 * h + 3] * o_w[rows])

    per = nk // SEL_KEY_PARTS
    for part in range(SEL_KEY_PARTS):
        pl.when((qi >= part * per) & (qi < (part + 1) * per))(functools.partial(attend, (part + 1) * per))


def _nsa_sel_win(qn, y, kv, sel, oc, bias_tiles, expand, b_gate):
    B, S, _ = y.shape
    T = ATT_TILE
    glblk = (NSA_WIDTH + 6 * NSA_KV) // LANES
    bg = jnp.zeros((1, LANES), F32).at[0, :3 * NSA_HEADS].set(b_gate)
    kspec = pl.BlockSpec((1, S, NSA_KV), lambda b, i: (b, 0, 0))
    vspec = pl.BlockSpec((1, S, NSA_GROUPS * LANES), lambda b, i: (b, 0, 0))
    return pl.pallas_call(
        _sel_win_kernel,
        out_shape=jax.ShapeDtypeStruct((B, S, NSA_WIDTH), F32),
        grid=(B, S // T),
        in_specs=[pl.BlockSpec((1, T, NSA_WIDTH), lambda b, i: (b, i, 0)),
                  kspec, vspec, kspec, vspec,
                  pl.BlockSpec((1, T, LANES), lambda b, i: (b, i, glblk)),
                  pl.BlockSpec((1, T, sel.shape[2]), lambda b, i: (b, i, 0)),
                  pl.BlockSpec((1, T, NSA_WIDTH), lambda b, i: (b, i, 0)),
                  pl.BlockSpec(bias_tiles.shape, lambda b, i: (0, 0, 0, 0)),
                  pl.BlockSpec(expand.shape, lambda b, i: (0, 0)),
                  pl.BlockSpec((1, LANES), lambda b, i: (0, 0))],
        out_specs=pl.BlockSpec((1, T, NSA_WIDTH), lambda b, i: (b, i, 0)),
        compiler_params=_cparams(("arbitrary", "arbitrary")),
        name="nsa_sel_win",
    )(qn, *kv, y, sel, oc, bias_tiles, expand, bg)


def _odd_mixer(x, sh, sc, gate, ng, w_in, b_gate, q_g, k_g, cmp_pos, cmp_w1, cmp_w2, bias_tiles, cmp_bias,
               w_out):
    B, S, _ = x.shape
    G = NSA_GROUPS
    w = jnp.zeros((D_MODEL, ODD_IN_PAD), BF16).at[:, :ODD_IN].set(w_in.astype(BF16))
    y = _norm_proj(x, sh, sc, ng, w)
    xr = y[:, :, NSA_WIDTH:NSA_WIDTH + 2 * NSA_KV].reshape(B, S, 2, G, HEAD_DIM)
    xr = xr.transpose(0, 2, 3, 1, 4).reshape(B, 2 * G, S // CMP_STRIDE, CMP_STRIDE * HEAD_DIM)
    kcvc = _nsa_compress(xr, cmp_pos.reshape(2, CMP_LEN * HEAD_DIM), cmp_w1, cmp_w2, k_g[0])
    n_slc = S // SLC_LEN
    starts = jnp.arange(S // CMP_STRIDE) * CMP_STRIDE
    bstart = jnp.arange(n_slc) * SLC_LEN
    real = (jnp.arange(S // CMP_STRIDE) < (S - CMP_LEN) // CMP_STRIDE + 1)[:, None]
    overlap = (real & (starts[:, None] < bstart[None, :] + SLC_LEN)
               & (starts[:, None] + CMP_LEN > bstart[None, :])).astype(F32)
    overlap = jnp.stack([jnp.pad(overlap, ((0, 0), (g * n_slc, (G - 1 - g) * n_slc))) for g in range(G)])
    expand = (jnp.arange(S)[None, :] // SLC_LEN == jnp.arange(n_slc)[:, None]).astype(BF16)
    oc, sel, qn = _nsa_cmp_attn(y, kcvc, overlap, cmp_bias, q_g)
    o = _nsa_sel_win(qn, y, _nsa_kv_prep(y, k_g), sel, oc, bias_tiles, expand, b_gate)
    return _out_proj(x, o, 0, o, 1, w_out.astype(BF16), gate)


def _even_mixer(x, sh, sc, gate, ng, w_in, b_f, conv_w, q_g, k_g, w_out):
    w = jnp.zeros((D_MODEL, EVEN_IN_PAD), BF16).at[:, :EVEN_IN].set(w_in.astype(BF16))
    y = _norm_proj(x, sh, sc, ng, w)
    conv = _short_conv(y, conv_w)
    attn = _fox_attention(y, b_f, q_g, k_g)
    return _out_proj(x, conv, 0, attn, 0, w_out.astype(BF16), gate)


def kernel(x, c, ada_w, ada_b, norm_g, even_w_in, even_b_f, even_conv_w, even_q_g, even_k_g, even_w_out, odd_w_in, odd_b_gate, odd_q_g, odd_k_g, odd_cmp_pos, odd_cmp_w1, odd_cmp_w2, odd_w_out, rel_table, peer_w_q, peer_keys, peer_u, peer_v):
    B = x.shape[0]
    mods = _ada(c, ada_w, ada_b)
    bias_tiles, cmp_bias = _relbias(rel_table)
    for layer in range(DEPTH):
        sh1, sc1, g1, sh2, sc2, g2 = [m.reshape(B, 1, D_MODEL) for m in jnp.split(mods[layer], 6, axis=-1)]
        i = layer // 2
        ng1 = norm_g[layer, 0].reshape(1, D_MODEL)
        ng2 = norm_g[layer, 1].reshape(1, D_MODEL)
        if layer % 2 == 0:
            x = _even_mixer(x, sh1, sc1, g1, ng1, even_w_in[i], even_b_f[i], even_conv_w[i], even_q_g[i],
                            even_k_g[i], even_w_out[i])
        else:
            x = _odd_mixer(x, sh1, sc1, g1, ng1, odd_w_in[i], odd_b_gate[i], odd_q_g[i], odd_k_g[i],
                           odd_cmp_pos[i], odd_cmp_w1[i], odd_cmp_w2[i], bias_tiles, cmp_bias, odd_w_out[i])
        x = _peer(x, sh2, sc2, g2, ng2, peer_w_q[layer], peer_keys[layer],
                  peer_u[layer].astype(BF16), peer_v[layer].T.astype(BF16))
    return x
```

```python
import functools
import math

import jax
import jax.numpy as jnp
from jax import lax
from jax.experimental import pallas as pl
from jax.experimental.pallas import tpu as pltpu

D_MODEL = 1024
DEPTH = 4
HEAD_DIM = 64
CONV_CH = 512
CONV_K = 3
FOX_HEADS = 8
FOX_WIDTH = FOX_HEADS * HEAD_DIM
EVEN_IN = 3 * CONV_CH + 3 * FOX_WIDTH + FOX_HEADS
NSA_GROUPS = 4
NSA_HPG = 4
NSA_HEADS = NSA_GROUPS * NSA_HPG
NSA_WIDTH = NSA_HEADS * HEAD_DIM
NSA_KV = NSA_GROUPS * HEAD_DIM
CMP_LEN = 32
CMP_STRIDE = 16
SLC_LEN = 64
SLC_TOPN = 16
WINDOW = 512
ODD_IN = NSA_WIDTH + 6 * NSA_KV + 3 * NSA_HEADS
REL_BUCKETS = 32
REL_MAX_DIST = 128
PEER_HEADS = 8
PEER_NKEYS = 128
PEER_EXPERTS = PEER_NKEYS * PEER_NKEYS
PEER_QDIM = 128
PEER_TOPK = 16
EPS = 1e-6
NEG_INF = -1e30
FORCE_SCORE = 1e6

LANES = 128
F32 = jnp.float32
BF16 = jnp.bfloat16
HIGHEST = lax.Precision.HIGHEST
VMEM_LIMIT = 56 * 1024 * 1024

EVEN_IN_PAD = -(-EVEN_IN // LANES) * LANES
ODD_IN_PAD = -(-ODD_IN // LANES) * LANES
ROW_TILE = 512
ATT_TILE = 128
FOX_TILE = 256
PEER_TN = 512
PEER_EC = 2048
SCALE = HEAD_DIM ** -0.5


def _cparams(sem):
    return pltpu.CompilerParams(dimension_semantics=sem, vmem_limit_bytes=VMEM_LIMIT)


def _dot(a, b, precision=None):
    return jnp.dot(a, b, precision=precision, preferred_element_type=F32)


def _dot_nt(a, b, precision=None):
    return lax.dot_general(a, b, (((1,), (1,)), ((), ())), precision=precision,
                           preferred_element_type=F32)


def _rms(x, g):
    return x * lax.rsqrt(jnp.mean(x * x, axis=-1, keepdims=True) + EPS) * g


def _ada_kernel(c_ref, w_ref, b_ref, o_ref):
    c = c_ref[...]
    sc = c * jax.nn.sigmoid(c)
    o_ref[0] = _dot(sc, w_ref[0], HIGHEST) + b_ref[0]


def _ada(c, ada_w, ada_b):
    B, D = c.shape
    E = ada_w.shape[2]
    tn = 2048
    return pl.pallas_call(
        _ada_kernel,
        out_shape=jax.ShapeDtypeStruct((DEPTH, B, E), F32),
        grid=(DEPTH, E // tn),
        in_specs=[pl.BlockSpec((B, D), lambda l, j: (0, 0)),
                  pl.BlockSpec((1, D, tn), lambda l, j: (l, 0, j)),
                  pl.BlockSpec((1, 1, tn), lambda l, j: (l, 0, j))],
        out_specs=pl.BlockSpec((1, B, tn), lambda l, j: (l, 0, j)),
        compiler_params=_cparams(("arbitrary", "arbitrary")),
        name="ada_mod",
    )(c, ada_w, ada_b.reshape(DEPTH, 1, E))


def _norm_proj_kernel(x_ref, sh_ref, sc_ref, ng_ref, w_ref, y_ref):
    h = _rms(x_ref[0], ng_ref[...]) * (1.0 + sc_ref[0]) + sh_ref[0]
    y_ref[0] = _dot(h.astype(BF16), w_ref[...])


def _norm_proj(x, sh, sc, ng, w):
    B, S, D = x.shape
    N = w.shape[1]
    return pl.pallas_call(
        _norm_proj_kernel,
        out_shape=jax.ShapeDtypeStruct((B, S, N), F32),
        grid=(B, S // ROW_TILE),
        in_specs=[pl.BlockSpec((1, ROW_TILE, D), lambda b, i: (b, i, 0)),
                  pl.BlockSpec((1, 1, D), lambda b, i: (b, 0, 0)),
                  pl.BlockSpec((1, 1, D), lambda b, i: (b, 0, 0)),
                  pl.BlockSpec((1, D), lambda b, i: (0, 0)),
                  pl.BlockSpec((D, N), lambda b, i: (0, 0))],
        out_specs=pl.BlockSpec((1, ROW_TILE, N), lambda b, i: (b, i, 0)),
        compiler_params=_cparams(("arbitrary", "arbitrary")),
        name="norm_proj",
    )(x, sh, sc, ng, w)


def _out_proj_kernel(x_ref, a_ref, b_ref, wa_ref, wb_ref, g_ref, o_ref):
    y = _dot(a_ref[0].astype(BF16), wa_ref[...]) + _dot(b_ref[0].astype(BF16), wb_ref[...])
    o_ref[0] = x_ref[0] + g_ref[0] * y


def _out_proj(x, a, a_blk, b, b_blk, w, gate):
    B, S, D = x.shape
    half = w.shape[0] // 2
    wa, wb = w[:half], w[half:]
    return pl.pallas_call(
        _out_proj_kernel,
        out_shape=jax.ShapeDtypeStruct((B, S, D), F32),
        grid=(B, S // ROW_TILE),
        in_specs=[pl.BlockSpec((1, ROW_TILE, D), lambda b, i: (b, i, 0)),
                  pl.BlockSpec((1, ROW_TILE, half), lambda b, i: (b, i, a_blk)),
                  pl.BlockSpec((1, ROW_TILE, half), lambda b, i: (b, i, b_blk)),
                  pl.BlockSpec((half, D), lambda b, i: (0, 0)),
                  pl.BlockSpec((half, D), lambda b, i: (0, 0)),
                  pl.BlockSpec((1, 1, D), lambda b, i: (b, 0, 0))],
        out_specs=pl.BlockSpec((1, ROW_TILE, D), lambda b, i: (b, i, 0)),
        compiler_params=_cparams(("arbitrary", "arbitrary")),
        name="out_proj",
    )(x, a, b, wa, wb, gate)


def _conv_kernel(cb_ref, cc_ref, ch_ref, w_ref, o_ref):
    u = cc_ref[0] * ch_ref[0]
    row = lax.broadcasted_iota(jnp.int32, u.shape, 0)
    u1 = jnp.where(row >= 1, pltpu.roll(u, 1, axis=0), 0.0)
    u2 = jnp.where(row >= 2, pltpu.roll(u, 2, axis=0), 0.0)
    w = w_ref[...]
    o_ref[0] = cb_ref[0] * (w[0:1] * u2 + w[1:2] * u1 + w[2:3] * u)


def _short_conv(y, conv_w):
    B, S, _ = y.shape
    nblk = CONV_CH // LANES
    w = jnp.zeros((8, CONV_CH), F32).at[:CONV_K].set(conv_w)
    return pl.pallas_call(
        _conv_kernel,
        out_shape=jax.ShapeDtypeStruct((B, S, CONV_CH), F32),
        grid=(B, nblk),
        in_specs=[pl.BlockSpec((1, S, LANES), lambda b, j: (b, 0, j)),
                  pl.BlockSpec((1, S, LANES), lambda b, j: (b, 0, nblk + j)),
                  pl.BlockSpec((1, S, LANES), lambda b, j: (b, 0, 2 * nblk + j)),
                  pl.BlockSpec((8, LANES), lambda b, j: (0, j))],
        out_specs=pl.BlockSpec((1, S, LANES), lambda b, j: (b, 0, j)),
        compiler_params=_cparams(("arbitrary", "arbitrary")),
        name="short_conv",
    )(y, y, y, w)


def _cumsum_rows(x):
    n = x.shape[0]
    row = lax.broadcasted_iota(jnp.int32, x.shape, 0)
    sh = 1
    while sh < n:
        x = x + jnp.where(row >= sh, pltpu.roll(x, sh, axis=0), 0.0)
        sh *= 2
    return x


FOX_KEY_PARTS = 4


def _fox_kernel(q_ref, k_ref, v_ref, f_ref, bf_ref, qg_ref, kg_ref, o_ref,
                kn_scr, vb_scr, cum_scr, cumrow_scr):
    qi = pl.program_id(1)
    S = k_ref.shape[1]
    T = q_ref.shape[1]
    nq = S // T

    @pl.when(qi == 0)
    def _prep():
        ones_col = (lax.broadcasted_iota(jnp.int32, (S, LANES - HEAD_DIM), 1) == 0).astype(BF16)
        for h in range(FOX_HEADS):
            cols = slice(h * HEAD_DIM, (h + 1) * HEAD_DIM)
            kn_scr[:, cols] = _rms(k_ref[0, :, cols], kg_ref[...]).astype(BF16)
            vb_scr[:, h * LANES:h * LANES + HEAD_DIM] = v_ref[0, :, cols].astype(BF16)
            vb_scr[:, h * LANES + HEAD_DIM:(h + 1) * LANES] = ones_col
        logf = jax.nn.log_sigmoid(f_ref[0] + bf_ref[...])
        cum = _cumsum_rows(logf)
        cum_scr[...] = cum
        for kt in range(S // T):
            cumrow_scr[:, kt * T:(kt + 1) * T] = cum[kt * T:(kt + 1) * T, :].T[0:8, :]

    cq = cum_scr[pl.ds(pl.multiple_of(qi * T, T), T), :]

    def attend(nkeys):
        row = qi * T + lax.broadcasted_iota(jnp.int32, (T, nkeys), 0)
        causal = jnp.where(lax.broadcasted_iota(jnp.int32, (T, nkeys), 1) <= row, 0.0, NEG_INF)
        for h in range(FOX_HEADS):
            cols = slice(h * HEAD_DIM, (h + 1) * HEAD_DIM)
            qh = (_rms(q_ref[0, :, cols], qg_ref[...]) * SCALE).astype(BF16)
            s = _dot_nt(qh, kn_scr[0:nkeys, cols]) + ((cq[:, h:h + 1] - cumrow_scr[h:h + 1, 0:nkeys]) + causal)
            p = jnp.exp(s - jnp.max(s, axis=-1, keepdims=True))
            acc = _dot(p.astype(BF16), vb_scr[0:nkeys, h * LANES:(h + 1) * LANES])
            o_ref[0, :, cols] = acc[:, :HEAD_DIM] / acc[:, HEAD_DIM:HEAD_DIM + 1]

    parts = min(FOX_KEY_PARTS, nq)
    per = nq // parts
    for part in range(parts):
        pl.when((qi >= part * per) & (qi < (part + 1) * per))(functools.partial(attend, (part + 1) * (S // parts)))


def _fox_attention(y, b_f, q_g, k_g):
    B, S, _ = y.shape
    T = FOX_TILE
    qblk = 3 * CONV_CH // FOX_WIDTH
    fblk = (3 * CONV_CH + 3 * FOX_WIDTH) // LANES
    bf = jnp.zeros((1, LANES), F32).at[0, :FOX_HEADS].set(b_f)
    return pl.pallas_call(
        _fox_kernel,
        out_shape=jax.ShapeDtypeStruct((B, S, FOX_WIDTH), F32),
        grid=(B, S // T),
        in_specs=[pl.BlockSpec((1, T, FOX_WIDTH), lambda b, i: (b, i, qblk)),
                  pl.BlockSpec((1, S, FOX_WIDTH), lambda b, i: (b, 0, qblk + 1)),
                  pl.BlockSpec((1, S, FOX_WIDTH), lambda b, i: (b, 0, qblk + 2)),
                  pl.BlockSpec((1, S, LANES), lambda b, i: (b, 0, fblk)),
                  pl.BlockSpec((1, LANES), lambda b, i: (0, 0)),
                  pl.BlockSpec((1, HEAD_DIM), lambda b, i: (0, 0)),
                  pl.BlockSpec((1, HEAD_DIM), lambda b, i: (0, 0))],
        out_specs=pl.BlockSpec((1, T, FOX_WIDTH), lambda b, i: (b, i, 0)),
        scratch_shapes=[pltpu.VMEM((S, FOX_WIDTH), BF16),
                        pltpu.VMEM((S, FOX_HEADS * LANES), BF16),
                        pltpu.VMEM((S, LANES), F32),
                        pltpu.VMEM((8, S), F32)],
        compiler_params=_cparams(("arbitrary", "arbitrary")),
        name="fox_attention",
    )(y, y, y, y, bf, q_g.reshape(1, HEAD_DIM), k_g.reshape(1, HEAD_DIM))


def _gelu_tanh(x):
    c0 = math.sqrt(2.0 / math.pi)
    return (0.5 * x) * (1.0 + jnp.tanh(x * (c0 + (c0 * 0.044715) * (x * x))))


def _extract_top(work, iota, k, exact):
    nrows = work.shape[0]
    vals = []
    for _ in range(k):
        m = jnp.max(work, axis=0, keepdims=True)
        if exact:
            idx = jnp.min(jnp.where(work == m, iota, nrows), axis=0, keepdims=True)
            work = jnp.where(iota == idx, -jnp.inf, work)
        else:
            work = jnp.where(work == m, -jnp.inf, work)
        vals.append(m)
    return vals, work


def _removed(work):
    return jnp.sum(jnp.where(work == -jnp.inf, 1.0, 0.0), axis=0, keepdims=True)


def _peer_kernel(x_ref, sh_ref, sc_ref, gt_ref, ng_ref, wqh_ref, wql_ref, keys_ref, u_ref, vt_ref, o_ref,
                 ht_scr, qt_scr, sp_scr, t0_scr, t1_scr, tau_scr, a_scr, p_scr, acc_scr):
    c = pl.program_id(2)
    nc = pl.num_programs(2)
    tn = x_ref.shape[1]
    nlt = tn // LANES
    half = PEER_QDIM // 2

    @pl.when(c == 0)
    def _prep():
        h = _rms(x_ref[0], ng_ref[...]) * (1.0 + sc_ref[0]) + sh_ref[0]
        hh = h.astype(BF16)
        hl = (h - hh.astype(F32)).astype(BF16)
        cw = 2 * LANES
        for k in range(h.shape[1] // cw):
            cs = slice(k * cw, (k + 1) * cw)
            q = _dot(hh, wqh_ref[:, cs]) + (_dot(hh, wql_ref[:, cs]) + _dot(hl, wqh_ref[:, cs]))
            qt_scr[cs, :] = q.T
            ht_scr[cs, :] = h[:, cs].T.astype(BF16)
        acc_scr[...] = jnp.zeros_like(acc_scr)
        iota_k = lax.broadcasted_iota(jnp.int32, (PEER_NKEYS, LANES), 0)
        iota_c = lax.broadcasted_iota(jnp.int32, (PEER_TOPK + 8 * 8, LANES), 0)

        n_pad = sum(8 - PEER_TOPK // (r0 + 1) for r0 in range(2, 8))

        def select(hd, lt, exact):
            lanes = slice(lt * LANES, (lt + 1) * LANES)
            s0 = sp_scr[0, :, lanes]
            s1 = sp_scr[1, :, lanes]
            sv0, w0 = _extract_top(s0, iota_k, PEER_TOPK, exact)
            sv1, w1 = _extract_top(s1, iota_k, PEER_TOPK, exact)
            s0m = jnp.where(w0 == -jnp.inf, s0, -jnp.inf)
            s1m = jnp.where(w1 == -jnp.inf, s1, -jnp.inf)
            sv0_all = jnp.concatenate(sv0, axis=0)
            sv1_all = jnp.concatenate(sv1, axis=0)
            parts = [sv0[0] + sv1_all, sv0[1] + sv1_all[:8]]
            for r0 in range(2, 8):
                parts.append(jnp.where(iota_c[:8] < PEER_TOPK // (r0 + 1), sv0[r0] + sv1_all[:8], -jnp.inf))
            parts.append(sv0_all[8:] + sv1[0])
            cv, wc = _extract_top(jnp.concatenate(parts, axis=0), iota_c, PEER_TOPK, exact)
            z = cv[0] - cv[0]
            for r in range(PEER_TOPK):
                z = z + jnp.exp(cv[r] - cv[0])
            nrt = PEER_NKEYS // 8
            t0_scr[hd, lt, :, 0] = s0m.reshape(nrt, 8, LANES)
            t0_scr[hd, lt, :, 1] = jnp.exp(s0m - sv0[0]).reshape(nrt, 8, LANES)
            t1_scr[hd, lt, :, 0] = s1m.reshape(nrt, 8, LANES)
            t1_scr[hd, lt, :, 1] = (jnp.exp(s1m - sv1[0]) / z).reshape(nrt, 8, LANES)
            tau_scr[hd, lt] = jnp.broadcast_to(cv[PEER_TOPK - 1], (8, LANES))
            return ((_removed(w0) != PEER_TOPK) | (_removed(w1) != PEER_TOPK)
                    | (_removed(wc) != PEER_TOPK + n_pad))

        def head_body(hd, carry):
            for p in range(2):
                qs = qt_scr[pl.ds(pl.multiple_of((hd * 2 + p) * half, half), half), :]
                sp_scr[p] = _dot(keys_ref[p], qs, HIGHEST)
            tied = jnp.zeros((1, LANES), F32)
            for lt in range(nlt):
                tied = jnp.maximum(tied, jnp.where(select(hd, lt, False), 1.0, 0.0))

            @pl.when(jnp.max(tied) > 0.0)
            def _redo():
                for lt in range(nlt):
                    select(hd, lt, True)
            return carry

        lax.fori_loop(0, PEER_HEADS, head_body, 0)

    nsub = 8
    nrq = 32
    nparts = PEER_EC // PEER_NKEYS // nsub
    for part in range(nparts):
        er = slice(part * nsub * PEER_NKEYS, (part + 1) * nsub * PEER_NKEYS)
        for lh in range(nlt // 2):
            lanes2 = slice(lh * 2 * LANES, (lh + 1) * 2 * LANES)
            a_scr[er, lanes2] = _gelu_tanh(_dot(u_ref[er, :], ht_scr[:, lanes2]))

    def weights_tile(part, lt, rq, prev):
        lanes = slice(lt * LANES, (lt + 1) * LANES)
        jt = slice(rq * nrq // 8, (rq + 1) * nrq // 8)
        blk = c * (PEER_EC // PEER_NKEYS // 8) + part * nsub // 8
        zero = 0.0 if prev is None else jnp.where(prev != prev, prev, 0.0)
        w = [jnp.zeros((nrq, LANES), F32) for _ in range(nsub)]
        for hd in range(PEER_HEADS):
            s1 = t1_scr[hd, lt, jt, 0].reshape(nrq, LANES)
            e1 = t1_scr[hd, lt, jt, 1].reshape(nrq, LANES)
            s0 = t0_scr[hd, lt, blk, 0] + zero
            e0 = t0_scr[hd, lt, blk, 1] + zero
            tau = tau_scr[hd, lt, 0:1, :]
            for ii in range(nsub):
                keep = s1 + s0[ii:ii + 1] >= tau
                w[ii] = w[ii] + jnp.where(keep, e1 * e0[ii:ii + 1], 0.0)
        for ii in range(nsub):
            rows = slice((part * nsub + ii) * PEER_NKEYS + rq * nrq, (part * nsub + ii) * PEER_NKEYS + (rq + 1) * nrq)
            w[ii] = w[ii] * a_scr[rows, lanes]
            p_scr[rows, lanes] = w[ii].astype(BF16)
        return w[nsub - 1][0:1, :]

    in_flight = 8
    done = [None] * in_flight
    for part in range(nparts):
        for lt in range(nlt):
            for rq in range(PEER_NKEYS // nrq):
                done.append(weights_tile(part, lt, rq, done[-in_flight]))
    acc_scr[...] += _dot(vt_ref[...], p_scr[...])

    @pl.when(c == nc - 1)
    def _fin():
        o_ref[0] = x_ref[0] + gt_ref[0] * acc_scr[...].T


def _peer(x, sh, sc, gate, ng, w_q, keys, u_b, vt_b):
    B, S, D = x.shape
    tn = PEER_TN
    nc = PEER_EXPERTS // PEER_EC
    tab = pltpu.VMEM((PEER_HEADS, tn // LANES, PEER_NKEYS // 8, 2, 8, LANES), F32)
    wq_hi = w_q.astype(BF16)
    wq_lo = (w_q - wq_hi.astype(F32)).astype(BF16)
    wspec = pl.BlockSpec((D, PEER_HEADS * PEER_QDIM), lambda b, i, c: (0, 0), pipeline_mode=pl.Buffered(1))
    return pl.pallas_call(
        _peer_kernel,
        out_shape=jax.ShapeDtypeStruct((B, S, D), F32),
        grid=(B, S // tn, nc),
        in_specs=[pl.BlockSpec((1, tn, D), lambda b, i, c: (b, i, 0)),
                  pl.BlockSpec((1, 1, D), lambda b, i, c: (b, 0, 0)),
                  pl.BlockSpec((1, 1, D), lambda b, i, c: (b, 0, 0)),
                  pl.BlockSpec((1, 1, D), lambda b, i, c: (b, 0, 0)),
                  pl.BlockSpec((1, D), lambda b, i, c: (0, 0)),
                  wspec, wspec,
                  pl.BlockSpec((2, PEER_NKEYS, PEER_QDIM // 2), lambda b, i, c: (0, 0, 0)),
                  pl.BlockSpec((PEER_EC, D), lambda b, i, c: (c, 0)),
                  pl.BlockSpec((D, PEER_EC), lambda b, i, c: (0, c))],
        out_specs=pl.BlockSpec((1, tn, D), lambda b, i, c: (b, i, 0)),
        scratch_shapes=[pltpu.VMEM((D, tn), BF16),
                        pltpu.VMEM((PEER_HEADS * PEER_QDIM, tn), F32),
                        pltpu.VMEM((2, PEER_NKEYS, tn), F32),
                        tab, tab,
                        pltpu.VMEM((PEER_HEADS, tn // LANES, 8, LANES), F32),
                        pltpu.VMEM((PEER_EC, tn), F32),
                        pltpu.VMEM((PEER_EC, tn), BF16),
                        pltpu.VMEM((D, tn), F32)],
        compiler_params=_cparams(("arbitrary", "arbitrary", "arbitrary")),
        name="peer",
    )(x, sh, sc, gate, ng, wq_hi, wq_lo, keys, u_b, vt_b)


def _rel_bucket(dist):
    n = jnp.maximum(dist, 0)
    max_exact = REL_BUCKETS // 2
    nf = jnp.maximum(n, 1).astype(F32)
    large = max_exact + (jnp.log(nf / max_exact) / math.log(REL_MAX_DIST / max_exact)
                         * (REL_BUCKETS - max_exact)).astype(jnp.int32)
    large = jnp.minimum(large, REL_BUCKETS - 1)
    return jnp.where(n < max_exact, n, large)


def _bias_lookup(bucket, tab_ref, h):
    bias = jnp.zeros(bucket.shape, F32)
    for b in range(REL_BUCKETS):
        bias = jnp.where(bucket == b, tab_ref[b, h], bias)
    return bias


N_BIAS_TILES = 5


def _relbias_kernel(tab_ref, o_ref, c_ref):
    h = pl.program_id(0)
    T = o_ref.shape[2]
    a = lax.broadcasted_iota(jnp.int32, (T, T), 0)
    b = lax.broadcasted_iota(jnp.int32, (T, T), 1)
    last = tab_ref[REL_BUCKETS - 1, h]
    o_ref[0, 0] = jnp.where(a >= b, _bias_lookup(_rel_bucket(a - b), tab_ref, h), NEG_INF)
    o_ref[0, 1] = _bias_lookup(_rel_bucket(a - b + T), tab_ref, h)
    o_ref[0, 2] = jnp.full((T, T), last, F32)
    o_ref[0, 3] = jnp.where(a < b, last, NEG_INF)
    o_ref[0, 4] = jnp.full((T, T), NEG_INF, F32)
    dist = a - ((b - T // 2) * CMP_STRIDE + CMP_LEN - 1)
    c_ref[0] = jnp.where(dist >= 0, _bias_lookup(_rel_bucket(dist), tab_ref, h), last)


def _relbias(rel_table):
    T = ATT_TILE
    assert 2 * T - (T - 1) >= REL_MAX_DIST
    assert WINDOW % T == 0
    return pl.pallas_call(
        _relbias_kernel,
        out_shape=(jax.ShapeDtypeStruct((NSA_HEADS, N_BIAS_TILES, T, T), F32),
                   jax.ShapeDtypeStruct((NSA_HEADS, T, T), F32)),
        grid=(NSA_HEADS,),
        in_specs=[pl.BlockSpec(memory_space=pltpu.SMEM)],
        out_specs=(pl.BlockSpec((1, N_BIAS_TILES, T, T), lambda h: (h, 0, 0, 0)),
                   pl.BlockSpec((1, T, T), lambda h: (h, 0, 0))),
        compiler_params=_cparams(("arbitrary",)),
        name="rel_bias_tiles",
    )(rel_table)


def _cmp_kernel(x_ref, pos_ref, w1_ref, w2_ref, kg_ref, o_ref):
    j = pl.program_id(1)
    half = CMP_STRIDE * HEAD_DIM
    x = x_ref[0, 0]
    pos = pos_ref[0]
    pa = _dot((x + pos[:, :half]).astype(BF16), w1_ref[0, :half, :])
    pb = _dot((x + pos[:, half:]).astype(BF16), w1_ref[0, half:, :])
    pre = pa + pltpu.roll(pb, pb.shape[0] - 1, axis=0)
    out = _dot(jax.nn.gelu(pre, approximate=True).astype(BF16), w2_ref[0])
    o_ref[0, 0] = jnp.where(j < NSA_GROUPS, _rms(out, kg_ref[...]), out)


def _nsa_compress(xr, pos, w1, w2, kg0):
    B, _, M, W = xr.shape
    G = NSA_GROUPS
    return pl.pallas_call(
        _cmp_kernel,
        out_shape=jax.ShapeDtypeStruct((B, 2 * G, M, HEAD_DIM), F32),
        grid=(B, 2 * G),
        in_specs=[pl.BlockSpec((1, 1, M, W), lambda b, j: (b, j, 0, 0)),
                  pl.BlockSpec((1, 1, 2 * W), lambda b, j: (j // G, 0, 0)),
                  pl.BlockSpec((1, 2 * W, HEAD_DIM), lambda b, j: (j // G, 0, 0)),
                  pl.BlockSpec((1, HEAD_DIM, HEAD_DIM), lambda b, j: (j // G, 0, 0)),
                  pl.BlockSpec((1, HEAD_DIM), lambda b, j: (0, 0))],
        out_specs=pl.BlockSpec((1, 1, M, HEAD_DIM), lambda b, j: (b, j, 0, 0)),
        compiler_params=_cparams(("arbitrary", "arbitrary")),
        name="nsa_compress",
    )(xr, pos.reshape(2, 1, 2 * W), w1.astype(BF16), w2.astype(BF16), kg0.reshape(1, HEAD_DIM))


def _cmp_attn_kernel(q_ref, kv_ref, ov_ref, cb_ref, qg_ref, oc_ref, sel_ref, qn_ref):
    qi = pl.program_id(1)
    T = q_ref.shape[1]
    NC = kv_ref.shape[2]
    G, J = NSA_GROUPS, NSA_HPG
    NS = sel_ref.shape[2] // G
    assert T == NC == LANES and T % CMP_STRIDE == 0 and G * NS == LANES
    t = qi * T + lax.broadcasted_iota(jnp.int32, (T, NC), 0)
    lane = lax.broadcasted_iota(jnp.int32, (T, NC), 1)
    vis = t - (lane * CMP_STRIDE + CMP_LEN - 1) >= 0
    visf = vis.astype(F32)
    shift = (qi * (T // CMP_STRIDE) + T // 2) % LANES
    imp = jnp.zeros((T, LANES), F32)
    vis4 = jnp.concatenate([vis] * J, axis=0)
    visf4 = jnp.concatenate([visf] * J, axis=0)
    for g in range(G):
        kc = kv_ref[0, g].astype(BF16)
        vc = kv_ref[0, G + g].astype(BF16)
        qhs = []
        for j in range(J):
            cols = slice((g * J + j) * HEAD_DIM, (g * J + j + 1) * HEAD_DIM)
            qhs.append((_rms(q_ref[0, :, cols], qg_ref[...]) * SCALE).astype(BF16))
            qn_ref[0, :, cols] = qhs[j]
        bias = jnp.concatenate([pltpu.roll(cb_ref[g * J + j], shift, axis=1) for j in range(J)], axis=0)
        s = jnp.where(vis4, _dot_nt(jnp.concatenate(qhs, axis=0), kc) + bias, NEG_INF)
        e = jnp.exp(s - jnp.max(s, axis=-1, keepdims=True))
        p = e / jnp.sum(e, axis=-1, keepdims=True) * visf4
        o = _dot(p.astype(BF16), vc)
        psum = p[0:T]
        for j in range(J):
            oc_ref[0, :, (g * J + j) * HEAD_DIM:(g * J + j + 1) * HEAD_DIM] = o[j * T:(j + 1) * T]
            if j:
                psum = psum + p[j * T:(j + 1) * T]
        imp = imp + _dot(psum, ov_ref[g], HIGHEST)
    mm = lane % NS
    cur = t // SLC_LEN
    forced = (mm == 0) | (mm == cur) | (mm == cur - 1)
    imp = jnp.where(forced, FORCE_SCORE, jnp.where(mm * SLC_LEN <= t, imp, -1.0))
    rank = jnp.zeros((T, LANES), F32)
    for r in range(1, NS):
        wrapped = mm >= NS - r
        other = jnp.where(wrapped, pltpu.roll(imp, NS - r, axis=1), pltpu.roll(imp, LANES - r, axis=1))
        rank = rank + jnp.where(other > imp, 1.0, jnp.where(other == imp, wrapped.astype(F32), 0.0))
    sel_ref[0] = jnp.where(rank < SLC_TOPN, jnp.where(imp >= 0.0, 1.0, 0.0), 0.0)


def _nsa_cmp_attn(y, kcvc, overlap, cmp_bias, q_g):
    B, S, _ = y.shape
    T = ATT_TILE
    NC = kcvc.shape[2]
    return pl.pallas_call(
        _cmp_attn_kernel,
        out_shape=(jax.ShapeDtypeStruct((B, S, NSA_WIDTH), F32),
                   jax.ShapeDtypeStruct((B, S, LANES), F32),
                   jax.ShapeDtypeStruct((B, S, NSA_WIDTH), BF16)),
        grid=(B, S // T),
        in_specs=[pl.BlockSpec((1, T, NSA_WIDTH), lambda b, i: (b, i, 0)),
                  pl.BlockSpec((1, 2 * NSA_GROUPS, NC, HEAD_DIM), lambda b, i: (b, 0, 0, 0)),
                  pl.BlockSpec(overlap.shape, lambda b, i: (0, 0, 0)),
                  pl.BlockSpec(cmp_bias.shape, lambda b, i: (0, 0, 0)),
                  pl.BlockSpec((1, HEAD_DIM), lambda b, i: (0, 0))],
        out_specs=(pl.BlockSpec((1, T, NSA_WIDTH), lambda b, i: (b, i, 0)),
                   pl.BlockSpec((1, T, LANES), lambda b, i: (b, i, 0)),
                   pl.BlockSpec((1, T, NSA_WIDTH), lambda b, i: (b, i, 0))),
        compiler_params=_cparams(("arbitrary", "arbitrary")),
        name="nsa_cmp_attn",
    )(y, kcvc, overlap, cmp_bias, q_g.reshape(1, HEAD_DIM))


def _kv_prep_kernel(ks_ref, kw_ref, kg1_ref, kg2_ref, ksn_ref, vs_ref, kwn_ref, vw_ref):
    KV = NSA_KV
    rows = ks_ref.shape[1]
    ones_col = (lax.broadcasted_iota(jnp.int32, (rows, LANES - HEAD_DIM), 1) == 0).astype(BF16)
    for g in range(NSA_GROUPS):
        cols = slice(g * HEAD_DIM, (g + 1) * HEAD_DIM)
        vcols = slice(KV + g * HEAD_DIM, KV + (g + 1) * HEAD_DIM)
        ksn_ref[0, :, cols] = _rms(ks_ref[0, :, cols], kg1_ref[...]).astype(BF16)
        kwn_ref[0, :, cols] = _rms(kw_ref[0, :, cols], kg2_ref[...]).astype(BF16)
        vs_ref[0, :, g * LANES:g * LANES + HEAD_DIM] = ks_ref[0, :, vcols].astype(BF16)
        vw_ref[0, :, g * LANES:g * LANES + HEAD_DIM] = kw_ref[0, :, vcols].astype(BF16)
        vs_ref[0, :, g * LANES + HEAD_DIM:(g + 1) * LANES] = ones_col
        vw_ref[0, :, g * LANES + HEAD_DIM:(g + 1) * LANES] = ones_col


def _nsa_kv_prep(y, k_g):
    B, S, _ = y.shape
    ksblk = (NSA_WIDTH + 2 * NSA_KV) // (2 * NSA_KV)
    kout = jax.ShapeDtypeStruct((B, S, NSA_KV), BF16)
    vout = jax.ShapeDtypeStruct((B, S, NSA_GROUPS * LANES), BF16)
    kspec = pl.BlockSpec((1, ROW_TILE, NSA_KV), lambda b, i: (b, i, 0))
    vspec = pl.BlockSpec((1, ROW_TILE, NSA_GROUPS * LANES), lambda b, i: (b, i, 0))
    return pl.pallas_call(
        _kv_prep_kernel,
        out_shape=(kout, vout, kout, vout),
        grid=(B, S // ROW_TILE),
        in_specs=[pl.BlockSpec((1, ROW_TILE, 2 * NSA_KV), lambda b, i: (b, i, ksblk)),
                  pl.BlockSpec((1, ROW_TILE, 2 * NSA_KV), lambda b, i: (b, i, ksblk + 1)),
                  pl.BlockSpec((1, HEAD_DIM), lambda b, i: (0, 0)),
                  pl.BlockSpec((1, HEAD_DIM), lambda b, i: (0, 0))],
        out_specs=(kspec, vspec, kspec, vspec),
        compiler_params=_cparams(("arbitrary", "arbitrary")),
        name="nsa_kv_prep",
    )(y, y, k_g[1].reshape(1, HEAD_DIM), k_g[2].reshape(1, HEAD_DIM))


SEL_KEY_PARTS = 4


def _sel_win_kernel(qn_ref, ks_ref, vs_ref, kw_ref, vw_ref, gl_ref, sel_ref, oc_ref, bias_ref, e_ref, bg_ref,
                    o_ref):
    qi = pl.program_id(1)
    T = qn_ref.shape[1]
    S = ks_ref.shape[1]
    nk = S // T
    G, J = NSA_GROUPS, NSA_HPG
    NS = e_ref.shape[0]
    wt = WINDOW // T
    nwin = wt + 1
    assert nk % SEL_KEY_PARTS == 0 and nk >= nwin

    def sel_tile(r):
        return jnp.where(r < 0, 4, jnp.minimum(r, 2))

    def win_tile(r):
        return jnp.where(r < 0, 4, jnp.where(r <= 1, r, jnp.where(r < wt, 2, jnp.where(r == wt, 3, 4))))

    def softmax_pv(s, v):
        p = jnp.exp(s - jnp.max(s, axis=-1, keepdims=True))
        acc = _dot(p.astype(BF16), v)
        return acc[:, :HEAD_DIM] / acc[:, HEAD_DIM:HEAD_DIM + 1]

    gates = jax.nn.sigmoid(gl_ref[0] + bg_ref[...])
    k0 = jnp.maximum(qi - wt, 0)
    wrows = pl.ds(pl.multiple_of(k0 * T, T), nwin * T)

    def attend(nkt):
        nkeys = nkt * T
        for g in range(G):
            gcols = slice(g * HEAD_DIM, (g + 1) * HEAD_DIM)
            vcols = slice(g * LANES, (g + 1) * LANES)
            q4 = jnp.concatenate(
                [qn_ref[0, :, (g * J + j) * HEAD_DIM:(g * J + j + 1) * HEAD_DIM] for j in range(J)], axis=0)
            selg = sel_ref[0, :, g * NS:(g + 1) * NS].astype(BF16)
            mask = (_dot(selg, e_ref[:, :nkeys]) - 1.0) * (-NEG_INF)
            bias = jnp.concatenate(
                [jnp.concatenate([bias_ref[g * J + j, sel_tile(qi - kt)] for kt in range(nkt)], axis=1) + mask
                 for j in range(J)], axis=0)
            o_s = softmax_pv(_dot_nt(q4, ks_ref[0, :nkeys, gcols]) + bias, vs_ref[0, :nkeys, vcols])
            wbias = jnp.concatenate(
                [jnp.concatenate([bias_ref[g * J + j, win_tile(qi - k0 - t)] for t in range(nwin)], axis=1)
                 for j in range(J)], axis=0)
            o_w = softmax_pv(_dot_nt(q4, kw_ref[0, wrows, gcols]) + wbias, vw_ref[0, wrows, vcols])
            for j in range(J):
                h = g * J + j
                cols = slice(h * HEAD_DIM, (h + 1) * HEAD_DIM)
                rows = slice(j * T, (j + 1) * T)
                o_ref[0, :, cols] = (gates[:, 3 * h:3 * h + 1] * oc_ref[0, :, cols]
                                     + gates[:, 3 * h + 1:3 * h + 2] * o_s[rows]
                                     + gates[:, 3 * h + 2:3 * h + 3] * o_w[rows])

    per = nk // SEL_KEY_PARTS
    for part in range(SEL_KEY_PARTS):
        pl.when((qi >= part * per) & (qi < (part + 1) * per))(functools.partial(attend, (part + 1) * per))


def _nsa_sel_win(qn, y, kv, sel, oc, bias_tiles, expand, b_gate):
    B, S, _ = y.shape
    T = ATT_TILE
    glblk = (NSA_WIDTH + 6 * NSA_KV) // LANES
    bg = jnp.zeros((1, LANES), F32).at[0, :3 * NSA_HEADS].set(b_gate)
    kspec = pl.BlockSpec((1, S, NSA_KV), lambda b, i: (b, 0, 0))
    vspec = pl.BlockSpec((1, S, NSA_GROUPS * LANES), lambda b, i: (b, 0, 0))
    return pl.pallas_call(
        _sel_win_kernel,
        out_shape=jax.ShapeDtypeStruct((B, S, NSA_WIDTH), F32),
        grid=(B, S // T),
        in_specs=[pl.BlockSpec((1, T, NSA_WIDTH), lambda b, i: (b, i, 0)),
                  kspec, vspec, kspec, vspec,
                  pl.BlockSpec((1, T, LANES), lambda b, i: (b, i, glblk)),
                  pl.BlockSpec((1, T, sel.shape[2]), lambda b, i: (b, i, 0)),
                  pl.BlockSpec((1, T, NSA_WIDTH), lambda b, i: (b, i, 0)),
                  pl.BlockSpec(bias_tiles.shape, lambda b, i: (0, 0, 0, 0)),
                  pl.BlockSpec(expand.shape, lambda b, i: (0, 0)),
                  pl.BlockSpec((1, LANES), lambda b, i: (0, 0))],
        out_specs=pl.BlockSpec((1, T, NSA_WIDTH), lambda b, i: (b, i, 0)),
        compiler_params=_cparams(("arbitrary", "arbitrary")),
        name="nsa_sel_win",
    )(qn, *kv, y, sel, oc, bias_tiles, expand, bg)


def _odd_mixer(x, sh, sc, gate, ng, w_in, b_gate, q_g, k_g, cmp_pos, cmp_w1, cmp_w2, bias_tiles, cmp_bias,
               w_out):
    B, S, _ = x.shape
    G = NSA_GROUPS
    w = jnp.zeros((D_MODEL, ODD_IN_PAD), BF16).at[:, :ODD_IN].set(w_in.astype(BF16))
    y = _norm_proj(x, sh, sc, ng, w)
    xr = y[:, :, NSA_WIDTH:NSA_WIDTH + 2 * NSA_KV].reshape(B, S, 2, G, HEAD_DIM)
    xr = xr.transpose(0, 2, 3, 1, 4).reshape(B, 2 * G, S // CMP_STRIDE, CMP_STRIDE * HEAD_DIM)
    kcvc = _nsa_compress(xr, cmp_pos.reshape(2, CMP_LEN * HEAD_DIM), cmp_w1, cmp_w2, k_g[0])
    n_slc = S // SLC_LEN
    starts = jnp.arange(S // CMP_STRIDE) * CMP_STRIDE
    bstart = jnp.arange(n_slc) * SLC_LEN
    real = (jnp.arange(S // CMP_STRIDE) < (S - CMP_LEN) // CMP_STRIDE + 1)[:, None]
    overlap = (real & (starts[:, None] < bstart[None, :] + SLC_LEN)
               & (starts[:, None] + CMP_LEN > bstart[None, :])).astype(F32)
    overlap = jnp.stack([jnp.pad(overlap, ((0, 0), (g * n_slc, (G - 1 - g) * n_slc))) for g in range(G)])
    expand = (jnp.arange(S)[None, :] // SLC_LEN == jnp.arange(n_slc)[:, None]).astype(BF16)
    oc, sel, qn = _nsa_cmp_attn(y, kcvc, overlap, cmp_bias, q_g)
    o = _nsa_sel_win(qn, y, _nsa_kv_prep(y, k_g), sel, oc, bias_tiles, expand, b_gate)
    return _out_proj(x, o, 0, o, 1, w_out.astype(BF16), gate)


def _even_mixer(x, sh, sc, gate, ng, w_in, b_f, conv_w, q_g, k_g, w_out):
    w = jnp.zeros((D_MODEL, EVEN_IN_PAD), BF16).at[:, :EVEN_IN].set(w_in.astype(BF16))
    y = _norm_proj(x, sh, sc, ng, w)
    conv = _short_conv(y, conv_w)
    attn = _fox_attention(y, b_f, q_g, k_g)
    return _out_proj(x, conv, 0, attn, 0, w_out.astype(BF16), gate)


def kernel(x, c, ada_w, ada_b, norm_g, even_w_in, even_b_f, even_conv_w, even_q_g, even_k_g, even_w_out, odd_w_in, odd_b_gate, odd_q_g, odd_k_g, odd_cmp_pos, odd_cmp_w1, odd_cmp_w2, odd_w_out, rel_table, peer_w_q, peer_keys, peer_u, peer_v):
    B = x.shape[0]
    mods = _ada(c, ada_w, ada_b)
    bias_tiles, cmp_bias = _relbias(rel_table)
    for layer in range(DEPTH):
        sh1, sc1, g1, sh2, sc2, g2 = [m.reshape(B, 1, D_MODEL) for m in jnp.split(mods[layer], 6, axis=-1)]
        i = layer // 2
        ng1 = norm_g[layer, 0].reshape(1, D_MODEL)
        ng2 = norm_g[layer, 1].reshape(1, D_MODEL)
        if layer % 2 == 0:
            x = _even_mixer(x, sh1, sc1, g1, ng1, even_w_in[i], even_b_f[i], even_conv_w[i], even_q_g[i],
                            even_k_g[i], even_w_out[i])
        else:
            x = _odd_mixer(x, sh1, sc1, g1, ng1, odd_w_in[i], odd_b_gate[i], odd_q_g[i], odd_k_g[i],
                           odd_cmp_pos[i], odd_cmp_w1[i], odd_cmp_w2[i], bias_tiles, cmp_bias, odd_w_out[i])
        x = _peer(x, sh2, sc2, g2, ng2, peer_w_q[layer], peer_keys[layer],
                  peer_u[layer].astype(BF16), peer_v[layer].T.astype(BF16))
    return x
```

```python
import functools
import math

import jax
import jax.numpy as jnp
from jax import lax
from jax.experimental import pallas as pl
from jax.experimental.pallas import tpu as pltpu

D_MODEL = 1024
DEPTH = 4
HEAD_DIM = 64
CONV_CH = 512
CONV_K = 3
FOX_HEADS = 8
FOX_WIDTH = FOX_HEADS * HEAD_DIM
EVEN_IN = 3 * CONV_CH + 3 * FOX_WIDTH + FOX_HEADS
NSA_GROUPS = 4
NSA_HPG = 4
NSA_HEADS = NSA_GROUPS * NSA_HPG
NSA_WIDTH = NSA_HEADS * HEAD_DIM
NSA_KV = NSA_GROUPS * HEAD_DIM
CMP_LEN = 32
CMP_STRIDE = 16
SLC_LEN = 64
SLC_TOPN = 16
WINDOW = 512
ODD_IN = NSA_WIDTH + 6 * NSA_KV + 3 * NSA_HEADS
REL_BUCKETS = 32
REL_MAX_DIST = 128
PEER_HEADS = 8
PEER_NKEYS = 128
PEER_EXPERTS = PEER_NKEYS * PEER_NKEYS
PEER_QDIM = 128
PEER_TOPK = 16
EPS = 1e-6
NEG_INF = -1e30
FORCE_SCORE = 1e6

LANES = 128
F32 = jnp.float32
BF16 = jnp.bfloat16
HIGHEST = lax.Precision.HIGHEST
VMEM_LIMIT = 56 * 1024 * 1024

EVEN_IN_PAD = -(-EVEN_IN // LANES) * LANES
ODD_IN_PAD = -(-ODD_IN // LANES) * LANES
ROW_TILE = 512
ATT_TILE = 128
FOX_TILE = 256
PEER_TN = 512
PEER_EC = 2048
SCALE = HEAD_DIM ** -0.5


def _cparams(sem):
    return pltpu.CompilerParams(dimension_semantics=sem, vmem_limit_bytes=VMEM_LIMIT)


def _dot(a, b, precision=None):
    return jnp.dot(a, b, precision=precision, preferred_element_type=F32)


def _dot_nt(a, b, precision=None):
    return lax.dot_general(a, b, (((1,), (1,)), ((), ())), precision=precision,
                           preferred_element_type=F32)


def _rms(x, g):
    return x * lax.rsqrt(jnp.mean(x * x, axis=-1, keepdims=True) + EPS) * g


def _ada_kernel(c_ref, w_ref, b_ref, o_ref):
    c = c_ref[...]
    sc = c * jax.nn.sigmoid(c)
    o_ref[0] = _dot(sc, w_ref[0], HIGHEST) + b_ref[0]


def _ada(c, ada_w, ada_b):
    B, D = c.shape
    E = ada_w.shape[2]
    tn = 2048
    return pl.pallas_call(
        _ada_kernel,
        out_shape=jax.ShapeDtypeStruct((DEPTH, B, E), F32),
        grid=(DEPTH, E // tn),
        in_specs=[pl.BlockSpec((B, D), lambda l, j: (0, 0)),
                  pl.BlockSpec((1, D, tn), lambda l, j: (l, 0, j)),
                  pl.BlockSpec((1, 1, tn), lambda l, j: (l, 0, j))],
        out_specs=pl.BlockSpec((1, B, tn), lambda l, j: (l, 0, j)),
        compiler_params=_cparams(("arbitrary", "arbitrary")),
        name="ada_mod",
    )(c, ada_w, ada_b.reshape(DEPTH, 1, E))


def _norm_proj_kernel(x_ref, sh_ref, sc_ref, ng_ref, w_ref, y_ref):
    h = _rms(x_ref[0], ng_ref[...]) * (1.0 + sc_ref[0]) + sh_ref[0]
    y_ref[0] = _dot(h.astype(BF16), w_ref[...])


def _norm_proj(x, sh, sc, ng, w):
    B, S, D = x.shape
    N = w.shape[1]
    return pl.pallas_call(
        _norm_proj_kernel,
        out_shape=jax.ShapeDtypeStruct((B, S, N), F32),
        grid=(B, S // ROW_TILE),
        in_specs=[pl.BlockSpec((1, ROW_TILE, D), lambda b, i: (b, i, 0)),
                  pl.BlockSpec((1, 1, D), lambda b, i: (b, 0, 0)),
                  pl.BlockSpec((1, 1, D), lambda b, i: (b, 0, 0)),
                  pl.BlockSpec((1, D), lambda b, i: (0, 0)),
                  pl.BlockSpec((D, N), lambda b, i: (0, 0))],
        out_specs=pl.BlockSpec((1, ROW_TILE, N), lambda b, i: (b, i, 0)),
        compiler_params=_cparams(("arbitrary", "arbitrary")),
        name="norm_proj",
    )(x, sh, sc, ng, w)


def _out_proj_kernel(x_ref, a_ref, b_ref, wa_ref, wb_ref, g_ref, o_ref):
    y = _dot(a_ref[0].astype(BF16), wa_ref[...]) + _dot(b_ref[0].astype(BF16), wb_ref[...])
    o_ref[0] = x_ref[0] + g_ref[0] * y


def _out_proj(x, a, a_blk, b, b_blk, w, gate):
    B, S, D = x.shape
    half = w.shape[0] // 2
    wa, wb = w[:half], w[half:]
    return pl.pallas_call(
        _out_proj_kernel,
        out_shape=jax.ShapeDtypeStruct((B, S, D), F32),
        grid=(B, S // ROW_TILE),
        in_specs=[pl.BlockSpec((1, ROW_TILE, D), lambda b, i: (b, i, 0)),
                  pl.BlockSpec((1, ROW_TILE, half), lambda b, i: (b, i, a_blk)),
                  pl.BlockSpec((1, ROW_TILE, half), lambda b, i: (b, i, b_blk)),
                  pl.BlockSpec((half, D), lambda b, i: (0, 0)),
                  pl.BlockSpec((half, D), lambda b, i: (0, 0)),
                  pl.BlockSpec((1, 1, D), lambda b, i: (b, 0, 0))],
        out_specs=pl.BlockSpec((1, ROW_TILE, D), lambda b, i: (b, i, 0)),
        compiler_params=_cparams(("arbitrary", "arbitrary")),
        name="out_proj",
    )(x, a, b, wa, wb, gate)


def _conv_kernel(cb_ref, cc_ref, ch_ref, w_ref, o_ref):
    u = cc_ref[0] * ch_ref[0]
    row = lax.broadcasted_iota(jnp.int32, u.shape, 0)
    u1 = jnp.where(row >= 1, pltpu.roll(u, 1, axis=0), 0.0)
    u2 = jnp.where(row >= 2, pltpu.roll(u, 2, axis=0), 0.0)
    w = w_ref[...]
    o_ref[0] = cb_ref[0] * (w[0:1] * u2 + w[1:2] * u1 + w[2:3] * u)


def _short_conv(y, conv_w):
    B, S, _ = y.shape
    nblk = CONV_CH // LANES
    w = jnp.zeros((8, CONV_CH), F32).at[:CONV_K].set(conv_w)
    return pl.pallas_call(
        _conv_kernel,
        out_shape=jax.ShapeDtypeStruct((B, S, CONV_CH), F32),
        grid=(B, nblk),
        in_specs=[pl.BlockSpec((1, S, LANES), lambda b, j: (b, 0, j)),
                  pl.BlockSpec((1, S, LANES), lambda b, j: (b, 0, nblk + j)),
                  pl.BlockSpec((1, S, LANES), lambda b, j: (b, 0, 2 * nblk + j)),
                  pl.BlockSpec((8, LANES), lambda b, j: (0, j))],
        out_specs=pl.BlockSpec((1, S, LANES), lambda b, j: (b, 0, j)),
        compiler_params=_cparams(("arbitrary", "arbitrary")),
        name="short_conv",
    )(y, y, y, w)


def _cumsum_rows(x):
    n = x.shape[0]
    row = lax.broadcasted_iota(jnp.int32, x.shape, 0)
    sh = 1
    while sh < n:
        x = x + jnp.where(row >= sh, pltpu.roll(x, sh, axis=0), 0.0)
        sh *= 2
    return x


FOX_KEY_PARTS = 4


def _fox_kernel(q_ref, k_ref, v_ref, f_ref, bf_ref, qg_ref, kg_ref, o_ref,
                kn_scr, vb_scr, cum_scr, cumrow_scr):
    qi = pl.program_id(1)
    S = k_ref.shape[1]
    T = q_ref.shape[1]
    nq = S // T

    @pl.when(qi == 0)
    def _prep():
        ones_col = (lax.broadcasted_iota(jnp.int32, (S, LANES - HEAD_DIM), 1) == 0).astype(BF16)
        for h in range(FOX_HEADS):
            cols = slice(h * HEAD_DIM, (h + 1) * HEAD_DIM)
            kn_scr[:, cols] = _rms(k_ref[0, :, cols], kg_ref[...]).astype(BF16)
            vb_scr[:, h * LANES:h * LANES + HEAD_DIM] = v_ref[0, :, cols].astype(BF16)
            vb_scr[:, h * LANES + HEAD_DIM:(h + 1) * LANES] = ones_col
        logf = jax.nn.log_sigmoid(f_ref[0] + bf_ref[...])
        cum = _cumsum_rows(logf)
        cum_scr[...] = cum
        for kt in range(S // T):
            cumrow_scr[:, kt * T:(kt + 1) * T] = cum[kt * T:(kt + 1) * T, :].T[0:8, :]

    cq = cum_scr[pl.ds(pl.multiple_of(qi * T, T), T), :]

    def attend(nkeys):
        row = qi * T + lax.broadcasted_iota(jnp.int32, (T, nkeys), 0)
        causal = jnp.where(lax.broadcasted_iota(jnp.int32, (T, nkeys), 1) <= row, 0.0, NEG_INF)
        for h in range(FOX_HEADS):
            cols = slice(h * HEAD_DIM, (h + 1) * HEAD_DIM)
            qh = (_rms(q_ref[0, :, cols], qg_ref[...]) * SCALE).astype(BF16)
            s = _dot_nt(qh, kn_scr[0:nkeys, cols]) + ((cq[:, h:h + 1] - cumrow_scr[h:h + 1, 0:nkeys]) + causal)
            p = jnp.exp(s - jnp.max(s, axis=-1, keepdims=True))
            acc = _dot(p.astype(BF16), vb_scr[0:nkeys, h * LANES:(h + 1) * LANES])
            o_ref[0, :, cols] = acc[:, :HEAD_DIM] / acc[:, HEAD_DIM:HEAD_DIM + 1]

    parts = min(FOX_KEY_PARTS, nq)
    per = nq // parts
    for part in range(parts):
        pl.when((qi >= part * per) & (qi < (part + 1) * per))(functools.partial(attend, (part + 1) * (S // parts)))


def _fox_attention(y, b_f, q_g, k_g):
    B, S, _ = y.shape
    T = FOX_TILE
    qblk = 3 * CONV_CH // FOX_WIDTH
    fblk = (3 * CONV_CH + 3 * FOX_WIDTH) // LANES
    bf = jnp.zeros((1, LANES), F32).at[0, :FOX_HEADS].set(b_f)
    return pl.pallas_call(
        _fox_kernel,
        out_shape=jax.ShapeDtypeStruct((B, S, FOX_WIDTH), F32),
        grid=(B, S // T),
        in_specs=[pl.BlockSpec((1, T, FOX_WIDTH), lambda b, i: (b, i, qblk)),
                  pl.BlockSpec((1, S, FOX_WIDTH), lambda b, i: (b, 0, qblk + 1)),
                  pl.BlockSpec((1, S, FOX_WIDTH), lambda b, i: (b, 0, qblk + 2)),
                  pl.BlockSpec((1, S, LANES), lambda b, i: (b, 0, fblk)),
                  pl.BlockSpec((1, LANES), lambda b, i: (0, 0)),
                  pl.BlockSpec((1, HEAD_DIM), lambda b, i: (0, 0)),
                  pl.BlockSpec((1, HEAD_DIM), lambda b, i: (0, 0))],
        out_specs=pl.BlockSpec((1, T, FOX_WIDTH), lambda b, i: (b, i, 0)),
        scratch_shapes=[pltpu.VMEM((S, FOX_WIDTH), BF16),
                        pltpu.VMEM((S, FOX_HEADS * LANES), BF16),
                        pltpu.VMEM((S, LANES), F32),
                        pltpu.VMEM((8, S), F32)],
        compiler_params=_cparams(("arbitrary", "arbitrary")),
        name="fox_attention",
    )(y, y, y, y, bf, q_g.reshape(1, HEAD_DIM), k_g.reshape(1, HEAD_DIM))


def _gelu_tanh(x):
    c0 = math.sqrt(2.0 / math.pi)
    return (0.5 * x) * (1.0 + jnp.tanh(x * (c0 + (c0 * 0.044715) * (x * x))))


def _extract_top(work, iota, k, exact):
    nrows = work.shape[0]
    vals = []
    for _ in range(k):
        m = jnp.max(work, axis=0, keepdims=True)
        if exact:
            idx = jnp.min(jnp.where(work == m, iota, nrows), axis=0, keepdims=True)
            work = jnp.where(iota == idx, -jnp.inf, work)
        else:
            work = jnp.where(work == m, -jnp.inf, work)
        vals.append(m)
    return vals, work


def _removed(work):
    return jnp.sum(jnp.where(work == -jnp.inf, 1.0, 0.0), axis=0, keepdims=True)


def _peer_kernel(x_ref, sh_ref, sc_ref, gt_ref, ng_ref, wqh_ref, wql_ref, keys_ref, u_ref, vt_ref, o_ref,
                 ht_scr, qt_scr, sp_scr, t0_scr, t1_scr, tau_scr, a_scr, p_scr, acc_scr):
    c = pl.program_id(2)
    nc = pl.num_programs(2)
    tn = x_ref.shape[1]
    nlt = tn // LANES
    half = PEER_QDIM // 2

    @pl.when(c == 0)
    def _prep():
        h = _rms(x_ref[0], ng_ref[...]) * (1.0 + sc_ref[0]) + sh_ref[0]
        hh = h.astype(BF16)
        hl = (h - hh.astype(F32)).astype(BF16)
        cw = 2 * LANES
        for k in range(h.shape[1] // cw):
            cs = slice(k * cw, (k + 1) * cw)
            q = _dot(hh, wqh_ref[:, cs]) + (_dot(hh, wql_ref[:, cs]) + _dot(hl, wqh_ref[:, cs]))
            qt_scr[cs, :] = q.T
            ht_scr[cs, :] = h[:, cs].T.astype(BF16)
        acc_scr[...] = jnp.zeros_like(acc_scr)
        iota_k = lax.broadcasted_iota(jnp.int32, (PEER_NKEYS, LANES), 0)
        iota_c = lax.broadcasted_iota(jnp.int32, (PEER_TOPK + 8 * 8, LANES), 0)

        n_pad = sum(8 - PEER_TOPK // (r0 + 1) for r0 in range(2, 8))

        def select(hd, slot, lt, exact):
            lanes = slice(lt * LANES, (lt + 1) * LANES)
            s0 = sp_scr[2 * slot, :, lanes]
            s1 = sp_scr[2 * slot + 1, :, lanes]
            sv0, w0 = _extract_top(s0, iota_k, PEER_TOPK, exact)
            sv1, w1 = _extract_top(s1, iota_k, PEER_TOPK, exact)
            s0m = jnp.where(w0 == -jnp.inf, s0, -jnp.inf)
            s1m = jnp.where(w1 == -jnp.inf, s1, -jnp.inf)
            sv0_all = jnp.concatenate(sv0, axis=0)
            sv1_all = jnp.concatenate(sv1, axis=0)
            parts = [sv0[0] + sv1_all, sv0[1] + sv1_all[:8]]
            for r0 in range(2, 8):
                parts.append(jnp.where(iota_c[:8] < PEER_TOPK // (r0 + 1), sv0[r0] + sv1_all[:8], -jnp.inf))
            parts.append(sv0_all[8:] + sv1[0])
            cv, wc = _extract_top(jnp.concatenate(parts, axis=0), iota_c, PEER_TOPK, exact)
            z = cv[0] - cv[0]
            for r in range(PEER_TOPK):
                z = z + jnp.exp(cv[r] - cv[0])
            nrt = PEER_NKEYS // 8
            t0_scr[hd, lt, :, 0] = s0m.reshape(nrt, 8, LANES)
            t0_scr[hd, lt, :, 1] = jnp.exp(s0m - sv0[0]).reshape(nrt, 8, LANES)
            t1_scr[hd, lt, :, 0] = s1m.reshape(nrt, 8, LANES)
            t1_scr[hd, lt, :, 1] = (jnp.exp(s1m - sv1[0]) / z).reshape(nrt, 8, LANES)
            tau_scr[hd, lt] = jnp.broadcast_to(cv[PEER_TOPK - 1], (8, LANES))
            return ((_removed(w0) != PEER_TOPK) | (_removed(w1) != PEER_TOPK)
                    | (_removed(wc) != PEER_TOPK + n_pad))

        def head_pair_body(hp, carry):
            for slot in range(2):
                for p in range(2):
                    qs = qt_scr[pl.ds(pl.multiple_of(((hp * 2 + slot) * 2 + p) * half, half), half), :]
                    sp_scr[2 * slot + p] = _dot(keys_ref[p], qs, HIGHEST)
            tied = [jnp.zeros((1, LANES), F32) for _ in range(2)]
            for lt in range(nlt):
                for slot in range(2):
                    tied[slot] = jnp.maximum(tied[slot], jnp.where(select(hp * 2 + slot, slot, lt, False), 1.0, 0.0))
            for slot in range(2):
                @pl.when(jnp.max(tied[slot]) > 0.0)
                def _redo(slot=slot):
                    for lt in range(nlt):
                        select(hp * 2 + slot, slot, lt, True)
            return carry

        lax.fori_loop(0, PEER_HEADS // 2, head_pair_body, 0)

    nsub = 8
    nrq = 32
    nparts = PEER_EC // PEER_NKEYS // nsub
    for part in range(nparts):
        er = slice(part * nsub * PEER_NKEYS, (part + 1) * nsub * PEER_NKEYS)
        for lh in range(nlt // 2):
            lanes2 = slice(lh * 2 * LANES, (lh + 1) * 2 * LANES)
            a_scr[er, lanes2] = _gelu_tanh(_dot(u_ref[er, :], ht_scr[:, lanes2]))

    def weights_tile(part, lt, rq, prev):
        lanes = slice(lt * LANES, (lt + 1) * LANES)
        jt = slice(rq * nrq // 8, (rq + 1) * nrq // 8)
        blk = c * (PEER_EC // PEER_NKEYS // 8) + part * nsub // 8
        zero = 0.0 if prev is None else jnp.where(prev != prev, prev, 0.0)
        w = [jnp.zeros((nrq, LANES), F32) for _ in range(nsub)]
        for hd in range(PEER_HEADS):
            s1 = t1_scr[hd, lt, jt, 0].reshape(nrq, LANES)
            e1 = t1_scr[hd, lt, jt, 1].reshape(nrq, LANES)
            s0 = t0_scr[hd, lt, blk, 0] + zero
            e0 = t0_scr[hd, lt, blk, 1] + zero
            tau = tau_scr[hd, lt, 0:1, :]
            for ii in range(nsub):
                keep = s1 + s0[ii:ii + 1] >= tau
                w[ii] = w[ii] + jnp.where(keep, e1 * e0[ii:ii + 1], 0.0)
        for ii in range(nsub):
            rows = slice((part * nsub + ii) * PEER_NKEYS + rq * nrq, (part * nsub + ii) * PEER_NKEYS + (rq + 1) * nrq)
            w[ii] = w[ii] * a_scr[rows, lanes]
            p_scr[rows, lanes] = w[ii].astype(BF16)
        return w[nsub - 1][0:1, :]

    in_flight = 8
    done = [None] * in_flight
    for part in range(nparts):
        for lt in range(nlt):
            for rq in range(PEER_NKEYS // nrq):
                done.append(weights_tile(part, lt, rq, done[-in_flight]))
    acc_scr[...] += _dot(vt_ref[...], p_scr[...])

    @pl.when(c == nc - 1)
    def _fin():
        o_ref[0] = x_ref[0] + gt_ref[0] * acc_scr[...].T


def _peer(x, sh, sc, gate, ng, w_q, keys, u_b, vt_b):
    B, S, D = x.shape
    tn = PEER_TN
    nc = PEER_EXPERTS // PEER_EC
    tab = pltpu.VMEM((PEER_HEADS, tn // LANES, PEER_NKEYS // 8, 2, 8, LANES), F32)
    wq_hi = w_q.astype(BF16)
    wq_lo = (w_q - wq_hi.astype(F32)).astype(BF16)
    wspec = pl.BlockSpec((D, PEER_HEADS * PEER_QDIM), lambda b, i, c: (0, 0), pipeline_mode=pl.Buffered(1))
    return pl.pallas_call(
        _peer_kernel,
        out_shape=jax.ShapeDtypeStruct((B, S, D), F32),
        grid=(B, S // tn, nc),
        in_specs=[pl.BlockSpec((1, tn, D), lambda b, i, c: (b, i, 0)),
                  pl.BlockSpec((1, 1, D), lambda b, i, c: (b, 0, 0)),
                  pl.BlockSpec((1, 1, D), lambda b, i, c: (b, 0, 0)),
                  pl.BlockSpec((1, 1, D), lambda b, i, c: (b, 0, 0)),
                  pl.BlockSpec((1, D), lambda b, i, c: (0, 0)),
                  wspec, wspec,
                  pl.BlockSpec((2, PEER_NKEYS, PEER_QDIM // 2), lambda b, i, c: (0, 0, 0)),
                  pl.BlockSpec((PEER_EC, D), lambda b, i, c: (c, 0)),
                  pl.BlockSpec((D, PEER_EC), lambda b, i, c: (0, c))],
        out_specs=pl.BlockSpec((1, tn, D), lambda b, i, c: (b, i, 0)),
        scratch_shapes=[pltpu.VMEM((D, tn), BF16),
                        pltpu.VMEM((PEER_HEADS * PEER_QDIM, tn), F32),
                        pltpu.VMEM((4, PEER_NKEYS, tn), F32),
                        tab, tab,
                        pltpu.VMEM((PEER_HEADS, tn // LANES, 8, LANES), F32),
                        pltpu.VMEM((PEER_EC, tn), F32),
                        pltpu.VMEM((PEER_EC, tn), BF16),
                        pltpu.VMEM((D, tn), F32)],
        compiler_params=_cparams(("arbitrary", "arbitrary", "arbitrary")),
        name="peer",
    )(x, sh, sc, gate, ng, wq_hi, wq_lo, keys, u_b, vt_b)


def _rel_bucket(dist):
    n = jnp.maximum(dist, 0)
    max_exact = REL_BUCKETS // 2
    nf = jnp.maximum(n, 1).astype(F32)
    large = max_exact + (jnp.log(nf / max_exact) / math.log(REL_MAX_DIST / max_exact)
                         * (REL_BUCKETS - max_exact)).astype(jnp.int32)
    large = jnp.minimum(large, REL_BUCKETS - 1)
    return jnp.where(n < max_exact, n, large)


def _bias_lookup(bucket, tab_ref, h):
    bias = jnp.zeros(bucket.shape, F32)
    for b in range(REL_BUCKETS):
        bias = jnp.where(bucket == b, tab_ref[b, h], bias)
    return bias


N_BIAS_TILES = 5


def _relbias_kernel(tab_ref, o_ref, c_ref):
    h = pl.program_id(0)
    T = o_ref.shape[2]
    a = lax.broadcasted_iota(jnp.int32, (T, T), 0)
    b = lax.broadcasted_iota(jnp.int32, (T, T), 1)
    last = tab_ref[REL_BUCKETS - 1, h]
    o_ref[0, 0] = jnp.where(a >= b, _bias_lookup(_rel_bucket(a - b), tab_ref, h), NEG_INF)
    o_ref[0, 1] = _bias_lookup(_rel_bucket(a - b + T), tab_ref, h)
    o_ref[0, 2] = jnp.full((T, T), last, F32)
    o_ref[0, 3] = jnp.where(a < b, last, NEG_INF)
    o_ref[0, 4] = jnp.full((T, T), NEG_INF, F32)
    dist = a - ((b - T // 2) * CMP_STRIDE + CMP_LEN - 1)
    c_ref[0] = jnp.where(dist >= 0, _bias_lookup(_rel_bucket(dist), tab_ref, h), last)


def _relbias(rel_table):
    T = ATT_TILE
    assert 2 * T - (T - 1) >= REL_MAX_DIST
    assert WINDOW % T == 0
    return pl.pallas_call(
        _relbias_kernel,
        out_shape=(jax.ShapeDtypeStruct((NSA_HEADS, N_BIAS_TILES, T, T), F32),
                   jax.ShapeDtypeStruct((NSA_HEADS, T, T), F32)),
        grid=(NSA_HEADS,),
        in_specs=[pl.BlockSpec(memory_space=pltpu.SMEM)],
        out_specs=(pl.BlockSpec((1, N_BIAS_TILES, T, T), lambda h: (h, 0, 0, 0)),
                   pl.BlockSpec((1, T, T), lambda h: (h, 0, 0))),
        compiler_params=_cparams(("arbitrary",)),
        name="rel_bias_tiles",
    )(rel_table)


def _cmp_kernel(x_ref, pos_ref, w1_ref, w2_ref, kg_ref, o_ref):
    j = pl.program_id(1)
    half = CMP_STRIDE * HEAD_DIM
    x = x_ref[0, 0]
    pos = pos_ref[0]
    pa = _dot((x + pos[:, :half]).astype(BF16), w1_ref[0, :half, :])
    pb = _dot((x + pos[:, half:]).astype(BF16), w1_ref[0, half:, :])
    pre = pa + pltpu.roll(pb, pb.shape[0] - 1, axis=0)
    out = _dot(jax.nn.gelu(pre, approximate=True).astype(BF16), w2_ref[0])
    o_ref[0, 0] = jnp.where(j < NSA_GROUPS, _rms(out, kg_ref[...]), out)


def _nsa_compress(xr, pos, w1, w2, kg0):
    B, _, M, W = xr.shape
    G = NSA_GROUPS
    return pl.pallas_call(
        _cmp_kernel,
        out_shape=jax.ShapeDtypeStruct((B, 2 * G, M, HEAD_DIM), F32),
        grid=(B, 2 * G),
        in_specs=[pl.BlockSpec((1, 1, M, W), lambda b, j: (b, j, 0, 0)),
                  pl.BlockSpec((1, 1, 2 * W), lambda b, j: (j // G, 0, 0)),
                  pl.BlockSpec((1, 2 * W, HEAD_DIM), lambda b, j: (j // G, 0, 0)),
                  pl.BlockSpec((1, HEAD_DIM, HEAD_DIM), lambda b, j: (j // G, 0, 0)),
                  pl.BlockSpec((1, HEAD_DIM), lambda b, j: (0, 0))],
        out_specs=pl.BlockSpec((1, 1, M, HEAD_DIM), lambda b, j: (b, j, 0, 0)),
        compiler_params=_cparams(("arbitrary", "arbitrary")),
        name="nsa_compress",
    )(xr, pos.reshape(2, 1, 2 * W), w1.astype(BF16), w2.astype(BF16), kg0.reshape(1, HEAD_DIM))


def _cmp_attn_kernel(q_ref, kv_ref, ov_ref, cb_ref, qg_ref, oc_ref, sel_ref, qn_ref):
    qi = pl.program_id(1)
    T = q_ref.shape[1]
    NC = kv_ref.shape[2]
    G, J = NSA_GROUPS, NSA_HPG
    NS = sel_ref.shape[2] // G
    assert T == NC == LANES and T % CMP_STRIDE == 0 and G * NS == LANES
    t = qi * T + lax.broadcasted_iota(jnp.int32, (T, NC), 0)
    lane = lax.broadcasted_iota(jnp.int32, (T, NC), 1)
    vis = t - (lane * CMP_STRIDE + CMP_LEN - 1) >= 0
    visf = vis.astype(F32)
    shift = (qi * (T // CMP_STRIDE) + T // 2) % LANES
    imp = jnp.zeros((T, LANES), F32)
    vis4 = jnp.concatenate([vis] * J, axis=0)
    visf4 = jnp.concatenate([visf] * J, axis=0)
    for g in range(G):
        kc = kv_ref[0, g].astype(BF16)
        vc = kv_ref[0, G + g].astype(BF16)
        qhs = []
        for j in range(J):
            cols = slice((g * J + j) * HEAD_DIM, (g * J + j + 1) * HEAD_DIM)
            qhs.append((_rms(q_ref[0, :, cols], qg_ref[...]) * SCALE).astype(BF16))
            qn_ref[0, :, cols] = qhs[j]
        bias = jnp.concatenate([pltpu.roll(cb_ref[g * J + j], shift, axis=1) for j in range(J)], axis=0)
        s = jnp.where(vis4, _dot_nt(jnp.concatenate(qhs, axis=0), kc) + bias, NEG_INF)
        e = jnp.exp(s - jnp.max(s, axis=-1, keepdims=True))
        p = e / jnp.sum(e, axis=-1, keepdims=True) * visf4
        o = _dot(p.astype(BF16), vc)
        psum = p[0:T]
        for j in range(J):
            oc_ref[0, :, (g * J + j) * HEAD_DIM:(g * J + j + 1) * HEAD_DIM] = o[j * T:(j + 1) * T]
            if j:
                psum = psum + p[j * T:(j + 1) * T]
        imp = imp + _dot(psum, ov_ref[g], HIGHEST)
    mm = lane % NS
    cur = t // SLC_LEN
    forced = (mm == 0) | (mm == cur) | (mm == cur - 1)
    imp = jnp.where(forced, FORCE_SCORE, jnp.where(mm * SLC_LEN <= t, imp, -1.0))
    rank = jnp.zeros((T, LANES), F32)
    for r in range(1, NS):
        wrapped = mm >= NS - r
        other = jnp.where(wrapped, pltpu.roll(imp, NS - r, axis=1), pltpu.roll(imp, LANES - r, axis=1))
        rank = rank + jnp.where(other > imp, 1.0, jnp.where(other == imp, wrapped.astype(F32), 0.0))
    sel_ref[0] = jnp.where(rank < SLC_TOPN, jnp.where(imp >= 0.0, 1.0, 0.0), 0.0)


def _nsa_cmp_attn(y, kcvc, overlap, cmp_bias, q_g):
    B, S, _ = y.shape
    T = ATT_TILE
    NC = kcvc.shape[2]
    return pl.pallas_call(
        _cmp_attn_kernel,
        out_shape=(jax.ShapeDtypeStruct((B, S, NSA_WIDTH), F32),
                   jax.ShapeDtypeStruct((B, S, LANES), F32),
                   jax.ShapeDtypeStruct((B, S, NSA_WIDTH), BF16)),
        grid=(B, S // T),
        in_specs=[pl.BlockSpec((1, T, NSA_WIDTH), lambda b, i: (b, i, 0)),
                  pl.BlockSpec((1, 2 * NSA_GROUPS, NC, HEAD_DIM), lambda b, i: (b, 0, 0, 0)),
                  pl.BlockSpec(overlap.shape, lambda b, i: (0, 0, 0)),
                  pl.BlockSpec(cmp_bias.shape, lambda b, i: (0, 0, 0)),
                  pl.BlockSpec((1, HEAD_DIM), lambda b, i: (0, 0))],
        out_specs=(pl.BlockSpec((1, T, NSA_WIDTH), lambda b, i: (b, i, 0)),
                   pl.BlockSpec((1, T, LANES), lambda b, i: (b, i, 0)),
                   pl.BlockSpec((1, T, NSA_WIDTH), lambda b, i: (b, i, 0))),
        compiler_params=_cparams(("arbitrary", "arbitrary")),
        name="nsa_cmp_attn",
    )(y, kcvc, overlap, cmp_bias, q_g.reshape(1, HEAD_DIM))


def _kv_prep_kernel(ks_ref, kw_ref, kg1_ref, kg2_ref, ksn_ref, vs_ref, kwn_ref, vw_ref):
    KV = NSA_KV
    rows = ks_ref.shape[1]
    ones_col = (lax.broadcasted_iota(jnp.int32, (rows, LANES - HEAD_DIM), 1) == 0).astype(BF16)
    for g in range(NSA_GROUPS):
        cols = slice(g * HEAD_DIM, (g + 1) * HEAD_DIM)
        vcols = slice(KV + g * HEAD_DIM, KV + (g + 1) * HEAD_DIM)
        ksn_ref[0, :, cols] = _rms(ks_ref[0, :, cols], kg1_ref[...]).astype(BF16)
        kwn_ref[0, :, cols] = _rms(kw_ref[0, :, cols], kg2_ref[...]).astype(BF16)
        vs_ref[0, :, g * LANES:g * LANES + HEAD_DIM] = ks_ref[0, :, vcols].astype(BF16)
        vw_ref[0, :, g * LANES:g * LANES + HEAD_DIM] = kw_ref[0, :, vcols].astype(BF16)
        vs_ref[0, :, g * LANES + HEAD_DIM:(g + 1) * LANES] = ones_col
        vw_ref[0, :, g * LANES + HEAD_DIM:(g + 1) * LANES] = ones_col


def _nsa_kv_prep(y, k_g):
    B, S, _ = y.shape
    ksblk = (NSA_WIDTH + 2 * NSA_KV) // (2 * NSA_KV)
    kout = jax.ShapeDtypeStruct((B, S, NSA_KV), BF16)
    vout = jax.ShapeDtypeStruct((B, S, NSA_GROUPS * LANES), BF16)
    kspec = pl.BlockSpec((1, ROW_TILE, NSA_KV), lambda b, i: (b, i, 0))
    vspec = pl.BlockSpec((1, ROW_TILE, NSA_GROUPS * LANES), lambda b, i: (b, i, 0))
    return pl.pallas_call(
        _kv_prep_kernel,
        out_shape=(kout, vout, kout, vout),
        grid=(B, S // ROW_TILE),
        in_specs=[pl.BlockSpec((1, ROW_TILE, 2 * NSA_KV), lambda b, i: (b, i, ksblk)),
                  pl.BlockSpec((1, ROW_TILE, 2 * NSA_KV), lambda b, i: (b, i, ksblk + 1)),
                  pl.BlockSpec((1, HEAD_DIM), lambda b, i: (0, 0)),
                  pl.BlockSpec((1, HEAD_DIM), lambda b, i: (0, 0))],
        out_specs=(kspec, vspec, kspec, vspec),
        compiler_params=_cparams(("arbitrary", "arbitrary")),
        name="nsa_kv_prep",
    )(y, y, k_g[1].reshape(1, HEAD_DIM), k_g[2].reshape(1, HEAD_DIM))


SEL_KEY_PARTS = 4


def _sel_win_kernel(qn_ref, ks_ref, vs_ref, kw_ref, vw_ref, gl_ref, sel_ref, oc_ref, bias_ref, e_ref, bg_ref,
                    o_ref):
    qi = pl.program_id(1)
    T = qn_ref.shape[1]
    S = ks_ref.shape[1]
    nk = S // T
    G, J = NSA_GROUPS, NSA_HPG
    NS = e_ref.shape[0]
    wt = WINDOW // T
    nwin = wt + 1
    assert nk % SEL_KEY_PARTS == 0 and nk >= nwin

    def sel_tile(r):
        return jnp.where(r < 0, 4, jnp.minimum(r, 2))

    def win_tile(r):
        return jnp.where(r < 0, 4, jnp.where(r <= 1, r, jnp.where(r < wt, 2, jnp.where(r == wt, 3, 4))))

    def softmax_pv(s, v):
        p = jnp.exp(s - jnp.max(s, axis=-1, keepdims=True))
        acc = _dot(p.astype(BF16), v)
        return acc[:, :HEAD_DIM] / acc[:, HEAD_DIM:HEAD_DIM + 1]

    gates = jax.nn.sigmoid(gl_ref[0] + bg_ref[...])
    k0 = jnp.maximum(qi - wt, 0)
    wrows = pl.ds(pl.multiple_of(k0 * T, T), nwin * T)

    def attend(nkt):
        nkeys = nkt * T
        for g in range(G):
            gcols = slice(g * HEAD_DIM, (g + 1) * HEAD_DIM)
            vcols = slice(g * LANES, (g + 1) * LANES)
            q4 = jnp.concatenate(
                [qn_ref[0, :, (g * J + j) * HEAD_DIM:(g * J + j + 1) * HEAD_DIM] for j in range(J)], axis=0)
            selg = sel_ref[0, :, g * NS:(g + 1) * NS].astype(BF16)
            mask = (_dot(selg, e_ref[:, :nkeys]) - 1.0) * (-NEG_INF)
            bias = jnp.concatenate(
                [jnp.concatenate([bias_ref[g * J + j, sel_tile(qi - kt)] for kt in range(nkt)], axis=1) + mask
                 for j in range(J)], axis=0)
            o_s = softmax_pv(_dot_nt(q4, ks_ref[0, :nkeys, gcols]) + bias, vs_ref[0, :nkeys, vcols])
            wbias = jnp.concatenate(
                [jnp.concatenate([bias_ref[g * J + j, win_tile(qi - k0 - t)] for t in range(nwin)], axis=1)
                 for j in range(J)], axis=0)
            o_w = softmax_pv(_dot_nt(q4, kw_ref[0, wrows, gcols]) + wbias, vw_ref[0, wrows, vcols])
            for j in range(J):
                h = g * J + j
                cols = slice(h * HEAD_DIM, (h + 1) * HEAD_DIM)
                rows = slice(j * T, (j + 1) * T)
                o_ref[0, :, cols] = (gates[:, 3 * h:3 * h + 1] * oc_ref[0, :, cols]
                                     + gates[:, 3 * h + 1:3 * h + 2] * o_s[rows]
                                     + gates[:, 3 * h + 2:3 * h + 3] * o_w[rows])

    per = nk // SEL_KEY_PARTS
    for part in range(SEL_KEY_PARTS):
        pl.when((qi >= part * per) & (qi < (part + 1) * per))(functools.partial(attend, (part + 1) * per))


def _nsa_sel_win(qn, y, kv, sel, oc, bias_tiles, expand, b_gate):
    B, S, _ = y.shape
    T = ATT_TILE
    glblk = (NSA_WIDTH + 6 * NSA_KV) // LANES
    bg = jnp.zeros((1, LANES), F32).at[0, :3 * NSA_HEADS].set(b_gate)
    kspec = pl.BlockSpec((1, S, NSA_KV), lambda b, i: (b, 0, 0))
    vspec = pl.BlockSpec((1, S, NSA_GROUPS * LANES), lambda b, i: (b, 0, 0))
    return pl.pallas_call(
        _sel_win_kernel,
        out_shape=jax.ShapeDtypeStruct((B, S, NSA_WIDTH), F32),
        grid=(B, S // T),
        in_specs=[pl.BlockSpec((1, T, NSA_WIDTH), lambda b, i: (b, i, 0)),
                  kspec, vspec, kspec, vspec,
                  pl.BlockSpec((1, T, LANES), lambda b, i: (b, i, glblk)),
                  pl.BlockSpec((1, T, sel.shape[2]), lambda b, i: (b, i, 0)),
                  pl.BlockSpec((1, T, NSA_WIDTH), lambda b, i: (b, i, 0)),
                  pl.BlockSpec(bias_tiles.shape, lambda b, i: (0, 0, 0, 0)),
                  pl.BlockSpec(expand.shape, lambda b, i: (0, 0)),
                  pl.BlockSpec((1, LANES), lambda b, i: (0, 0))],
        out_specs=pl.BlockSpec((1, T, NSA_WIDTH), lambda b, i: (b, i, 0)),
        compiler_params=_cparams(("arbitrary", "arbitrary")),
        name="nsa_sel_win",
    )(qn, *kv, y, sel, oc, bias_tiles, expand, bg)


def _odd_mixer(x, sh, sc, gate, ng, w_in, b_gate, q_g, k_g, cmp_pos, cmp_w1, cmp_w2, bias_tiles, cmp_bias,
               w_out):
    B, S, _ = x.shape
    G = NSA_GROUPS
    w = jnp.zeros((D_MODEL, ODD_IN_PAD), BF16).at[:, :ODD_IN].set(w_in.astype(BF16))
    y = _norm_proj(x, sh, sc, ng, w)
    xr = y[:, :, NSA_WIDTH:NSA_WIDTH + 2 * NSA_KV].reshape(B, S, 2, G, HEAD_DIM)
    xr = xr.transpose(0, 2, 3, 1, 4).reshape(B, 2 * G, S // CMP_STRIDE, CMP_STRIDE * HEAD_DIM)
    kcvc = _nsa_compress(xr, cmp_pos.reshape(2, CMP_LEN * HEAD_DIM), cmp_w1, cmp_w2, k_g[0])
    n_slc = S // SLC_LEN
    starts = jnp.arange(S // CMP_STRIDE) * CMP_STRIDE
    bstart = jnp.arange(n_slc) * SLC_LEN
    real = (jnp.arange(S // CMP_STRIDE) < (S - CMP_LEN) // CMP_STRIDE + 1)[:, None]
    overlap = (real & (starts[:, None] < bstart[None, :] + SLC_LEN)
               & (starts[:, None] + CMP_LEN > bstart[None, :])).astype(F32)
    overlap = jnp.stack([jnp.pad(overlap, ((0, 0), (g * n_slc, (G - 1 - g) * n_slc))) for g in range(G)])
    expand = (jnp.arange(S)[None, :] // SLC_LEN == jnp.arange(n_slc)[:, None]).astype(BF16)
    oc, sel, qn = _nsa_cmp_attn(y, kcvc, overlap, cmp_bias, q_g)
    o = _nsa_sel_win(qn, y, _nsa_kv_prep(y, k_g), sel, oc, bias_tiles, expand, b_gate)
    return _out_proj(x, o, 0, o, 1, w_out.astype(BF16), gate)


def _even_mixer(x, sh, sc, gate, ng, w_in, b_f, conv_w, q_g, k_g, w_out):
    w = jnp.zeros((D_MODEL, EVEN_IN_PAD), BF16).at[:, :EVEN_IN].set(w_in.astype(BF16))
    y = _norm_proj(x, sh, sc, ng, w)
    conv = _short_conv(y, conv_w)
    attn = _fox_attention(y, b_f, q_g, k_g)
    return _out_proj(x, conv, 0, attn, 0, w_out.astype(BF16), gate)


def kernel(x, c, ada_w, ada_b, norm_g, even_w_in, even_b_f, even_conv_w, even_q_g, even_k_g, even_w_out, odd_w_in, odd_b_gate, odd_q_g, odd_k_g, odd_cmp_pos, odd_cmp_w1, odd_cmp_w2, odd_w_out, rel_table, peer_w_q, peer_keys, peer_u, peer_v):
    B = x.shape[0]
    mods = _ada(c, ada_w, ada_b)
    bias_tiles, cmp_bias = _relbias(rel_table)
    for layer in range(DEPTH):
        sh1, sc1, g1, sh2, sc2, g2 = [m.reshape(B, 1, D_MODEL) for m in jnp.split(mods[layer], 6, axis=-1)]
        i = layer // 2
        ng1 = norm_g[layer, 0].reshape(1, D_MODEL)
        ng2 = norm_g[layer, 1].reshape(1, D_MODEL)
        if layer % 2 == 0:
            x = _even_mixer(x, sh1, sc1, g1, ng1, even_w_in[i], even_b_f[i], even_conv_w[i], even_q_g[i],
                            even_k_g[i], even_w_out[i])
        else:
            x = _odd_mixer(x, sh1, sc1, g1, ng1, odd_w_in[i], odd_b_gate[i], odd_q_g[i], odd_k_g[i],
                           odd_cmp_pos[i], odd_cmp_w1[i], odd_cmp_w2[i], bias_tiles, cmp_bias, odd_w_out[i])
        x = _peer(x, sh2, sc2, g2, ng2, peer_w_q[layer], peer_keys[layer],
                  peer_u[layer].astype(BF16), peer_v[layer].T.astype(BF16))
    return x
```

```python
import functools
import math

import jax
import jax.numpy as jnp
from jax import lax
from jax.experimental import pallas as pl
from jax.experimental.pallas import tpu as pltpu

D_MODEL = 1024
DEPTH = 4
HEAD_DIM = 64
CONV_CH = 512
CONV_K = 3
FOX_HEADS = 8
FOX_WIDTH = FOX_HEADS * HEAD_DIM
EVEN_IN = 3 * CONV_CH + 3 * FOX_WIDTH + FOX_HEADS
NSA_GROUPS = 4
NSA_HPG = 4
NSA_HEADS = NSA_GROUPS * NSA_HPG
NSA_WIDTH = NSA_HEADS * HEAD_DIM
NSA_KV = NSA_GROUPS * HEAD_DIM
CMP_LEN = 32
CMP_STRIDE = 16
SLC_LEN = 64
SLC_TOPN = 16
WINDOW = 512
ODD_IN = NSA_WIDTH + 6 * NSA_KV + 3 * NSA_HEADS
REL_BUCKETS = 32
REL_MAX_DIST = 128
PEER_HEADS = 8
PEER_NKEYS = 128
PEER_EXPERTS = PEER_NKEYS * PEER_NKEYS
PEER_QDIM = 128
PEER_TOPK = 16
EPS = 1e-6
NEG_INF = -1e30
FORCE_SCORE = 1e6

LANES = 128
F32 = jnp.float32
BF16 = jnp.bfloat16
HIGHEST = lax.Precision.HIGHEST
VMEM_LIMIT = 56 * 1024 * 1024

EVEN_IN_PAD = -(-EVEN_IN // LANES) * LANES
ODD_IN_PAD = -(-ODD_IN // LANES) * LANES
ROW_TILE = 512
ATT_TILE = 128
FOX_TILE = 256
PEER_TN = 512
PEER_EC = 2048
SCALE = HEAD_DIM ** -0.5


def _cparams(sem):
    return pltpu.CompilerParams(dimension_semantics=sem, vmem_limit_bytes=VMEM_LIMIT)


def _dot(a, b, precision=None):
    return jnp.dot(a, b, precision=precision, preferred_element_type=F32)


def _dot_nt(a, b, precision=None):
    return lax.dot_general(a, b, (((1,), (1,)), ((), ())), precision=precision,
                           preferred_element_type=F32)


def _rms(x, g):
    return x * lax.rsqrt(jnp.mean(x * x, axis=-1, keepdims=True) + EPS) * g


def _ada_kernel(c_ref, w_ref, b_ref, o_ref):
    c = c_ref[...]
    sc = c * jax.nn.sigmoid(c)
    o_ref[0] = _dot(sc, w_ref[0], HIGHEST) + b_ref[0]


def _ada(c, ada_w, ada_b):
    B, D = c.shape
    E = ada_w.shape[2]
    tn = 2048
    return pl.pallas_call(
        _ada_kernel,
        out_shape=jax.ShapeDtypeStruct((DEPTH, B, E), F32),
        grid=(DEPTH, E // tn),
        in_specs=[pl.BlockSpec((B, D), lambda l, j: (0, 0)),
                  pl.BlockSpec((1, D, tn), lambda l, j: (l, 0, j)),
                  pl.BlockSpec((1, 1, tn), lambda l, j: (l, 0, j))],
        out_specs=pl.BlockSpec((1, B, tn), lambda l, j: (l, 0, j)),
        compiler_params=_cparams(("arbitrary", "arbitrary")),
        name="ada_mod",
    )(c, ada_w, ada_b.reshape(DEPTH, 1, E))


def _norm_proj_kernel(x_ref, sh_ref, sc_ref, ng_ref, w_ref, y_ref):
    h = _rms(x_ref[0], ng_ref[...]) * (1.0 + sc_ref[0]) + sh_ref[0]
    y_ref[0] = _dot(h.astype(BF16), w_ref[...])


def _norm_proj(x, sh, sc, ng, w):
    B, S, D = x.shape
    N = w.shape[1]
    return pl.pallas_call(
        _norm_proj_kernel,
        out_shape=jax.ShapeDtypeStruct((B, S, N), F32),
        grid=(B, S // ROW_TILE),
        in_specs=[pl.BlockSpec((1, ROW_TILE, D), lambda b, i: (b, i, 0)),
                  pl.BlockSpec((1, 1, D), lambda b, i: (b, 0, 0)),
                  pl.BlockSpec((1, 1, D), lambda b, i: (b, 0, 0)),
                  pl.BlockSpec((1, D), lambda b, i: (0, 0)),
                  pl.BlockSpec((D, N), lambda b, i: (0, 0))],
        out_specs=pl.BlockSpec((1, ROW_TILE, N), lambda b, i: (b, i, 0)),
        compiler_params=_cparams(("arbitrary", "arbitrary")),
        name="norm_proj",
    )(x, sh, sc, ng, w)


def _out_proj_kernel(x_ref, a_ref, b_ref, wa_ref, wb_ref, g_ref, o_ref):
    y = _dot(a_ref[0].astype(BF16), wa_ref[...]) + _dot(b_ref[0].astype(BF16), wb_ref[...])
    o_ref[0] = x_ref[0] + g_ref[0] * y


def _out_proj(x, a, a_blk, b, b_blk, w, gate):
    B, S, D = x.shape
    half = w.shape[0] // 2
    wa, wb = w[:half], w[half:]
    return pl.pallas_call(
        _out_proj_kernel,
        out_shape=jax.ShapeDtypeStruct((B, S, D), F32),
        grid=(B, S // ROW_TILE),
        in_specs=[pl.BlockSpec((1, ROW_TILE, D), lambda b, i: (b, i, 0)),
                  pl.BlockSpec((1, ROW_TILE, half), lambda b, i: (b, i, a_blk)),
                  pl.BlockSpec((1, ROW_TILE, half), lambda b, i: (b, i, b_blk)),
                  pl.BlockSpec((half, D), lambda b, i: (0, 0)),
                  pl.BlockSpec((half, D), lambda b, i: (0, 0)),
                  pl.BlockSpec((1, 1, D), lambda b, i: (b, 0, 0))],
        out_specs=pl.BlockSpec((1, ROW_TILE, D), lambda b, i: (b, i, 0)),
        compiler_params=_cparams(("arbitrary", "arbitrary")),
        name="out_proj",
    )(x, a, b, wa, wb, gate)


def _conv_kernel(cb_ref, cc_ref, ch_ref, w_ref, o_ref):
    u = cc_ref[0] * ch_ref[0]
    row = lax.broadcasted_iota(jnp.int32, u.shape, 0)
    u1 = jnp.where(row >= 1, pltpu.roll(u, 1, axis=0), 0.0)
    u2 = jnp.where(row >= 2, pltpu.roll(u, 2, axis=0), 0.0)
    w = w_ref[...]
    o_ref[0] = cb_ref[0] * (w[0:1] * u2 + w[1:2] * u1 + w[2:3] * u)


def _short_conv(y, conv_w):
    B, S, _ = y.shape
    nblk = CONV_CH // LANES
    w = jnp.zeros((8, CONV_CH), F32).at[:CONV_K].set(conv_w)
    return pl.pallas_call(
        _conv_kernel,
        out_shape=jax.ShapeDtypeStruct((B, S, CONV_CH), F32),
        grid=(B, nblk),
        in_specs=[pl.BlockSpec((1, S, LANES), lambda b, j: (b, 0, j)),
                  pl.BlockSpec((1, S, LANES), lambda b, j: (b, 0, nblk + j)),
                  pl.BlockSpec((1, S, LANES), lambda b, j: (b, 0, 2 * nblk + j)),
                  pl.BlockSpec((8, LANES), lambda b, j: (0, j))],
        out_specs=pl.BlockSpec((1, S, LANES), lambda b, j: (b, 0, j)),
        compiler_params=_cparams(("arbitrary", "arbitrary")),
        name="short_conv",
    )(y, y, y, w)


def _cumsum_rows(x):
    n = x.shape[0]
    row = lax.broadcasted_iota(jnp.int32, x.shape, 0)
    sh = 1
    while sh < n:
        x = x + jnp.where(row >= sh, pltpu.roll(x, sh, axis=0), 0.0)
        sh *= 2
    return x


FOX_KEY_PARTS = 8


def _fox_kernel(q_ref, k_ref, v_ref, f_ref, bf_ref, qg_ref, kg_ref, o_ref,
                kn_scr, vb_scr, cum_scr, cumrow_scr):
    qi = pl.program_id(1)
    S = k_ref.shape[1]
    T = q_ref.shape[1]
    nq = S // T

    @pl.when(qi == 0)
    def _prep():
        ones_col = (lax.broadcasted_iota(jnp.int32, (S, LANES - HEAD_DIM), 1) == 0).astype(BF16)
        for h in range(FOX_HEADS):
            cols = slice(h * HEAD_DIM, (h + 1) * HEAD_DIM)
            kn_scr[:, cols] = _rms(k_ref[0, :, cols], kg_ref[...]).astype(BF16)
            vb_scr[:, h * LANES:h * LANES + HEAD_DIM] = v_ref[0, :, cols].astype(BF16)
            vb_scr[:, h * LANES + HEAD_DIM:(h + 1) * LANES] = ones_col
        logf = jax.nn.log_sigmoid(f_ref[0] + bf_ref[...])
        cum = _cumsum_rows(logf)
        cum_scr[...] = cum
        for kt in range(S // T):
            cumrow_scr[:, kt * T:(kt + 1) * T] = cum[kt * T:(kt + 1) * T, :].T[0:8, :]

    cq = cum_scr[pl.ds(pl.multiple_of(qi * T, T), T), :]

    def attend(nkeys):
        row = qi * T + lax.broadcasted_iota(jnp.int32, (T, nkeys), 0)
        causal = jnp.where(lax.broadcasted_iota(jnp.int32, (T, nkeys), 1) <= row, 0.0, NEG_INF)
        for h in range(FOX_HEADS):
            cols = slice(h * HEAD_DIM, (h + 1) * HEAD_DIM)
            qh = (_rms(q_ref[0, :, cols], qg_ref[...]) * SCALE).astype(BF16)
            s = _dot_nt(qh, kn_scr[0:nkeys, cols]) + ((cq[:, h:h + 1] - cumrow_scr[h:h + 1, 0:nkeys]) + causal)
            p = jnp.exp(s - jnp.max(s, axis=-1, keepdims=True))
            acc = _dot(p.astype(BF16), vb_scr[0:nkeys, h * LANES:(h + 1) * LANES])
            o_ref[0, :, cols] = acc[:, :HEAD_DIM] / acc[:, HEAD_DIM:HEAD_DIM + 1]

    parts = min(FOX_KEY_PARTS, nq)
    per = nq // parts
    for part in range(parts):
        pl.when((qi >= part * per) & (qi < (part + 1) * per))(functools.partial(attend, (part + 1) * (S // parts)))


def _fox_attention(y, b_f, q_g, k_g):
    B, S, _ = y.shape
    T = FOX_TILE
    qblk = 3 * CONV_CH // FOX_WIDTH
    fblk = (3 * CONV_CH + 3 * FOX_WIDTH) // LANES
    bf = jnp.zeros((1, LANES), F32).at[0, :FOX_HEADS].set(b_f)
    return pl.pallas_call(
        _fox_kernel,
        out_shape=jax.ShapeDtypeStruct((B, S, FOX_WIDTH), F32),
        grid=(B, S // T),
        in_specs=[pl.BlockSpec((1, T, FOX_WIDTH), lambda b, i: (b, i, qblk)),
                  pl.BlockSpec((1, S, FOX_WIDTH), lambda b, i: (b, 0, qblk + 1)),
                  pl.BlockSpec((1, S, FOX_WIDTH), lambda b, i: (b, 0, qblk + 2)),
                  pl.BlockSpec((1, S, LANES), lambda b, i: (b, 0, fblk)),
                  pl.BlockSpec((1, LANES), lambda b, i: (0, 0)),
                  pl.BlockSpec((1, HEAD_DIM), lambda b, i: (0, 0)),
                  pl.BlockSpec((1, HEAD_DIM), lambda b, i: (0, 0))],
        out_specs=pl.BlockSpec((1, T, FOX_WIDTH), lambda b, i: (b, i, 0)),
        scratch_shapes=[pltpu.VMEM((S, FOX_WIDTH), BF16),
                        pltpu.VMEM((S, FOX_HEADS * LANES), BF16),
                        pltpu.VMEM((S, LANES), F32),
                        pltpu.VMEM((8, S), F32)],
        compiler_params=_cparams(("arbitrary", "arbitrary")),
        name="fox_attention",
    )(y, y, y, y, bf, q_g.reshape(1, HEAD_DIM), k_g.reshape(1, HEAD_DIM))


def _gelu_tanh(x):
    c0 = math.sqrt(2.0 / math.pi)
    return (0.5 * x) * (1.0 + jnp.tanh(x * (c0 + (c0 * 0.044715) * (x * x))))


def _extract_top(work, iota, k, exact):
    nrows = work.shape[0]
    vals = []
    for _ in range(k):
        m = jnp.max(work, axis=0, keepdims=True)
        if exact:
            idx = jnp.min(jnp.where(work == m, iota, nrows), axis=0, keepdims=True)
            work = jnp.where(iota == idx, -jnp.inf, work)
        else:
            work = jnp.where(work == m, -jnp.inf, work)
        vals.append(m)
    return vals, work


def _removed(work):
    return jnp.sum(jnp.where(work == -jnp.inf, 1.0, 0.0), axis=0, keepdims=True)


def _peer_kernel(x_ref, sh_ref, sc_ref, gt_ref, ng_ref, wqh_ref, wql_ref, keys_ref, u_ref, vt_ref, o_ref,
                 ht_scr, qt_scr, sp_scr, t0_scr, t1_scr, tau_scr, a_scr, p_scr, acc_scr):
    c = pl.program_id(2)
    nc = pl.num_programs(2)
    tn = x_ref.shape[1]
    nlt = tn // LANES
    half = PEER_QDIM // 2

    @pl.when(c == 0)
    def _prep():
        h = _rms(x_ref[0], ng_ref[...]) * (1.0 + sc_ref[0]) + sh_ref[0]
        hh = h.astype(BF16)
        hl = (h - hh.astype(F32)).astype(BF16)
        cw = 2 * LANES
        for k in range(h.shape[1] // cw):
            cs = slice(k * cw, (k + 1) * cw)
            q = _dot(hh, wqh_ref[:, cs]) + (_dot(hh, wql_ref[:, cs]) + _dot(hl, wqh_ref[:, cs]))
            qt_scr[cs, :] = q.T
            ht_scr[cs, :] = h[:, cs].T.astype(BF16)
        acc_scr[...] = jnp.zeros_like(acc_scr)
        iota_k = lax.broadcasted_iota(jnp.int32, (PEER_NKEYS, LANES), 0)
        iota_c = lax.broadcasted_iota(jnp.int32, (PEER_TOPK + 8 * 8, LANES), 0)

        n_pad = sum(8 - PEER_TOPK // (r0 + 1) for r0 in range(2, 8))

        def select(hd, slot, lt, exact):
            lanes = slice(lt * LANES, (lt + 1) * LANES)
            s0 = sp_scr[2 * slot, :, lanes]
            s1 = sp_scr[2 * slot + 1, :, lanes]
            sv0, w0 = _extract_top(s0, iota_k, PEER_TOPK, exact)
            sv1, w1 = _extract_top(s1, iota_k, PEER_TOPK, exact)
            s0m = jnp.where(w0 == -jnp.inf, s0, -jnp.inf)
            s1m = jnp.where(w1 == -jnp.inf, s1, -jnp.inf)
            sv0_all = jnp.concatenate(sv0, axis=0)
            sv1_all = jnp.concatenate(sv1, axis=0)
            parts = [sv0[0] + sv1_all, sv0[1] + sv1_all[:8]]
            for r0 in range(2, 8):
                parts.append(jnp.where(iota_c[:8] < PEER_TOPK // (r0 + 1), sv0[r0] + sv1_all[:8], -jnp.inf))
            parts.append(sv0_all[8:] + sv1[0])
            cv, wc = _extract_top(jnp.concatenate(parts, axis=0), iota_c, PEER_TOPK, exact)
            z = cv[0] - cv[0]
            for r in range(PEER_TOPK):
                z = z + jnp.exp(cv[r] - cv[0])
            nrt = PEER_NKEYS // 8
            t0_scr[hd, lt, :, 0] = s0m.reshape(nrt, 8, LANES)
            t0_scr[hd, lt, :, 1] = jnp.exp(s0m - sv0[0]).reshape(nrt, 8, LANES)
            t1_scr[hd, lt, :, 0] = s1m.reshape(nrt, 8, LANES)
            t1_scr[hd, lt, :, 1] = (jnp.exp(s1m - sv1[0]) / z).reshape(nrt, 8, LANES)
            tau_scr[hd, lt] = jnp.broadcast_to(cv[PEER_TOPK - 1], (8, LANES))
            return ((_removed(w0) != PEER_TOPK) | (_removed(w1) != PEER_TOPK)
                    | (_removed(wc) != PEER_TOPK + n_pad))

        def head_pair_body(hp, carry):
            for slot in range(2):
                for p in range(2):
                    qs = qt_scr[pl.ds(pl.multiple_of(((hp * 2 + slot) * 2 + p) * half, half), half), :]
                    sp_scr[2 * slot + p] = _dot(keys_ref[p], qs, HIGHEST)
            tied = [jnp.zeros((1, LANES), F32) for _ in range(2)]
            for lt in range(nlt):
                for slot in range(2):
                    tied[slot] = jnp.maximum(tied[slot], jnp.where(select(hp * 2 + slot, slot, lt, False), 1.0, 0.0))
            for slot in range(2):
                @pl.when(jnp.max(tied[slot]) > 0.0)
                def _redo(slot=slot):
                    for lt in range(nlt):
                        select(hp * 2 + slot, slot, lt, True)
            return carry

        lax.fori_loop(0, PEER_HEADS // 2, head_pair_body, 0)

    nsub = 8
    nrq = 32
    nparts = PEER_EC // PEER_NKEYS // nsub
    for part in range(nparts):
        er = slice(part * nsub * PEER_NKEYS, (part + 1) * nsub * PEER_NKEYS)
        for lh in range(nlt // 2):
            lanes2 = slice(lh * 2 * LANES, (lh + 1) * 2 * LANES)
            a_scr[er, lanes2] = _gelu_tanh(_dot(u_ref[er, :], ht_scr[:, lanes2]))

    def weights_tile(part, lt, rq, prev):
        lanes = slice(lt * LANES, (lt + 1) * LANES)
        jt = slice(rq * nrq // 8, (rq + 1) * nrq // 8)
        blk = c * (PEER_EC // PEER_NKEYS // 8) + part * nsub // 8
        zero = 0.0 if prev is None else jnp.where(prev != prev, prev, 0.0)
        w = [jnp.zeros((nrq, LANES), F32) for _ in range(nsub)]
        for hd in range(PEER_HEADS):
            s1 = t1_scr[hd, lt, jt, 0].reshape(nrq, LANES)
            e1 = t1_scr[hd, lt, jt, 1].reshape(nrq, LANES)
            s0 = t0_scr[hd, lt, blk, 0] + zero
            e0 = t0_scr[hd, lt, blk, 1] + zero
            tau = tau_scr[hd, lt, 0:1, :]
            for ii in range(nsub):
                keep = s1 + s0[ii:ii + 1] >= tau
                w[ii] = w[ii] + jnp.where(keep, e1 * e0[ii:ii + 1], 0.0)
        for ii in range(nsub):
            rows = slice((part * nsub + ii) * PEER_NKEYS + rq * nrq, (part * nsub + ii) * PEER_NKEYS + (rq + 1) * nrq)
            w[ii] = w[ii] * a_scr[rows, lanes]
            p_scr[rows, lanes] = w[ii].astype(BF16)
        return w[nsub - 1][0:1, :]

    in_flight = 8
    done = [None] * in_flight
    for part in range(nparts):
        for lt in range(nlt):
            for rq in range(PEER_NKEYS // nrq):
                done.append(weights_tile(part, lt, rq, done[-in_flight]))
    acc_scr[...] += _dot(vt_ref[...], p_scr[...])

    @pl.when(c == nc - 1)
    def _fin():
        o_ref[0] = x_ref[0] + gt_ref[0] * acc_scr[...].T


def _peer(x, sh, sc, gate, ng, w_q, keys, u_b, vt_b):
    B, S, D = x.shape
    tn = PEER_TN
    nc = PEER_EXPERTS // PEER_EC
    tab = pltpu.VMEM((PEER_HEADS, tn // LANES, PEER_NKEYS // 8, 2, 8, LANES), F32)
    wq_hi = w_q.astype(BF16)
    wq_lo = (w_q - wq_hi.astype(F32)).astype(BF16)
    wspec = pl.BlockSpec((D, PEER_HEADS * PEER_QDIM), lambda b, i, c: (0, 0), pipeline_mode=pl.Buffered(1))
    return pl.pallas_call(
        _peer_kernel,
        out_shape=jax.ShapeDtypeStruct((B, S, D), F32),
        grid=(B, S // tn, nc),
        in_specs=[pl.BlockSpec((1, tn, D), lambda b, i, c: (b, i, 0)),
                  pl.BlockSpec((1, 1, D), lambda b, i, c: (b, 0, 0)),
                  pl.BlockSpec((1, 1, D), lambda b, i, c: (b, 0, 0)),
                  pl.BlockSpec((1, 1, D), lambda b, i, c: (b, 0, 0)),
                  pl.BlockSpec((1, D), lambda b, i, c: (0, 0)),
                  wspec, wspec,
                  pl.BlockSpec((2, PEER_NKEYS, PEER_QDIM // 2), lambda b, i, c: (0, 0, 0)),
                  pl.BlockSpec((PEER_EC, D), lambda b, i, c: (c, 0)),
                  pl.BlockSpec((D, PEER_EC), lambda b, i, c: (0, c))],
        out_specs=pl.BlockSpec((1, tn, D), lambda b, i, c: (b, i, 0)),
        scratch_shapes=[pltpu.VMEM((D, tn), BF16),
                        pltpu.VMEM((PEER_HEADS * PEER_QDIM, tn), F32),
                        pltpu.VMEM((4, PEER_NKEYS, tn), F32),
                        tab, tab,
                        pltpu.VMEM((PEER_HEADS, tn // LANES, 8, LANES), F32),
                        pltpu.VMEM((PEER_EC, tn), F32),
                        pltpu.VMEM((PEER_EC, tn), BF16),
                        pltpu.VMEM((D, tn), F32)],
        compiler_params=_cparams(("arbitrary", "arbitrary", "arbitrary")),
        name="peer",
    )(x, sh, sc, gate, ng, wq_hi, wq_lo, keys, u_b, vt_b)


def _rel_bucket(dist):
    n = jnp.maximum(dist, 0)
    max_exact = REL_BUCKETS // 2
    nf = jnp.maximum(n, 1).astype(F32)
    large = max_exact + (jnp.log(nf / max_exact) / math.log(REL_MAX_DIST / max_exact)
                         * (REL_BUCKETS - max_exact)).astype(jnp.int32)
    large = jnp.minimum(large, REL_BUCKETS - 1)
    return jnp.where(n < max_exact, n, large)


def _bias_lookup(bucket, tab_ref, h):
    bias = jnp.zeros(bucket.shape, F32)
    for b in range(REL_BUCKETS):
        bias = jnp.where(bucket == b, tab_ref[b, h], bias)
    return bias


N_BIAS_TILES = 5


def _relbias_kernel(tab_ref, o_ref, c_ref):
    h = pl.program_id(0)
    T = o_ref.shape[2]
    a = lax.broadcasted_iota(jnp.int32, (T, T), 0)
    b = lax.broadcasted_iota(jnp.int32, (T, T), 1)
    last = tab_ref[REL_BUCKETS - 1, h]
    o_ref[0, 0] = jnp.where(a >= b, _bias_lookup(_rel_bucket(a - b), tab_ref, h), NEG_INF)
    o_ref[0, 1] = _bias_lookup(_rel_bucket(a - b + T), tab_ref, h)
    o_ref[0, 2] = jnp.full((T, T), last, F32)
    o_ref[0, 3] = jnp.where(a < b, last, NEG_INF)
    o_ref[0, 4] = jnp.full((T, T), NEG_INF, F32)
    dist = a - ((b - T // 2) * CMP_STRIDE + CMP_LEN - 1)
    c_ref[0] = jnp.where(dist >= 0, _bias_lookup(_rel_bucket(dist), tab_ref, h), last)


def _relbias(rel_table):
    T = ATT_TILE
    assert 2 * T - (T - 1) >= REL_MAX_DIST
    assert WINDOW % T == 0
    return pl.pallas_call(
        _relbias_kernel,
        out_shape=(jax.ShapeDtypeStruct((NSA_HEADS, N_BIAS_TILES, T, T), F32),
                   jax.ShapeDtypeStruct((NSA_HEADS, T, T), F32)),
        grid=(NSA_HEADS,),
        in_specs=[pl.BlockSpec(memory_space=pltpu.SMEM)],
        out_specs=(pl.BlockSpec((1, N_BIAS_TILES, T, T), lambda h: (h, 0, 0, 0)),
                   pl.BlockSpec((1, T, T), lambda h: (h, 0, 0))),
        compiler_params=_cparams(("arbitrary",)),
        name="rel_bias_tiles",
    )(rel_table)


def _cmp_kernel(x_ref, pos_ref, w1_ref, w2_ref, kg_ref, o_ref):
    j = pl.program_id(1)
    half = CMP_STRIDE * HEAD_DIM
    x = x_ref[0, 0]
    pos = pos_ref[0]
    pa = _dot((x + pos[:, :half]).astype(BF16), w1_ref[0, :half, :])
    pb = _dot((x + pos[:, half:]).astype(BF16), w1_ref[0, half:, :])
    pre = pa + pltpu.roll(pb, pb.shape[0] - 1, axis=0)
    out = _dot(jax.nn.gelu(pre, approximate=True).astype(BF16), w2_ref[0])
    o_ref[0, 0] = jnp.where(j < NSA_GROUPS, _rms(out, kg_ref[...]), out)


def _nsa_compress(xr, pos, w1, w2, kg0):
    B, _, M, W = xr.shape
    G = NSA_GROUPS
    return pl.pallas_call(
        _cmp_kernel,
        out_shape=jax.ShapeDtypeStruct((B, 2 * G, M, HEAD_DIM), F32),
        grid=(B, 2 * G),
        in_specs=[pl.BlockSpec((1, 1, M, W), lambda b, j: (b, j, 0, 0)),
                  pl.BlockSpec((1, 1, 2 * W), lambda b, j: (j // G, 0, 0)),
                  pl.BlockSpec((1, 2 * W, HEAD_DIM), lambda b, j: (j // G, 0, 0)),
                  pl.BlockSpec((1, HEAD_DIM, HEAD_DIM), lambda b, j: (j // G, 0, 0)),
                  pl.BlockSpec((1, HEAD_DIM), lambda b, j: (0, 0))],
        out_specs=pl.BlockSpec((1, 1, M, HEAD_DIM), lambda b, j: (b, j, 0, 0)),
        compiler_params=_cparams(("arbitrary", "arbitrary")),
        name="nsa_compress",
    )(xr, pos.reshape(2, 1, 2 * W), w1.astype(BF16), w2.astype(BF16), kg0.reshape(1, HEAD_DIM))


def _cmp_attn_kernel(q_ref, kv_ref, ov_ref, cb_ref, qg_ref, oc_ref, sel_ref, qn_ref):
    qi = pl.program_id(1)
    T = q_ref.shape[1]
    NC = kv_ref.shape[2]
    G, J = NSA_GROUPS, NSA_HPG
    NS = sel_ref.shape[2] // G
    assert T == NC == LANES and T % CMP_STRIDE == 0 and G * NS == LANES
    t = qi * T + lax.broadcasted_iota(jnp.int32, (T, NC), 0)
    lane = lax.broadcasted_iota(jnp.int32, (T, NC), 1)
    vis = t - (lane * CMP_STRIDE + CMP_LEN - 1) >= 0
    visf = vis.astype(F32)
    shift = (qi * (T // CMP_STRIDE) + T // 2) % LANES
    imp = jnp.zeros((T, LANES), F32)
    vis4 = jnp.concatenate([vis] * J, axis=0)
    visf4 = jnp.concatenate([visf] * J, axis=0)
    for g in range(G):
        kc = kv_ref[0, g].astype(BF16)
        vc = kv_ref[0, G + g].astype(BF16)
        qhs = []
        for j in range(J):
            cols = slice((g * J + j) * HEAD_DIM, (g * J + j + 1) * HEAD_DIM)
            qhs.append((_rms(q_ref[0, :, cols], qg_ref[...]) * SCALE).astype(BF16))
            qn_ref[0, :, cols] = qhs[j]
        bias = jnp.concatenate([pltpu.roll(cb_ref[g * J + j], shift, axis=1) for j in range(J)], axis=0)
        s = jnp.where(vis4, _dot_nt(jnp.concatenate(qhs, axis=0), kc) + bias, NEG_INF)
        e = jnp.exp(s - jnp.max(s, axis=-1, keepdims=True))
        p = e / jnp.sum(e, axis=-1, keepdims=True) * visf4
        o = _dot(p.astype(BF16), vc)
        psum = p[0:T]
        for j in range(J):
            oc_ref[0, :, (g * J + j) * HEAD_DIM:(g * J + j + 1) * HEAD_DIM] = o[j * T:(j + 1) * T]
            if j:
                psum = psum + p[j * T:(j + 1) * T]
        imp = imp + _dot(psum, ov_ref[g], HIGHEST)
    mm = lane % NS
    cur = t // SLC_LEN
    forced = (mm == 0) | (mm == cur) | (mm == cur - 1)
    imp = jnp.where(forced, FORCE_SCORE, jnp.where(mm * SLC_LEN <= t, imp, -1.0))
    rank = jnp.zeros((T, LANES), F32)
    for r in range(1, NS):
        wrapped = mm >= NS - r
        other = jnp.where(wrapped, pltpu.roll(imp, NS - r, axis=1), pltpu.roll(imp, LANES - r, axis=1))
        rank = rank + jnp.where(other > imp, 1.0, jnp.where(other == imp, wrapped.astype(F32), 0.0))
    sel_ref[0] = jnp.where(rank < SLC_TOPN, jnp.where(imp >= 0.0, 1.0, 0.0), 0.0)


def _nsa_cmp_attn(y, kcvc, overlap, cmp_bias, q_g):
    B, S, _ = y.shape
    T = ATT_TILE
    NC = kcvc.shape[2]
    return pl.pallas_call(
        _cmp_attn_kernel,
        out_shape=(jax.ShapeDtypeStruct((B, S, NSA_WIDTH), F32),
                   jax.ShapeDtypeStruct((B, S, LANES), F32),
                   jax.ShapeDtypeStruct((B, S, NSA_WIDTH), BF16)),
        grid=(B, S // T),
        in_specs=[pl.BlockSpec((1, T, NSA_WIDTH), lambda b, i: (b, i, 0)),
                  pl.BlockSpec((1, 2 * NSA_GROUPS, NC, HEAD_DIM), lambda b, i: (b, 0, 0, 0)),
                  pl.BlockSpec(overlap.shape, lambda b, i: (0, 0, 0)),
                  pl.BlockSpec(cmp_bias.shape, lambda b, i: (0, 0, 0)),
                  pl.BlockSpec((1, HEAD_DIM), lambda b, i: (0, 0))],
        out_specs=(pl.BlockSpec((1, T, NSA_WIDTH), lambda b, i: (b, i, 0)),
                   pl.BlockSpec((1, T, LANES), lambda b, i: (b, i, 0)),
                   pl.BlockSpec((1, T, NSA_WIDTH), lambda b, i: (b, i, 0))),
        compiler_params=_cparams(("arbitrary", "arbitrary")),
        name="nsa_cmp_attn",
    )(y, kcvc, overlap, cmp_bias, q_g.reshape(1, HEAD_DIM))


def _kv_prep_kernel(ks_ref, kw_ref, kg1_ref, kg2_ref, ksn_ref, vs_ref, kwn_ref, vw_ref):
    KV = NSA_KV
    rows = ks_ref.shape[1]
    ones_col = (lax.broadcasted_iota(jnp.int32, (rows, LANES - HEAD_DIM), 1) == 0).astype(BF16)
    for g in range(NSA_GROUPS):
        cols = slice(g * HEAD_DIM, (g + 1) * HEAD_DIM)
        vcols = slice(KV + g * HEAD_DIM, KV + (g + 1) * HEAD_DIM)
        ksn_ref[0, :, cols] = _rms(ks_ref[0, :, cols], kg1_ref[...]).astype(BF16)
        kwn_ref[0, :, cols] = _rms(kw_ref[0, :, cols], kg2_ref[...]).astype(BF16)
        vs_ref[0, :, g * LANES:g * LANES + HEAD_DIM] = ks_ref[0, :, vcols].astype(BF16)
        vw_ref[0, :, g * LANES:g * LANES + HEAD_DIM] = kw_ref[0, :, vcols].astype(BF16)
        vs_ref[0, :, g * LANES + HEAD_DIM:(g + 1) * LANES] = ones_col
        vw_ref[0, :, g * LANES + HEAD_DIM:(g + 1) * LANES] = ones_col


def _nsa_kv_prep(y, k_g):
    B, S, _ = y.shape
    ksblk = (NSA_WIDTH + 2 * NSA_KV) // (2 * NSA_KV)
    kout = jax.ShapeDtypeStruct((B, S, NSA_KV), BF16)
    vout = jax.ShapeDtypeStruct((B, S, NSA_GROUPS * LANES), BF16)
    kspec = pl.BlockSpec((1, ROW_TILE, NSA_KV), lambda b, i: (b, i, 0))
    vspec = pl.BlockSpec((1, ROW_TILE, NSA_GROUPS * LANES), lambda b, i: (b, i, 0))
    return pl.pallas_call(
        _kv_prep_kernel,
        out_shape=(kout, vout, kout, vout),
        grid=(B, S // ROW_TILE),
        in_specs=[pl.BlockSpec((1, ROW_TILE, 2 * NSA_KV), lambda b, i: (b, i, ksblk)),
                  pl.BlockSpec((1, ROW_TILE, 2 * NSA_KV), lambda b, i: (b, i, ksblk + 1)),
                  pl.BlockSpec((1, HEAD_DIM), lambda b, i: (0, 0)),
                  pl.BlockSpec((1, HEAD_DIM), lambda b, i: (0, 0))],
        out_specs=(kspec, vspec, kspec, vspec),
        compiler_params=_cparams(("arbitrary", "arbitrary")),
        name="nsa_kv_prep",
    )(y, y, k_g[1].reshape(1, HEAD_DIM), k_g[2].reshape(1, HEAD_DIM))


SEL_KEY_PARTS = 8


def _sel_win_kernel(qn_ref, ks_ref, vs_ref, kw_ref, vw_ref, gl_ref, sel_ref, oc_ref, bias_ref, e_ref, bg_ref,
                    o_ref):
    qi = pl.program_id(1)
    T = qn_ref.shape[1]
    S = ks_ref.shape[1]
    nk = S // T
    G, J = NSA_GROUPS, NSA_HPG
    NS = e_ref.shape[0]
    wt = WINDOW // T
    nwin = wt + 1
    assert nk % SEL_KEY_PARTS == 0 and nk >= nwin

    def sel_tile(r):
        return jnp.where(r < 0, 4, jnp.minimum(r, 2))

    def win_tile(r):
        return jnp.where(r < 0, 4, jnp.where(r <= 1, r, jnp.where(r < wt, 2, jnp.where(r == wt, 3, 4))))

    def softmax_pv(s, v):
        p = jnp.exp(s - jnp.max(s, axis=-1, keepdims=True))
        acc = _dot(p.astype(BF16), v)
        return acc[:, :HEAD_DIM] / acc[:, HEAD_DIM:HEAD_DIM + 1]

    gates = jax.nn.sigmoid(gl_ref[0] + bg_ref[...])
    k0 = jnp.maximum(qi - wt, 0)
    wrows = pl.ds(pl.multiple_of(k0 * T, T), nwin * T)

    def attend(nkt):
        nkeys = nkt * T
        for g in range(G):
            gcols = slice(g * HEAD_DIM, (g + 1) * HEAD_DIM)
            vcols = slice(g * LANES, (g + 1) * LANES)
            q4 = jnp.concatenate(
                [qn_ref[0, :, (g * J + j) * HEAD_DIM:(g * J + j + 1) * HEAD_DIM] for j in range(J)], axis=0)
            selg = sel_ref[0, :, g * NS:(g + 1) * NS].astype(BF16)
            mask = (_dot(selg, e_ref[:, :nkeys]) - 1.0) * (-NEG_INF)
            bias = jnp.concatenate(
                [jnp.concatenate([bias_ref[g * J + j, sel_tile(qi - kt)] for kt in range(nkt)], axis=1) + mask
                 for j in range(J)], axis=0)
            o_s = softmax_pv(_dot_nt(q4, ks_ref[0, :nkeys, gcols]) + bias, vs_ref[0, :nkeys, vcols])
            wbias = jnp.concatenate(
                [jnp.concatenate([bias_ref[g * J + j, win_tile(qi - k0 - t)] for t in range(nwin)], axis=1)
                 for j in range(J)], axis=0)
            o_w = softmax_pv(_dot_nt(q4, kw_ref[0, wrows, gcols]) + wbias, vw_ref[0, wrows, vcols])
            for j in range(J):
                h = g * J + j
                cols = slice(h * HEAD_DIM, (h + 1) * HEAD_DIM)
                rows = slice(j * T, (j + 1) * T)
                o_ref[0, :, cols] = (gates[:, 3 * h:3 * h + 1] * oc_ref[0, :, cols]
                                     + gates[:, 3 * h + 1:3 * h + 2] * o_s[rows]
                                     + gates[:, 3 * h + 2:3 * h + 3] * o_w[rows])

    per = nk // SEL_KEY_PARTS
    for part in range(SEL_KEY_PARTS):
        pl.when((qi >= part * per) & (qi < (part + 1) * per))(functools.partial(attend, (part + 1) * per))


def _nsa_sel_win(qn, y, kv, sel, oc, bias_tiles, expand, b_gate):
    B, S, _ = y.shape
    T = ATT_TILE
    glblk = (NSA_WIDTH + 6 * NSA_KV) // LANES
    bg = jnp.zeros((1, LANES), F32).at[0, :3 * NSA_HEADS].set(b_gate)
    kspec = pl.BlockSpec((1, S, NSA_KV), lambda b, i: (b, 0, 0))
    vspec = pl.BlockSpec((1, S, NSA_GROUPS * LANES), lambda b, i: (b, 0, 0))
    return pl.pallas_call(
        _sel_win_kernel,
        out_shape=jax.ShapeDtypeStruct((B, S, NSA_WIDTH), F32),
        grid=(B, S // T),
        in_specs=[pl.BlockSpec((1, T, NSA_WIDTH), lambda b, i: (b, i, 0)),
                  kspec, vspec, kspec, vspec,
                  pl.BlockSpec((1, T, LANES), lambda b, i: (b, i, glblk)),
                  pl.BlockSpec((1, T, sel.shape[2]), lambda b, i: (b, i, 0)),
                  pl.BlockSpec((1, T, NSA_WIDTH), lambda b, i: (b, i, 0)),
                  pl.BlockSpec(bias_tiles.shape, lambda b, i: (0, 0, 0, 0)),
                  pl.BlockSpec(expand.shape, lambda b, i: (0, 0)),
                  pl.BlockSpec((1, LANES), lambda b, i: (0, 0))],
        out_specs=pl.BlockSpec((1, T, NSA_WIDTH), lambda b, i: (b, i, 0)),
        compiler_params=_cparams(("arbitrary", "arbitrary")),
        name="nsa_sel_win",
    )(qn, *kv, y, sel, oc, bias_tiles, expand, bg)


def _odd_mixer(x, sh, sc, gate, ng, w_in, b_gate, q_g, k_g, cmp_pos, cmp_w1, cmp_w2, bias_tiles, cmp_bias,
               w_out):
    B, S, _ = x.shape
    G = NSA_GROUPS
    w = jnp.zeros((D_MODEL, ODD_IN_PAD), BF16).at[:, :ODD_IN].set(w_in.astype(BF16))
    y = _norm_proj(x, sh, sc, ng, w)
    xr = y[:, :, NSA_WIDTH:NSA_WIDTH + 2 * NSA_KV].reshape(B, S, 2, G, HEAD_DIM)
    xr = xr.transpose(0, 2, 3, 1, 4).reshape(B, 2 * G, S // CMP_STRIDE, CMP_STRIDE * HEAD_DIM)
    kcvc = _nsa_compress(xr, cmp_pos.reshape(2, CMP_LEN * HEAD_DIM), cmp_w1, cmp_w2, k_g[0])
    n_slc = S // SLC_LEN
    starts = jnp.arange(S // CMP_STRIDE) * CMP_STRIDE
    bstart = jnp.arange(n_slc) * SLC_LEN
    real = (jnp.arange(S // CMP_STRIDE) < (S - CMP_LEN) // CMP_STRIDE + 1)[:, None]
    overlap = (real & (starts[:, None] < bstart[None, :] + SLC_LEN)
               & (starts[:, None] + CMP_LEN > bstart[None, :])).astype(F32)
    overlap = jnp.stack([jnp.pad(overlap, ((0, 0), (g * n_slc, (G - 1 - g) * n_slc))) for g in range(G)])
    expand = (jnp.arange(S)[None, :] // SLC_LEN == jnp.arange(n_slc)[:, None]).astype(BF16)
    oc, sel, qn = _nsa_cmp_attn(y, kcvc, overlap, cmp_bias, q_g)
    o = _nsa_sel_win(qn, y, _nsa_kv_prep(y, k_g), sel, oc, bias_tiles, expand, b_gate)
    return _out_proj(x, o, 0, o, 1, w_out.astype(BF16), gate)


def _even_mixer(x, sh, sc, gate, ng, w_in, b_f, conv_w, q_g, k_g, w_out):
    w = jnp.zeros((D_MODEL, EVEN_IN_PAD), BF16).at[:, :EVEN_IN].set(w_in.astype(BF16))
    y = _norm_proj(x, sh, sc, ng, w)
    conv = _short_conv(y, conv_w)
    attn = _fox_attention(y, b_f, q_g, k_g)
    return _out_proj(x, conv, 0, attn, 0, w_out.astype(BF16), gate)


def kernel(x, c, ada_w, ada_b, norm_g, even_w_in, even_b_f, even_conv_w, even_q_g, even_k_g, even_w_out, odd_w_in, odd_b_gate, odd_q_g, odd_k_g, odd_cmp_pos, odd_cmp_w1, odd_cmp_w2, odd_w_out, rel_table, peer_w_q, peer_keys, peer_u, peer_v):
    B = x.shape[0]
    mods = _ada(c, ada_w, ada_b)
    bias_tiles, cmp_bias = _relbias(rel_table)
    for layer in range(DEPTH):
        sh1, sc1, g1, sh2, sc2, g2 = [m.reshape(B, 1, D_MODEL) for m in jnp.split(mods[layer], 6, axis=-1)]
        i = layer // 2
        ng1 = norm_g[layer, 0].reshape(1, D_MODEL)
        ng2 = norm_g[layer, 1].reshape(1, D_MODEL)
        if layer % 2 == 0:
            x = _even_mixer(x, sh1, sc1, g1, ng1, even_w_in[i], even_b_f[i], even_conv_w[i], even_q_g[i],
                            even_k_g[i], even_w_out[i])
        else:
            x = _odd_mixer(x, sh1, sc1, g1, ng1, odd_w_in[i], odd_b_gate[i], odd_q_g[i], odd_k_g[i],
                           odd_cmp_pos[i], odd_cmp_w1[i], odd_cmp_w2[i], bias_tiles, cmp_bias, odd_w_out[i])
        x = _peer(x, sh2, sc2, g2, ng2, peer_w_q[layer], peer_keys[layer],
                  peer_u[layer].astype(BF16), peer_v[layer].T.astype(BF16))
    return x
```
